```python
import math
import jax
import jax.numpy as jnp
from jax import lax
import numpy as np

D_MODEL = 2048
BATCH = 1
SEQ = 16384
DEPTH = 4

GRID_W = 64
CTX_LEN = 256
N_MIXERS = 3
N_HYENA_LAYERS = (DEPTH + 2) // 3
N_ATTN_LAYERS = (DEPTH + 1) // 3
N_POOL_LAYERS = DEPTH // 3
N_MOD = 9
NORM_EPS = 1e-6
D_FF = 5632
FFN_RES = 0.5
FILTER_BANDS = 16
FILTER_EMB = 1 + 2 * FILTER_BANDS
FILTER_HIDDEN = 64
DECAY_TARGET = 1e-2
SHORT_DECAY_PCT = 0.3
LONG_DECAY_PCT = 1.5
HEAD_DIM = 128
N_HEADS = D_MODEL // HEAD_DIM
N_KV_HEADS = 4
GROUP = N_HEADS // N_KV_HEADS
WINDOW = 128
ATTN_BLOCK = 128
ROPE_THETA = 10000.0
ROPE_PAIRS = HEAD_DIM // 4
QKV_DIM = (N_HEADS + 2 * N_KV_HEADS) * HEAD_DIM
POOL_SIZES = (2, 4, 8, 16)
POOL_GROUP = D_MODEL // len(POOL_SIZES)

kernel_name = 'hybrid_hyena_swa_pool_diffusion_trunk'


def rms_norm(x, g):
    xf = x.astype(jnp.float32)
    y = xf * lax.rsqrt(jnp.mean(xf * xf, axis=-1, keepdims=True) + NORM_EPS)
    return (y * g.astype(jnp.float32)).astype(x.dtype)


def modulate(h, g, mod, k):
    return rms_norm(h, g) * (1.0 + mod[:, 3 * k + 1]) + mod[:, 3 * k]


def gated_residual(h, y, g, mod, k, weight):
    return h + weight * mod[:, 3 * k + 2] * rms_norm(y, g)


def swiglu(u, w_in, w_out):
    gate, up = jnp.split(u @ w_in, 2, axis=-1)
    return (jax.nn.silu(gate) * up) @ w_out


def short_conv3(u, w, b):
    up = jnp.pad(u, ((0, 0), (1, 1), (0, 0)))
    return up[:, :-2] * w[0] + u * w[1] + up[:, 2:] * w[2] + b


def implicit_filter(L, w1, b1, w2, b2, w3, b3, w4, freq):
    D = w4.shape[1] // 2
    t = jnp.linspace(0.0, 1.0, L, dtype=jnp.float32)[:, None]
    omega = 2.0 * math.pi * jnp.arange(L, dtype=jnp.float32)[:, None] / L
    bands = jnp.linspace(1e-4, FILTER_BANDS - 1, FILTER_BANDS, dtype=jnp.float32)[None, :]
    z = jnp.concatenate([t, jnp.cos(bands * omega), -jnp.sin(bands * omega)], axis=-1)
    f = jnp.sin(freq[0] * (z @ w1 + b1))
    f = jnp.sin(freq[1] * (f @ w2 + b2))
    f = jnp.sin(freq[2] * (f @ w3 + b3))
    f = (f @ w4).astype(jnp.float32)
    deltas = jnp.abs(jnp.linspace(math.log(DECAY_TARGET) / LONG_DECAY_PCT,
                                  math.log(DECAY_TARGET) / SHORT_DECAY_PCT, D, dtype=jnp.float32))
    decay = jnp.exp(-t * deltas[None, :])
    h_fwd = f[:, :D] * decay
    h_bwd = f[:, D:] * decay
    return jnp.concatenate([h_fwd, jnp.zeros((1, D), jnp.float32), h_bwd[1:][::-1]], axis=0)


def long_conv(u, filt, skip):
    L = u.shape[1]
    uf = jnp.fft.rfft(u.astype(jnp.float32), n=2 * L, axis=1)
    ff = jnp.fft.rfft(filt, n=2 * L, axis=0)
    y = jnp.fft.irfft(uf * ff[None], n=2 * L, axis=1)[:, :L]
    return (y + u.astype(jnp.float32) * skip.astype(jnp.float32)).astype(u.dtype)


def hyena_mixer(u, w_in, b_in, w_sc, b_sc, f_w1, f_b1, f_w2, f_b2, f_w3, f_b3, f_w4, f_freq, skip, w_out, b_out):
    L = u.shape[1]
    z = short_conv3(u @ w_in + b_in, w_sc, b_sc)
    x0, x1, v = jnp.split(z, 3, axis=-1)
    filt = implicit_filter(L, f_w1, f_b1, f_w2, f_b2, f_w3, f_b3, f_w4, f_freq)
    y = x0 * long_conv(v * x1, filt, skip)
    return y @ w_out + b_out


def axial_rope(x, ang_row, ang_col):
    extra = x.ndim - 3

    def rot(v, ang):
        ang = ang.reshape((ang.shape[0],) + (1,) * extra + (ang.shape[1],))
        cos, sin = jnp.cos(ang).astype(v.dtype), jnp.sin(ang).astype(v.dtype)
        v1, v2 = jnp.split(v, 2, axis=-1)
        return jnp.concatenate([v1 * cos - v2 * sin, v2 * cos + v1 * sin], axis=-1)

    xr, xc = jnp.split(x, 2, axis=-1)
    return jnp.concatenate([rot(xr, ang_row), rot(xc, ang_col)], axis=-1)


def windowed_gqa(u, uc, w_qkv, b_qkv, sink, w_o, b_o, ang_row, ang_col, ctx_queries):
    B, L, D = u.shape
    C = uc.shape[1]
    nb = L // ATTN_BLOCK
    scale = HEAD_DIM ** -0.5
    qd, kd = N_HEADS * HEAD_DIM, N_KV_HEADS * HEAD_DIM
    qkv = u @ w_qkv + b_qkv
    q = qkv[..., :qd].reshape(B, L, N_KV_HEADS, GROUP, HEAD_DIM)
    k = qkv[..., qd:qd + kd].reshape(B, L, N_KV_HEADS, HEAD_DIM)
    v = qkv[..., qd + kd:].reshape(B, L, N_KV_HEADS, HEAD_DIM)
    q = axial_rope(q, ang_row, ang_col)
    k = axial_rope(k, ang_row, ang_col)
    qkv_c = uc @ w_qkv + b_qkv
    kc = qkv_c[..., qd:qd + kd].reshape(B, C, N_KV_HEADS, HEAD_DIM)
    vc = qkv_c[..., qd + kd:].reshape(B, C, N_KV_HEADS, HEAD_DIM)
    sink_l = sink.astype(jnp.float32).reshape(N_KV_HEADS, GROUP)[None, :, :, None, None, None]

    qb = q.reshape(B, nb, ATTN_BLOCK, N_KV_HEADS, GROUP, HEAD_DIM)

    def neighbours(t):
        tp = jnp.pad(t, ((0, 0), (ATTN_BLOCK, ATTN_BLOCK), (0, 0), (0, 0)))
        tp = tp.reshape(B, nb + 2, ATTN_BLOCK, N_KV_HEADS, HEAD_DIM)
        return jnp.concatenate([tp[:, :-2], tp[:, 1:-1], tp[:, 2:]], axis=2)

    kb, vb = neighbours(k), neighbours(v)
    qi = jnp.arange(ATTN_BLOCK)[:, None]
    si = jnp.arange(3 * ATTN_BLOCK)[None, :]
    rel = si - ATTN_BLOCK - qi
    key_pos = (jnp.arange(nb)[:, None, None] - 1) * ATTN_BLOCK + si[None]
    valid = (jnp.abs(rel) <= WINDOW)[None] & (key_pos >= 0) & (key_pos < L)

    s_loc = jnp.einsum('bnqkgd,bnskd->bkgnqs', qb, kb, preferred_element_type=jnp.float32) * scale
    s_loc = jnp.where(valid, s_loc, -jnp.inf)
    s_ctx = jnp.einsum('bnqkgd,bckd->bkgnqc', qb, kc, preferred_element_type=jnp.float32) * scale
    mx = jnp.maximum(jnp.maximum(s_loc.max(-1, keepdims=True), s_ctx.max(-1, keepdims=True)), sink_l)
    p_loc = jnp.exp(s_loc - mx)
    p_ctx = jnp.exp(s_ctx - mx)
    denom = p_loc.sum(-1, keepdims=True) + p_ctx.sum(-1, keepdims=True) + jnp.exp(sink_l - mx)
    o = (jnp.einsum('bkgnqs,bnskd->bnqkgd', (p_loc / denom).astype(vb.dtype), vb)
         + jnp.einsum('bkgnqc,bckd->bnqkgd', (p_ctx / denom).astype(vc.dtype), vc))
    y = o.reshape(B, L, D) @ w_o + b_o

    yc = None
    if ctx_queries:
        qc = qkv_c[..., :qd].reshape(B, C, N_KV_HEADS, GROUP, HEAD_DIM)
        sc = jnp.einsum('bckgd,bekd->bkgce', qc, kc, preferred_element_type=jnp.float32) * scale
        sink_c = sink_l[..., 0, :, :]
        mxc = jnp.maximum(sc.max(-1, keepdims=True), sink_c)
        pc = jnp.exp(sc - mxc)
        den_c = pc.sum(-1, keepdims=True) + jnp.exp(sink_c - mxc)
        oc = jnp.einsum('bkgce,bekd->bckgd', (pc / den_c).astype(vc.dtype), vc)
        yc = oc.reshape(B, C, D) @ w_o + b_o
    return y, yc


def pool_mixer(u, w, b, scale):
    B, L, D = u.shape
    uf = u.astype(jnp.float32)
    csum = jnp.concatenate([jnp.zeros((B, 1, D), jnp.float32), jnp.cumsum(uf, axis=1)], axis=1)
    t = jnp.arange(L)
    parts = []
    for g, size in enumerate(POOL_SIZES):
        lo = jnp.clip(t - size // 2, 0, L)
        hi = jnp.clip(t - size // 2 + size, 0, L)
        cs = csum[..., g * POOL_GROUP:(g + 1) * POOL_GROUP]
        mean = (cs[:, hi] - cs[:, lo]) / (hi - lo).astype(jnp.float32)[:, None]
        parts.append(mean - uf[..., g * POOL_GROUP:(g + 1) * POOL_GROUP])
    y = jnp.stack(parts, axis=2).astype(u.dtype)
    y = jnp.einsum('blgc,gcd->blgd', y, w).reshape(B, L, D) + b
    return y * scale


def setup_inputs(seed: int = 0) -> dict:
    key = jax.random.key(seed)
    keys = iter(jax.random.split(key, 48))
    D = D_MODEL

    def nrm(shape, std):
        return jax.random.normal(next(keys), shape, jnp.float32) * std

    NH, NA, NP = N_HYENA_LAYERS, N_ATTN_LAYERS, N_POOL_LAYERS
    return {
        'x': nrm((BATCH, SEQ, D), 1.0),
        'c': nrm((BATCH, D), 1.0),
        'ctx': nrm((BATCH, CTX_LEN, D), 1.0),
        'c_ctx': nrm((D,), 1.0),
        'w_ada': nrm((DEPTH, D, N_MOD * D), 0.5 * D ** -0.5),
        'b_ada': nrm((DEPTH, N_MOD * D), 0.02),
        'norm_pre': 1.0 + nrm((DEPTH, 3, D), 0.05),
        'norm_post': 1.0 + nrm((DEPTH, 3, D), 0.05),
        'w_ffn_in': nrm((DEPTH, 2, D, 2 * D_FF), D ** -0.5),
        'w_ffn_out': nrm((DEPTH, 2, D_FF, D), D_FF ** -0.5),
        'hy_w_in': nrm((NH, D, 3 * D), D ** -0.5),
        'hy_b_in': nrm((NH, 3 * D), 0.02),
        'hy_w_sc': nrm((NH, 3, 3 * D), 0.5),
        'hy_b_sc': nrm((NH, 3 * D), 0.02),
        'hy_f_w1': nrm((NH, FILTER_EMB, FILTER_HIDDEN), FILTER_EMB ** -0.5),
        'hy_f_b1': nrm((NH, FILTER_HIDDEN), 0.1),
        'hy_f_w2': nrm((NH, FILTER_HIDDEN, FILTER_HIDDEN), FILTER_HIDDEN ** -0.5),
        'hy_f_b2': nrm((NH, FILTER_HIDDEN), 0.1),
        'hy_f_w3': nrm((NH, FILTER_HIDDEN, FILTER_HIDDEN), FILTER_HIDDEN ** -0.5),
        'hy_f_b3': nrm((NH, FILTER_HIDDEN), 0.1),
        'hy_f_w4': nrm((NH, FILTER_HIDDEN, 2 * D), FILTER_HIDDEN ** -0.5),
        'hy_f_freq': 1.0 + nrm((NH, 3, FILTER_HIDDEN), 0.1),
        'hy_skip': nrm((NH, D), 1.0),
        'hy_w_out': nrm((NH, D, D), D ** -0.5),
        'hy_b_out': nrm((NH, D), 0.02),
        'at_w_qkv': nrm((NA, D, QKV_DIM), D ** -0.5),
        'at_b_qkv': nrm((NA, QKV_DIM), 0.02),
        'at_sink': nrm((NA, N_HEADS), 0.5),
        'at_w_o': nrm((NA, D, D), D ** -0.5),
        'at_b_o': nrm((NA, D), 0.02),
        'pl_w': nrm((NP, len(POOL_SIZES), POOL_GROUP, POOL_GROUP), POOL_GROUP ** -0.5),
        'pl_b': nrm((NP, D), 0.02),
        'pl_scale': 1.0 + nrm((NP, D), 0.1),
    }


def reference(x, c, ctx, c_ctx, w_ada, b_ada, norm_pre, norm_post, w_ffn_in, w_ffn_out,
              hy_w_in, hy_b_in, hy_w_sc, hy_b_sc, hy_f_w1, hy_f_b1, hy_f_w2, hy_f_b2,
              hy_f_w3, hy_f_b3, hy_f_w4, hy_f_freq, hy_skip, hy_w_out, hy_b_out,
              at_w_qkv, at_b_qkv, at_sink, at_w_o, at_b_o,
              pl_w, pl_b, pl_scale):
    B, L, D = x.shape
    ROWS = L // GRID_W
    pos_row = jnp.broadcast_to(jnp.arange(ROWS)[:, None], (ROWS, GRID_W)).reshape(-1).astype(jnp.float32)
    pos_col = jnp.broadcast_to(jnp.arange(GRID_W)[None, :], (ROWS, GRID_W)).reshape(-1).astype(jnp.float32)
    inv_freq = ROPE_THETA ** (-jnp.arange(ROPE_PAIRS, dtype=jnp.float32) / ROPE_PAIRS)
    ang_row = pos_row[:, None] * inv_freq[None, :]
    ang_col = pos_col[:, None] * inv_freq[None, :]

    attn_layers = [i for i in range(DEPTH) if i % N_MIXERS == 1]
    last_ctx_layer = attn_layers[-1] if attn_layers else -1

    h, hc = x, ctx
    for i in range(DEPTH):
        kind, j = i % N_MIXERS, i // N_MIXERS
        ctx_live = i <= last_ctx_layer
        ctx_out = i < last_ctx_layer
        mod = (jax.nn.silu(c) @ w_ada[i] + b_ada[i]).reshape(B, N_MOD, 1, D)
        if ctx_live:
            mod_c = (jax.nn.silu(c_ctx) @ w_ada[i] + b_ada[i]).reshape(1, N_MOD, 1, D)

        h = gated_residual(h, swiglu(modulate(h, norm_pre[i, 0], mod, 0), w_ffn_in[i, 0], w_ffn_out[i, 0]),
                           norm_post[i, 0], mod, 0, FFN_RES)
        if ctx_live:
            hc = gated_residual(hc, swiglu(modulate(hc, norm_pre[i, 0], mod_c, 0), w_ffn_in[i, 0], w_ffn_out[i, 0]),
                                norm_post[i, 0], mod_c, 0, FFN_RES)

        u = modulate(h, norm_pre[i, 1], mod, 1)
        uc = modulate(hc, norm_pre[i, 1], mod_c, 1) if ctx_live else None
        yc = None
        if kind == 0:
            hp = (hy_w_in[j], hy_b_in[j], hy_w_sc[j], hy_b_sc[j], hy_f_w1[j], hy_f_b1[j], hy_f_w2[j], hy_f_b2[j],
                  hy_f_w3[j], hy_f_b3[j], hy_f_w4[j], hy_f_freq[j], hy_skip[j], hy_w_out[j], hy_b_out[j])
            y = hyena_mixer(u, *hp)
            if ctx_out:
                yc = hyena_mixer(uc, *hp)
        elif kind == 1:
            y, yc = windowed_gqa(u, uc, at_w_qkv[j], at_b_qkv[j], at_sink[j], at_w_o[j], at_b_o[j],
                                 ang_row, ang_col, ctx_out)
        else:
            y = pool_mixer(u, pl_w[j], pl_b[j], pl_scale[j])
            if ctx_out:
                yc = pool_mixer(uc, pl_w[j], pl_b[j], pl_scale[j])
        h = gated_residual(h, y, norm_post[i, 1], mod, 1, 1.0)
        if ctx_out:
            hc = gated_residual(hc, yc, norm_post[i, 1], mod_c, 1, 1.0)

        h = gated_residual(h, swiglu(modulate(h, norm_pre[i, 2], mod, 2), w_ffn_in[i, 1], w_ffn_out[i, 1]),
                           norm_post[i, 2], mod, 2, FFN_RES)
        if ctx_out:
            hc = gated_residual(hc, swiglu(modulate(hc, norm_pre[i, 2], mod_c, 2), w_ffn_in[i, 1], w_ffn_out[i, 1]),
                                norm_post[i, 2], mod_c, 2, FFN_RES)
    return h
```

```python
import functools
import math

import jax
import jax.numpy as jnp
from jax import lax
from jax.experimental import pallas as pl
from jax.experimental.pallas import tpu as pltpu

F32 = jnp.float32
BF16 = jnp.bfloat16

GRID_W = 64
N_MOD = 9
NORM_EPS = 1e-6
FFN_RES = 0.5
FILTER_BANDS = 16
FILTER_EMB = 1 + 2 * FILTER_BANDS
DECAY_TARGET = 1e-2
SHORT_DECAY_PCT = 0.3
LONG_DECAY_PCT = 1.5
HEAD_DIM = 128
N_KV_HEADS = 4
WINDOW = 128
ATTN_BLOCK = 128
ROPE_THETA = 10000.0
POOL_SIZES = (2, 4, 8, 16)

VMEM_LIMIT_BYTES = 56 * 1024 * 1024
SUBLANES = 8
LANES = 128
DFT_INNER = 128
MASK_BIAS = -1e30


def _cparams(*sem):
    return pltpu.CompilerParams(dimension_semantics=sem, vmem_limit_bytes=VMEM_LIMIT_BYTES)


def _rms(x, g):
    ms = jnp.mean(x * x, axis=-1, keepdims=True)
    return x * lax.rsqrt(ms + NORM_EPS) * g


def _modulated(h, vec_ref):
    return _rms(h, vec_ref[3:4, :]) * (1.0 + vec_ref[1:2, :]) + vec_ref[0:1, :]


def _mods_body(cb_ref, w_ref, b_ref, o_ref):
    k_dim, bn = w_ref.shape[1], w_ref.shape[2]
    nl = bn // LANES

    def step(kg, acc):
        rows = pl.ds(pl.multiple_of(kg * SUBLANES, SUBLANES), SUBLANES)
        s = []
        for r in range(2):
            cv = cb_ref[r, rows, :]
            s.append(cv * jax.nn.sigmoid(cv))
        new = list(acc)
        for j in range(nl):
            wv = w_ref[0, rows, j * LANES:(j + 1) * LANES]
            for r in range(2):
                new[r * nl + j] = acc[r * nl + j] + wv * s[r]
        return tuple(new)

    init = tuple(jnp.zeros((SUBLANES, LANES), F32) for _ in range(2 * nl))
    acc = lax.fori_loop(0, k_dim // SUBLANES, step, init, unroll=2)
    for r in range(2):
        for j in range(nl):
            cols = slice(j * LANES, (j + 1) * LANES)
            o_ref[0, r:r + 1, cols] = jnp.sum(acc[r * nl + j], axis=0, keepdims=True) + b_ref[0, :, cols]


def _adaln_mods(c, c_ctx, w_ada, b_ada):
    depth, d, n = w_ada.shape
    bn = 1024
    cb = jnp.broadcast_to(jnp.stack([c[0], c_ctx])[:, :, None], (2, d, LANES))
    return pl.pallas_call(
        _mods_body,
        grid=(depth, n // bn),
        in_specs=[pl.BlockSpec((2, d, LANES), lambda i, j: (0, 0, 0)),
                  pl.BlockSpec((1, d, bn), lambda i, j: (i, 0, j)),
                  pl.BlockSpec((1, 1, bn), lambda i, j: (i, 0, j))],
        out_specs=pl.BlockSpec((1, 2, bn), lambda i, j: (i, 0, j)),
        out_shape=jax.ShapeDtypeStruct((depth, 2, n), F32),
        compiler_params=_cparams("arbitrary", "arbitrary"),
        name="adaln_mods",
    )(cb, w_ada, b_ada.reshape(depth, 1, n))


def _ffn_body(h_ref, vec_ref, wg_ref, wu_ref, wo_ref, o_ref, u_ref, acc_ref, *, res_w):
    j = pl.program_id(1)

    @pl.when(j == 0)
    def _():
        u_ref[...] = _modulated(h_ref[...], vec_ref).astype(BF16)
        acc_ref[...] = jnp.zeros_like(acc_ref)

    u = u_ref[...]
    g = jnp.dot(u, wg_ref[...], preferred_element_type=F32)
    p = jnp.dot(u, wu_ref[...], preferred_element_type=F32)
    a = (g * jax.nn.sigmoid(g) * p).astype(BF16)
    acc_ref[...] += jnp.dot(a, wo_ref[...], preferred_element_type=F32)

    @pl.when(j == pl.num_programs(1) - 1)
    def _():
        o_ref[...] = h_ref[...] + (res_w * vec_ref[2:3, :]) * _rms(acc_ref[...], vec_ref[4:5, :])


def _ffn(h, vecs, w_in, w_out, res_w):
    m, d = h.shape
    f = w_out.shape[0]
    bm = min(m, 512)
    bf = 512 if f % 512 == 0 else f
    nf = f // bf
    return pl.pallas_call(
        functools.partial(_ffn_body, res_w=res_w),
        grid=(m // bm, nf),
        in_specs=[pl.BlockSpec((bm, d), lambda i, j: (i, 0)),
                  pl.BlockSpec((SUBLANES, d), lambda i, j: (0, 0)),
                  pl.BlockSpec((d, bf), lambda i, j: (0, j)),
                  pl.BlockSpec((d, bf), lambda i, j: (0, j + nf)),
                  pl.BlockSpec((bf, d), lambda i, j: (j, 0))],
        out_specs=pl.BlockSpec((bm, d), lambda i, j: (i, 0)),
        out_shape=jax.ShapeDtypeStruct((m, d), F32),
        scratch_shapes=[pltpu.VMEM((bm, d), BF16), pltpu.VMEM((bm, d), F32)],
        compiler_params=_cparams("arbitrary", "arbitrary"),
        name="ffn",
    )(h, vecs, w_in, w_in, w_out)


def _proj_in_body(h_ref, vec_ref, w_ref, b_ref, *rest, rope_blocks, q_blocks, q_scale):
    if rope_blocks:
        cos_ref, sa_ref, sb_ref, o_ref, u_ref = rest
    else:
        o_ref, u_ref = rest
    j = pl.program_id(1)

    @pl.when(j == 0)
    def _():
        u_ref[...] = _modulated(h_ref[...], vec_ref).astype(BF16)

    y = jnp.dot(u_ref[...], w_ref[...], preferred_element_type=F32) + b_ref[...]
    if not rope_blocks:
        o_ref[...] = y.astype(o_ref.dtype)
        return

    @pl.when(j < rope_blocks)
    def _():
        sc = jnp.where(j < q_blocks, q_scale, 1.0).astype(F32)
        cos, sa, sb = cos_ref[...] * sc, sa_ref[...] * sc, sb_ref[...] * sc
        for hd in range(y.shape[1] // HEAD_DIM):
            cols = slice(hd * HEAD_DIM, (hd + 1) * HEAD_DIM)
            xh = y[:, cols]
            r = (xh * cos + pltpu.roll(xh, HEAD_DIM - HEAD_DIM // 4, 1) * sa
                 + pltpu.roll(xh, HEAD_DIM // 4, 1) * sb)
            o_ref[:, cols] = r.astype(o_ref.dtype)

    @pl.when(j >= rope_blocks)
    def _():
        o_ref[...] = y.astype(o_ref.dtype)


def _proj_in(h, vecs, w, b, out_dtype, rope=None, q_blocks=0, q_scale=1.0):
    m, d = h.shape
    n = w.shape[1]
    bm = min(m, 512)
    bn = 512
    in_specs = [pl.BlockSpec((bm, d), lambda i, j: (i, 0)),
                pl.BlockSpec((SUBLANES, d), lambda i, j: (0, 0)),
                pl.BlockSpec((d, bn), lambda i, j: (0, j)),
                pl.BlockSpec((1, bn), lambda i, j: (0, j))]
    args = [h, vecs, w, b.reshape(1, n)]
    rope_blocks = 0
    if rope is not None:
        rope_blocks = q_blocks + 1
        in_specs += [pl.BlockSpec((bm, HEAD_DIM), lambda i, j: (i, 0))] * 3
        args += list(rope)
    return pl.pallas_call(
        functools.partial(_proj_in_body, rope_blocks=rope_blocks, q_blocks=q_blocks, q_scale=q_scale),
        grid=(m // bm, n // bn),
        in_specs=in_specs,
        out_specs=pl.BlockSpec((bm, bn), lambda i, j: (i, j)),
        out_shape=jax.ShapeDtypeStruct((m, n), out_dtype),
        scratch_shapes=[pltpu.VMEM((bm, d), BF16)],
        compiler_params=_cparams("arbitrary", "arbitrary"),
        name="proj_in",
    )(*args)


def _proj_out_body(a_ref, w_ref, b_ref, h_ref, vec_ref, o_ref):
    y = jnp.dot(a_ref[...], w_ref[...], preferred_element_type=F32) + b_ref[...]
    o_ref[...] = h_ref[...] + vec_ref[2:3, :] * _rms(y, vec_ref[4:5, :])


def _proj_out(a, w, b, h, vecs):
    m, d = h.shape
    k = a.shape[1]
    bm = min(m, 256)
    return pl.pallas_call(
        _proj_out_body,
        grid=(m // bm,),
        in_specs=[pl.BlockSpec((bm, k), lambda i: (i, 0)),
                  pl.BlockSpec((k, d), lambda i: (0, 0)),
                  pl.BlockSpec((1, d), lambda i: (0, 0)),
                  pl.BlockSpec((bm, d), lambda i: (i, 0)),
                  pl.BlockSpec((SUBLANES, d), lambda i: (0, 0))],
        out_specs=pl.BlockSpec((bm, d), lambda i: (i, 0)),
        out_shape=jax.ShapeDtypeStruct((m, d), F32),
        compiler_params=_cparams("arbitrary"),
        name="proj_out",
    )(a, w, b.reshape(1, d), h, vecs)


def _hy_gate_body(x0c, x0p, x0n, x1c, x1p, x1n, vc, vp, vn, w_ref, b_ref, x0_out, xin_out):
    i = pl.program_id(0)
    last = pl.num_programs(0) - 1

    def conv(cur_ref, prev_ref, next_ref, part):
        cur = cur_ref[...]
        bm = cur.shape[0]
        prev_row = jnp.where(i > 0, prev_ref[SUBLANES - 1:SUBLANES, :], 0.0)
        next_row = jnp.where(i < last, next_ref[0:1, :], 0.0)
        rid = lax.broadcasted_iota(jnp.int32, cur.shape, 0)
        up = jnp.where(rid == 0, prev_row, pltpu.roll(cur, 1, 0))
        dn = jnp.where(rid == bm - 1, next_row, pltpu.roll(cur, bm - 1, 0))
        return (up * w_ref[0, part:part + 1, :] + cur * w_ref[1, part:part + 1, :]
                + dn * w_ref[2, part:part + 1, :] + b_ref[part:part + 1, :])

    x0_out[...] = conv(x0c, x0p, x0n, 0)
    xin_out[...] = conv(vc, vp, vn, 2) * conv(x1c, x1p, x1n, 1)


def _hy_gate(z, w_sc, b_sc):
    l, d3 = z.shape
    d = d3 // 3
    bm = min(l, 256)
    bc = min(d, 512)
    nc = d // bc
    rb = bm // SUBLANES
    nrow8 = l // SUBLANES

    def cur(part):
        return pl.BlockSpec((bm, bc), lambda i, j: (i, part * nc + j))

    def prev(part):
        return pl.BlockSpec((SUBLANES, bc), lambda i, j: (jnp.maximum(i * rb - 1, 0), part * nc + j))

    def nxt(part):
        return pl.BlockSpec((SUBLANES, bc), lambda i, j: (jnp.minimum((i + 1) * rb, nrow8 - 1), part * nc + j))

    in_specs = []
    for part in range(3):
        in_specs += [cur(part), prev(part), nxt(part)]
    in_specs += [pl.BlockSpec((3, 3, bc), lambda i, j: (0, 0, j)),
                 pl.BlockSpec((3, bc), lambda i, j: (0, j))]
    out_spec = pl.BlockSpec((bm, bc), lambda i, j: (i, j))
    return pl.pallas_call(
        _hy_gate_body,
        grid=(l // bm, nc),
        in_specs=in_specs,
        out_specs=[out_spec, out_spec],
        out_shape=[jax.ShapeDtypeStruct((l, d), F32)] * 2,
        compiler_params=_cparams("arbitrary", "arbitrary"),
        name="hyena_gate",
    )(*([z] * 9), w_sc.reshape(3, 3, d), b_sc.reshape(3, d))


def _filter_body(z_ref, w1, b1, w2, b2, w3, b3, fr, w4_ref, dl_ref, o_ref):
    hp = lax.Precision.HIGHEST
    z = z_ref[...]
    f = jnp.sin(fr[0:1, :] * (jnp.dot(z, w1[...], precision=hp, preferred_element_type=F32) + b1[...]))
    f = jnp.sin(fr[1:2, :] * (jnp.dot(f, w2[...], precision=hp, preferred_element_type=F32) + b2[...]))
    f = jnp.sin(fr[2:3, :] * (jnp.dot(f, w3[...], precision=hp, preferred_element_type=F32) + b3[...]))
    y = jnp.dot(f, w4_ref[...], precision=hp, preferred_element_type=F32)
    t = z[:, 0:1]
    keep = z[:, FILTER_EMB:FILTER_EMB + 1]
    o_ref[...] = y * jnp.exp(-t * dl_ref[...]) * keep


def _implicit_filter(l, w1, b1, w2, b2, w3, b3, w4, freq):
    d = w4.shape[1] // 2
    hid = w1.shape[1]
    t = jnp.linspace(0.0, 1.0, l, dtype=F32)[:, None]
    omega = 2.0 * math.pi * jnp.arange(l, dtype=F32)[:, None] / l
    bands = jnp.linspace(1e-4, FILTER_BANDS - 1, FILTER_BANDS, dtype=F32)[None, :]
    emb = jnp.concatenate([t, jnp.cos(bands * omega), -jnp.sin(bands * omega), jnp.ones((l, 1), F32)], axis=-1)
    emb = jnp.pad(emb, ((0, 0), (0, LANES - emb.shape[1])))
    back = jnp.concatenate([jnp.zeros((1, LANES), F32), emb[1:][::-1]], axis=0)
    ztab = jnp.concatenate([emb, back], axis=0)
    w1p = jnp.pad(w1, ((0, LANES - w1.shape[0]), (0, 0)))
    deltas = jnp.abs(jnp.linspace(math.log(DECAY_TARGET) / LONG_DECAY_PCT,
                                  math.log(DECAY_TARGET) / SHORT_DECAY_PCT, d, dtype=F32))[None, :]
    bm = min(l, 512)
    nb = l // bm

    def full(shape):
        return pl.BlockSpec(shape, lambda i: (0,) * len(shape))

    return pl.pallas_call(
        _filter_body,
        grid=(2 * nb,),
        in_specs=[pl.BlockSpec((bm, LANES), lambda i: (i, 0)),
                  full((LANES, hid)), full((1, hid)), full((hid, hid)), full((1, hid)),
                  full((hid, hid)), full((1, hid)), full((3, hid)),
                  pl.BlockSpec((hid, d), lambda i: (0, i // nb)),
                  full((1, d))],
        out_specs=pl.BlockSpec((bm, d), lambda i: (i, 0)),
        out_shape=jax.ShapeDtypeStruct((2 * l, d), F32),
        compiler_params=_cparams("arbitrary"),
        name="hyena_filter",
    )(ztab, w1p, b1.reshape(1, hid), w2, b2.reshape(1, hid), w3, b3.reshape(1, hid), freq, w4, deltas)


def _dft_a_body(x_ref, f_ref, o_ref):
    o_ref[...] = jnp.dot(f_ref[...], x_ref[...].astype(BF16), preferred_element_type=F32).astype(o_ref.dtype)


def _dft_a(x2, fa):
    k, cols = x2.shape
    rows = fa.shape[0]
    cb = min(cols, 4096)
    return pl.pallas_call(
        _dft_a_body,
        grid=(cols // cb,),
        in_specs=[pl.BlockSpec((k, cb), lambda j: (0, j)),
                  pl.BlockSpec((rows, k), lambda j: (0, 0))],
        out_specs=pl.BlockSpec((rows, cb), lambda j: (0, j)),
        out_shape=jax.ShapeDtypeStruct((rows, cols), BF16),
        compiler_params=_cparams("arbitrary"),
        name="dft_outer",
    )(x2, fa)


def _dft_spec_body(y_ref, g_ref, o_ref):
    for ci in range(y_ref.shape[1]):
        ys = y_ref[:, ci].reshape(g_ref.shape[1], y_ref.shape[3])
        z = jnp.dot(g_ref[ci], ys, preferred_element_type=F32)
        o_ref[:, ci] = z.reshape(o_ref.shape[0], o_ref.shape[2], o_ref.shape[3])


def _dft_mul_body(y_ref, gf_ref, h_ref, gi_ref, o_ref):
    b = y_ref.shape[2]
    for ci in range(y_ref.shape[1]):
        ys = y_ref[:, ci].reshape(2 * b, y_ref.shape[3])
        z = jnp.dot(gf_ref[ci], ys, preferred_element_type=F32)
        zr, zi = z[:b], z[b:]
        hr, hi = h_ref[0, ci], h_ref[1, ci]
        p = jnp.concatenate([zr * hr - zi * hi, zr * hi + zi * hr], axis=0).astype(BF16)
        cc = jnp.dot(gi_ref[ci], p, preferred_element_type=F32)
        o_ref[:, ci] = cc.reshape(2, b, y_ref.shape[3]).astype(o_ref.dtype)


def _dft_inner(y4, gf, h4=None, gi=None):
    _, n1, b, d = y4.shape
    cblk = min(n1, 8)
    dc = min(d, 512)
    yspec = pl.BlockSpec((2, cblk, b, dc), lambda c, j: (0, c, 0, j))
    gspec = pl.BlockSpec((cblk, 2 * b, 2 * b), lambda c, j: (c, 0, 0))
    if h4 is None:
        body, in_specs, args, odt = _dft_spec_body, [yspec, gspec], (y4, gf), F32
    else:
        body, in_specs, args, odt = _dft_mul_body, [yspec, gspec, yspec, gspec], (y4, gf, h4, gi), BF16
    return pl.pallas_call(
        body,
        grid=(n1 // cblk, d // dc),
        in_specs=in_specs,
        out_specs=yspec,
        out_shape=jax.ShapeDtypeStruct(y4.shape, odt),
        compiler_params=_cparams("arbitrary", "arbitrary"),
        name="dft_inner",
    )(*args)


def _dft_c_body(c_ref, m_ref, xin_ref, x0_ref, skip_ref, o_ref):
    conv = jnp.dot(m_ref[...], c_ref[...], preferred_element_type=F32)
    o_ref[...] = (x0_ref[...] * (conv + xin_ref[...] * skip_ref[...])).astype(o_ref.dtype)


def _dft_c(c2, mc, xin2, x02, skip_t):
    rows, cols = c2.shape
    a = mc.shape[0]
    cb = skip_t.shape[1]
    return pl.pallas_call(
        _dft_c_body,
        grid=(cols // cb,),
        in_specs=[pl.BlockSpec((rows, cb), lambda j: (0, j)),
                  pl.BlockSpec((a, rows), lambda j: (0, 0)),
                  pl.BlockSpec((a, cb), lambda j: (0, j)),
                  pl.BlockSpec((a, cb), lambda j: (0, j)),
                  pl.BlockSpec((1, cb), lambda j: (0, 0))],
        out_specs=pl.BlockSpec((a, cb), lambda j: (0, j)),
        out_shape=jax.ShapeDtypeStruct((a, cols), BF16),
        compiler_params=_cparams("arbitrary"),
        name="dft_outer_inv",
    )(c2, mc, xin2, x02, skip_t)


def _cis(num, period):
    ang = (2.0 * math.pi / period) * num.astype(F32)
    return jnp.cos(ang), jnp.sin(ang)


def _long_conv_gated(xin, x0, filt, skip):
    l, d = xin.shape
    b = DFT_INNER
    a = l // b
    n1, n = 2 * a, 2 * l
    ci = jnp.arange(n1, dtype=jnp.int32)
    ai = jnp.arange(n1, dtype=jnp.int32)
    cr, sr = _cis((ci[:, None] * ai[None, :]) % n1, n1)
    fa_full = jnp.concatenate([cr, -sr], axis=0).astype(BF16)
    fa = fa_full[:, :a]
    mc = (jnp.concatenate([cr[:a], -sr[:a]], axis=1) * (1.0 / n)).astype(BF16)
    ei = jnp.arange(b, dtype=jnp.int32)
    kk = ci[:, None, None] + n1 * ei[None, :, None]
    tr, ts = _cis((kk * ei[None, None, :]) % n, n)
    ti = -ts
    gf = jnp.concatenate([jnp.concatenate([tr, -ti], axis=2),
                          jnp.concatenate([ti, tr], axis=2)], axis=1).astype(BF16)
    trt, tit = jnp.swapaxes(tr, 1, 2), jnp.swapaxes(ti, 1, 2)
    gi = jnp.concatenate([jnp.concatenate([trt, tit], axis=2),
                          jnp.concatenate([-tit, trt], axis=2)], axis=1).astype(BF16)

    hf = _dft_a(filt.reshape(n1, b * d), fa_full).reshape(2, n1, b, d)
    h4 = _dft_inner(hf, gf)
    yx = _dft_a(xin.reshape(a, b * d), fa).reshape(2, n1, b, d)
    c4 = _dft_inner(yx, gf, h4, gi)
    cb = min(b * d, max(d, 4096))
    y2 = _dft_c(c4.reshape(2 * n1, b * d), mc, xin.reshape(a, b * d), x0.reshape(a, b * d),
                jnp.tile(skip.reshape(1, d), (1, cb // d)))
    return y2.reshape(l, d)


def _ctx_conv_body(x_ref, x0_ref, f_ref, skip_ref, ff_ref, fh_ref, mi_ref, o_ref):
    x = x_ref[...]
    n = f_ref.shape[0]
    xs = jnp.dot(ff_ref[...], x.astype(BF16), preferred_element_type=F32)
    hs = jnp.dot(fh_ref[...], f_ref[...].astype(BF16), preferred_element_type=F32)
    xr, xi, hr, hi = xs[:n], xs[n:], hs[:n], hs[n:]
    p = jnp.concatenate([xr * hr - xi * hi, xr * hi + xi * hr], axis=0).astype(BF16)
    conv = jnp.dot(mi_ref[...], p, preferred_element_type=F32)
    o_ref[...] = (x0_ref[...] * (conv + x * skip_ref[...])).astype(o_ref.dtype)


def _short_seq_conv_gated(xin, x0, filt, skip):
    l, d = xin.shape
    n = 2 * l
    ni = jnp.arange(n, dtype=jnp.int32)
    cr, sr = _cis((ni[:, None] * ni[None, :]) % n, n)
    fh = jnp.concatenate([cr, -sr], axis=0).astype(BF16)
    ff = fh[:, :l]
    mi = (jnp.concatenate([cr[:l], -sr[:l]], axis=1) * (1.0 / n)).astype(BF16)
    dc = min(d, 512)
    return pl.pallas_call(
        _ctx_conv_body,
        grid=(d // dc,),
        in_specs=[pl.BlockSpec((l, dc), lambda j: (0, j)),
                  pl.BlockSpec((l, dc), lambda j: (0, j)),
                  pl.BlockSpec((n, dc), lambda j: (0, j)),
                  pl.BlockSpec((1, dc), lambda j: (0, j)),
                  pl.BlockSpec((2 * n, l), lambda j: (0, 0)),
                  pl.BlockSpec((2 * n, n), lambda j: (0, 0)),
                  pl.BlockSpec((l, 2 * n), lambda j: (0, 0))],
        out_specs=pl.BlockSpec((l, dc), lambda j: (0, j)),
        out_shape=jax.ShapeDtypeStruct((l, d), BF16),
        compiler_params=_cparams("arbitrary"),
        name="ctx_conv",
    )(xin, x0, filt, skip.reshape(1, d), ff, fh, mi)


def _hyena_mixer(h, vecs, p, long_seq):
    w_in, b_in, w_sc, b_sc, f_w1, f_b1, f_w2, f_b2, f_w3, f_b3, f_w4, f_freq, skip, w_out, b_out = p
    l = h.shape[0]
    z = _proj_in(h, vecs, w_in, b_in, F32)
    x0, xin = _hy_gate(z, w_sc, b_sc)
    filt = _implicit_filter(l, f_w1, f_b1, f_w2, f_b2, f_w3, f_b3, f_w4, f_freq)
    conv = _long_conv_gated if long_seq else _short_seq_conv_gated
    y = conv(xin, x0, filt, skip)
    return _proj_out(y, w_out, b_out, h, vecs)


def _attn_body(sink_ref, q_ref, kp_ref, kc_ref, kn_ref, vp_ref, vc_ref, vn_ref, kx_ref, vx_ref, bias_ref, o_ref,
               *, group):
    kh = pl.program_id(1)
    blk = q_ref.shape[0]
    q = jnp.concatenate([q_ref[:, g * HEAD_DIM:(g + 1) * HEAD_DIM] for g in range(group)], axis=0)
    keys = jnp.concatenate([kp_ref[...], kc_ref[...], kn_ref[...], kx_ref[...]], axis=0)
    vals = jnp.concatenate([vp_ref[...], vc_ref[...], vn_ref[...], vx_ref[...]], axis=0)
    s = lax.dot_general(q, keys, (((1,), (1,)), ((), ())), preferred_element_type=F32) + bias_ref[0]
    rid = lax.broadcasted_iota(jnp.int32, (group * blk, 1), 0)
    sink = jnp.zeros((group * blk, 1), F32)
    for g in range(group):
        sink = jnp.where(rid // blk == g, sink_ref[kh * group + g], sink)
    mx = jnp.maximum(jnp.max(s, axis=-1, keepdims=True), sink)
    pr = jnp.exp(s - mx)
    denom = jnp.sum(pr, axis=-1, keepdims=True) + jnp.exp(sink - mx)
    o = jnp.dot(pr.astype(BF16), vals, preferred_element_type=F32) / denom
    for g in range(group):
        o_ref[:, g * HEAD_DIM:(g + 1) * HEAD_DIM] = o[g * blk:(g + 1) * blk].astype(o_ref.dtype)


def _attention(qkv, kvc, sink, d):
    l = qkv.shape[0]
    c = kvc.shape[0]
    blk = ATTN_BLOCK
    nb = l // blk
    group = d // HEAD_DIM // N_KV_HEADS
    gw = group * HEAD_DIM
    kcol = d // HEAD_DIM
    vcol = kcol + N_KV_HEADS
    qi = jnp.arange(group * blk, dtype=jnp.int32)[:, None] % blk
    si = jnp.arange(3 * blk + c, dtype=jnp.int32)[None, :]
    kb, ki = si // blk, si % blk
    prev_ok, next_ok = (kb == 0) & (ki >= qi), (kb == 2) & (ki <= qi)
    always = (kb == 1) | (kb >= 3)
    variants = [always | next_ok, always | prev_ok | next_ok, always | prev_ok]
    bias = jnp.stack([jnp.where(v, 0.0, MASK_BIAS).astype(F32) for v in variants])

    def kv_spec(col0, shift):
        return pl.BlockSpec((blk, HEAD_DIM), lambda n, kh: (jnp.clip(n + shift, 0, nb - 1), col0 + kh))

    return pl.pallas_call(
        functools.partial(_attn_body, group=group),
        grid=(nb, N_KV_HEADS),
        in_specs=[pl.BlockSpec(memory_space=pltpu.SMEM),
                  pl.BlockSpec((blk, gw), lambda n, kh: (n, kh)),
                  kv_spec(kcol, -1), kv_spec(kcol, 0), kv_spec(kcol, 1),
                  kv_spec(vcol, -1), kv_spec(vcol, 0), kv_spec(vcol, 1),
                  pl.BlockSpec((c, HEAD_DIM), lambda n, kh: (0, kh)),
                  pl.BlockSpec((c, HEAD_DIM), lambda n, kh: (0, N_KV_HEADS + kh)),
                  pl.BlockSpec((1, group * blk, 3 * blk + c),
                               lambda n, kh: (jnp.where(n == 0, 0, jnp.where(n == nb - 1, 2, 1)), 0, 0))],
        out_specs=pl.BlockSpec((blk, gw), lambda n, kh: (n, kh)),
        out_shape=jax.ShapeDtypeStruct((l, d), BF16),
        compiler_params=_cparams("arbitrary", "arbitrary"),
        name="window_attn",
    )(sink, qkv, qkv, qkv, qkv, qkv, qkv, qkv, kvc, kvc, bias)


def _rope_tables(l):
    rows = l // GRID_W
    pos_row = jnp.broadcast_to(jnp.arange(rows)[:, None], (rows, GRID_W)).reshape(-1).astype(F32)
    pos_col = jnp.broadcast_to(jnp.arange(GRID_W)[None, :], (rows, GRID_W)).reshape(-1).astype(F32)
    pairs = HEAD_DIM // 4
    inv_freq = ROPE_THETA ** (-jnp.arange(pairs, dtype=F32) / pairs)
    ang_row = pos_row[:, None] * inv_freq[None, :]
    ang_col = pos_col[:, None] * inv_freq[None, :]
    zeros = jnp.zeros_like(ang_row)
    cos = jnp.concatenate([jnp.cos(ang_row)] * 2 + [jnp.cos(ang_col)] * 2, axis=-1)
    sin_a = jnp.concatenate([-jnp.sin(ang_row), zeros, -jnp.sin(ang_col), zeros], axis=-1)
    sin_b = jnp.concatenate([zeros, jnp.sin(ang_row), zeros, jnp.sin(ang_col)], axis=-1)
    return cos, sin_a, sin_b


def _pool_body(hc_ref, hp_ref, hn_ref, vec_ref, w_ref, b_ref, sc_ref, o_ref, y_ref, *, seq_len):
    i = pl.program_id(0)
    last = pl.num_programs(0) - 1
    bm, d = hc_ref.shape
    gw = d // len(POOL_SIZES)
    h = hc_ref[...]
    u = _modulated(h, vec_ref)
    up = jnp.where(i > 0, _modulated(hp_ref[...], vec_ref), 0.0)
    un = jnp.where(i < last, _modulated(hn_ref[...], vec_ref), 0.0)
    ext_rows = bm + 2 * SUBLANES
    t = i * bm + lax.broadcasted_iota(jnp.int32, (bm, 1), 0)
    for g, size in enumerate(POOL_SIZES):
        cols = slice(g * gw, (g + 1) * gw)
        ext = jnp.concatenate([up[:, cols], u[:, cols], un[:, cols]], axis=0)
        acc, span = ext, 1
        while span < size:
            acc = acc + pltpu.roll(acc, ext_rows - span, 0)
            span *= 2
        start = SUBLANES - size // 2
        win = pltpu.roll(acc, ext_rows - start, 0)[:bm] if start else acc[:bm]
        lo = jnp.clip(t - size // 2, 0, seq_len)
        hi = jnp.clip(t - size // 2 + size, 0, seq_len)
        part = win / (hi - lo).astype(F32) - u[:, cols]
        yg = jnp.dot(part.astype(BF16), w_ref[g], preferred_element_type=F32)
        y_ref[:, cols] = (yg + b_ref[:, cols]) * sc_ref[:, cols]
    o_ref[...] = h + vec_ref[2:3, :] * _rms(y_ref[...], vec_ref[4:5, :])


def _pool_mixer(h, vecs, w, b, scale):
    l, d = h.shape
    bm = min(l, 256)
    rb = bm // SUBLANES
    nrow8 = l // SUBLANES
    ng, gw = w.shape[0], w.shape[1]
    return pl.pallas_call(
        functools.partial(_pool_body, seq_len=l),
        grid=(l // bm,),
        in_specs=[pl.BlockSpec((bm, d), lambda i: (i, 0)),
                  pl.BlockSpec((SUBLANES, d), lambda i: (jnp.maximum(i * rb - 1, 0), 0)),
                  pl.BlockSpec((SUBLANES, d), lambda i: (jnp.minimum((i + 1) * rb, nrow8 - 1), 0)),
                  pl.BlockSpec((SUBLANES, d), lambda i: (0, 0)),
                  pl.BlockSpec((ng, gw, gw), lambda i: (0, 0, 0)),
                  pl.BlockSpec((1, d), lambda i: (0, 0)),
                  pl.BlockSpec((1, d), lambda i: (0, 0))],
        out_specs=pl.BlockSpec((bm, d), lambda i: (i, 0)),
        out_shape=jax.ShapeDtypeStruct((l, d), F32),
        scratch_shapes=[pltpu.VMEM((bm, d), F32)],
        compiler_params=_cparams("arbitrary"),
        name="pool_mixer",
    )(h, h, h, vecs, w.astype(BF16), b.reshape(1, d), scale.reshape(1, d))


def _sub_vecs(mod, k, g_pre, g_post):
    rows = [mod[3 * k], mod[3 * k + 1], mod[3 * k + 2], g_pre, g_post]
    return jnp.stack(rows + [jnp.zeros_like(g_pre)] * (SUBLANES - len(rows)))


def kernel(x, c, ctx, c_ctx, w_ada, b_ada, norm_pre, norm_post, w_ffn_in, w_ffn_out, hy_w_in, hy_b_in, hy_w_sc, hy_b_sc, hy_f_w1, hy_f_b1, hy_f_w2, hy_f_b2, hy_f_w3, hy_f_b3, hy_f_w4, hy_f_freq, hy_skip, hy_w_out, hy_b_out, at_w_qkv, at_b_qkv, at_sink, at_w_o, at_b_o, pl_w, pl_b, pl_scale):
    bsz, l, d = x.shape
    assert bsz == 1, "kernel handles a single batch element"
    depth = w_ada.shape[0]
    n_mixers = 3
    attn_layers = [i for i in range(depth) if i % n_mixers == 1]
    last_ctx_layer = attn_layers[-1] if attn_layers else -1

    mods = _adaln_mods(c, c_ctx, w_ada, b_ada).reshape(depth, 2, N_MOD, d)
    rope = _rope_tables(l)
    h, hc = x[0], ctx[0]
    for i in range(depth):
        kind, j = i % n_mixers, i // n_mixers
        ctx_live = i <= last_ctx_layer
        ctx_out = i < last_ctx_layer
        vec = [_sub_vecs(mods[i, 0], k, norm_pre[i, k], norm_post[i, k]) for k in range(3)]
        vec_c = [_sub_vecs(mods[i, 1], k, norm_pre[i, k], norm_post[i, k]) for k in range(3)]
        w_in = [w_ffn_in[i, s].astype(BF16) for s in range(2)]
        w_out = [w_ffn_out[i, s].astype(BF16) for s in range(2)]

        h = _ffn(h, vec[0], w_in[0], w_out[0], FFN_RES)
        if ctx_live:
            hc = _ffn(hc, vec_c[0], w_in[0], w_out[0], FFN_RES)

        if kind == 0:
            hp = (hy_w_in[j].astype(BF16), hy_b_in[j], hy_w_sc[j], hy_b_sc[j], hy_f_w1[j], hy_f_b1[j],
                  hy_f_w2[j], hy_f_b2[j], hy_f_w3[j], hy_f_b3[j], hy_f_w4[j], hy_f_freq[j], hy_skip[j],
                  hy_w_out[j].astype(BF16), hy_b_out[j])
            h = _hyena_mixer(h, vec[1], hp, long_seq=True)
            if ctx_out:
                hc = _hyena_mixer(hc, vec_c[1], hp, long_seq=False)
        elif kind == 1:
            w_qkv = at_w_qkv[j].astype(BF16)
            qkv = _proj_in(h, vec[1], w_qkv, at_b_qkv[j], BF16, rope=rope,
                           q_blocks=d // 512, q_scale=HEAD_DIM ** -0.5)
            kvc = _proj_in(hc, vec_c[1], w_qkv[:, d:], at_b_qkv[j][d:], BF16)
            o = _attention(qkv, kvc, at_sink[j], d)
            h = _proj_out(o, at_w_o[j].astype(BF16), at_b_o[j], h, vec[1])
            assert not ctx_out, "context-query attention path is not needed for this depth"
        else:
            h = _pool_mixer(h, vec[1], pl_w[j], pl_b[j], pl_scale[j])
            assert not ctx_out, "context pooling path is not needed for this depth"

        h = _ffn(h, vec[2], w_in[1], w_out[1], FFN_RES)
        if ctx_out:
            hc = _ffn(hc, vec_c[2], w_in[1], w_out[1], FFN_RES)
    return h[None]
```

```python
import functools
import math

import jax
import jax.numpy as jnp
from jax import lax
from jax.experimental import pallas as pl
from jax.experimental.pallas import tpu as pltpu

F32 = jnp.float32
BF16 = jnp.bfloat16

GRID_W = 64
N_MOD = 9
NORM_EPS = 1e-6
FFN_RES = 0.5
FILTER_BANDS = 16
FILTER_EMB = 1 + 2 * FILTER_BANDS
DECAY_TARGET = 1e-2
SHORT_DECAY_PCT = 0.3
LONG_DECAY_PCT = 1.5
HEAD_DIM = 128
N_KV_HEADS = 4
WINDOW = 128
ATTN_BLOCK = 128
ROPE_THETA = 10000.0
POOL_SIZES = (2, 4, 8, 16)

VMEM_LIMIT_BYTES = 56 * 1024 * 1024
SUBLANES = 8
LANES = 128
DFT_INNER = 128
HALO = 16
MASK_BIAS = -1e30


def _cparams(*sem):
    return pltpu.CompilerParams(dimension_semantics=sem, vmem_limit_bytes=VMEM_LIMIT_BYTES)


def _rms(x, g):
    ms = jnp.mean(x * x, axis=-1, keepdims=True)
    return x * lax.rsqrt(ms + NORM_EPS) * g


def _modulated(h, vec_ref):
    return _rms(h, vec_ref[3:4, :]) * (1.0 + vec_ref[1:2, :]) + vec_ref[0:1, :]


def _store_modulated(h_ref, vec_ref, u_ref):
    gain = vec_ref[3:4, :] * (1.0 + vec_ref[1:2, :])
    u_ref[...] = (_rms(h_ref[...], gain) + vec_ref[0:1, :]).astype(u_ref.dtype)


def _mods_body(cb_ref, w_ref, b_ref, o_ref, s_ref):
    k_dim, bn = w_ref.shape[1], w_ref.shape[2]
    nl = bn // LANES

    @pl.when((pl.program_id(0) == 0) & (pl.program_id(1) == 0))
    def _():
        cv = cb_ref[...]
        s_ref[...] = cv * jax.nn.sigmoid(cv)

    def step(kg, acc):
        rows = pl.ds(pl.multiple_of(kg * SUBLANES, SUBLANES), SUBLANES)
        s = [s_ref[r, rows, :] for r in range(2)]
        new = list(acc)
        for j in range(nl):
            wv = w_ref[0, rows, j * LANES:(j + 1) * LANES]
            for r in range(2):
                new[r * nl + j] = acc[r * nl + j] + wv * s[r]
        return tuple(new)

    init = tuple(jnp.zeros((SUBLANES, LANES), F32) for _ in range(2 * nl))
    acc = lax.fori_loop(0, k_dim // SUBLANES, step, init, unroll=4)
    for r in range(2):
        for j in range(nl):
            cols = slice(j * LANES, (j + 1) * LANES)
            o_ref[0, r:r + 1, cols] = jnp.sum(acc[r * nl + j], axis=0, keepdims=True) + b_ref[0, :, cols]


def _adaln_mods(c, c_ctx, w_ada, b_ada):
    depth, d, n = w_ada.shape
    bn = 1024
    cb = jnp.broadcast_to(jnp.stack([c[0], c_ctx])[:, :, None], (2, d, LANES))
    return pl.pallas_call(
        _mods_body,
        grid=(depth, n // bn),
        in_specs=[pl.BlockSpec((2, d, LANES), lambda i, j: (0, 0, 0)),
                  pl.BlockSpec((1, d, bn), lambda i, j: (i, 0, j)),
                  pl.BlockSpec((1, 1, bn), lambda i, j: (i, 0, j))],
        out_specs=pl.BlockSpec((1, 2, bn), lambda i, j: (i, 0, j)),
        out_shape=jax.ShapeDtypeStruct((depth, 2, n), F32),
        scratch_shapes=[pltpu.VMEM((2, d, LANES), F32)],
        compiler_params=_cparams("arbitrary", "arbitrary"),
        name="adaln_mods",
    )(cb, w_ada, b_ada.reshape(depth, 1, n))


def _ffn_body(h_ref, vec_ref, wg_ref, wu_ref, wo_ref, o_ref, u_ref, acc_ref, *, res_w):
    j = pl.program_id(1)

    @pl.when(j == 0)
    def _():
        _store_modulated(h_ref, vec_ref, u_ref)
        acc_ref[...] = jnp.zeros_like(acc_ref)

    u = u_ref[...]
    g = jnp.dot(u, wg_ref[...], preferred_element_type=F32)
    p = jnp.dot(u, wu_ref[...], preferred_element_type=F32)
    a = (g * jax.nn.sigmoid(g) * p).astype(BF16)
    acc_ref[...] += jnp.dot(a, wo_ref[...], preferred_element_type=F32)

    @pl.when(j == pl.num_programs(1) - 1)
    def _():
        gate = (res_w * vec_ref[2:3, :]) * vec_ref[4:5, :]
        o_ref[...] = h_ref[...] + _rms(acc_ref[...], gate)


def _ffn(h, vecs, w_in, w_out, res_w):
    m, d = h.shape
    f = w_out.shape[0]
    bm = min(m, 512)
    bf = 512 if f % 512 == 0 else f
    nf = f // bf
    return pl.pallas_call(
        functools.partial(_ffn_body, res_w=res_w),
        grid=(m // bm, nf),
        in_specs=[pl.BlockSpec((bm, d), lambda i, j: (i, 0)),
                  pl.BlockSpec((SUBLANES, d), lambda i, j: (0, 0)),
                  pl.BlockSpec((d, bf), lambda i, j: (0, j)),
                  pl.BlockSpec((d, bf), lambda i, j: (0, j + nf)),
                  pl.BlockSpec((bf, d), lambda i, j: (j, 0))],
        out_specs=pl.BlockSpec((bm, d), lambda i, j: (i, 0)),
        out_shape=jax.ShapeDtypeStruct((m, d), F32),
        scratch_shapes=[pltpu.VMEM((bm, d), BF16), pltpu.VMEM((bm, d), F32)],
        compiler_params=_cparams("arbitrary", "arbitrary"),
        name="ffn",
    )(h, vecs, w_in, w_in, w_out)


def _proj_in_body(h_ref, vec_ref, w_ref, b_ref, *rest, rope_blocks, q_blocks, q_scale):
    if rope_blocks:
        cos_ref, sa_ref, sb_ref, o_ref, u_ref = rest
    else:
        o_ref, u_ref = rest
    j = pl.program_id(1)

    @pl.when(j == 0)
    def _():
        _store_modulated(h_ref, vec_ref, u_ref)

    y = jnp.dot(u_ref[...], w_ref[...], preferred_element_type=F32) + b_ref[...]
    if not rope_blocks:
        o_ref[...] = y.astype(o_ref.dtype)
        return

    @pl.when(j < rope_blocks)
    def _():
        sc = jnp.where(j < q_blocks, q_scale, 1.0).astype(F32)
        cos, sa, sb = cos_ref[...] * sc, sa_ref[...] * sc, sb_ref[...] * sc
        for hd in range(y.shape[1] // HEAD_DIM):
            cols = slice(hd * HEAD_DIM, (hd + 1) * HEAD_DIM)
            xh = y[:, cols]
            r = (xh * cos + pltpu.roll(xh, HEAD_DIM - HEAD_DIM // 4, 1) * sa
                 + pltpu.roll(xh, HEAD_DIM // 4, 1) * sb)
            o_ref[:, cols] = r.astype(o_ref.dtype)

    @pl.when(j >= rope_blocks)
    def _():
        o_ref[...] = y.astype(o_ref.dtype)


def _proj_in(h, vecs, w, b, out_dtype, rope=None, q_blocks=0, q_scale=1.0):
    m, d = h.shape
    n = w.shape[1]
    bm = min(m, 1024)
    bn = 512
    in_specs = [pl.BlockSpec((bm, d), lambda i, j: (i, 0)),
                pl.BlockSpec((SUBLANES, d), lambda i, j: (0, 0)),
                pl.BlockSpec((d, bn), lambda i, j: (0, j)),
                pl.BlockSpec((1, bn), lambda i, j: (0, j))]
    args = [h, vecs, w, b.reshape(1, n)]
    rope_blocks = 0
    if rope is not None:
        rope_blocks = q_blocks + 1
        in_specs += [pl.BlockSpec((bm, HEAD_DIM), lambda i, j: (i, 0))] * 3
        args += list(rope)
    return pl.pallas_call(
        functools.partial(_proj_in_body, rope_blocks=rope_blocks, q_blocks=q_blocks, q_scale=q_scale),
        grid=(m // bm, n // bn),
        in_specs=in_specs,
        out_specs=pl.BlockSpec((bm, bn), lambda i, j: (i, j)),
        out_shape=jax.ShapeDtypeStruct((m, n), out_dtype),
        scratch_shapes=[pltpu.VMEM((bm, d), BF16)],
        compiler_params=_cparams("arbitrary", "arbitrary"),
        name="proj_in",
    )(*args)


def _proj_out_body(a_ref, w_ref, b_ref, h_ref, vec_ref, o_ref):
    y = jnp.dot(a_ref[...], w_ref[...], preferred_element_type=F32) + b_ref[...]
    o_ref[...] = h_ref[...] + vec_ref[2:3, :] * _rms(y, vec_ref[4:5, :])


def _proj_out(a, w, b, h, vecs):
    m, d = h.shape
    k = a.shape[1]
    bm = min(m, 256)
    return pl.pallas_call(
        _proj_out_body,
        grid=(m // bm,),
        in_specs=[pl.BlockSpec((bm, k), lambda i: (i, 0)),
                  pl.BlockSpec((k, d), lambda i: (0, 0)),
                  pl.BlockSpec((1, d), lambda i: (0, 0)),
                  pl.BlockSpec((bm, d), lambda i: (i, 0)),
                  pl.BlockSpec((SUBLANES, d), lambda i: (0, 0))],
        out_specs=pl.BlockSpec((bm, d), lambda i: (i, 0)),
        out_shape=jax.ShapeDtypeStruct((m, d), F32),
        compiler_params=_cparams("arbitrary"),
        name="proj_out",
    )(a, w, b.reshape(1, d), h, vecs)


def _hy_gate_body(x0c, x0p, x0n, x1c, x1p, x1n, vc, vp, vn, w_ref, b_ref, x0_out, xin_out):
    i = pl.program_id(0)
    last = pl.num_programs(0) - 1

    def conv(cur_ref, prev_ref, next_ref, part):
        cur = cur_ref[...].astype(F32)
        bm = cur.shape[0]
        prev_row = jnp.where(i > 0, prev_ref[...].astype(F32)[HALO - 1:HALO, :], 0.0)
        next_row = jnp.where(i < last, next_ref[...].astype(F32)[0:1, :], 0.0)
        rid = lax.broadcasted_iota(jnp.int32, cur.shape, 0)
        up = jnp.where(rid == 0, prev_row, pltpu.roll(cur, 1, 0))
        dn = jnp.where(rid == bm - 1, next_row, pltpu.roll(cur, bm - 1, 0))
        return (up * w_ref[0, part:part + 1, :] + cur * w_ref[1, part:part + 1, :]
                + dn * w_ref[2, part:part + 1, :] + b_ref[part:part + 1, :])

    x0_out[...] = conv(x0c, x0p, x0n, 0).astype(x0_out.dtype)
    xin_out[...] = (conv(vc, vp, vn, 2) * conv(x1c, x1p, x1n, 1)).astype(xin_out.dtype)


def _hy_gate(z, w_sc, b_sc):
    l, d3 = z.shape
    d = d3 // 3
    bm = min(l, 256)
    bc = min(d, 512)
    nc = d // bc
    rb = bm // HALO
    nrow = l // HALO

    def cur(part):
        return pl.BlockSpec((bm, bc), lambda i, j: (i, part * nc + j))

    def prev(part):
        return pl.BlockSpec((HALO, bc), lambda i, j: (jnp.maximum(i * rb - 1, 0), part * nc + j))

    def nxt(part):
        return pl.BlockSpec((HALO, bc), lambda i, j: (jnp.minimum((i + 1) * rb, nrow - 1), part * nc + j))

    in_specs = []
    for part in range(3):
        in_specs += [cur(part), prev(part), nxt(part)]
    in_specs += [pl.BlockSpec((3, 3, bc), lambda i, j: (0, 0, j)),
                 pl.BlockSpec((3, bc), lambda i, j: (0, j))]
    out_spec = pl.BlockSpec((bm, bc), lambda i, j: (i, j))
    return pl.pallas_call(
        _hy_gate_body,
        grid=(l // bm, nc),
        in_specs=in_specs,
        out_specs=[out_spec, out_spec],
        out_shape=[jax.ShapeDtypeStruct((l, d), BF16)] * 2,
        compiler_params=_cparams("arbitrary", "arbitrary"),
        name="hyena_gate",
    )(*([z] * 9), w_sc.reshape(3, 3, d), b_sc.reshape(3, d))


def _filter_mlp_body(z_ref, w1, b1, w2, b2, w3, b3, fr, o_ref):
    hp = lax.Precision.HIGHEST
    f = jnp.sin(fr[0:1, :] * (jnp.dot(z_ref[...], w1[...], precision=hp, preferred_element_type=F32) + b1[...]))
    f = jnp.sin(fr[1:2, :] * (jnp.dot(f, w2[...], precision=hp, preferred_element_type=F32) + b2[...]))
    o_ref[...] = jnp.sin(fr[2:3, :] * (jnp.dot(f, w3[...], precision=hp, preferred_element_type=F32) + b3[...]))


def _filter_table(l, w1, b1, w2, b2, w3, b3, freq):
    hid = w1.shape[1]
    t = jnp.linspace(0.0, 1.0, l, dtype=F32)[:, None]
    omega = 2.0 * math.pi * jnp.arange(l, dtype=F32)[:, None] / l
    bands = jnp.linspace(1e-4, FILTER_BANDS - 1, FILTER_BANDS, dtype=F32)[None, :]
    emb = jnp.concatenate([t, jnp.cos(bands * omega), -jnp.sin(bands * omega)], axis=-1)
    emb = jnp.pad(emb, ((0, 0), (0, LANES - emb.shape[1])))
    w1p = jnp.pad(w1, ((0, LANES - w1.shape[0]), (0, 0)))
    bm = min(l, 512)

    def full(shape):
        return pl.BlockSpec(shape, lambda i: (0,) * len(shape))

    f = pl.pallas_call(
        _filter_mlp_body,
        grid=(l // bm,),
        in_specs=[pl.BlockSpec((bm, LANES), lambda i: (i, 0)),
                  full((LANES, hid)), full((1, hid)), full((hid, hid)), full((1, hid)),
                  full((hid, hid)), full((1, hid)), full((3, hid))],
        out_specs=pl.BlockSpec((bm, hid), lambda i: (i, 0)),
        out_shape=jax.ShapeDtypeStruct((l, hid), F32),
        compiler_params=_cparams("arbitrary"),
        name="hyena_filter_mlp",
    )(emb, w1p, b1.reshape(1, hid), w2, b2.reshape(1, hid), w3, b3.reshape(1, hid), freq)
    tab = jnp.concatenate([f, t, jnp.ones((l, 1), F32)], axis=-1)
    tab = jnp.pad(tab, ((0, 0), (0, LANES - tab.shape[1])))
    return jnp.concatenate([tab, jnp.zeros((1, LANES), F32), tab[1:][::-1]], axis=0)


def _decay_rates(d):
    return jnp.abs(jnp.linspace(math.log(DECAY_TARGET) / LONG_DECAY_PCT,
                                math.log(DECAY_TARGET) / SHORT_DECAY_PCT, d, dtype=F32))[None, :]


def _filter_rows(ft, wa_ref, wb_ref, dl_ref):
    hp = lax.Precision.HIGHEST
    hid = wa_ref.shape[0]
    half = ft.shape[0] // 2
    f, t, keep = ft[:, :hid], ft[:, hid:hid + 1], ft[:, hid + 1:hid + 2]
    y = jnp.concatenate([jnp.dot(f[:half], wa_ref[...], precision=hp, preferred_element_type=F32),
                         jnp.dot(f[half:], wb_ref[...], precision=hp, preferred_element_type=F32)], axis=0)
    return y * (jnp.exp(-t * dl_ref[...]) * keep)


def _outer_stage(fa_ref, cols, o_ref):
    ys = [jnp.dot(fa_ref[...], xj, preferred_element_type=F32).astype(BF16) for xj in cols]
    o_ref[...] = pltpu.einshape("brd->rbd", jnp.stack(ys))


def _dft_a_body(x_ref, fa_ref, o_ref):
    xb = pltpu.einshape("abd->bad", x_ref[...])
    _outer_stage(fa_ref, [xb[j] for j in range(xb.shape[0])], o_ref)


def _dft_a_filter_body(ft_ref, wa_ref, wb_ref, dl_ref, fa_ref, o_ref):
    cols = [_filter_rows(ft_ref[j], wa_ref, wb_ref, dl_ref).astype(BF16) for j in range(ft_ref.shape[0])]
    _outer_stage(fa_ref, cols, o_ref)


def _dft_a(x3, fa, d_chunk):
    a, b, d = x3.shape
    rows = fa.shape[0]
    return pl.pallas_call(
        _dft_a_body,
        grid=(b // HALO, d // d_chunk),
        in_specs=[pl.BlockSpec((a, HALO, d_chunk), lambda i, j: (0, i, j)),
                  pl.BlockSpec((rows, a), lambda i, j: (0, 0))],
        out_specs=pl.BlockSpec((rows, HALO, d_chunk), lambda i, j: (0, i, j)),
        out_shape=jax.ShapeDtypeStruct((rows, b, d), BF16),
        compiler_params=_cparams("arbitrary", "arbitrary"),
        name="dft_outer",
    )(x3, fa)


def _dft_a_filter(ft3, w4, deltas, fa, d_chunk):
    b, n1, _ = ft3.shape
    hid, d2 = w4.shape
    d = d2 // 2
    rows = fa.shape[0]
    nd = d // d_chunk
    return pl.pallas_call(
        _dft_a_filter_body,
        grid=(b // HALO, nd),
        in_specs=[pl.BlockSpec((HALO, n1, LANES), lambda i, j: (i, 0, 0)),
                  pl.BlockSpec((hid, d_chunk), lambda i, j: (0, j)),
                  pl.BlockSpec((hid, d_chunk), lambda i, j: (0, j + nd)),
                  pl.BlockSpec((1, d_chunk), lambda i, j: (0, j)),
                  pl.BlockSpec((rows, n1), lambda i, j: (0, 0))],
        out_specs=pl.BlockSpec((rows, HALO, d_chunk), lambda i, j: (0, i, j)),
        out_shape=jax.ShapeDtypeStruct((rows, b, d), BF16),
        compiler_params=_cparams("arbitrary", "arbitrary"),
        name="dft_outer_filter",
    )(ft3, w4, w4, deltas, fa)


def _dft_spec_body(y_ref, g_ref, o_ref):
    for ci in range(y_ref.shape[1]):
        ys = y_ref[:, ci].reshape(g_ref.shape[1], y_ref.shape[3])
        z = jnp.dot(g_ref[ci], ys, preferred_element_type=F32)
        o_ref[:, ci] = z.reshape(o_ref.shape[0], o_ref.shape[2], o_ref.shape[3]).astype(o_ref.dtype)


def _dft_mul_body(y_ref, gf_ref, h_ref, gi_ref, o_ref):
    b = y_ref.shape[2]
    for ci in range(y_ref.shape[1]):
        ys = y_ref[:, ci].reshape(2 * b, y_ref.shape[3])
        z = jnp.dot(gf_ref[ci], ys, preferred_element_type=F32)
        zr, zi = z[:b], z[b:]
        hr, hi = h_ref[0, ci].astype(F32), h_ref[1, ci].astype(F32)
        p = jnp.concatenate([zr * hr - zi * hi, zr * hi + zi * hr], axis=0).astype(BF16)
        cc = jnp.dot(gi_ref[ci], p, preferred_element_type=F32)
        o_ref[:, ci] = cc.reshape(2, b, y_ref.shape[3]).astype(o_ref.dtype)


def _dft_inner(y4, gf, h4=None, gi=None):
    _, nc, b, d = y4.shape
    cblk = SUBLANES
    dc = min(d, 512)
    yspec = pl.BlockSpec((2, cblk, b, dc), lambda c, j: (0, c, 0, j))
    gspec = pl.BlockSpec((cblk, 2 * b, 2 * b), lambda c, j: (c, 0, 0))
    if h4 is None:
        body, in_specs, args = _dft_spec_body, [yspec, gspec], (y4, gf)
    else:
        body, in_specs, args = _dft_mul_body, [yspec, gspec, yspec, gspec], (y4, gf, h4, gi)
    return pl.pallas_call(
        body,
        grid=(nc // cblk, d // dc),
        in_specs=in_specs,
        out_specs=yspec,
        out_shape=jax.ShapeDtypeStruct(y4.shape, BF16),
        compiler_params=_cparams("arbitrary", "arbitrary"),
        name="dft_inner",
    )(*args)


def _dft_c_body(c_ref, m_ref, xin_ref, x0_ref, skip_ref, o_ref):
    cb = pltpu.einshape("rbd->brd", c_ref[...])
    outs = [jnp.dot(m_ref[...], cb[j], preferred_element_type=F32).astype(BF16) for j in range(cb.shape[0])]
    conv = pltpu.einshape("bad->abd", jnp.stack(outs)).astype(F32)
    gated = x0_ref[...].astype(F32) * (conv + xin_ref[...].astype(F32) * skip_ref[...][None])
    o_ref[...] = gated.astype(o_ref.dtype)


def _dft_c(c3, mc, xin3, x03, skip, d_chunk):
    rows, b, d = c3.shape
    a = mc.shape[0]
    xspec = pl.BlockSpec((a, HALO, d_chunk), lambda i, j: (0, i, j))
    return pl.pallas_call(
        _dft_c_body,
        grid=(b // HALO, d // d_chunk),
        in_specs=[pl.BlockSpec((rows, HALO, d_chunk), lambda i, j: (0, i, j)),
                  pl.BlockSpec((a, rows), lambda i, j: (0, 0)),
                  xspec, xspec,
                  pl.BlockSpec((1, d_chunk), lambda i, j: (0, j))],
        out_specs=xspec,
        out_shape=jax.ShapeDtypeStruct((a, b, d), BF16),
        compiler_params=_cparams("arbitrary", "arbitrary"),
        name="dft_outer_inv",
    )(c3, mc, xin3, x03, skip)


def _cis(num, period):
    ang = (2.0 * math.pi / period) * num.astype(F32)
    return jnp.cos(ang), jnp.sin(ang)


def _long_conv_gated(xin, x0, ftab, w4, skip):
    l, d = xin.shape
    b = DFT_INNER
    a = l // b
    n1, n = 2 * a, 2 * l
    nc = n1 // 2 + SUBLANES
    dc = min(d, 512)
    ci = jnp.arange(nc, dtype=jnp.int32)
    ai = jnp.arange(n1, dtype=jnp.int32)
    live = (ci <= n1 // 2).astype(F32)[:, None]
    cr, sr = _cis((ci[:, None] * ai[None, :]) % n1, n1)
    cr, sr = cr * live, sr * live
    fa_full = jnp.concatenate([cr, -sr], axis=0).astype(BF16)
    fa = fa_full[:, :a]
    fold = jnp.where((ci == 0) | (ci == n1 // 2), 1.0, 2.0)[None, :] / n
    mc = jnp.concatenate([cr[:, :a].T * fold, -sr[:, :a].T * fold], axis=1).astype(BF16)
    ei = jnp.arange(b, dtype=jnp.int32)
    kk = ci[:, None, None] + n1 * ei[None, :, None]
    tr, ts = _cis((kk * ei[None, None, :]) % n, n)
    ti = -ts
    gf = jnp.concatenate([jnp.concatenate([tr, -ti], axis=2),
                          jnp.concatenate([ti, tr], axis=2)], axis=1).astype(BF16)
    trt, tit = jnp.swapaxes(tr, 1, 2), jnp.swapaxes(ti, 1, 2)
    gi = jnp.concatenate([jnp.concatenate([trt, tit], axis=2),
                          jnp.concatenate([-tit, trt], axis=2)], axis=1).astype(BF16)

    ft3 = jnp.swapaxes(ftab.reshape(n1, b, LANES), 0, 1)
    hf = _dft_a_filter(ft3, w4, _decay_rates(d), fa_full, dc)
    h4 = _dft_inner(hf.reshape(2, nc, b, d), gf)
    yx = _dft_a(xin.reshape(a, b, d), fa, dc)
    c4 = _dft_inner(yx.reshape(2, nc, b, d), gf, h4, gi)
    y3 = _dft_c(c4.reshape(2 * nc, b, d), mc, xin.reshape(a, b, d), x0.reshape(a, b, d), skip.reshape(1, d), dc)
    return y3.reshape(l, d)


def _ctx_conv_body(x_ref, x0_ref, ft_ref, wa_ref, wb_ref, dl_ref, skip_ref, ff_ref, fh_ref, mi_ref, o_ref):
    x = x_ref[...].astype(F32)
    n = ft_ref.shape[0]
    filt = _filter_rows(ft_ref[...], wa_ref, wb_ref, dl_ref)
    xs = jnp.dot(ff_ref[...], x.astype(BF16), preferred_element_type=F32)
    hs = jnp.dot(fh_ref[...], filt.astype(BF16), preferred_element_type=F32)
    xr, xi, hr, hi = xs[:n], xs[n:], hs[:n], hs[n:]
    p = jnp.concatenate([xr * hr - xi * hi, xr * hi + xi * hr], axis=0).astype(BF16)
    conv = jnp.dot(mi_ref[...], p, preferred_element_type=F32)
    o_ref[...] = (x0_ref[...].astype(F32) * (conv + x * skip_ref[...])).astype(o_ref.dtype)


def _short_seq_conv_gated(xin, x0, ftab, w4, skip):
    l, d = xin.shape
    n = 2 * l
    hid = w4.shape[0]
    ni = jnp.arange(n, dtype=jnp.int32)
    cr, sr = _cis((ni[:, None] * ni[None, :]) % n, n)
    fh = jnp.concatenate([cr, -sr], axis=0).astype(BF16)
    ff = fh[:, :l]
    mi = (jnp.concatenate([cr[:l], -sr[:l]], axis=1) * (1.0 / n)).astype(BF16)
    dc = min(d, 512)
    nd = d // dc
    return pl.pallas_call(
        _ctx_conv_body,
        grid=(nd,),
        in_specs=[pl.BlockSpec((l, dc), lambda j: (0, j)),
                  pl.BlockSpec((l, dc), lambda j: (0, j)),
                  pl.BlockSpec((n, LANES), lambda j: (0, 0)),
                  pl.BlockSpec((hid, dc), lambda j: (0, j)),
                  pl.BlockSpec((hid, dc), lambda j: (0, j + nd)),
                  pl.BlockSpec((1, dc), lambda j: (0, j)),
                  pl.BlockSpec((1, dc), lambda j: (0, j)),
                  pl.BlockSpec((2 * n, l), lambda j: (0, 0)),
                  pl.BlockSpec((2 * n, n), lambda j: (0, 0)),
                  pl.BlockSpec((l, 2 * n), lambda j: (0, 0))],
        out_specs=pl.BlockSpec((l, dc), lambda j: (0, j)),
        out_shape=jax.ShapeDtypeStruct((l, d), BF16),
        compiler_params=_cparams("arbitrary"),
        name="ctx_conv",
    )(xin, x0, ftab, w4, w4, _decay_rates(d), skip.reshape(1, d), ff, fh, mi)


def _hyena_mixer(h, vecs, p, long_seq):
    w_in, b_in, w_sc, b_sc, f_w1, f_b1, f_w2, f_b2, f_w3, f_b3, f_w4, f_freq, skip, w_out, b_out = p
    l = h.shape[0]
    z = _proj_in(h, vecs, w_in, b_in, BF16)
    x0, xin = _hy_gate(z, w_sc, b_sc)
    ftab = _filter_table(l, f_w1, f_b1, f_w2, f_b2, f_w3, f_b3, f_freq)
    conv = _long_conv_gated if long_seq else _short_seq_conv_gated
    y = conv(xin, x0, ftab, f_w4, skip)
    return _proj_out(y, w_out, b_out, h, vecs)


def _attn_body(sink_ref, q_ref, kp_ref, kc_ref, kn_ref, vp_ref, vc_ref, vn_ref, kx_ref, vx_ref, bias_ref, o_ref,
               *, group):
    blk = q_ref.shape[0]
    bias = bias_ref[0]
    rid = lax.broadcasted_iota(jnp.int32, (group * blk, 1), 0)
    for kh in range(N_KV_HEADS):
        hs = slice(kh * HEAD_DIM, (kh + 1) * HEAD_DIM)
        heads = [kh * group + g for g in range(group)]
        q = jnp.concatenate([q_ref[:, hd * HEAD_DIM:(hd + 1) * HEAD_DIM] for hd in heads], axis=0)
        keys = jnp.concatenate([kp_ref[:, hs], kc_ref[:, hs], kn_ref[:, hs], kx_ref[:, hs]], axis=0)
        vals = jnp.concatenate([vp_ref[:, hs], vc_ref[:, hs], vn_ref[:, hs], vx_ref[:, hs]], axis=0)
        s = lax.dot_general(q, keys, (((1,), (1,)), ((), ())), preferred_element_type=F32) + bias
        sink = jnp.zeros((group * blk, 1), F32)
        for g, hd in enumerate(heads):
            sink = jnp.where((rid >= g * blk) & (rid < (g + 1) * blk), sink_ref[hd], sink)
        mx = jnp.maximum(jnp.max(s, axis=-1, keepdims=True), sink)
        pr = jnp.exp(s - mx)
        denom = jnp.sum(pr, axis=-1, keepdims=True) + jnp.exp(sink - mx)
        o = jnp.dot(pr.astype(BF16), vals, preferred_element_type=F32) / denom
        for g, hd in enumerate(heads):
            o_ref[:, hd * HEAD_DIM:(hd + 1) * HEAD_DIM] = o[g * blk:(g + 1) * blk].astype(o_ref.dtype)


def _attention(qkv, kvc, sink, d):
    l = qkv.shape[0]
    c = kvc.shape[0]
    blk = ATTN_BLOCK
    nb = l // blk
    group = d // HEAD_DIM // N_KV_HEADS
    kvw = N_KV_HEADS * HEAD_DIM
    kcol = d // kvw
    qi = jnp.arange(group * blk, dtype=jnp.int32)[:, None] % blk
    si = jnp.arange(3 * blk + c, dtype=jnp.int32)[None, :]
    kb, ki = si // blk, si % blk
    prev_ok, next_ok = (kb == 0) & (ki >= qi), (kb == 2) & (ki <= qi)
    always = (kb == 1) | (kb >= 3)
    variants = [always | next_ok, always | prev_ok | next_ok, always | prev_ok]
    bias = jnp.stack([jnp.where(v, 0.0, MASK_BIAS).astype(F32) for v in variants])

    def kv_spec(col, shift):
        return pl.BlockSpec((blk, kvw), lambda n: (jnp.clip(n + shift, 0, nb - 1), col))

    return pl.pallas_call(
        functools.partial(_attn_body, group=group),
        grid=(nb,),
        in_specs=[pl.BlockSpec(memory_space=pltpu.SMEM),
                  pl.BlockSpec((blk, d), lambda n: (n, 0)),
                  kv_spec(kcol, -1), kv_spec(kcol, 0), kv_spec(kcol, 1),
                  kv_spec(kcol + 1, -1), kv_spec(kcol + 1, 0), kv_spec(kcol + 1, 1),
                  pl.BlockSpec((c, kvw), lambda n: (0, 0)),
                  pl.BlockSpec((c, kvw), lambda n: (0, 1)),
                  pl.BlockSpec((1, group * blk, 3 * blk + c),
                               lambda n: (jnp.where(n == 0, 0, jnp.where(n == nb - 1, 2, 1)), 0, 0))],
        out_specs=pl.BlockSpec((blk, d), lambda n: (n, 0)),
        out_shape=jax.ShapeDtypeStruct((l, d), BF16),
        compiler_params=_cparams("arbitrary"),
        name="window_attn",
    )(sink, qkv, qkv, qkv, qkv, qkv, qkv, qkv, kvc, kvc, bias)


def _rope_tables(l):
    rows = l // GRID_W
    pos_row = jnp.broadcast_to(jnp.arange(rows)[:, None], (rows, GRID_W)).reshape(-1).astype(F32)
    pos_col = jnp.broadcast_to(jnp.arange(GRID_W)[None, :], (rows, GRID_W)).reshape(-1).astype(F32)
    pairs = HEAD_DIM // 4
    inv_freq = ROPE_THETA ** (-jnp.arange(pairs, dtype=F32) / pairs)
    ang_row = pos_row[:, None] * inv_freq[None, :]
    ang_col = pos_col[:, None] * inv_freq[None, :]
    zeros = jnp.zeros_like(ang_row)
    cos = jnp.concatenate([jnp.cos(ang_row)] * 2 + [jnp.cos(ang_col)] * 2, axis=-1)
    sin_a = jnp.concatenate([-jnp.sin(ang_row), zeros, -jnp.sin(ang_col), zeros], axis=-1)
    sin_b = jnp.concatenate([zeros, jnp.sin(ang_row), zeros, jnp.sin(ang_col)], axis=-1)
    return cos, sin_a, sin_b


def _pool_body(hc_ref, hp_ref, hn_ref, vec_ref, w_ref, b_ref, sc_ref, o_ref, y_ref, *, seq_len):
    i = pl.program_id(0)
    last = pl.num_programs(0) - 1
    bm, d = hc_ref.shape
    gw = d // len(POOL_SIZES)
    h = hc_ref[...]
    u = _modulated(h, vec_ref)
    up = jnp.where(i > 0, _modulated(hp_ref[...], vec_ref), 0.0)
    un = jnp.where(i < last, _modulated(hn_ref[...], vec_ref), 0.0)
    ext_rows = bm + 2 * SUBLANES
    t = i * bm + lax.broadcasted_iota(jnp.int32, (bm, 1), 0)
    for g, size in enumerate(POOL_SIZES):
        cols = slice(g * gw, (g + 1) * gw)
        ext = jnp.concatenate([up[:, cols], u[:, cols], un[:, cols]], axis=0)
        acc, span = ext, 1
        while span < size:
            acc = acc + pltpu.roll(acc, ext_rows - span, 0)
            span *= 2
        start = SUBLANES - size // 2
        win = pltpu.roll(acc, ext_rows - start, 0)[:bm] if start else acc[:bm]
        lo = jnp.clip(t - size // 2, 0, seq_len)
        hi = jnp.clip(t - size // 2 + size, 0, seq_len)
        part = win / (hi - lo).astype(F32) - u[:, cols]
        yg = jnp.dot(part.astype(BF16), w_ref[g], preferred_element_type=F32)
        y_ref[:, cols] = (yg + b_ref[:, cols]) * sc_ref[:, cols]
    o_ref[...] = h + vec_ref[2:3, :] * _rms(y_ref[...], vec_ref[4:5, :])


def _pool_mixer(h, vecs, w, b, scale):
    l, d = h.shape
    bm = min(l, 256)
    rb = bm // SUBLANES
    nrow8 = l // SUBLANES
    ng, gw = w.shape[0], w.shape[1]
    return pl.pallas_call(
        functools.partial(_pool_body, seq_len=l),
        grid=(l // bm,),
        in_specs=[pl.BlockSpec((bm, d), lambda i: (i, 0)),
                  pl.BlockSpec((SUBLANES, d), lambda i: (jnp.maximum(i * rb - 1, 0), 0)),
                  pl.BlockSpec((SUBLANES, d), lambda i: (jnp.minimum((i + 1) * rb, nrow8 - 1), 0)),
                  pl.BlockSpec((SUBLANES, d), lambda i: (0, 0)),
                  pl.BlockSpec((ng, gw, gw), lambda i: (0, 0, 0)),
                  pl.BlockSpec((1, d), lambda i: (0, 0)),
                  pl.BlockSpec((1, d), lambda i: (0, 0))],
        out_specs=pl.BlockSpec((bm, d), lambda i: (i, 0)),
        out_shape=jax.ShapeDtypeStruct((l, d), F32),
        scratch_shapes=[pltpu.VMEM((bm, d), F32)],
        compiler_params=_cparams("arbitrary"),
        name="pool_mixer",
    )(h, h, h, vecs, w.astype(BF16), b.reshape(1, d), scale.reshape(1, d))


def _sub_vecs(mod, k, g_pre, g_post):
    rows = [mod[3 * k], mod[3 * k + 1], mod[3 * k + 2], g_pre, g_post]
    return jnp.stack(rows + [jnp.zeros_like(g_pre)] * (SUBLANES - len(rows)))


def kernel(x, c, ctx, c_ctx, w_ada, b_ada, norm_pre, norm_post, w_ffn_in, w_ffn_out, hy_w_in, hy_b_in, hy_w_sc, hy_b_sc, hy_f_w1, hy_f_b1, hy_f_w2, hy_f_b2, hy_f_w3, hy_f_b3, hy_f_w4, hy_f_freq, hy_skip, hy_w_out, hy_b_out, at_w_qkv, at_b_qkv, at_sink, at_w_o, at_b_o, pl_w, pl_b, pl_scale):
    bsz, l, d = x.shape
    assert bsz == 1, "kernel handles a single batch element"
    assert d % 1024 == 0 and l % 1024 == 0 and ctx.shape[1] % 256 == 0, "block sizes assume these multiples"
    assert w_ffn_out.shape[2] % 512 == 0, "FFN width must be a multiple of the 512-column weight blocks"
    depth = w_ada.shape[0]
    n_mixers = 3
    attn_layers = [i for i in range(depth) if i % n_mixers == 1]
    last_ctx_layer = attn_layers[-1] if attn_layers else -1

    mods = _adaln_mods(c, c_ctx, w_ada, b_ada).reshape(depth, 2, N_MOD, d)
    rope = _rope_tables(l)
    h, hc = x[0], ctx[0]
    for i in range(depth):
        kind, j = i % n_mixers, i // n_mixers
        ctx_live = i <= last_ctx_layer
        ctx_out = i < last_ctx_layer
        vec = [_sub_vecs(mods[i, 0], k, norm_pre[i, k], norm_post[i, k]) for k in range(3)]
        vec_c = [_sub_vecs(mods[i, 1], k, norm_pre[i, k], norm_post[i, k]) for k in range(3)]
        w_in = [w_ffn_in[i, s].astype(BF16) for s in range(2)]
        w_out = [w_ffn_out[i, s].astype(BF16) for s in range(2)]

        h = _ffn(h, vec[0], w_in[0], w_out[0], FFN_RES)
        if ctx_live:
            hc = _ffn(hc, vec_c[0], w_in[0], w_out[0], FFN_RES)

        if kind == 0:
            hp = (hy_w_in[j].astype(BF16), hy_b_in[j], hy_w_sc[j], hy_b_sc[j], hy_f_w1[j], hy_f_b1[j],
                  hy_f_w2[j], hy_f_b2[j], hy_f_w3[j], hy_f_b3[j], hy_f_w4[j], hy_f_freq[j], hy_skip[j],
                  hy_w_out[j].astype(BF16), hy_b_out[j])
            h = _hyena_mixer(h, vec[1], hp, long_seq=True)
            if ctx_out:
                hc = _hyena_mixer(hc, vec_c[1], hp, long_seq=False)
        elif kind == 1:
            w_qkv = at_w_qkv[j].astype(BF16)
            qkv = _proj_in(h, vec[1], w_qkv, at_b_qkv[j], BF16, rope=rope,
                           q_blocks=d // 512, q_scale=HEAD_DIM ** -0.5)
            kvc = _proj_in(hc, vec_c[1], w_qkv[:, d:], at_b_qkv[j][d:], BF16)
            o = _attention(qkv, kvc, at_sink[j], d)
            h = _proj_out(o, at_w_o[j].astype(BF16), at_b_o[j], h, vec[1])
            assert not ctx_out, "context-query attention path is not needed for this depth"
        else:
            h = _pool_mixer(h, vec[1], pl_w[j], pl_b[j], pl_scale[j])
            assert not ctx_out, "context pooling path is not needed for this depth"

        h = _ffn(h, vec[2], w_in[1], w_out[1], FFN_RES)
        if ctx_out:
            hc = _ffn(hc, vec_c[2], w_in[1], w_out[1], FFN_RES)
    return h[None]
```

```python
import functools
import math

import jax
import jax.numpy as jnp
from jax import lax
from jax.experimental import pallas as pl
from jax.experimental.pallas import tpu as pltpu

F32 = jnp.float32
BF16 = jnp.bfloat16

GRID_W = 64
N_MOD = 9
NORM_EPS = 1e-6
FFN_RES = 0.5
FILTER_BANDS = 16
FILTER_EMB = 1 + 2 * FILTER_BANDS
DECAY_TARGET = 1e-2
SHORT_DECAY_PCT = 0.3
LONG_DECAY_PCT = 1.5
HEAD_DIM = 128
N_KV_HEADS = 4
WINDOW = 128
ATTN_BLOCK = 128
ROPE_THETA = 10000.0
POOL_SIZES = (2, 4, 8, 16)

VMEM_LIMIT_BYTES = 56 * 1024 * 1024
SUBLANES = 8
LANES = 128
DFT_INNER = 128
HALO = 16
MASK_BIAS = -1e30


def _cparams(*sem):
    return pltpu.CompilerParams(dimension_semantics=sem, vmem_limit_bytes=VMEM_LIMIT_BYTES)


def _rms(x, g):
    ms = jnp.mean(x * x, axis=-1, keepdims=True)
    return x * lax.rsqrt(ms + NORM_EPS) * g


def _modulated(h, vec_ref):
    return _rms(h, vec_ref[3:4, :]) * (1.0 + vec_ref[1:2, :]) + vec_ref[0:1, :]


def _store_modulated(h_ref, vec_ref, u_ref):
    gain = vec_ref[3:4, :] * (1.0 + vec_ref[1:2, :])
    u_ref[...] = (_rms(h_ref[...], gain) + vec_ref[0:1, :]).astype(u_ref.dtype)


def _mods_body(cb_ref, w_ref, b_ref, o_ref, s_ref):
    k_dim, bn = w_ref.shape[1], w_ref.shape[2]
    nl = bn // LANES

    @pl.when((pl.program_id(0) == 0) & (pl.program_id(1) == 0))
    def _():
        cv = cb_ref[...]
        s_ref[...] = cv * jax.nn.sigmoid(cv)

    def step(kg, acc):
        rows = pl.ds(pl.multiple_of(kg * SUBLANES, SUBLANES), SUBLANES)
        s = [s_ref[r, rows, :] for r in range(2)]
        new = list(acc)
        for j in range(nl):
            wv = w_ref[0, rows, j * LANES:(j + 1) * LANES]
            for r in range(2):
                new[r * nl + j] = acc[r * nl + j] + wv * s[r]
        return tuple(new)

    init = tuple(jnp.zeros((SUBLANES, LANES), F32) for _ in range(2 * nl))
    acc = lax.fori_loop(0, k_dim // SUBLANES, step, init, unroll=4)
    for r in range(2):
        for j in range(nl):
            cols = slice(j * LANES, (j + 1) * LANES)
            o_ref[0, r:r + 1, cols] = jnp.sum(acc[r * nl + j], axis=0, keepdims=True) + b_ref[0, :, cols]


def _adaln_mods(c, c_ctx, w_ada, b_ada):
    depth, d, n = w_ada.shape
    bn = 1024
    cb = jnp.broadcast_to(jnp.stack([c[0], c_ctx])[:, :, None], (2, d, LANES))
    return pl.pallas_call(
        _mods_body,
        grid=(depth, n // bn),
        in_specs=[pl.BlockSpec((2, d, LANES), lambda i, j: (0, 0, 0)),
                  pl.BlockSpec((1, d, bn), lambda i, j: (i, 0, j)),
                  pl.BlockSpec((1, 1, bn), lambda i, j: (i, 0, j))],
        out_specs=pl.BlockSpec((1, 2, bn), lambda i, j: (i, 0, j)),
        out_shape=jax.ShapeDtypeStruct((depth, 2, n), F32),
        scratch_shapes=[pltpu.VMEM((2, d, LANES), F32)],
        compiler_params=_cparams("arbitrary", "arbitrary"),
        name="adaln_mods",
    )(cb, w_ada, b_ada.reshape(depth, 1, n))


def _ffn_body(h_ref, vec_ref, wg_ref, wu_ref, wo_ref, o_ref, u_ref, acc_ref, *, res_w):
    j = pl.program_id(1)

    @pl.when((pl.program_id(0) == 0) & (j == 0))
    def _():
        acc_ref[...] = jnp.zeros_like(acc_ref)

    @pl.when(j == 0)
    def _():
        _store_modulated(h_ref, vec_ref, u_ref)

    u = u_ref[...]
    g = jnp.dot(u, wg_ref[...], preferred_element_type=F32)
    p = jnp.dot(u, wu_ref[...], preferred_element_type=F32)
    a = (g * jax.nn.sigmoid(g) * p).astype(BF16)
    acc_ref[...] += jnp.dot(a, wo_ref[...], preferred_element_type=F32)

    @pl.when(j == pl.num_programs(1) - 1)
    def _():
        gate = (res_w * vec_ref[2:3, :]) * vec_ref[4:5, :]
        o_ref[...] = h_ref[...] + _rms(acc_ref[...], gate)
        acc_ref[...] = jnp.zeros_like(acc_ref)


def _ffn(h, vecs, w_in_all, w_out_all, layer, slot, res_w):
    m, d = h.shape
    f = w_out_all.shape[2]
    bm = min(m, 512)
    bf = 512
    nf = f // bf
    return pl.pallas_call(
        functools.partial(_ffn_body, res_w=res_w),
        grid=(m // bm, nf),
        in_specs=[pl.BlockSpec((bm, d), lambda i, j: (i, 0)),
                  pl.BlockSpec((SUBLANES, d), lambda i, j: (0, 0)),
                  pl.BlockSpec((None, None, d, bf), lambda i, j: (layer, slot, 0, j)),
                  pl.BlockSpec((None, None, d, bf), lambda i, j: (layer, slot, 0, j + nf)),
                  pl.BlockSpec((None, None, bf, d), lambda i, j: (layer, slot, j, 0))],
        out_specs=pl.BlockSpec((bm, d), lambda i, j: (i, 0)),
        out_shape=jax.ShapeDtypeStruct((m, d), F32),
        scratch_shapes=[pltpu.VMEM((bm, d), BF16), pltpu.VMEM((bm, d), F32)],
        compiler_params=_cparams("arbitrary", "arbitrary"),
        name="ffn",
    )(h, vecs, w_in_all, w_in_all, w_out_all)


def _proj_in_body(h_ref, vec_ref, w_ref, b_ref, *rest, rope_blocks, q_blocks, q_scale):
    if rope_blocks:
        cos_ref, sa_ref, sb_ref, o_ref, u_ref = rest
    else:
        o_ref, u_ref = rest
    j = pl.program_id(1)

    @pl.when(j == 0)
    def _():
        _store_modulated(h_ref, vec_ref, u_ref)

    y = jnp.dot(u_ref[...], w_ref[...], preferred_element_type=F32) + b_ref[...]
    if not rope_blocks:
        o_ref[...] = y.astype(o_ref.dtype)
        return

    rot = jnp.where(j < q_blocks, q_scale, jnp.where(j < rope_blocks, 1.0, 0.0)).astype(F32)
    keep = jnp.where(j < rope_blocks, 0.0, 1.0).astype(F32)
    cos, sa, sb = cos_ref[...] * rot + keep, sa_ref[...] * rot, sb_ref[...] * rot
    for hd in range(y.shape[1] // HEAD_DIM):
        cols = slice(hd * HEAD_DIM, (hd + 1) * HEAD_DIM)
        xh = y[:, cols]
        r = (xh * cos + pltpu.roll(xh, HEAD_DIM - HEAD_DIM // 4, 1) * sa
             + pltpu.roll(xh, HEAD_DIM // 4, 1) * sb)
        o_ref[:, cols] = r.astype(o_ref.dtype)


def _proj_in(h, vecs, w, b, out_dtype, rope=None, q_blocks=0, q_scale=1.0):
    m, d = h.shape
    n = w.shape[1]
    bm = min(m, 1024)
    bn = 512
    in_specs = [pl.BlockSpec((bm, d), lambda i, j: (i, 0)),
                pl.BlockSpec((SUBLANES, d), lambda i, j: (0, 0)),
                pl.BlockSpec((d, bn), lambda i, j: (0, j)),
                pl.BlockSpec((1, bn), lambda i, j: (0, j))]
    args = [h, vecs, w, b.reshape(1, n)]
    rope_blocks = 0
    if rope is not None:
        rope_blocks = q_blocks + 1
        in_specs += [pl.BlockSpec((bm, HEAD_DIM), lambda i, j: (i, 0))] * 3
        args += list(rope)
    return pl.pallas_call(
        functools.partial(_proj_in_body, rope_blocks=rope_blocks, q_blocks=q_blocks, q_scale=q_scale),
        grid=(m // bm, n // bn),
        in_specs=in_specs,
        out_specs=pl.BlockSpec((bm, bn), lambda i, j: (i, j)),
        out_shape=jax.ShapeDtypeStruct((m, n), out_dtype),
        scratch_shapes=[pltpu.VMEM((bm, d), BF16)],
        compiler_params=_cparams("arbitrary", "arbitrary"),
        name="proj_in",
    )(*args)


def _proj_out_body(a_ref, w_ref, b_ref, h_ref, vec_ref, o_ref):
    y = jnp.dot(a_ref[...], w_ref[...], preferred_element_type=F32) + b_ref[...]
    o_ref[...] = h_ref[...] + vec_ref[2:3, :] * _rms(y, vec_ref[4:5, :])


def _proj_out(a, w, b, h, vecs):
    m, d = h.shape
    k = a.shape[1]
    bm = min(m, 256)
    return pl.pallas_call(
        _proj_out_body,
        grid=(m // bm,),
        in_specs=[pl.BlockSpec((bm, k), lambda i: (i, 0)),
                  pl.BlockSpec((k, d), lambda i: (0, 0)),
                  pl.BlockSpec((1, d), lambda i: (0, 0)),
                  pl.BlockSpec((bm, d), lambda i: (i, 0)),
                  pl.BlockSpec((SUBLANES, d), lambda i: (0, 0))],
        out_specs=pl.BlockSpec((bm, d), lambda i: (i, 0)),
        out_shape=jax.ShapeDtypeStruct((m, d), F32),
        compiler_params=_cparams("arbitrary"),
        name="proj_out",
    )(a, w, b.reshape(1, d), h, vecs)


def _hy_in_body(h_ref, hp_ref, hn_ref, vec_ref, w0_ref, w1_ref, w2_ref, b0_ref, b1_ref, b2_ref, wsc_ref, bsc_ref,
                x0_out, xin_out, u_ref, uh_ref):
    i = pl.program_id(0)
    last = pl.num_programs(0) - 1

    @pl.when(pl.program_id(1) == 0)
    def _():
        _store_modulated(h_ref, vec_ref, u_ref)
        uh_ref[0:SUBLANES, :] = hp_ref[...]
        uh_ref[SUBLANES:, :] = hn_ref[...]
        _store_modulated(uh_ref, vec_ref, uh_ref)

    def conv(w_ref, b_ref, part):
        z = jnp.dot(u_ref[...], w_ref[...], preferred_element_type=F32) + b_ref[...]
        zh = jnp.dot(uh_ref[...].astype(BF16), w_ref[...], preferred_element_type=F32) + b_ref[...]
        bm = z.shape[0]
        prev_row = jnp.where(i > 0, zh[SUBLANES - 1:SUBLANES, :], 0.0)
        next_row = jnp.where(i < last, zh[SUBLANES:SUBLANES + 1, :], 0.0)
        rid = lax.broadcasted_iota(jnp.int32, z.shape, 0)
        up = jnp.where(rid == 0, prev_row, pltpu.roll(z, 1, 0))
        dn = jnp.where(rid == bm - 1, next_row, pltpu.roll(z, bm - 1, 0))
        return (up * wsc_ref[0, part:part + 1, :] + z * wsc_ref[1, part:part + 1, :]
                + dn * wsc_ref[2, part:part + 1, :] + bsc_ref[part:part + 1, :])

    x0_out[...] = conv(w0_ref, b0_ref, 0).astype(x0_out.dtype)
    xin_out[...] = (conv(w2_ref, b2_ref, 2) * conv(w1_ref, b1_ref, 1)).astype(xin_out.dtype)


def _hy_in_gate(h, vecs, w_in, b_in, w_sc, b_sc):
    l, d = h.shape
    bm = min(l, 1024)
    bc = min(d, 512)
    nc = d // bc
    rb = bm // SUBLANES
    nrow8 = l // SUBLANES
    wspecs = [pl.BlockSpec((d, bc), functools.partial(lambda i, j, p: (0, p * nc + j), p=part)) for part in range(3)]
    bspecs = [pl.BlockSpec((1, bc), functools.partial(lambda i, j, p: (0, p * nc + j), p=part)) for part in range(3)]
    out_spec = pl.BlockSpec((bm, bc), lambda i, j: (i, j))
    return pl.pallas_call(
        _hy_in_body,
        grid=(l // bm, nc),
        in_specs=[pl.BlockSpec((bm, d), lambda i, j: (i, 0)),
                  pl.BlockSpec((SUBLANES, d), lambda i, j: (jnp.maximum(i * rb - 1, 0), 0)),
                  pl.BlockSpec((SUBLANES, d), lambda i, j: (jnp.minimum((i + 1) * rb, nrow8 - 1), 0)),
                  pl.BlockSpec((SUBLANES, d), lambda i, j: (0, 0))] + wspecs + bspecs
                 + [pl.BlockSpec((3, 3, bc), lambda i, j: (0, 0, j)),
                    pl.BlockSpec((3, bc), lambda i, j: (0, j))],
        out_specs=[out_spec, out_spec],
        out_shape=[jax.ShapeDtypeStruct((l, d), BF16)] * 2,
        scratch_shapes=[pltpu.VMEM((bm, d), BF16), pltpu.VMEM((2 * SUBLANES, d), F32)],
        compiler_params=_cparams("arbitrary", "arbitrary"),
        name="hyena_in_gate",
    )(h, h, h, vecs, w_in, w_in, w_in, *([b_in.reshape(1, 3 * d)] * 3), w_sc.reshape(3, 3, d), b_sc.reshape(3, d))


def _filter_mlp_body(z_ref, w1, b1, w2, b2, w3, b3, fr, o_ref):
    hp = lax.Precision.HIGHEST
    f = jnp.sin(fr[0:1, :] * (jnp.dot(z_ref[...], w1[...], precision=hp, preferred_element_type=F32) + b1[...]))
    f = jnp.sin(fr[1:2, :] * (jnp.dot(f, w2[...], precision=hp, preferred_element_type=F32) + b2[...]))
    o_ref[...] = jnp.sin(fr[2:3, :] * (jnp.dot(f, w3[...], precision=hp, preferred_element_type=F32) + b3[...]))


def _filter_table(l, w1, b1, w2, b2, w3, b3, freq):
    hid = w1.shape[1]
    t = jnp.linspace(0.0, 1.0, l, dtype=F32)[:, None]
    omega = 2.0 * math.pi * jnp.arange(l, dtype=F32)[:, None] / l
    bands = jnp.linspace(1e-4, FILTER_BANDS - 1, FILTER_BANDS, dtype=F32)[None, :]
    emb = jnp.concatenate([t, jnp.cos(bands * omega), -jnp.sin(bands * omega)], axis=-1)
    emb = jnp.pad(emb, ((0, 0), (0, LANES - emb.shape[1])))
    w1p = jnp.pad(w1, ((0, LANES - w1.shape[0]), (0, 0)))
    bm = min(l, 512)

    def full(shape):
        return pl.BlockSpec(shape, lambda i: (0,) * len(shape))

    f = pl.pallas_call(
        _filter_mlp_body,
        grid=(l // bm,),
        in_specs=[pl.BlockSpec((bm, LANES), lambda i: (i, 0)),
                  full((LANES, hid)), full((1, hid)), full((hid, hid)), full((1, hid)),
                  full((hid, hid)), full((1, hid)), full((3, hid))],
        out_specs=pl.BlockSpec((bm, hid), lambda i: (i, 0)),
        out_shape=jax.ShapeDtypeStruct((l, hid), F32),
        compiler_params=_cparams("arbitrary"),
        name="hyena_filter_mlp",
    )(emb, w1p, b1.reshape(1, hid), w2, b2.reshape(1, hid), w3, b3.reshape(1, hid), freq)
    tab = jnp.concatenate([f, t, jnp.ones((l, 1), F32)], axis=-1)
    tab = jnp.pad(tab, ((0, 0), (0, LANES - tab.shape[1])))
    return jnp.concatenate([tab, jnp.zeros((1, LANES), F32), tab[1:][::-1]], axis=0)


def _decay_rates(d):
    return jnp.abs(jnp.linspace(math.log(DECAY_TARGET) / LONG_DECAY_PCT,
                                math.log(DECAY_TARGET) / SHORT_DECAY_PCT, d, dtype=F32))[None, :]


def _split_bf16(x):
    hi = x.astype(BF16)
    return hi, (x - hi.astype(F32)).astype(BF16)


def _stack_3pass(w):
    w_hi, w_lo = _split_bf16(w)
    return jnp.concatenate([w_hi, w_hi, w_lo], axis=0)


def _dot_3pass(f, w3):
    f_hi, f_lo = _split_bf16(f)
    return jnp.dot(jnp.concatenate([f_hi, f_lo, f_hi], axis=1), w3, preferred_element_type=F32)


def _filter_rows(ft, wa3, wb3, dl_ref):
    hid = wa3.shape[0] // 3
    half = ft.shape[0] // 2
    f, t, keep = ft[:, :hid], ft[:, hid:hid + 1], ft[:, hid + 1:hid + 2]
    y = jnp.concatenate([_dot_3pass(f[:half], wa3), _dot_3pass(f[half:], wb3)], axis=0)
    return y * (jnp.exp(-t * dl_ref[...]) * keep)


def _outer_stage(fa_ref, cols, o_ref):
    ys = [jnp.dot(fa_ref[...], xj, preferred_element_type=F32).astype(BF16) for xj in cols]
    o_ref[...] = pltpu.einshape("brd->rbd", jnp.stack(ys))


def _dft_a_body(x_ref, fa_ref, o_ref):
    xb = pltpu.einshape("abd->bad", x_ref[...])
    _outer_stage(fa_ref, [xb[j] for j in range(xb.shape[0])], o_ref)


def _dft_a_filter_body(ft_ref, wa_ref, wb_ref, dl_ref, fa_ref, o_ref):
    wa3, wb3 = _stack_3pass(wa_ref[...]), _stack_3pass(wb_ref[...])
    cols = [_filter_rows(ft_ref[j], wa3, wb3, dl_ref).astype(BF16) for j in range(ft_ref.shape[0])]
    _outer_stage(fa_ref, cols, o_ref)


def _dft_a(x3, fa, d_chunk):
    a, b, d = x3.shape
    rows = fa.shape[0]
    return pl.pallas_call(
        _dft_a_body,
        grid=(b // HALO, d // d_chunk),
        in_specs=[pl.BlockSpec((a, HALO, d_chunk), lambda i, j: (0, i, j)),
                  pl.BlockSpec((rows, a), lambda i, j: (0, 0))],
        out_specs=pl.BlockSpec((rows, HALO, d_chunk), lambda i, j: (0, i, j)),
        out_shape=jax.ShapeDtypeStruct((rows, b, d), BF16),
        compiler_params=_cparams("arbitrary", "arbitrary"),
        name="dft_outer",
    )(x3, fa)


def _dft_a_filter(ft3, w4, deltas, fa, d_chunk):
    b, n1, _ = ft3.shape
    hid, d2 = w4.shape
    d = d2 // 2
    rows = fa.shape[0]
    nd = d // d_chunk
    return pl.pallas_call(
        _dft_a_filter_body,
        grid=(b // HALO, nd),
        in_specs=[pl.BlockSpec((HALO, n1, LANES), lambda i, j: (i, 0, 0)),
                  pl.BlockSpec((hid, d_chunk), lambda i, j: (0, j)),
                  pl.BlockSpec((hid, d_chunk), lambda i, j: (0, j + nd)),
                  pl.BlockSpec((1, d_chunk), lambda i, j: (0, j)),
                  pl.BlockSpec((rows, n1), lambda i, j: (0, 0))],
        out_specs=pl.BlockSpec((rows, HALO, d_chunk), lambda i, j: (0, i, j)),
        out_shape=jax.ShapeDtypeStruct((rows, b, d), BF16),
        compiler_params=_cparams("arbitrary", "arbitrary"),
        name="dft_outer_filter",
    )(ft3, w4, w4, deltas, fa)


def _dft_spec_body(y_ref, g_ref, o_ref):
    for ci in range(y_ref.shape[1]):
        ys = y_ref[:, ci].reshape(g_ref.shape[1], y_ref.shape[3])
        z = jnp.dot(g_ref[ci], ys, preferred_element_type=F32)
        o_ref[:, ci] = z.reshape(o_ref.shape[0], o_ref.shape[2], o_ref.shape[3]).astype(o_ref.dtype)


def _dft_mul_body(y_ref, gf_ref, h_ref, gi_ref, o_ref):
    b = y_ref.shape[2]
    for ci in range(y_ref.shape[1]):
        ys = y_ref[:, ci].reshape(2 * b, y_ref.shape[3])
        z = jnp.dot(gf_ref[ci], ys, preferred_element_type=F32)
        zr, zi = z[:b], z[b:]
        hr, hi = h_ref[0, ci].astype(F32), h_ref[1, ci].astype(F32)
        p = jnp.concatenate([zr * hr - zi * hi, zr * hi + zi * hr], axis=0).astype(BF16)
        cc = jnp.dot(gi_ref[ci], p, preferred_element_type=F32)
        o_ref[:, ci] = cc.reshape(2, b, y_ref.shape[3]).astype(o_ref.dtype)


def _dft_inner(y4, gf, h4=None, gi=None):
    _, nc, b, d = y4.shape
    cblk = SUBLANES
    dc = min(d, 512)
    yspec = pl.BlockSpec((2, cblk, b, dc), lambda c, j: (0, c, 0, j))
    gspec = pl.BlockSpec((cblk, 2 * b, 2 * b), lambda c, j: (c, 0, 0))
    if h4 is None:
        body, in_specs, args = _dft_spec_body, [yspec, gspec], (y4, gf)
    else:
        body, in_specs, args = _dft_mul_body, [yspec, gspec, yspec, gspec], (y4, gf, h4, gi)
    return pl.pallas_call(
        body,
        grid=(nc // cblk, d // dc),
        in_specs=in_specs,
        out_specs=yspec,
        out_shape=jax.ShapeDtypeStruct(y4.shape, BF16),
        compiler_params=_cparams("arbitrary", "arbitrary"),
        name="dft_inner",
    )(*args)


def _dft_c_body(c_ref, m_ref, xin_ref, x0_ref, skip_ref, o_ref):
    cb = pltpu.einshape("rbd->brd", c_ref[...])
    outs = [jnp.dot(m_ref[...], cb[j], preferred_element_type=F32).astype(BF16) for j in range(cb.shape[0])]
    conv = pltpu.einshape("bad->abd", jnp.stack(outs)).astype(F32)
    gated = x0_ref[...].astype(F32) * (conv + xin_ref[...].astype(F32) * skip_ref[...][None])
    o_ref[...] = gated.astype(o_ref.dtype)


def _dft_c(c3, mc, xin3, x03, skip, d_chunk):
    rows, b, d = c3.shape
    a = mc.shape[0]
    xspec = pl.BlockSpec((a, HALO, d_chunk), lambda i, j: (0, i, j))
    return pl.pallas_call(
        _dft_c_body,
        grid=(b // HALO, d // d_chunk),
        in_specs=[pl.BlockSpec((rows, HALO, d_chunk), lambda i, j: (0, i, j)),
                  pl.BlockSpec((a, rows), lambda i, j: (0, 0)),
                  xspec, xspec,
                  pl.BlockSpec((1, d_chunk), lambda i, j: (0, j))],
        out_specs=xspec,
        out_shape=jax.ShapeDtypeStruct((a, b, d), BF16),
        compiler_params=_cparams("arbitrary", "arbitrary"),
        name="dft_outer_inv",
    )(c3, mc, xin3, x03, skip)


def _cis(num, period):
    ang = (2.0 * math.pi / period) * num.astype(F32)
    return jnp.cos(ang), jnp.sin(ang)


def _long_conv_gated(xin, x0, ftab, w4, skip):
    l, d = xin.shape
    b = DFT_INNER
    a = l // b
    n1, n = 2 * a, 2 * l
    nc = n1 // 2 + SUBLANES
    dc = min(d, 512)
    ci = jnp.arange(nc, dtype=jnp.int32)
    ai = jnp.arange(n1, dtype=jnp.int32)
    live = (ci <= n1 // 2).astype(F32)[:, None]
    cr, sr = _cis((ci[:, None] * ai[None, :]) % n1, n1)
    cr, sr = cr * live, sr * live
    fa_full = jnp.concatenate([cr, -sr], axis=0).astype(BF16)
    fa = fa_full[:, :a]
    fold = jnp.where((ci == 0) | (ci == n1 // 2), 1.0, 2.0)[None, :] / n
    mc = jnp.concatenate([cr[:, :a].T * fold, -sr[:, :a].T * fold], axis=1).astype(BF16)
    ei = jnp.arange(b, dtype=jnp.int32)
    kk = ci[:, None, None] + n1 * ei[None, :, None]
    tr, ts = _cis((kk * ei[None, None, :]) % n, n)
    ti = -ts
    gf = jnp.concatenate([jnp.concatenate([tr, -ti], axis=2),
                          jnp.concatenate([ti, tr], axis=2)], axis=1).astype(BF16)
    trt, tit = jnp.swapaxes(tr, 1, 2), jnp.swapaxes(ti, 1, 2)
    gi = jnp.concatenate([jnp.concatenate([trt, tit], axis=2),
                          jnp.concatenate([-tit, trt], axis=2)], axis=1).astype(BF16)

    ft3 = jnp.swapaxes(ftab.reshape(n1, b, LANES), 0, 1)
    hf = _dft_a_filter(ft3, w4, _decay_rates(d), fa_full, dc)
    h4 = _dft_inner(hf.reshape(2, nc, b, d), gf)
    yx = _dft_a(xin.reshape(a, b, d), fa, dc)
    c4 = _dft_inner(yx.reshape(2, nc, b, d), gf, h4, gi)
    y3 = _dft_c(c4.reshape(2 * nc, b, d), mc, xin.reshape(a, b, d), x0.reshape(a, b, d), skip.reshape(1, d), dc)
    return y3.reshape(l, d)


def _ctx_conv_body(x_ref, x0_ref, ft_ref, wa_ref, wb_ref, dl_ref, skip_ref, ff_ref, fh_ref, mi_ref, o_ref):
    x = x_ref[...].astype(F32)
    n = ft_ref.shape[0]
    filt = _filter_rows(ft_ref[...], _stack_3pass(wa_ref[...]), _stack_3pass(wb_ref[...]), dl_ref)
    xs = jnp.dot(ff_ref[...], x.astype(BF16), preferred_element_type=F32)
    hs = jnp.dot(fh_ref[...], filt.astype(BF16), preferred_element_type=F32)
    xr, xi, hr, hi = xs[:n], xs[n:], hs[:n], hs[n:]
    p = jnp.concatenate([xr * hr - xi * hi, xr * hi + xi * hr], axis=0).astype(BF16)
    conv = jnp.dot(mi_ref[...], p, preferred_element_type=F32)
    o_ref[...] = (x0_ref[...].astype(F32) * (conv + x * skip_ref[...])).astype(o_ref.dtype)


def _short_seq_conv_gated(xin, x0, ftab, w4, skip):
    l, d = xin.shape
    n = 2 * l
    hid = w4.shape[0]
    ni = jnp.arange(n, dtype=jnp.int32)
    cr, sr = _cis((ni[:, None] * ni[None, :]) % n, n)
    fh = jnp.concatenate([cr, -sr], axis=0).astype(BF16)
    ff = fh[:, :l]
    mi = (jnp.concatenate([cr[:l], -sr[:l]], axis=1) * (1.0 / n)).astype(BF16)
    dc = min(d, 512)
    nd = d // dc
    return pl.pallas_call(
        _ctx_conv_body,
        grid=(nd,),
        in_specs=[pl.BlockSpec((l, dc), lambda j: (0, j)),
                  pl.BlockSpec((l, dc), lambda j: (0, j)),
                  pl.BlockSpec((n, LANES), lambda j: (0, 0)),
                  pl.BlockSpec((hid, dc), lambda j: (0, j)),
                  pl.BlockSpec((hid, dc), lambda j: (0, j + nd)),
                  pl.BlockSpec((1, dc), lambda j: (0, j)),
                  pl.BlockSpec((1, dc), lambda j: (0, j)),
                  pl.BlockSpec((2 * n, l), lambda j: (0, 0)),
                  pl.BlockSpec((2 * n, n), lambda j: (0, 0)),
                  pl.BlockSpec((l, 2 * n), lambda j: (0, 0))],
        out_specs=pl.BlockSpec((l, dc), lambda j: (0, j)),
        out_shape=jax.ShapeDtypeStruct((l, d), BF16),
        compiler_params=_cparams("arbitrary"),
        name="ctx_conv",
    )(xin, x0, ftab, w4, w4, _decay_rates(d), skip.reshape(1, d), ff, fh, mi)


def _hyena_mixer(h, vecs, p, long_seq):
    w_in, b_in, w_sc, b_sc, f_w1, f_b1, f_w2, f_b2, f_w3, f_b3, f_w4, f_freq, skip, w_out, b_out = p
    l = h.shape[0]
    x0, xin = _hy_in_gate(h, vecs, w_in, b_in, w_sc, b_sc)
    ftab = _filter_table(l, f_w1, f_b1, f_w2, f_b2, f_w3, f_b3, f_freq)
    conv = _long_conv_gated if long_seq else _short_seq_conv_gated
    y = conv(xin, x0, ftab, f_w4, skip)
    return _proj_out(y, w_out, b_out, h, vecs)


def _attn_body(sink_ref, q_ref, kp_ref, kc_ref, kn_ref, vp_ref, vc_ref, vn_ref, kx_ref, vx_ref, bias_ref, o_ref,
               *, group):
    blk = q_ref.shape[0]
    bias = bias_ref[0]
    rid = lax.broadcasted_iota(jnp.int32, (group * blk, 1), 0)
    for kh in range(N_KV_HEADS):
        hs = slice(kh * HEAD_DIM, (kh + 1) * HEAD_DIM)
        heads = [kh * group + g for g in range(group)]
        q = jnp.concatenate([q_ref[:, hd * HEAD_DIM:(hd + 1) * HEAD_DIM] for hd in heads], axis=0)
        keys = jnp.concatenate([kp_ref[:, hs], kc_ref[:, hs], kn_ref[:, hs], kx_ref[:, hs]], axis=0)
        vals = jnp.concatenate([vp_ref[:, hs], vc_ref[:, hs], vn_ref[:, hs], vx_ref[:, hs]], axis=0)
        s = lax.dot_general(q, keys, (((1,), (1,)), ((), ())), preferred_element_type=F32) + bias
        sink = jnp.zeros((group * blk, 1), F32)
        for g, hd in enumerate(heads):
            sink = jnp.where((rid >= g * blk) & (rid < (g + 1) * blk), sink_ref[hd], sink)
        mx = jnp.maximum(jnp.max(s, axis=-1, keepdims=True), sink)
        pr = jnp.exp(s - mx)
        denom = jnp.sum(pr, axis=-1, keepdims=True) + jnp.exp(sink - mx)
        o = jnp.dot(pr.astype(BF16), vals, preferred_element_type=F32) / denom
        for g, hd in enumerate(heads):
            o_ref[:, hd * HEAD_DIM:(hd + 1) * HEAD_DIM] = o[g * blk:(g + 1) * blk].astype(o_ref.dtype)


def _attention(qkv, kvc, sink, d):
    l = qkv.shape[0]
    c = kvc.shape[0]
    blk = ATTN_BLOCK
    nb = l // blk
    group = d // HEAD_DIM // N_KV_HEADS
    kvw = N_KV_HEADS * HEAD_DIM
    kcol = d // kvw
    qi = jnp.arange(group * blk, dtype=jnp.int32)[:, None] % blk
    si = jnp.arange(3 * blk + c, dtype=jnp.int32)[None, :]
    kb, ki = si // blk, si % blk
    prev_ok, next_ok = (kb == 0) & (ki >= qi), (kb == 2) & (ki <= qi)
    always = (kb == 1) | (kb >= 3)
    variants = [always | next_ok, always | prev_ok | next_ok, always | prev_ok]
    bias = jnp.stack([jnp.where(v, 0.0, MASK_BIAS).astype(F32) for v in variants])

    def kv_spec(col, shift):
        return pl.BlockSpec((blk, kvw), lambda n: (jnp.clip(n + shift, 0, nb - 1), col))

    return pl.pallas_call(
        functools.partial(_attn_body, group=group),
        grid=(nb,),
        in_specs=[pl.BlockSpec(memory_space=pltpu.SMEM),
                  pl.BlockSpec((blk, d), lambda n: (n, 0)),
                  kv_spec(kcol, -1), kv_spec(kcol, 0), kv_spec(kcol, 1),
                  kv_spec(kcol + 1, -1), kv_spec(kcol + 1, 0), kv_spec(kcol + 1, 1),
                  pl.BlockSpec((c, kvw), lambda n: (0, 0)),
                  pl.BlockSpec((c, kvw), lambda n: (0, 1)),
                  pl.BlockSpec((1, group * blk, 3 * blk + c),
                               lambda n: (jnp.where(n == 0, 0, jnp.where(n == nb - 1, 2, 1)), 0, 0))],
        out_specs=pl.BlockSpec((blk, d), lambda n: (n, 0)),
        out_shape=jax.ShapeDtypeStruct((l, d), BF16),
        compiler_params=_cparams("arbitrary"),
        name="window_attn",
    )(sink, qkv, qkv, qkv, qkv, qkv, qkv, qkv, kvc, kvc, bias)


def _rope_tables(l):
    rows = l // GRID_W
    pos_row = jnp.broadcast_to(jnp.arange(rows)[:, None], (rows, GRID_W)).reshape(-1).astype(F32)
    pos_col = jnp.broadcast_to(jnp.arange(GRID_W)[None, :], (rows, GRID_W)).reshape(-1).astype(F32)
    pairs = HEAD_DIM // 4
    inv_freq = ROPE_THETA ** (-jnp.arange(pairs, dtype=F32) / pairs)
    ang_row = pos_row[:, None] * inv_freq[None, :]
    ang_col = pos_col[:, None] * inv_freq[None, :]
    zeros = jnp.zeros_like(ang_row)
    cos = jnp.concatenate([jnp.cos(ang_row)] * 2 + [jnp.cos(ang_col)] * 2, axis=-1)
    sin_a = jnp.concatenate([-jnp.sin(ang_row), zeros, -jnp.sin(ang_col), zeros], axis=-1)
    sin_b = jnp.concatenate([zeros, jnp.sin(ang_row), zeros, jnp.sin(ang_col)], axis=-1)
    return cos, sin_a, sin_b


def _pool_body(hc_ref, hp_ref, hn_ref, vec_ref, w_ref, b_ref, sc_ref, o_ref, y_ref, *, seq_len):
    i = pl.program_id(0)
    last = pl.num_programs(0) - 1
    bm, d = hc_ref.shape
    gw = d // len(POOL_SIZES)
    h = hc_ref[...]
    u = _modulated(h, vec_ref)
    up = jnp.where(i > 0, _modulated(hp_ref[...], vec_ref), 0.0)
    un = jnp.where(i < last, _modulated(hn_ref[...], vec_ref), 0.0)
    ext_rows = bm + 2 * SUBLANES
    t = i * bm + lax.broadcasted_iota(jnp.int32, (bm, 1), 0)
    for g, size in enumerate(POOL_SIZES):
        cols = slice(g * gw, (g + 1) * gw)
        ext = jnp.concatenate([up[:, cols], u[:, cols], un[:, cols]], axis=0)
        acc, span = ext, 1
        while span < size:
            acc = acc + pltpu.roll(acc, ext_rows - span, 0)
            span *= 2
        start = SUBLANES - size // 2
        win = pltpu.roll(acc, ext_rows - start, 0)[:bm] if start else acc[:bm]
        lo = jnp.clip(t - size // 2, 0, seq_len)
        hi = jnp.clip(t - size // 2 + size, 0, seq_len)
        part = win / (hi - lo).astype(F32) - u[:, cols]
        yg = jnp.dot(part.astype(BF16), w_ref[g], preferred_element_type=F32)
        y_ref[:, cols] = (yg + b_ref[:, cols]) * sc_ref[:, cols]
    o_ref[...] = h + vec_ref[2:3, :] * _rms(y_ref[...], vec_ref[4:5, :])


def _pool_mixer(h, vecs, w, b, scale):
    l, d = h.shape
    bm = min(l, 256)
    rb = bm // SUBLANES
    nrow8 = l // SUBLANES
    ng, gw = w.shape[0], w.shape[1]
    return pl.pallas_call(
        functools.partial(_pool_body, seq_len=l),
        grid=(l // bm,),
        in_specs=[pl.BlockSpec((bm, d), lambda i: (i, 0)),
                  pl.BlockSpec((SUBLANES, d), lambda i: (jnp.maximum(i * rb - 1, 0), 0)),
                  pl.BlockSpec((SUBLANES, d), lambda i: (jnp.minimum((i + 1) * rb, nrow8 - 1), 0)),
                  pl.BlockSpec((SUBLANES, d), lambda i: (0, 0)),
                  pl.BlockSpec((ng, gw, gw), lambda i: (0, 0, 0)),
                  pl.BlockSpec((1, d), lambda i: (0, 0)),
                  pl.BlockSpec((1, d), lambda i: (0, 0))],
        out_specs=pl.BlockSpec((bm, d), lambda i: (i, 0)),
        out_shape=jax.ShapeDtypeStruct((l, d), F32),
        scratch_shapes=[pltpu.VMEM((bm, d), F32)],
        compiler_params=_cparams("arbitrary"),
        name="pool_mixer",
    )(h, h, h, vecs, w.astype(BF16), b.reshape(1, d), scale.reshape(1, d))


def _sub_vecs(mod, k, g_pre, g_post):
    rows = [mod[3 * k], mod[3 * k + 1], mod[3 * k + 2], g_pre, g_post]
    return jnp.stack(rows + [jnp.zeros_like(g_pre)] * (SUBLANES - len(rows)))


def kernel(x, c, ctx, c_ctx, w_ada, b_ada, norm_pre, norm_post, w_ffn_in, w_ffn_out, hy_w_in, hy_b_in, hy_w_sc, hy_b_sc, hy_f_w1, hy_f_b1, hy_f_w2, hy_f_b2, hy_f_w3, hy_f_b3, hy_f_w4, hy_f_freq, hy_skip, hy_w_out, hy_b_out, at_w_qkv, at_b_qkv, at_sink, at_w_o, at_b_o, pl_w, pl_b, pl_scale):
    bsz, l, d = x.shape
    assert bsz == 1, "kernel handles a single batch element"
    assert d % 1024 == 0 and l % 1024 == 0 and ctx.shape[1] % 256 == 0, "block sizes assume these multiples"
    assert w_ffn_out.shape[2] % 512 == 0, "FFN width must be a multiple of the 512-column weight blocks"
    depth = w_ada.shape[0]
    n_mixers = 3
    attn_layers = [i for i in range(depth) if i % n_mixers == 1]
    last_ctx_layer = attn_layers[-1] if attn_layers else -1

    mods = _adaln_mods(c, c_ctx, w_ada, b_ada).reshape(depth, 2, N_MOD, d)
    rope = _rope_tables(l)
    w_in, w_out = w_ffn_in.astype(BF16), w_ffn_out.astype(BF16)
    h, hc = x[0], ctx[0]
    for i in range(depth):
        kind, j = i % n_mixers, i // n_mixers
        ctx_live = i <= last_ctx_layer
        ctx_out = i < last_ctx_layer
        vec = [_sub_vecs(mods[i, 0], k, norm_pre[i, k], norm_post[i, k]) for k in range(3)]
        vec_c = [_sub_vecs(mods[i, 1], k, norm_pre[i, k], norm_post[i, k]) for k in range(3)]

        h = _ffn(h, vec[0], w_in, w_out, i, 0, FFN_RES)
        if ctx_live:
            hc = _ffn(hc, vec_c[0], w_in, w_out, i, 0, FFN_RES)

        if kind == 0:
            hp = (hy_w_in[j].astype(BF16), hy_b_in[j], hy_w_sc[j], hy_b_sc[j], hy_f_w1[j], hy_f_b1[j],
                  hy_f_w2[j], hy_f_b2[j], hy_f_w3[j], hy_f_b3[j], hy_f_w4[j], hy_f_freq[j], hy_skip[j],
                  hy_w_out[j].astype(BF16), hy_b_out[j])
            h = _hyena_mixer(h, vec[1], hp, long_seq=True)
            if ctx_out:
                hc = _hyena_mixer(hc, vec_c[1], hp, long_seq=False)
        elif kind == 1:
            w_qkv = at_w_qkv[j].astype(BF16)
            qkv = _proj_in(h, vec[1], w_qkv, at_b_qkv[j], BF16, rope=rope,
                           q_blocks=d // 512, q_scale=HEAD_DIM ** -0.5)
            kvc = _proj_in(hc, vec_c[1], w_qkv[:, d:], at_b_qkv[j][d:], BF16)
            o = _attention(qkv, kvc, at_sink[j], d)
            h = _proj_out(o, at_w_o[j].astype(BF16), at_b_o[j], h, vec[1])
            assert not ctx_out, "context-query attention path is not needed for this depth"
        else:
            h = _pool_mixer(h, vec[1], pl_w[j], pl_b[j], pl_scale[j])
            assert not ctx_out, "context pooling path is not needed for this depth"

        h = _ffn(h, vec[2], w_in, w_out, i, 1, FFN_RES)
        if ctx_out:
            hc = _ffn(hc, vec_c[2], w_in, w_out, i, 1, FFN_RES)
    return h[None]
```

```python
import functools
import math

import jax
import jax.numpy as jnp
from jax import lax
from jax.experimental import pallas as pl
from jax.experimental.pallas import tpu as pltpu

F32 = jnp.float32
BF16 = jnp.bfloat16

GRID_W = 64
N_MOD = 9
NORM_EPS = 1e-6
FFN_RES = 0.5
FILTER_BANDS = 16
FILTER_EMB = 1 + 2 * FILTER_BANDS
DECAY_TARGET = 1e-2
SHORT_DECAY_PCT = 0.3
LONG_DECAY_PCT = 1.5
HEAD_DIM = 128
N_KV_HEADS = 4
WINDOW = 128
ATTN_BLOCK = 128
ROPE_THETA = 10000.0
POOL_SIZES = (2, 4, 8, 16)

VMEM_LIMIT_BYTES = 56 * 1024 * 1024
SUBLANES = 8
LANES = 128
DFT_INNER = 128
HALO = 16
MASK_BIAS = -1e30


def _cparams(*sem):
    return pltpu.CompilerParams(dimension_semantics=sem, vmem_limit_bytes=VMEM_LIMIT_BYTES)


def _rms(x, g):
    ms = jnp.mean(x * x, axis=-1, keepdims=True)
    return x * lax.rsqrt(ms + NORM_EPS) * g


def _modulated(h, vec_ref):
    return _rms(h, vec_ref[3:4, :]) * (1.0 + vec_ref[1:2, :]) + vec_ref[0:1, :]


def _store_modulated(h_ref, vec_ref, u_ref):
    gain = vec_ref[3:4, :] * (1.0 + vec_ref[1:2, :])
    u_ref[...] = (_rms(h_ref[...], gain) + vec_ref[0:1, :]).astype(u_ref.dtype)


def _mods_body(cb_ref, w_ref, b_ref, o_ref, s_ref):
    k_dim, bn = w_ref.shape[1], w_ref.shape[2]
    nl = bn // LANES

    @pl.when((pl.program_id(0) == 0) & (pl.program_id(1) == 0))
    def _():
        cv = cb_ref[...]
        s_ref[...] = cv * jax.nn.sigmoid(cv)

    def step(kg, acc):
        rows = pl.ds(pl.multiple_of(kg * SUBLANES, SUBLANES), SUBLANES)
        s = [s_ref[r, rows, :] for r in range(2)]
        new = list(acc)
        for j in range(nl):
            wv = w_ref[0, rows, j * LANES:(j + 1) * LANES]
            for r in range(2):
                new[r * nl + j] = acc[r * nl + j] + wv * s[r]
        return tuple(new)

    init = tuple(jnp.zeros((SUBLANES, LANES), F32) for _ in range(2 * nl))
    acc = lax.fori_loop(0, k_dim // SUBLANES, step, init, unroll=4)
    for r in range(2):
        for j in range(nl):
            cols = slice(j * LANES, (j + 1) * LANES)
            o_ref[0, r:r + 1, cols] = jnp.sum(acc[r * nl + j], axis=0, keepdims=True) + b_ref[0, :, cols]


def _adaln_mods(c, c_ctx, w_ada, b_ada):
    depth, d, n = w_ada.shape
    bn = 1024
    cb = jnp.broadcast_to(jnp.stack([c[0], c_ctx])[:, :, None], (2, d, LANES))
    return pl.pallas_call(
        _mods_body,
        grid=(depth, n // bn),
        in_specs=[pl.BlockSpec((2, d, LANES), lambda i, j: (0, 0, 0)),
                  pl.BlockSpec((1, d, bn), lambda i, j: (i, 0, j)),
                  pl.BlockSpec((1, 1, bn), lambda i, j: (i, 0, j))],
        out_specs=pl.BlockSpec((1, 2, bn), lambda i, j: (i, 0, j)),
        out_shape=jax.ShapeDtypeStruct((depth, 2, n), F32),
        scratch_shapes=[pltpu.VMEM((2, d, LANES), F32)],
        compiler_params=_cparams("arbitrary", "arbitrary"),
        name="adaln_mods",
    )(cb, w_ada, b_ada.reshape(depth, 1, n))


def _ffn_body(h_ref, vec_ref, wg_ref, wu_ref, wo_ref, o_ref, u_ref, acc_ref, *, res_w):
    j = pl.program_id(1)

    @pl.when((pl.program_id(0) == 0) & (j == 0))
    def _():
        acc_ref[...] = jnp.zeros_like(acc_ref)

    @pl.when(j == 0)
    def _():
        _store_modulated(h_ref, vec_ref, u_ref)

    u = u_ref[...]
    g = jnp.dot(u, wg_ref[...], preferred_element_type=F32)
    p = jnp.dot(u, wu_ref[...], preferred_element_type=F32)
    a = (g * jax.nn.sigmoid(g) * p).astype(BF16)
    acc_ref[...] += jnp.dot(a, wo_ref[...], preferred_element_type=F32)

    @pl.when(j == pl.num_programs(1) - 1)
    def _():
        gate = (res_w * vec_ref[2:3, :]) * vec_ref[4:5, :]
        o_ref[...] = h_ref[...] + _rms(acc_ref[...], gate)
        acc_ref[...] = jnp.zeros_like(acc_ref)


def _ffn(h, vecs, w_in_all, w_out_all, layer, slot, res_w):
    m, d = h.shape
    f = w_out_all.shape[2]
    bm = min(m, 512)
    bf = 512
    nf = f // bf
    return pl.pallas_call(
        functools.partial(_ffn_body, res_w=res_w),
        grid=(m // bm, nf),
        in_specs=[pl.BlockSpec((bm, d), lambda i, j: (i, 0)),
                  pl.BlockSpec((SUBLANES, d), lambda i, j: (0, 0)),
                  pl.BlockSpec((None, None, d, bf), lambda i, j: (layer, slot, 0, j)),
                  pl.BlockSpec((None, None, d, bf), lambda i, j: (layer, slot, 0, j + nf)),
                  pl.BlockSpec((None, None, bf, d), lambda i, j: (layer, slot, j, 0))],
        out_specs=pl.BlockSpec((bm, d), lambda i, j: (i, 0)),
        out_shape=jax.ShapeDtypeStruct((m, d), F32),
        scratch_shapes=[pltpu.VMEM((bm, d), BF16), pltpu.VMEM((bm, d), F32)],
        compiler_params=_cparams("arbitrary", "arbitrary"),
        name="ffn",
    )(h, vecs, w_in_all, w_in_all, w_out_all)


def _proj_in_body(h_ref, vec_ref, w_ref, b_ref, *rest, rope_blocks, q_blocks, q_scale):
    if rope_blocks:
        cos_ref, sa_ref, sb_ref, o_ref, u_ref = rest
    else:
        o_ref, u_ref = rest
    j = pl.program_id(1)

    @pl.when(j == 0)
    def _():
        _store_modulated(h_ref, vec_ref, u_ref)

    y = jnp.dot(u_ref[...], w_ref[...], preferred_element_type=F32) + b_ref[...]
    if not rope_blocks:
        o_ref[...] = y.astype(o_ref.dtype)
        return

    rot = jnp.where(j < q_blocks, q_scale, jnp.where(j < rope_blocks, 1.0, 0.0)).astype(F32)
    keep = jnp.where(j < rope_blocks, 0.0, 1.0).astype(F32)
    cos, sa, sb = cos_ref[...] * rot + keep, sa_ref[...] * rot, sb_ref[...] * rot
    for hd in range(y.shape[1] // HEAD_DIM):
        cols = slice(hd * HEAD_DIM, (hd + 1) * HEAD_DIM)
        xh = y[:, cols]
        r = (xh * cos + pltpu.roll(xh, HEAD_DIM - HEAD_DIM // 4, 1) * sa
             + pltpu.roll(xh, HEAD_DIM // 4, 1) * sb)
        o_ref[:, cols] = r.astype(o_ref.dtype)


def _proj_in(h, vecs, w, b, out_dtype, rope=None, q_blocks=0, q_scale=1.0):
    m, d = h.shape
    n = w.shape[1]
    bm = min(m, 1024)
    bn = 512
    in_specs = [pl.BlockSpec((bm, d), lambda i, j: (i, 0)),
                pl.BlockSpec((SUBLANES, d), lambda i, j: (0, 0)),
                pl.BlockSpec((d, bn), lambda i, j: (0, j)),
                pl.BlockSpec((1, bn), lambda i, j: (0, j))]
    args = [h, vecs, w, b.reshape(1, n)]
    rope_blocks = 0
    if rope is not None:
        rope_blocks = q_blocks + 1
        in_specs += [pl.BlockSpec((bm, HEAD_DIM), lambda i, j: (i, 0))] * 3
        args += list(rope)
    return pl.pallas_call(
        functools.partial(_proj_in_body, rope_blocks=rope_blocks, q_blocks=q_blocks, q_scale=q_scale),
        grid=(m // bm, n // bn),
        in_specs=in_specs,
        out_specs=pl.BlockSpec((bm, bn), lambda i, j: (i, j)),
        out_shape=jax.ShapeDtypeStruct((m, n), out_dtype),
        scratch_shapes=[pltpu.VMEM((bm, d), BF16)],
        compiler_params=_cparams("arbitrary", "arbitrary"),
        name="proj_in",
    )(*args)


def _proj_out_body(a_ref, w_ref, b_ref, h_ref, vec_ref, o_ref):
    y = jnp.dot(a_ref[...], w_ref[...], preferred_element_type=F32) + b_ref[...]
    o_ref[...] = h_ref[...] + vec_ref[2:3, :] * _rms(y, vec_ref[4:5, :])


def _proj_out(a, w, b, h, vecs):
    m, d = h.shape
    k = a.shape[1]
    bm = min(m, 256)
    return pl.pallas_call(
        _proj_out_body,
        grid=(m // bm,),
        in_specs=[pl.BlockSpec((bm, k), lambda i: (i, 0)),
                  pl.BlockSpec((k, d), lambda i: (0, 0)),
                  pl.BlockSpec((1, d), lambda i: (0, 0)),
                  pl.BlockSpec((bm, d), lambda i: (i, 0)),
                  pl.BlockSpec((SUBLANES, d), lambda i: (0, 0))],
        out_specs=pl.BlockSpec((bm, d), lambda i: (i, 0)),
        out_shape=jax.ShapeDtypeStruct((m, d), F32),
        compiler_params=_cparams("arbitrary"),
        name="proj_out",
    )(a, w, b.reshape(1, d), h, vecs)


def _hy_in_body(h_ref, hp_ref, hn_ref, vec_ref, w0_ref, w1_ref, w2_ref, b0_ref, b1_ref, b2_ref, wsc_ref, bsc_ref,
                x0_out, xin_out, u_ref, uh_ref):
    i = pl.program_id(0)
    last = pl.num_programs(0) - 1

    @pl.when(pl.program_id(1) == 0)
    def _():
        _store_modulated(h_ref, vec_ref, u_ref)
        uh_ref[0:SUBLANES, :] = hp_ref[...]
        uh_ref[SUBLANES:, :] = hn_ref[...]
        _store_modulated(uh_ref, vec_ref, uh_ref)

    def conv(w_ref, b_ref, part):
        z = jnp.dot(u_ref[...], w_ref[...], preferred_element_type=F32) + b_ref[...]
        zh = jnp.dot(uh_ref[...].astype(BF16), w_ref[...], preferred_element_type=F32) + b_ref[...]
        bm = z.shape[0]
        prev_row = jnp.where(i > 0, zh[SUBLANES - 1:SUBLANES, :], 0.0)
        next_row = jnp.where(i < last, zh[SUBLANES:SUBLANES + 1, :], 0.0)
        rid = lax.broadcasted_iota(jnp.int32, z.shape, 0)
        up = jnp.where(rid == 0, prev_row, pltpu.roll(z, 1, 0))
        dn = jnp.where(rid == bm - 1, next_row, pltpu.roll(z, bm - 1, 0))
        return (up * wsc_ref[0, part:part + 1, :] + z * wsc_ref[1, part:part + 1, :]
                + dn * wsc_ref[2, part:part + 1, :] + bsc_ref[part:part + 1, :])

    x0_out[...] = conv(w0_ref, b0_ref, 0).astype(x0_out.dtype)
    xin_out[...] = (conv(w2_ref, b2_ref, 2) * conv(w1_ref, b1_ref, 1)).astype(xin_out.dtype)


def _hy_in_gate(h, vecs, w_in, b_in, w_sc, b_sc):
    l, d = h.shape
    bm = min(l, 1024)
    bc = min(d, 512)
    nc = d // bc
    rb = bm // SUBLANES
    nrow8 = l // SUBLANES
    wspecs = [pl.BlockSpec((d, bc), functools.partial(lambda i, j, p: (0, p * nc + j), p=part)) for part in range(3)]
    bspecs = [pl.BlockSpec((1, bc), functools.partial(lambda i, j, p: (0, p * nc + j), p=part)) for part in range(3)]
    out_spec = pl.BlockSpec((bm, bc), lambda i, j: (i, j))
    return pl.pallas_call(
        _hy_in_body,
        grid=(l // bm, nc),
        in_specs=[pl.BlockSpec((bm, d), lambda i, j: (i, 0)),
                  pl.BlockSpec((SUBLANES, d), lambda i, j: (jnp.maximum(i * rb - 1, 0), 0)),
                  pl.BlockSpec((SUBLANES, d), lambda i, j: (jnp.minimum((i + 1) * rb, nrow8 - 1), 0)),
                  pl.BlockSpec((SUBLANES, d), lambda i, j: (0, 0))] + wspecs + bspecs
                 + [pl.BlockSpec((3, 3, bc), lambda i, j: (0, 0, j)),
                    pl.BlockSpec((3, bc), lambda i, j: (0, j))],
        out_specs=[out_spec, out_spec],
        out_shape=[jax.ShapeDtypeStruct((l, d), BF16)] * 2,
        scratch_shapes=[pltpu.VMEM((bm, d), BF16), pltpu.VMEM((2 * SUBLANES, d), F32)],
        compiler_params=_cparams("arbitrary", "arbitrary"),
        name="hyena_in_gate",
    )(h, h, h, vecs, w_in, w_in, w_in, *([b_in.reshape(1, 3 * d)] * 3), w_sc.reshape(3, 3, d), b_sc.reshape(3, d))


def _filter_mlp_body(z_ref, w1, b1, w2, b2, w3, b3, fr, o_ref):
    hp = lax.Precision.HIGHEST
    f = jnp.sin(fr[0:1, :] * (jnp.dot(z_ref[...], w1[...], precision=hp, preferred_element_type=F32) + b1[...]))
    f = jnp.sin(fr[1:2, :] * (jnp.dot(f, w2[...], precision=hp, preferred_element_type=F32) + b2[...]))
    o_ref[...] = jnp.sin(fr[2:3, :] * (jnp.dot(f, w3[...], precision=hp, preferred_element_type=F32) + b3[...]))


def _filter_table(l, w1, b1, w2, b2, w3, b3, freq):
    hid = w1.shape[1]
    t = jnp.linspace(0.0, 1.0, l, dtype=F32)[:, None]
    omega = 2.0 * math.pi * jnp.arange(l, dtype=F32)[:, None] / l
    bands = jnp.linspace(1e-4, FILTER_BANDS - 1, FILTER_BANDS, dtype=F32)[None, :]
    emb = jnp.concatenate([t, jnp.cos(bands * omega), -jnp.sin(bands * omega)], axis=-1)
    emb = jnp.pad(emb, ((0, 0), (0, LANES - emb.shape[1])))
    w1p = jnp.pad(w1, ((0, LANES - w1.shape[0]), (0, 0)))
    pack = LANES // hid
    rows = l // pack
    eye = jnp.eye(pack, dtype=F32)
    emb_p = jnp.concatenate([emb[p * rows:(p + 1) * rows] for p in range(pack)], axis=1)
    wide = pack * hid
    bm = min(rows, 512)

    def full(shape):
        return pl.BlockSpec(shape, lambda i: (0,) * len(shape))

    f_p = pl.pallas_call(
        _filter_mlp_body,
        grid=(rows // bm,),
        in_specs=[pl.BlockSpec((bm, pack * LANES), lambda i: (i, 0)),
                  full((pack * LANES, wide)), full((1, wide)), full((wide, wide)), full((1, wide)),
                  full((wide, wide)), full((1, wide)), full((3, wide))],
        out_specs=pl.BlockSpec((bm, wide), lambda i: (i, 0)),
        out_shape=jax.ShapeDtypeStruct((rows, wide), F32),
        compiler_params=_cparams("arbitrary"),
        name="hyena_filter_mlp",
    )(emb_p, jnp.kron(eye, w1p), jnp.tile(b1, pack).reshape(1, wide), jnp.kron(eye, w2),
      jnp.tile(b2, pack).reshape(1, wide), jnp.kron(eye, w3), jnp.tile(b3, pack).reshape(1, wide),
      jnp.tile(freq, (1, pack)))
    f = jnp.concatenate([f_p[:, p * hid:(p + 1) * hid] for p in range(pack)], axis=0)
    tab = jnp.concatenate([f, t, jnp.ones((l, 1), F32)], axis=-1)
    tab = jnp.pad(tab, ((0, 0), (0, LANES - tab.shape[1])))
    return jnp.concatenate([tab, jnp.zeros((1, LANES), F32), tab[1:][::-1]], axis=0)


def _decay_rates(d):
    return jnp.abs(jnp.linspace(math.log(DECAY_TARGET) / LONG_DECAY_PCT,
                                math.log(DECAY_TARGET) / SHORT_DECAY_PCT, d, dtype=F32))[None, :]


def _split_bf16(x):
    hi = x.astype(BF16)
    return hi, (x - hi.astype(F32)).astype(BF16)


def _stack_3pass(w):
    w_hi, w_lo = _split_bf16(w)
    return jnp.concatenate([w_hi, w_hi, w_lo], axis=0)


def _dot_3pass(f, w3):
    f_hi, f_lo = _split_bf16(f)
    return jnp.dot(jnp.concatenate([f_hi, f_lo, f_hi], axis=1), w3, preferred_element_type=F32)


def _filter_rows(ft, wa3, wb3, dl_ref):
    hid = wa3.shape[0] // 3
    half = ft.shape[0] // 2
    f, t, keep = ft[:, :hid], ft[:, hid:hid + 1], ft[:, hid + 1:hid + 2]
    y = jnp.concatenate([_dot_3pass(f[:half], wa3), _dot_3pass(f[half:], wb3)], axis=0)
    return y * (jnp.exp(-t * dl_ref[...]) * keep)


def _outer_stage(fa_ref, cols, o_ref):
    ys = [jnp.dot(fa_ref[...], xj, preferred_element_type=F32).astype(BF16) for xj in cols]
    o_ref[...] = pltpu.einshape("brd->rbd", jnp.stack(ys))


def _dft_a_body(x_ref, fa_ref, o_ref):
    xb = pltpu.einshape("abd->bad", x_ref[...])
    _outer_stage(fa_ref, [xb[j] for j in range(xb.shape[0])], o_ref)


def _dft_a_filter_body(ft_ref, wa_ref, wb_ref, dl_ref, fa_ref, o_ref):
    wa3, wb3 = _stack_3pass(wa_ref[...]), _stack_3pass(wb_ref[...])
    cols = [_filter_rows(ft_ref[j], wa3, wb3, dl_ref).astype(BF16) for j in range(ft_ref.shape[0])]
    _outer_stage(fa_ref, cols, o_ref)


def _dft_a(x3, fa, d_chunk):
    a, b, d = x3.shape
    rows = fa.shape[0]
    return pl.pallas_call(
        _dft_a_body,
        grid=(b // HALO, d // d_chunk),
        in_specs=[pl.BlockSpec((a, HALO, d_chunk), lambda i, j: (0, i, j)),
                  pl.BlockSpec((rows, a), lambda i, j: (0, 0))],
        out_specs=pl.BlockSpec((rows, HALO, d_chunk), lambda i, j: (0, i, j)),
        out_shape=jax.ShapeDtypeStruct((rows, b, d), BF16),
        compiler_params=_cparams("arbitrary", "arbitrary"),
        name="dft_outer",
    )(x3, fa)


def _dft_a_filter(ft3, w4, deltas, fa, d_chunk):
    b, n1, _ = ft3.shape
    hid, d2 = w4.shape
    d = d2 // 2
    rows = fa.shape[0]
    nd = d // d_chunk
    return pl.pallas_call(
        _dft_a_filter_body,
        grid=(b // HALO, nd),
        in_specs=[pl.BlockSpec((HALO, n1, LANES), lambda i, j: (i, 0, 0)),
                  pl.BlockSpec((hid, d_chunk), lambda i, j: (0, j)),
                  pl.BlockSpec((hid, d_chunk), lambda i, j: (0, j + nd)),
                  pl.BlockSpec((1, d_chunk), lambda i, j: (0, j)),
                  pl.BlockSpec((rows, n1), lambda i, j: (0, 0))],
        out_specs=pl.BlockSpec((rows, HALO, d_chunk), lambda i, j: (0, i, j)),
        out_shape=jax.ShapeDtypeStruct((rows, b, d), BF16),
        compiler_params=_cparams("arbitrary", "arbitrary"),
        name="dft_outer_filter",
    )(ft3, w4, w4, deltas, fa)


def _dft_inner_body(y_ref, yf_ref, gf_ref, gi_ref, o_ref):
    b, dc = y_ref.shape[2], y_ref.shape[3]
    for ci in range(y_ref.shape[1]):
        z = jnp.dot(gf_ref[ci], y_ref[:, ci].reshape(2 * b, dc), preferred_element_type=F32)
        hh = jnp.dot(gf_ref[ci], yf_ref[:, ci].reshape(2 * b, dc), preferred_element_type=F32)
        zr, zi, hr, hi = z[:b], z[b:], hh[:b], hh[b:]
        p = jnp.concatenate([zr * hr - zi * hi, zr * hi + zi * hr], axis=0).astype(BF16)
        cc = jnp.dot(gi_ref[ci], p, preferred_element_type=F32)
        o_ref[:, ci] = cc.reshape(2, b, dc).astype(o_ref.dtype)


def _dft_inner(y4, yf4, gf, gi):
    _, nc, b, d = y4.shape
    cblk = SUBLANES
    dc = min(d, 1024)
    yspec = pl.BlockSpec((2, cblk, b, dc), lambda c, j: (0, c, 0, j))
    gspec = pl.BlockSpec((cblk, 2 * b, 2 * b), lambda c, j: (c, 0, 0))
    return pl.pallas_call(
        _dft_inner_body,
        grid=(nc // cblk, d // dc),
        in_specs=[yspec, yspec, gspec, gspec],
        out_specs=yspec,
        out_shape=jax.ShapeDtypeStruct(y4.shape, BF16),
        compiler_params=_cparams("arbitrary", "arbitrary"),
        name="dft_inner",
    )(y4, yf4, gf, gi)


def _dft_c_body(c_ref, m_ref, xin_ref, x0_ref, skip_ref, o_ref):
    cb = pltpu.einshape("rbd->brd", c_ref[...])
    outs = [jnp.dot(m_ref[...], cb[j], preferred_element_type=F32).astype(BF16) for j in range(cb.shape[0])]
    conv = pltpu.einshape("bad->abd", jnp.stack(outs)).astype(F32)
    gated = x0_ref[...].astype(F32) * (conv + xin_ref[...].astype(F32) * skip_ref[...][None])
    o_ref[...] = gated.astype(o_ref.dtype)


def _dft_c(c3, mc, xin3, x03, skip, d_chunk):
    rows, b, d = c3.shape
    a = mc.shape[0]
    xspec = pl.BlockSpec((a, HALO, d_chunk), lambda i, j: (0, i, j))
    return pl.pallas_call(
        _dft_c_body,
        grid=(b // HALO, d // d_chunk),
        in_specs=[pl.BlockSpec((rows, HALO, d_chunk), lambda i, j: (0, i, j)),
                  pl.BlockSpec((a, rows), lambda i, j: (0, 0)),
                  xspec, xspec,
                  pl.BlockSpec((1, d_chunk), lambda i, j: (0, j))],
        out_specs=xspec,
        out_shape=jax.ShapeDtypeStruct((a, b, d), BF16),
        compiler_params=_cparams("arbitrary", "arbitrary"),
        name="dft_outer_inv",
    )(c3, mc, xin3, x03, skip)


def _cis(num, period):
    ang = (2.0 * math.pi / period) * num.astype(F32)
    return jnp.cos(ang), jnp.sin(ang)


def _long_conv_gated(xin, x0, ftab, w4, skip):
    l, d = xin.shape
    b = DFT_INNER
    a = l // b
    n1, n = 2 * a, 2 * l
    nc = n1 // 2 + SUBLANES
    dc = min(d, 512)
    ci = jnp.arange(nc, dtype=jnp.int32)
    ai = jnp.arange(n1, dtype=jnp.int32)
    live = (ci <= n1 // 2).astype(F32)[:, None]
    cr, sr = _cis((ci[:, None] * ai[None, :]) % n1, n1)
    cr, sr = cr * live, sr * live
    fa_full = jnp.concatenate([cr, -sr], axis=0).astype(BF16)
    fa = fa_full[:, :a]
    fold = jnp.where((ci == 0) | (ci == n1 // 2), 1.0, 2.0)[None, :] / n
    mc = jnp.concatenate([cr[:, :a].T * fold, -sr[:, :a].T * fold], axis=1).astype(BF16)
    ei = jnp.arange(b, dtype=jnp.int32)
    kk = ci[:, None, None] + n1 * ei[None, :, None]
    tr, ts = _cis((kk * ei[None, None, :]) % n, n)
    ti = -ts
    gf = jnp.concatenate([jnp.concatenate([tr, -ti], axis=2),
                          jnp.concatenate([ti, tr], axis=2)], axis=1).astype(BF16)
    trt, tit = jnp.swapaxes(tr, 1, 2), jnp.swapaxes(ti, 1, 2)
    gi = jnp.concatenate([jnp.concatenate([trt, tit], axis=2),
                          jnp.concatenate([-tit, trt], axis=2)], axis=1).astype(BF16)

    ft3 = jnp.swapaxes(ftab.reshape(n1, b, LANES), 0, 1)
    hf = _dft_a_filter(ft3, w4, _decay_rates(d), fa_full, dc)
    yx = _dft_a(xin.reshape(a, b, d), fa, dc)
    c4 = _dft_inner(yx.reshape(2, nc, b, d), hf.reshape(2, nc, b, d), gf, gi)
    y3 = _dft_c(c4.reshape(2 * nc, b, d), mc, xin.reshape(a, b, d), x0.reshape(a, b, d), skip.reshape(1, d), dc)
    return y3.reshape(l, d)


def _ctx_conv_body(x_ref, x0_ref, ft_ref, wa_ref, wb_ref, dl_ref, skip_ref, ff_ref, fh_ref, mi_ref, o_ref):
    x = x_ref[...].astype(F32)
    n = ft_ref.shape[0]
    filt = _filter_rows(ft_ref[...], _stack_3pass(wa_ref[...]), _stack_3pass(wb_ref[...]), dl_ref)
    xs = jnp.dot(ff_ref[...], x.astype(BF16), preferred_element_type=F32)
    hs = jnp.dot(fh_ref[...], filt.astype(BF16), preferred_element_type=F32)
    xr, xi, hr, hi = xs[:n], xs[n:], hs[:n], hs[n:]
    p = jnp.concatenate([xr * hr - xi * hi, xr * hi + xi * hr], axis=0).astype(BF16)
    conv = jnp.dot(mi_ref[...], p, preferred_element_type=F32)
    o_ref[...] = (x0_ref[...].astype(F32) * (conv + x * skip_ref[...])).astype(o_ref.dtype)


def _short_seq_conv_gated(xin, x0, ftab, w4, skip):
    l, d = xin.shape
    n = 2 * l
    hid = w4.shape[0]
    ni = jnp.arange(n, dtype=jnp.int32)
    cr, sr = _cis((ni[:, None] * ni[None, :]) % n, n)
    fh = jnp.concatenate([cr, -sr], axis=0).astype(BF16)
    ff = fh[:, :l]
    mi = (jnp.concatenate([cr[:l], -sr[:l]], axis=1) * (1.0 / n)).astype(BF16)
    dc = min(d, 512)
    nd = d // dc
    return pl.pallas_call(
        _ctx_conv_body,
        grid=(nd,),
        in_specs=[pl.BlockSpec((l, dc), lambda j: (0, j)),
                  pl.BlockSpec((l, dc), lambda j: (0, j)),
                  pl.BlockSpec((n, LANES), lambda j: (0, 0)),
                  pl.BlockSpec((hid, dc), lambda j: (0, j)),
                  pl.BlockSpec((hid, dc), lambda j: (0, j + nd)),
                  pl.BlockSpec((1, dc), lambda j: (0, j)),
                  pl.BlockSpec((1, dc), lambda j: (0, j)),
                  pl.BlockSpec((2 * n, l), lambda j: (0, 0)),
                  pl.BlockSpec((2 * n, n), lambda j: (0, 0)),
                  pl.BlockSpec((l, 2 * n), lambda j: (0, 0))],
        out_specs=pl.BlockSpec((l, dc), lambda j: (0, j)),
        out_shape=jax.ShapeDtypeStruct((l, d), BF16),
        compiler_params=_cparams("arbitrary"),
        name="ctx_conv",
    )(xin, x0, ftab, w4, w4, _decay_rates(d), skip.reshape(1, d), ff, fh, mi)


def _hyena_mixer(h, vecs, p, long_seq):
    w_in, b_in, w_sc, b_sc, f_w1, f_b1, f_w2, f_b2, f_w3, f_b3, f_w4, f_freq, skip, w_out, b_out = p
    l = h.shape[0]
    x0, xin = _hy_in_gate(h, vecs, w_in, b_in, w_sc, b_sc)
    ftab = _filter_table(l, f_w1, f_b1, f_w2, f_b2, f_w3, f_b3, f_freq)
    conv = _long_conv_gated if long_seq else _short_seq_conv_gated
    y = conv(xin, x0, ftab, f_w4, skip)
    return _proj_out(y, w_out, b_out, h, vecs)


def _attn_body(sink_ref, q_ref, kp_ref, kc_ref, kn_ref, vp_ref, vc_ref, vn_ref, kx_ref, vx_ref, bias_ref, o_ref,
               *, group):
    blk = q_ref.shape[0]
    bias = bias_ref[0]
    rid = lax.broadcasted_iota(jnp.int32, (group * blk, 1), 0)
    for kh in range(N_KV_HEADS):
        hs = slice(kh * HEAD_DIM, (kh + 1) * HEAD_DIM)
        heads = [kh * group + g for g in range(group)]
        q = jnp.concatenate([q_ref[:, hd * HEAD_DIM:(hd + 1) * HEAD_DIM] for hd in heads], axis=0)
        keys = jnp.concatenate([kp_ref[:, hs], kc_ref[:, hs], kn_ref[:, hs], kx_ref[:, hs]], axis=0)
        vals = jnp.concatenate([vp_ref[:, hs], vc_ref[:, hs], vn_ref[:, hs], vx_ref[:, hs]], axis=0)
        s = lax.dot_general(q, keys, (((1,), (1,)), ((), ())), preferred_element_type=F32) + bias
        sink = jnp.zeros((group * blk, 1), F32)
        for g, hd in enumerate(heads):
            sink = jnp.where((rid >= g * blk) & (rid < (g + 1) * blk), sink_ref[hd], sink)
        mx = jnp.maximum(jnp.max(s, axis=-1, keepdims=True), sink)
        pr = jnp.exp(s - mx)
        denom = jnp.sum(pr, axis=-1, keepdims=True) + jnp.exp(sink - mx)
        o = jnp.dot(pr.astype(BF16), vals, preferred_element_type=F32) / denom
        for g, hd in enumerate(heads):
            o_ref[:, hd * HEAD_DIM:(hd + 1) * HEAD_DIM] = o[g * blk:(g + 1) * blk].astype(o_ref.dtype)


def _attention(qkv, kvc, sink, d):
    l = qkv.shape[0]
    c = kvc.shape[0]
    blk = ATTN_BLOCK
    nb = l // blk
    group = d // HEAD_DIM // N_KV_HEADS
    kvw = N_KV_HEADS * HEAD_DIM
    kcol = d // kvw
    qi = jnp.arange(group * blk, dtype=jnp.int32)[:, None] % blk
    si = jnp.arange(3 * blk + c, dtype=jnp.int32)[None, :]
    kb, ki = si // blk, si % blk
    prev_ok, next_ok = (kb == 0) & (ki >= qi), (kb == 2) & (ki <= qi)
    always = (kb == 1) | (kb >= 3)
    variants = [always | next_ok, always | prev_ok | next_ok, always | prev_ok]
    bias = jnp.stack([jnp.where(v, 0.0, MASK_BIAS).astype(F32) for v in variants])

    def kv_spec(col, shift):
        return pl.BlockSpec((blk, kvw), lambda n: (jnp.clip(n + shift, 0, nb - 1), col))

    return pl.pallas_call(
        functools.partial(_attn_body, group=group),
        grid=(nb,),
        in_specs=[pl.BlockSpec(memory_space=pltpu.SMEM),
                  pl.BlockSpec((blk, d), lambda n: (n, 0)),
                  kv_spec(kcol, -1), kv_spec(kcol, 0), kv_spec(kcol, 1),
                  kv_spec(kcol + 1, -1), kv_spec(kcol + 1, 0), kv_spec(kcol + 1, 1),
                  pl.BlockSpec((c, kvw), lambda n: (0, 0)),
                  pl.BlockSpec((c, kvw), lambda n: (0, 1)),
                  pl.BlockSpec((1, group * blk, 3 * blk + c),
                               lambda n: (jnp.where(n == 0, 0, jnp.where(n == nb - 1, 2, 1)), 0, 0))],
        out_specs=pl.BlockSpec((blk, d), lambda n: (n, 0)),
        out_shape=jax.ShapeDtypeStruct((l, d), BF16),
        compiler_params=_cparams("arbitrary"),
        name="window_attn",
    )(sink, qkv, qkv, qkv, qkv, qkv, qkv, qkv, kvc, kvc, bias)


def _rope_tables(l):
    rows = l // GRID_W
    pos_row = jnp.broadcast_to(jnp.arange(rows)[:, None], (rows, GRID_W)).reshape(-1).astype(F32)
    pos_col = jnp.broadcast_to(jnp.arange(GRID_W)[None, :], (rows, GRID_W)).reshape(-1).astype(F32)
    pairs = HEAD_DIM // 4
    inv_freq = ROPE_THETA ** (-jnp.arange(pairs, dtype=F32) / pairs)
    ang_row = pos_row[:, None] * inv_freq[None, :]
    ang_col = pos_col[:, None] * inv_freq[None, :]
    zeros = jnp.zeros_like(ang_row)
    cos = jnp.concatenate([jnp.cos(ang_row)] * 2 + [jnp.cos(ang_col)] * 2, axis=-1)
    sin_a = jnp.concatenate([-jnp.sin(ang_row), zeros, -jnp.sin(ang_col), zeros], axis=-1)
    sin_b = jnp.concatenate([zeros, jnp.sin(ang_row), zeros, jnp.sin(ang_col)], axis=-1)
    return cos, sin_a, sin_b


def _pool_body(hc_ref, hp_ref, hn_ref, vec_ref, w_ref, b_ref, sc_ref, o_ref, y_ref, *, seq_len):
    i = pl.program_id(0)
    last = pl.num_programs(0) - 1
    bm, d = hc_ref.shape
    gw = d // len(POOL_SIZES)
    h = hc_ref[...]
    u = _modulated(h, vec_ref)
    up = jnp.where(i > 0, _modulated(hp_ref[...], vec_ref), 0.0)
    un = jnp.where(i < last, _modulated(hn_ref[...], vec_ref), 0.0)
    ext_rows = bm + 2 * SUBLANES
    t = i * bm + lax.broadcasted_iota(jnp.int32, (bm, 1), 0)
    for g, size in enumerate(POOL_SIZES):
        cols = slice(g * gw, (g + 1) * gw)
        ext = jnp.concatenate([up[:, cols], u[:, cols], un[:, cols]], axis=0)
        acc, span = ext, 1
        while span < size:
            acc = acc + pltpu.roll(acc, ext_rows - span, 0)
            span *= 2
        start = SUBLANES - size // 2
        win = pltpu.roll(acc, ext_rows - start, 0)[:bm] if start else acc[:bm]
        lo = jnp.clip(t - size // 2, 0, seq_len)
        hi = jnp.clip(t - size // 2 + size, 0, seq_len)
        part = win / (hi - lo).astype(F32) - u[:, cols]
        yg = jnp.dot(part.astype(BF16), w_ref[g], preferred_element_type=F32)
        y_ref[:, cols] = (yg + b_ref[:, cols]) * sc_ref[:, cols]
    o_ref[...] = h + vec_ref[2:3, :] * _rms(y_ref[...], vec_ref[4:5, :])


def _pool_mixer(h, vecs, w, b, scale):
    l, d = h.shape
    bm = min(l, 256)
    rb = bm // SUBLANES
    nrow8 = l // SUBLANES
    ng, gw = w.shape[0], w.shape[1]
    return pl.pallas_call(
        functools.partial(_pool_body, seq_len=l),
        grid=(l // bm,),
        in_specs=[pl.BlockSpec((bm, d), lambda i: (i, 0)),
                  pl.BlockSpec((SUBLANES, d), lambda i: (jnp.maximum(i * rb - 1, 0), 0)),
                  pl.BlockSpec((SUBLANES, d), lambda i: (jnp.minimum((i + 1) * rb, nrow8 - 1), 0)),
                  pl.BlockSpec((SUBLANES, d), lambda i: (0, 0)),
                  pl.BlockSpec((ng, gw, gw), lambda i: (0, 0, 0)),
                  pl.BlockSpec((1, d), lambda i: (0, 0)),
                  pl.BlockSpec((1, d), lambda i: (0, 0))],
        out_specs=pl.BlockSpec((bm, d), lambda i: (i, 0)),
        out_shape=jax.ShapeDtypeStruct((l, d), F32),
        scratch_shapes=[pltpu.VMEM((bm, d), F32)],
        compiler_params=_cparams("arbitrary"),
        name="pool_mixer",
    )(h, h, h, vecs, w.astype(BF16), b.reshape(1, d), scale.reshape(1, d))


def _sub_vecs(mod, k, g_pre, g_post):
    rows = [mod[3 * k], mod[3 * k + 1], mod[3 * k + 2], g_pre, g_post]
    return jnp.stack(rows + [jnp.zeros_like(g_pre)] * (SUBLANES - len(rows)))


def kernel(x, c, ctx, c_ctx, w_ada, b_ada, norm_pre, norm_post, w_ffn_in, w_ffn_out, hy_w_in, hy_b_in, hy_w_sc, hy_b_sc, hy_f_w1, hy_f_b1, hy_f_w2, hy_f_b2, hy_f_w3, hy_f_b3, hy_f_w4, hy_f_freq, hy_skip, hy_w_out, hy_b_out, at_w_qkv, at_b_qkv, at_sink, at_w_o, at_b_o, pl_w, pl_b, pl_scale):
    bsz, l, d = x.shape
    assert bsz == 1, "kernel handles a single batch element"
    assert d % 1024 == 0 and l % 1024 == 0 and ctx.shape[1] % 256 == 0, "block sizes assume these multiples"
    assert w_ffn_out.shape[2] % 512 == 0, "FFN width must be a multiple of the 512-column weight blocks"
    depth = w_ada.shape[0]
    n_mixers = 3
    attn_layers = [i for i in range(depth) if i % n_mixers == 1]
    last_ctx_layer = attn_layers[-1] if attn_layers else -1

    mods = _adaln_mods(c, c_ctx, w_ada, b_ada).reshape(depth, 2, N_MOD, d)
    rope = _rope_tables(l)
    w_in, w_out = w_ffn_in.astype(BF16), w_ffn_out.astype(BF16)
    h, hc = x[0], ctx[0]
    for i in range(depth):
        kind, j = i % n_mixers, i // n_mixers
        ctx_live = i <= last_ctx_layer
        ctx_out = i < last_ctx_layer
        vec = [_sub_vecs(mods[i, 0], k, norm_pre[i, k], norm_post[i, k]) for k in range(3)]
        vec_c = [_sub_vecs(mods[i, 1], k, norm_pre[i, k], norm_post[i, k]) for k in range(3)]

        h = _ffn(h, vec[0], w_in, w_out, i, 0, FFN_RES)
        if ctx_live:
            hc = _ffn(hc, vec_c[0], w_in, w_out, i, 0, FFN_RES)

        if kind == 0:
            hp = (hy_w_in[j].astype(BF16), hy_b_in[j], hy_w_sc[j], hy_b_sc[j], hy_f_w1[j], hy_f_b1[j],
                  hy_f_w2[j], hy_f_b2[j], hy_f_w3[j], hy_f_b3[j], hy_f_w4[j], hy_f_freq[j], hy_skip[j],
                  hy_w_out[j].astype(BF16), hy_b_out[j])
            h = _hyena_mixer(h, vec[1], hp, long_seq=True)
            if ctx_out:
                hc = _hyena_mixer(hc, vec_c[1], hp, long_seq=False)
        elif kind == 1:
            w_qkv = at_w_qkv[j].astype(BF16)
            qkv = _proj_in(h, vec[1], w_qkv, at_b_qkv[j], BF16, rope=rope,
                           q_blocks=d // 512, q_scale=HEAD_DIM ** -0.5)
            kvc = _proj_in(hc, vec_c[1], w_qkv[:, d:], at_b_qkv[j][d:], BF16)
            o = _attention(qkv, kvc, at_sink[j], d)
            h = _proj_out(o, at_w_o[j].astype(BF16), at_b_o[j], h, vec[1])
            assert not ctx_out, "context-query attention path is not needed for this depth"
        else:
            h = _pool_mixer(h, vec[1], pl_w[j], pl_b[j], pl_scale[j])
            assert not ctx_out, "context pooling path is not needed for this depth"

        h = _ffn(h, vec[2], w_in, w_out, i, 1, FFN_RES)
        if ctx_out:
            hc = _ffn(hc, vec_c[2], w_in, w_out, i, 1, FFN_RES)
    return h[None]
```

```python
import functools
import math

import jax
import jax.numpy as jnp
from jax import lax
from jax.experimental import pallas as pl
from jax.experimental.pallas import tpu as pltpu

F32 = jnp.float32
BF16 = jnp.bfloat16

GRID_W = 64
N_MOD = 9
NORM_EPS = 1e-6
FFN_RES = 0.5
FILTER_BANDS = 16
FILTER_EMB = 1 + 2 * FILTER_BANDS
DECAY_TARGET = 1e-2
SHORT_DECAY_PCT = 0.3
LONG_DECAY_PCT = 1.5
HEAD_DIM = 128
N_KV_HEADS = 4
WINDOW = 128
ATTN_BLOCK = 128
ROPE_THETA = 10000.0
POOL_SIZES = (2, 4, 8, 16)

VMEM_LIMIT_BYTES = 56 * 1024 * 1024
SUBLANES = 8
LANES = 128
DFT_INNER = 128
HALO = 16
MASK_BIAS = -1e30


def _cparams(*sem):
    return pltpu.CompilerParams(dimension_semantics=sem, vmem_limit_bytes=VMEM_LIMIT_BYTES)


def _rms(x, g):
    ms = jnp.mean(x * x, axis=-1, keepdims=True)
    return x * lax.rsqrt(ms + NORM_EPS) * g


def _modulated(h, vec_ref):
    return _rms(h, vec_ref[3:4, :]) * (1.0 + vec_ref[1:2, :]) + vec_ref[0:1, :]


def _store_modulated(h_ref, vec_ref, u_ref):
    gain = vec_ref[3:4, :] * (1.0 + vec_ref[1:2, :])
    u_ref[...] = (_rms(h_ref[...], gain) + vec_ref[0:1, :]).astype(u_ref.dtype)


def _mods_body(cb_ref, w_ref, b_ref, o_ref, s_ref):
    k_dim, bn = w_ref.shape[1], w_ref.shape[2]
    nl = bn // LANES

    @pl.when((pl.program_id(0) == 0) & (pl.program_id(1) == 0))
    def _():
        cv = cb_ref[...]
        s_ref[...] = cv * jax.nn.sigmoid(cv)

    def step(kg, acc):
        rows = pl.ds(pl.multiple_of(kg * SUBLANES, SUBLANES), SUBLANES)
        s = [s_ref[r, rows, :] for r in range(2)]
        new = list(acc)
        for j in range(nl):
            wv = w_ref[0, rows, j * LANES:(j + 1) * LANES]
            for r in range(2):
                new[r * nl + j] = acc[r * nl + j] + wv * s[r]
        return tuple(new)

    init = tuple(jnp.zeros((SUBLANES, LANES), F32) for _ in range(2 * nl))
    acc = lax.fori_loop(0, k_dim // SUBLANES, step, init, unroll=4)
    for r in range(2):
        for j in range(nl):
            cols = slice(j * LANES, (j + 1) * LANES)
            o_ref[0, r:r + 1, cols] = jnp.sum(acc[r * nl + j], axis=0, keepdims=True) + b_ref[0, :, cols]


def _adaln_mods(c, c_ctx, w_ada, b_ada):
    depth, d, n = w_ada.shape
    bn = 1024
    cb = jnp.broadcast_to(jnp.stack([c[0], c_ctx])[:, :, None], (2, d, LANES))
    return pl.pallas_call(
        _mods_body,
        grid=(depth, n // bn),
        in_specs=[pl.BlockSpec((2, d, LANES), lambda i, j: (0, 0, 0)),
                  pl.BlockSpec((1, d, bn), lambda i, j: (i, 0, j)),
                  pl.BlockSpec((1, 1, bn), lambda i, j: (i, 0, j))],
        out_specs=pl.BlockSpec((1, 2, bn), lambda i, j: (i, 0, j)),
        out_shape=jax.ShapeDtypeStruct((depth, 2, n), F32),
        scratch_shapes=[pltpu.VMEM((2, d, LANES), F32)],
        compiler_params=_cparams("arbitrary", "arbitrary"),
        name="adaln_mods",
    )(cb, w_ada, b_ada.reshape(depth, 1, n))


def _ffn_body(h_ref, vec_ref, wg_ref, wu_ref, wo_ref, *rest, res_w, cast_plans):
    if cast_plans:
        ci_ref, co_ref, o_ref, cib_ref, cob_ref, u_ref, acc_ref = rest
    else:
        o_ref, u_ref, acc_ref = rest
    j = pl.program_id(1)

    @pl.when((pl.program_id(0) == 0) & (j == 0))
    def _():
        acc_ref[...] = jnp.zeros_like(acc_ref)

    @pl.when(j == 0)
    def _():
        _store_modulated(h_ref, vec_ref, u_ref)

    u = u_ref[...]
    g = jnp.dot(u, wg_ref[...], preferred_element_type=F32)
    p = jnp.dot(u, wu_ref[...], preferred_element_type=F32)
    a = (g * jax.nn.sigmoid(g) * p).astype(BF16)
    acc_ref[...] += jnp.dot(a, wo_ref[...], preferred_element_type=F32)

    if cast_plans:
        step = pl.program_id(0) * pl.num_programs(1) + j
        for (src, dst), (_, per_block, rows) in zip(((ci_ref, cib_ref), (co_ref, cob_ref)), cast_plans):
            rs = pl.ds(pl.multiple_of((step % per_block) * rows, rows), rows)
            dst[rs, :] = src[rs, :].astype(dst.dtype)

    @pl.when(j == pl.num_programs(1) - 1)
    def _():
        gate = (res_w * vec_ref[2:3, :]) * vec_ref[4:5, :]
        o_ref[...] = h_ref[...] + _rms(acc_ref[...], gate)
        acc_ref[...] = jnp.zeros_like(acc_ref)


def _cast_plan(n_rows, n_cols, steps):
    for width in range(LANES, n_cols + 1, LANES):
        if n_cols % width or steps % (n_cols // width):
            continue
        per_block = steps // (n_cols // width)
        if n_rows % per_block == 0 and (n_rows // per_block) % HALO == 0:
            return width, per_block, n_rows // per_block
    raise ValueError("no slicing of the weight cast fits this grid")


def _ffn(h, vecs, w_in, w_out, res_w, cast_next=None):
    m, d = h.shape
    f = w_out.shape[0]
    bm = min(m, 512)
    bf = 512
    nf = f // bf
    in_specs = [pl.BlockSpec((bm, d), lambda i, j: (i, 0)),
                pl.BlockSpec((SUBLANES, d), lambda i, j: (0, 0)),
                pl.BlockSpec((d, bf), lambda i, j: (0, j)),
                pl.BlockSpec((d, bf), lambda i, j: (0, j + nf)),
                pl.BlockSpec((bf, d), lambda i, j: (j, 0))]
    out_specs = [pl.BlockSpec((bm, d), lambda i, j: (i, 0))]
    out_shape = [jax.ShapeDtypeStruct((m, d), F32)]
    args = [h, vecs, w_in, w_in, w_out]
    plans = ()
    if cast_next is not None:
        *srcs, layer, slot = cast_next
        plans = tuple(_cast_plan(w.shape[2], w.shape[3], (m // bm) * nf) for w in srcs)
        for w, (width, per_block, _) in zip(srcs, plans):
            rows = w.shape[2]
            in_specs.append(pl.BlockSpec(
                (None, None, rows, width),
                functools.partial(lambda i, j, pb: (layer, slot, 0, (i * nf + j) // pb), pb=per_block)))
            out_specs.append(pl.BlockSpec(
                (rows, width), functools.partial(lambda i, j, pb: (0, (i * nf + j) // pb), pb=per_block)))
            out_shape.append(jax.ShapeDtypeStruct(w.shape[2:], BF16))
            args.append(w)
    outs = pl.pallas_call(
        functools.partial(_ffn_body, res_w=res_w, cast_plans=plans),
        grid=(m // bm, nf),
        in_specs=in_specs,
        out_specs=out_specs,
        out_shape=out_shape,
        scratch_shapes=[pltpu.VMEM((bm, d), BF16), pltpu.VMEM((bm, d), F32)],
        compiler_params=_cparams("arbitrary", "arbitrary"),
        name="ffn",
    )(*args)
    return outs if cast_next is not None else outs[0]


def _proj_in_body(h_ref, vec_ref, w_ref, b_ref, *rest, rope_blocks, q_blocks, q_scale):
    if rope_blocks:
        cos_ref, sa_ref, sb_ref, o_ref, u_ref = rest
    else:
        o_ref, u_ref = rest
    j = pl.program_id(1)

    @pl.when(j == 0)
    def _():
        _store_modulated(h_ref, vec_ref, u_ref)

    y = jnp.dot(u_ref[...], w_ref[...], preferred_element_type=F32) + b_ref[...]
    if not rope_blocks:
        o_ref[...] = y.astype(o_ref.dtype)
        return

    rot = jnp.where(j < q_blocks, q_scale, jnp.where(j < rope_blocks, 1.0, 0.0)).astype(F32)
    keep = jnp.where(j < rope_blocks, 0.0, 1.0).astype(F32)
    cos, sa, sb = cos_ref[...] * rot + keep, sa_ref[...] * rot, sb_ref[...] * rot
    for hd in range(y.shape[1] // HEAD_DIM):
        cols = slice(hd * HEAD_DIM, (hd + 1) * HEAD_DIM)
        xh = y[:, cols]
        r = (xh * cos + pltpu.roll(xh, HEAD_DIM - HEAD_DIM // 4, 1) * sa
             + pltpu.roll(xh, HEAD_DIM // 4, 1) * sb)
        o_ref[:, cols] = r.astype(o_ref.dtype)


def _proj_in(h, vecs, w, b, out_dtype, rope=None, q_blocks=0, q_scale=1.0):
    m, d = h.shape
    n = w.shape[1]
    bm = min(m, 1024)
    bn = 512
    in_specs = [pl.BlockSpec((bm, d), lambda i, j: (i, 0)),
                pl.BlockSpec((SUBLANES, d), lambda i, j: (0, 0)),
                pl.BlockSpec((d, bn), lambda i, j: (0, j)),
                pl.BlockSpec((1, bn), lambda i, j: (0, j))]
    args = [h, vecs, w, b.reshape(1, n)]
    rope_blocks = 0
    if rope is not None:
        rope_blocks = q_blocks + 1
        in_specs += [pl.BlockSpec((bm, HEAD_DIM), lambda i, j: (i, 0))] * 3
        args += list(rope)
    return pl.pallas_call(
        functools.partial(_proj_in_body, rope_blocks=rope_blocks, q_blocks=q_blocks, q_scale=q_scale),
        grid=(m // bm, n // bn),
        in_specs=in_specs,
        out_specs=pl.BlockSpec((bm, bn), lambda i, j: (i, j)),
        out_shape=jax.ShapeDtypeStruct((m, n), out_dtype),
        scratch_shapes=[pltpu.VMEM((bm, d), BF16)],
        compiler_params=_cparams("arbitrary", "arbitrary"),
        name="proj_in",
    )(*args)


def _proj_out_body(a_ref, w_ref, b_ref, h_ref, vec_ref, o_ref):
    y = jnp.dot(a_ref[...], w_ref[...], preferred_element_type=F32) + b_ref[...]
    o_ref[...] = h_ref[...] + vec_ref[2:3, :] * _rms(y, vec_ref[4:5, :])


def _proj_out(a, w, b, h, vecs):
    m, d = h.shape
    k = a.shape[1]
    bm = min(m, 256)
    return pl.pallas_call(
        _proj_out_body,
        grid=(m // bm,),
        in_specs=[pl.BlockSpec((bm, k), lambda i: (i, 0)),
                  pl.BlockSpec((k, d), lambda i: (0, 0)),
                  pl.BlockSpec((1, d), lambda i: (0, 0)),
                  pl.BlockSpec((bm, d), lambda i: (i, 0)),
                  pl.BlockSpec((SUBLANES, d), lambda i: (0, 0))],
        out_specs=pl.BlockSpec((bm, d), lambda i: (i, 0)),
        out_shape=jax.ShapeDtypeStruct((m, d), F32),
        compiler_params=_cparams("arbitrary"),
        name="proj_out",
    )(a, w, b.reshape(1, d), h, vecs)


def _hy_in_body(h_ref, hp_ref, hn_ref, vec_ref, w0_ref, w1_ref, w2_ref, b0_ref, b1_ref, b2_ref, wsc_ref, bsc_ref,
                x0_out, xin_out, u_ref, uh_ref):
    i = pl.program_id(0)
    last = pl.num_programs(0) - 1

    @pl.when(pl.program_id(1) == 0)
    def _():
        _store_modulated(h_ref, vec_ref, u_ref)
        uh_ref[0:SUBLANES, :] = hp_ref[...]
        uh_ref[SUBLANES:, :] = hn_ref[...]
        _store_modulated(uh_ref, vec_ref, uh_ref)

    def conv(w_ref, b_ref, part):
        z = jnp.dot(u_ref[...], w_ref[...], preferred_element_type=F32) + b_ref[...]
        zh = jnp.dot(uh_ref[...].astype(BF16), w_ref[...], preferred_element_type=F32) + b_ref[...]
        bm = z.shape[0]
        prev_row = jnp.where(i > 0, zh[SUBLANES - 1:SUBLANES, :], 0.0)
        next_row = jnp.where(i < last, zh[SUBLANES:SUBLANES + 1, :], 0.0)
        rid = lax.broadcasted_iota(jnp.int32, z.shape, 0)
        up = jnp.where(rid == 0, prev_row, pltpu.roll(z, 1, 0))
        dn = jnp.where(rid == bm - 1, next_row, pltpu.roll(z, bm - 1, 0))
        return (up * wsc_ref[0, part:part + 1, :] + z * wsc_ref[1, part:part + 1, :]
                + dn * wsc_ref[2, part:part + 1, :] + bsc_ref[part:part + 1, :])

    x0_out[...] = conv(w0_ref, b0_ref, 0).astype(x0_out.dtype)
    xin_out[...] = (conv(w2_ref, b2_ref, 2) * conv(w1_ref, b1_ref, 1)).astype(xin_out.dtype)


def _hy_in_gate(h, vecs, w_in, b_in, w_sc, b_sc):
    l, d = h.shape
    bm = min(l, 1024)
    bc = min(d, 512)
    nc = d // bc
    rb = bm // SUBLANES
    nrow8 = l // SUBLANES
    wspecs = [pl.BlockSpec((d, bc), functools.partial(lambda i, j, p: (0, p * nc + j), p=part)) for part in range(3)]
    bspecs = [pl.BlockSpec((1, bc), functools.partial(lambda i, j, p: (0, p * nc + j), p=part)) for part in range(3)]
    out_spec = pl.BlockSpec((bm, bc), lambda i, j: (i, j))
    return pl.pallas_call(
        _hy_in_body,
        grid=(l // bm, nc),
        in_specs=[pl.BlockSpec((bm, d), lambda i, j: (i, 0)),
                  pl.BlockSpec((SUBLANES, d), lambda i, j: (jnp.maximum(i * rb - 1, 0), 0)),
                  pl.BlockSpec((SUBLANES, d), lambda i, j: (jnp.minimum((i + 1) * rb, nrow8 - 1), 0)),
                  pl.BlockSpec((SUBLANES, d), lambda i, j: (0, 0))] + wspecs + bspecs
                 + [pl.BlockSpec((3, 3, bc), lambda i, j: (0, 0, j)),
                    pl.BlockSpec((3, bc), lambda i, j: (0, j))],
        out_specs=[out_spec, out_spec],
        out_shape=[jax.ShapeDtypeStruct((l, d), BF16)] * 2,
        scratch_shapes=[pltpu.VMEM((bm, d), BF16), pltpu.VMEM((2 * SUBLANES, d), F32)],
        compiler_params=_cparams("arbitrary", "arbitrary"),
        name="hyena_in_gate",
    )(h, h, h, vecs, w_in, w_in, w_in, *([b_in.reshape(1, 3 * d)] * 3), w_sc.reshape(3, 3, d), b_sc.reshape(3, d))


def _filter_mlp_body(z_ref, w1, b1, w2, b2, w3, b3, fr, o_ref):
    hp = lax.Precision.HIGHEST
    f = jnp.sin(fr[0:1, :] * (jnp.dot(z_ref[...], w1[...], precision=hp, preferred_element_type=F32) + b1[...]))
    f = jnp.sin(fr[1:2, :] * (jnp.dot(f, w2[...], precision=hp, preferred_element_type=F32) + b2[...]))
    o_ref[...] = jnp.sin(fr[2:3, :] * (jnp.dot(f, w3[...], precision=hp, preferred_element_type=F32) + b3[...]))


def _filter_table(l, w1, b1, w2, b2, w3, b3, freq):
    hid = w1.shape[1]
    t = jnp.linspace(0.0, 1.0, l, dtype=F32)[:, None]
    omega = 2.0 * math.pi * jnp.arange(l, dtype=F32)[:, None] / l
    bands = jnp.linspace(1e-4, FILTER_BANDS - 1, FILTER_BANDS, dtype=F32)[None, :]
    emb = jnp.concatenate([t, jnp.cos(bands * omega), -jnp.sin(bands * omega)], axis=-1)
    emb = jnp.pad(emb, ((0, 0), (0, LANES - emb.shape[1])))
    w1p = jnp.pad(w1, ((0, LANES - w1.shape[0]), (0, 0)))
    pack = LANES // hid
    rows = l // pack
    eye = jnp.eye(pack, dtype=F32)
    emb_p = jnp.concatenate([emb[p * rows:(p + 1) * rows] for p in range(pack)], axis=1)
    wide = pack * hid
    bm = min(rows, 512)

    def full(shape):
        return pl.BlockSpec(shape, lambda i: (0,) * len(shape))

    f_p = pl.pallas_call(
        _filter_mlp_body,
        grid=(rows // bm,),
        in_specs=[pl.BlockSpec((bm, pack * LANES), lambda i: (i, 0)),
                  full((pack * LANES, wide)), full((1, wide)), full((wide, wide)), full((1, wide)),
                  full((wide, wide)), full((1, wide)), full((3, wide))],
        out_specs=pl.BlockSpec((bm, wide), lambda i: (i, 0)),
        out_shape=jax.ShapeDtypeStruct((rows, wide), F32),
        compiler_params=_cparams("arbitrary"),
        name="hyena_filter_mlp",
    )(emb_p, jnp.kron(eye, w1p), jnp.tile(b1, pack).reshape(1, wide), jnp.kron(eye, w2),
      jnp.tile(b2, pack).reshape(1, wide), jnp.kron(eye, w3), jnp.tile(b3, pack).reshape(1, wide),
      jnp.tile(freq, (1, pack)))
    f = jnp.concatenate([f_p[:, p * hid:(p + 1) * hid] for p in range(pack)], axis=0)
    tab = jnp.concatenate([f, t, jnp.ones((l, 1), F32)], axis=-1)
    tab = jnp.pad(tab, ((0, 0), (0, LANES - tab.shape[1])))
    return jnp.concatenate([tab, jnp.zeros((1, LANES), F32), tab[1:][::-1]], axis=0)


def _decay_rates(d):
    return jnp.abs(jnp.linspace(math.log(DECAY_TARGET) / LONG_DECAY_PCT,
                                math.log(DECAY_TARGET) / SHORT_DECAY_PCT, d, dtype=F32))[None, :]


def _split_bf16(x):
    hi = x.astype(BF16)
    return hi, (x - hi.astype(F32)).astype(BF16)


def _stack_3pass(w):
    w_hi, w_lo = _split_bf16(w)
    return jnp.concatenate([w_hi, w_hi, w_lo], axis=0)


def _dot_3pass(f, w3):
    f_hi, f_lo = _split_bf16(f)
    return jnp.dot(jnp.concatenate([f_hi, f_lo, f_hi], axis=1), w3, preferred_element_type=F32)


def _filter_rows(ft, wa3, wb3, dl_ref):
    hid = wa3.shape[0] // 3
    half = ft.shape[0] // 2
    f, t, keep = ft[:, :hid], ft[:, hid:hid + 1], ft[:, hid + 1:hid + 2]
    y = jnp.concatenate([_dot_3pass(f[:half], wa3), _dot_3pass(f[half:], wb3)], axis=0)
    return y * (jnp.exp(-t * dl_ref[...]) * keep)


def _outer_stage(fa_ref, cols, o_ref):
    ys = [jnp.dot(fa_ref[...], xj, preferred_element_type=F32).astype(BF16) for xj in cols]
    o_ref[...] = pltpu.einshape("brd->rbd", jnp.stack(ys))


def _dft_a_body(x_ref, fa_ref, o_ref):
    xb = pltpu.einshape("abd->bad", x_ref[...])
    _outer_stage(fa_ref, [xb[j] for j in range(xb.shape[0])], o_ref)


def _dft_a_filter_body(ft_ref, wa_ref, wb_ref, dl_ref, fa_ref, o_ref):
    wa3, wb3 = _stack_3pass(wa_ref[...]), _stack_3pass(wb_ref[...])
    cols = [_filter_rows(ft_ref[j], wa3, wb3, dl_ref).astype(BF16) for j in range(ft_ref.shape[0])]
    _outer_stage(fa_ref, cols, o_ref)


def _dft_a(x3, fa, d_chunk):
    a, b, d = x3.shape
    rows = fa.shape[0]
    return pl.pallas_call(
        _dft_a_body,
        grid=(b // HALO, d // d_chunk),
        in_specs=[pl.BlockSpec((a, HALO, d_chunk), lambda i, j: (0, i, j)),
                  pl.BlockSpec((rows, a), lambda i, j: (0, 0))],
        out_specs=pl.BlockSpec((rows, HALO, d_chunk), lambda i, j: (0, i, j)),
        out_shape=jax.ShapeDtypeStruct((rows, b, d), BF16),
        compiler_params=_cparams("arbitrary", "arbitrary"),
        name="dft_outer",
    )(x3, fa)


def _dft_a_filter(ft3, w4, deltas, fa, d_chunk):
    b, n1, _ = ft3.shape
    hid, d2 = w4.shape
    d = d2 // 2
    rows = fa.shape[0]
    nd = d // d_chunk
    return pl.pallas_call(
        _dft_a_filter_body,
        grid=(b // HALO, nd),
        in_specs=[pl.BlockSpec((HALO, n1, LANES), lambda i, j: (i, 0, 0)),
                  pl.BlockSpec((hid, d_chunk), lambda i, j: (0, j)),
                  pl.BlockSpec((hid, d_chunk), lambda i, j: (0, j + nd)),
                  pl.BlockSpec((1, d_chunk), lambda i, j: (0, j)),
                  pl.BlockSpec((rows, n1), lambda i, j: (0, 0))],
        out_specs=pl.BlockSpec((rows, HALO, d_chunk), lambda i, j: (0, i, j)),
        out_shape=jax.ShapeDtypeStruct((rows, b, d), BF16),
        compiler_params=_cparams("arbitrary", "arbitrary"),
        name="dft_outer_filter",
    )(ft3, w4, w4, deltas, fa)


def _dft_inner_body(y_ref, yf_ref, gf_ref, gi_ref, o_ref):
    b, dc = y_ref.shape[2], y_ref.shape[3]
    for ci in range(y_ref.shape[1]):
        z = jnp.dot(gf_ref[ci], y_ref[:, ci].reshape(2 * b, dc), preferred_element_type=F32)
        hh = jnp.dot(gf_ref[ci], yf_ref[:, ci].reshape(2 * b, dc), preferred_element_type=F32)
        zr, zi, hr, hi = z[:b], z[b:], hh[:b], hh[b:]
        p = jnp.concatenate([zr * hr - zi * hi, zr * hi + zi * hr], axis=0).astype(BF16)
        cc = jnp.dot(gi_ref[ci], p, preferred_element_type=F32)
        o_ref[:, ci] = cc.reshape(2, b, dc).astype(o_ref.dtype)


def _dft_inner(y4, yf4, gf, gi):
    _, nc, b, d = y4.shape
    cblk = SUBLANES
    dc = min(d, 1024)
    yspec = pl.BlockSpec((2, cblk, b, dc), lambda c, j: (0, c, 0, j))
    gspec = pl.BlockSpec((cblk, 2 * b, 2 * b), lambda c, j: (c, 0, 0))
    return pl.pallas_call(
        _dft_inner_body,
        grid=(nc // cblk, d // dc),
        in_specs=[yspec, yspec, gspec, gspec],
        out_specs=yspec,
        out_shape=jax.ShapeDtypeStruct(y4.shape, BF16),
        compiler_params=_cparams("arbitrary", "arbitrary"),
        name="dft_inner",
    )(y4, yf4, gf, gi)


def _dft_c_body(c_ref, m_ref, xin_ref, x0_ref, skip_ref, o_ref):
    cb = pltpu.einshape("rbd->brd", c_ref[...])
    outs = [jnp.dot(m_ref[...], cb[j], preferred_element_type=F32).astype(BF16) for j in range(cb.shape[0])]
    conv = pltpu.einshape("bad->abd", jnp.stack(outs)).astype(F32)
    gated = x0_ref[...].astype(F32) * (conv + xin_ref[...].astype(F32) * skip_ref[...][None])
    o_ref[...] = gated.astype(o_ref.dtype)


def _dft_c(c3, mc, xin3, x03, skip, d_chunk):
    rows, b, d = c3.shape
    a = mc.shape[0]
    xspec = pl.BlockSpec((a, HALO, d_chunk), lambda i, j: (0, i, j))
    return pl.pallas_call(
        _dft_c_body,
        grid=(b // HALO, d // d_chunk),
        in_specs=[pl.BlockSpec((rows, HALO, d_chunk), lambda i, j: (0, i, j)),
                  pl.BlockSpec((a, rows), lambda i, j: (0, 0)),
                  xspec, xspec,
                  pl.BlockSpec((1, d_chunk), lambda i, j: (0, j))],
        out_specs=xspec,
        out_shape=jax.ShapeDtypeStruct((a, b, d), BF16),
        compiler_params=_cparams("arbitrary", "arbitrary"),
        name="dft_outer_inv",
    )(c3, mc, xin3, x03, skip)


def _cis(num, period):
    ang = (2.0 * math.pi / period) * num.astype(F32)
    return jnp.cos(ang), jnp.sin(ang)


def _long_conv_gated(xin, x0, ftab, w4, skip):
    l, d = xin.shape
    b = DFT_INNER
    a = l // b
    n1, n = 2 * a, 2 * l
    nc = n1 // 2 + SUBLANES
    dc = min(d, 512)
    ci = jnp.arange(nc, dtype=jnp.int32)
    ai = jnp.arange(n1, dtype=jnp.int32)
    live = (ci <= n1 // 2).astype(F32)[:, None]
    cr, sr = _cis((ci[:, None] * ai[None, :]) % n1, n1)
    cr, sr = cr * live, sr * live
    fa_full = jnp.concatenate([cr, -sr], axis=0).astype(BF16)
    fa = fa_full[:, :a]
    fold = jnp.where((ci == 0) | (ci == n1 // 2), 1.0, 2.0)[None, :] / n
    mc = jnp.concatenate([cr[:, :a].T * fold, -sr[:, :a].T * fold], axis=1).astype(BF16)
    ei = jnp.arange(b, dtype=jnp.int32)
    kk = ci[:, None, None] + n1 * ei[None, :, None]
    tr, ts = _cis((kk * ei[None, None, :]) % n, n)
    ti = -ts
    gf = jnp.concatenate([jnp.concatenate([tr, -ti], axis=2),
                          jnp.concatenate([ti, tr], axis=2)], axis=1).astype(BF16)
    trt, tit = jnp.swapaxes(tr, 1, 2), jnp.swapaxes(ti, 1, 2)
    gi = jnp.concatenate([jnp.concatenate([trt, tit], axis=2),
                          jnp.concatenate([-tit, trt], axis=2)], axis=1).astype(BF16)

    ft3 = jnp.swapaxes(ftab.reshape(n1, b, LANES), 0, 1)
    hf = _dft_a_filter(ft3, w4, _decay_rates(d), fa_full, dc)
    yx = _dft_a(xin.reshape(a, b, d), fa, dc)
    c4 = _dft_inner(yx.reshape(2, nc, b, d), hf.reshape(2, nc, b, d), gf, gi)
    y3 = _dft_c(c4.reshape(2 * nc, b, d), mc, xin.reshape(a, b, d), x0.reshape(a, b, d), skip.reshape(1, d), dc)
    return y3.reshape(l, d)


def _ctx_conv_body(x_ref, x0_ref, ft_ref, wa_ref, wb_ref, dl_ref, skip_ref, ff_ref, fh_ref, mi_ref, o_ref):
    x = x_ref[...].astype(F32)
    n = ft_ref.shape[0]
    filt = _filter_rows(ft_ref[...], _stack_3pass(wa_ref[...]), _stack_3pass(wb_ref[...]), dl_ref)
    xs = jnp.dot(ff_ref[...], x.astype(BF16), preferred_element_type=F32)
    hs = jnp.dot(fh_ref[...], filt.astype(BF16), preferred_element_type=F32)
    xr, xi, hr, hi = xs[:n], xs[n:], hs[:n], hs[n:]
    p = jnp.concatenate([xr * hr - xi * hi, xr * hi + xi * hr], axis=0).astype(BF16)
    conv = jnp.dot(mi_ref[...], p, preferred_element_type=F32)
    o_ref[...] = (x0_ref[...].astype(F32) * (conv + x * skip_ref[...])).astype(o_ref.dtype)


def _short_seq_conv_gated(xin, x0, ftab, w4, skip):
    l, d = xin.shape
    n = 2 * l
    hid = w4.shape[0]
    ni = jnp.arange(n, dtype=jnp.int32)
    cr, sr = _cis((ni[:, None] * ni[None, :]) % n, n)
    fh = jnp.concatenate([cr, -sr], axis=0).astype(BF16)
    ff = fh[:, :l]
    mi = (jnp.concatenate([cr[:l], -sr[:l]], axis=1) * (1.0 / n)).astype(BF16)
    dc = min(d, 512)
    nd = d // dc
    return pl.pallas_call(
        _ctx_conv_body,
        grid=(nd,),
        in_specs=[pl.BlockSpec((l, dc), lambda j: (0, j)),
                  pl.BlockSpec((l, dc), lambda j: (0, j)),
                  pl.BlockSpec((n, LANES), lambda j: (0, 0)),
                  pl.BlockSpec((hid, dc), lambda j: (0, j)),
                  pl.BlockSpec((hid, dc), lambda j: (0, j + nd)),
                  pl.BlockSpec((1, dc), lambda j: (0, j)),
                  pl.BlockSpec((1, dc), lambda j: (0, j)),
                  pl.BlockSpec((2 * n, l), lambda j: (0, 0)),
                  pl.BlockSpec((2 * n, n), lambda j: (0, 0)),
                  pl.BlockSpec((l, 2 * n), lambda j: (0, 0))],
        out_specs=pl.BlockSpec((l, dc), lambda j: (0, j)),
        out_shape=jax.ShapeDtypeStruct((l, d), BF16),
        compiler_params=_cparams("arbitrary"),
        name="ctx_conv",
    )(xin, x0, ftab, w4, w4, _decay_rates(d), skip.reshape(1, d), ff, fh, mi)


def _hyena_mixer(h, vecs, p, long_seq):
    w_in, b_in, w_sc, b_sc, f_w1, f_b1, f_w2, f_b2, f_w3, f_b3, f_w4, f_freq, skip, w_out, b_out = p
    l = h.shape[0]
    x0, xin = _hy_in_gate(h, vecs, w_in, b_in, w_sc, b_sc)
    ftab = _filter_table(l, f_w1, f_b1, f_w2, f_b2, f_w3, f_b3, f_freq)
    conv = _long_conv_gated if long_seq else _short_seq_conv_gated
    y = conv(xin, x0, ftab, f_w4, skip)
    return _proj_out(y, w_out, b_out, h, vecs)


def _attn_body(sink_ref, q_ref, kp_ref, kc_ref, kn_ref, vp_ref, vc_ref, vn_ref, kx_ref, vx_ref, bias_ref, o_ref,
               *, group):
    blk = q_ref.shape[0]
    bias = bias_ref[0]
    rid = lax.broadcasted_iota(jnp.int32, (group * blk, 1), 0)
    for kh in range(N_KV_HEADS):
        hs = slice(kh * HEAD_DIM, (kh + 1) * HEAD_DIM)
        heads = [kh * group + g for g in range(group)]
        q = jnp.concatenate([q_ref[:, hd * HEAD_DIM:(hd + 1) * HEAD_DIM] for hd in heads], axis=0)
        keys = jnp.concatenate([kp_ref[:, hs], kc_ref[:, hs], kn_ref[:, hs], kx_ref[:, hs]], axis=0)
        vals = jnp.concatenate([vp_ref[:, hs], vc_ref[:, hs], vn_ref[:, hs], vx_ref[:, hs]], axis=0)
        s = lax.dot_general(q, keys, (((1,), (1,)), ((), ())), preferred_element_type=F32) + bias
        sink = jnp.zeros((group * blk, 1), F32)
        for g, hd in enumerate(heads):
            sink = jnp.where((rid >= g * blk) & (rid < (g + 1) * blk), sink_ref[hd], sink)
        mx = jnp.maximum(jnp.max(s, axis=-1, keepdims=True), sink)
        pr = jnp.exp(s - mx)
        denom = jnp.sum(pr, axis=-1, keepdims=True) + jnp.exp(sink - mx)
        o = jnp.dot(pr.astype(BF16), vals, preferred_element_type=F32) / denom
        for g, hd in enumerate(heads):
            o_ref[:, hd * HEAD_DIM:(hd + 1) * HEAD_DIM] = o[g * blk:(g + 1) * blk].astype(o_ref.dtype)


def _attention(qkv, kvc, sink, d):
    l = qkv.shape[0]
    c = kvc.shape[0]
    blk = ATTN_BLOCK
    nb = l // blk
    group = d // HEAD_DIM // N_KV_HEADS
    kvw = N_KV_HEADS * HEAD_DIM
    kcol = d // kvw
    qi = jnp.arange(group * blk, dtype=jnp.int32)[:, None] % blk
    si = jnp.arange(3 * blk + c, dtype=jnp.int32)[None, :]
    kb, ki = si // blk, si % blk
    prev_ok, next_ok = (kb == 0) & (ki >= qi), (kb == 2) & (ki <= qi)
    always = (kb == 1) | (kb >= 3)
    variants = [always | next_ok, always | prev_ok | next_ok, always | prev_ok]
    bias = jnp.stack([jnp.where(v, 0.0, MASK_BIAS).astype(F32) for v in variants])

    def kv_spec(col, shift):
        return pl.BlockSpec((blk, kvw), lambda n: (jnp.clip(n + shift, 0, nb - 1), col))

    return pl.pallas_call(
        functools.partial(_attn_body, group=group),
        grid=(nb,),
        in_specs=[pl.BlockSpec(memory_space=pltpu.SMEM),
                  pl.BlockSpec((blk, d), lambda n: (n, 0)),
                  kv_spec(kcol, -1), kv_spec(kcol, 0), kv_spec(kcol, 1),
                  kv_spec(kcol + 1, -1), kv_spec(kcol + 1, 0), kv_spec(kcol + 1, 1),
                  pl.BlockSpec((c, kvw), lambda n: (0, 0)),
                  pl.BlockSpec((c, kvw), lambda n: (0, 1)),
                  pl.BlockSpec((1, group * blk, 3 * blk + c),
                               lambda n: (jnp.where(n == 0, 0, jnp.where(n == nb - 1, 2, 1)), 0, 0))],
        out_specs=pl.BlockSpec((blk, d), lambda n: (n, 0)),
        out_shape=jax.ShapeDtypeStruct((l, d), BF16),
        compiler_params=_cparams("arbitrary"),
        name="window_attn",
    )(sink, qkv, qkv, qkv, qkv, qkv, qkv, qkv, kvc, kvc, bias)


def _rope_tables(l):
    rows = l // GRID_W
    pos_row = jnp.broadcast_to(jnp.arange(rows)[:, None], (rows, GRID_W)).reshape(-1).astype(F32)
    pos_col = jnp.broadcast_to(jnp.arange(GRID_W)[None, :], (rows, GRID_W)).reshape(-1).astype(F32)
    pairs = HEAD_DIM // 4
    inv_freq = ROPE_THETA ** (-jnp.arange(pairs, dtype=F32) / pairs)
    ang_row = pos_row[:, None] * inv_freq[None, :]
    ang_col = pos_col[:, None] * inv_freq[None, :]
    zeros = jnp.zeros_like(ang_row)
    cos = jnp.concatenate([jnp.cos(ang_row)] * 2 + [jnp.cos(ang_col)] * 2, axis=-1)
    sin_a = jnp.concatenate([-jnp.sin(ang_row), zeros, -jnp.sin(ang_col), zeros], axis=-1)
    sin_b = jnp.concatenate([zeros, jnp.sin(ang_row), zeros, jnp.sin(ang_col)], axis=-1)
    return cos, sin_a, sin_b


def _pool_body(hc_ref, hp_ref, hn_ref, vec_ref, w_ref, b_ref, sc_ref, o_ref, y_ref, *, seq_len):
    i = pl.program_id(0)
    last = pl.num_programs(0) - 1
    bm, d = hc_ref.shape
    gw = d // len(POOL_SIZES)
    h = hc_ref[...]
    u = _modulated(h, vec_ref)
    up = jnp.where(i > 0, _modulated(hp_ref[...], vec_ref), 0.0)
    un = jnp.where(i < last, _modulated(hn_ref[...], vec_ref), 0.0)
    ext_rows = bm + 2 * SUBLANES
    t = i * bm + lax.broadcasted_iota(jnp.int32, (bm, 1), 0)
    for g, size in enumerate(POOL_SIZES):
        cols = slice(g * gw, (g + 1) * gw)
        ext = jnp.concatenate([up[:, cols], u[:, cols], un[:, cols]], axis=0)
        acc, span = ext, 1
        while span < size:
            acc = acc + pltpu.roll(acc, ext_rows - span, 0)
            span *= 2
        start = SUBLANES - size // 2
        win = pltpu.roll(acc, ext_rows - start, 0)[:bm] if start else acc[:bm]
        lo = jnp.clip(t - size // 2, 0, seq_len)
        hi = jnp.clip(t - size // 2 + size, 0, seq_len)
        part = win / (hi - lo).astype(F32) - u[:, cols]
        yg = jnp.dot(part.astype(BF16), w_ref[g], preferred_element_type=F32)
        y_ref[:, cols] = (yg + b_ref[:, cols]) * sc_ref[:, cols]
    o_ref[...] = h + vec_ref[2:3, :] * _rms(y_ref[...], vec_ref[4:5, :])


def _pool_mixer(h, vecs, w, b, scale):
    l, d = h.shape
    bm = min(l, 256)
    rb = bm // SUBLANES
    nrow8 = l // SUBLANES
    ng, gw = w.shape[0], w.shape[1]
    return pl.pallas_call(
        functools.partial(_pool_body, seq_len=l),
        grid=(l // bm,),
        in_specs=[pl.BlockSpec((bm, d), lambda i: (i, 0)),
                  pl.BlockSpec((SUBLANES, d), lambda i: (jnp.maximum(i * rb - 1, 0), 0)),
                  pl.BlockSpec((SUBLANES, d), lambda i: (jnp.minimum((i + 1) * rb, nrow8 - 1), 0)),
                  pl.BlockSpec((SUBLANES, d), lambda i: (0, 0)),
                  pl.BlockSpec((ng, gw, gw), lambda i: (0, 0, 0)),
                  pl.BlockSpec((1, d), lambda i: (0, 0)),
                  pl.BlockSpec((1, d), lambda i: (0, 0))],
        out_specs=pl.BlockSpec((bm, d), lambda i: (i, 0)),
        out_shape=jax.ShapeDtypeStruct((l, d), F32),
        scratch_shapes=[pltpu.VMEM((bm, d), F32)],
        compiler_params=_cparams("arbitrary"),
        name="pool_mixer",
    )(h, h, h, vecs, w.astype(BF16), b.reshape(1, d), scale.reshape(1, d))


def _sub_vecs(mod, k, g_pre, g_post):
    rows = [mod[3 * k], mod[3 * k + 1], mod[3 * k + 2], g_pre, g_post]
    return jnp.stack(rows + [jnp.zeros_like(g_pre)] * (SUBLANES - len(rows)))


def kernel(x, c, ctx, c_ctx, w_ada, b_ada, norm_pre, norm_post, w_ffn_in, w_ffn_out, hy_w_in, hy_b_in, hy_w_sc, hy_b_sc, hy_f_w1, hy_f_b1, hy_f_w2, hy_f_b2, hy_f_w3, hy_f_b3, hy_f_w4, hy_f_freq, hy_skip, hy_w_out, hy_b_out, at_w_qkv, at_b_qkv, at_sink, at_w_o, at_b_o, pl_w, pl_b, pl_scale):
    bsz, l, d = x.shape
    assert bsz == 1, "kernel handles a single batch element"
    assert d % 1024 == 0 and l % 1024 == 0 and ctx.shape[1] % 256 == 0, "block sizes assume these multiples"
    assert w_ffn_out.shape[2] % 512 == 0, "FFN width must be a multiple of the 512-column weight blocks"
    depth = w_ada.shape[0]
    n_mixers = 3
    attn_layers = [i for i in range(depth) if i % n_mixers == 1]
    last_ctx_layer = attn_layers[-1] if attn_layers else -1

    mods = _adaln_mods(c, c_ctx, w_ada, b_ada).reshape(depth, 2, N_MOD, d)
    rope = _rope_tables(l)
    w_next = [w_ffn_in[0, 0].astype(BF16), w_ffn_out[0, 0].astype(BF16)]
    h, hc = x[0], ctx[0]
    for i in range(depth):
        kind, j = i % n_mixers, i // n_mixers
        ctx_live = i <= last_ctx_layer
        ctx_out = i < last_ctx_layer
        vec = [_sub_vecs(mods[i, 0], k, norm_pre[i, k], norm_post[i, k]) for k in range(3)]
        vec_c = [_sub_vecs(mods[i, 1], k, norm_pre[i, k], norm_post[i, k]) for k in range(3)]

        w_now = w_next
        h, *w_next = _ffn(h, vec[0], *w_now, FFN_RES, cast_next=(w_ffn_in, w_ffn_out, i, 1))
        if ctx_live:
            hc = _ffn(hc, vec_c[0], *w_now, FFN_RES)

        if kind == 0:
            hp = (hy_w_in[j].astype(BF16), hy_b_in[j], hy_w_sc[j], hy_b_sc[j], hy_f_w1[j], hy_f_b1[j],
                  hy_f_w2[j], hy_f_b2[j], hy_f_w3[j], hy_f_b3[j], hy_f_w4[j], hy_f_freq[j], hy_skip[j],
                  hy_w_out[j].astype(BF16), hy_b_out[j])
            h = _hyena_mixer(h, vec[1], hp, long_seq=True)
            if ctx_out:
                hc = _hyena_mixer(hc, vec_c[1], hp, long_seq=False)
        elif kind == 1:
            w_qkv = at_w_qkv[j].astype(BF16)
            qkv = _proj_in(h, vec[1], w_qkv, at_b_qkv[j], BF16, rope=rope,
                           q_blocks=d // 512, q_scale=HEAD_DIM ** -0.5)
            kvc = _proj_in(hc, vec_c[1], w_qkv[:, d:], at_b_qkv[j][d:], BF16)
            o = _attention(qkv, kvc, at_sink[j], d)
            h = _proj_out(o, at_w_o[j].astype(BF16), at_b_o[j], h, vec[1])
            assert not ctx_out, "context-query attention path is not needed for this depth"
        else:
            h = _pool_mixer(h, vec[1], pl_w[j], pl_b[j], pl_scale[j])
            assert not ctx_out, "context pooling path is not needed for this depth"

        w_now = w_next
        if i + 1 < depth:
            h, *w_next = _ffn(h, vec[2], *w_now, FFN_RES, cast_next=(w_ffn_in, w_ffn_out, i + 1, 0))
        else:
            h = _ffn(h, vec[2], *w_now, FFN_RES)
        if ctx_out:
            hc = _ffn(hc, vec_c[2], *w_now, FFN_RES)
    return h[None]
```

```python
import functools
import math

import jax
import jax.numpy as jnp
from jax import lax
from jax.experimental import pallas as pl
from jax.experimental.pallas import tpu as pltpu

F32 = jnp.float32
BF16 = jnp.bfloat16

GRID_W = 64
N_MOD = 9
NORM_EPS = 1e-6
FFN_RES = 0.5
FILTER_BANDS = 16
FILTER_EMB = 1 + 2 * FILTER_BANDS
DECAY_TARGET = 1e-2
SHORT_DECAY_PCT = 0.3
LONG_DECAY_PCT = 1.5
HEAD_DIM = 128
N_KV_HEADS = 4
WINDOW = 128
ATTN_BLOCK = 128
ROPE_THETA = 10000.0
POOL_SIZES = (2, 4, 8, 16)

VMEM_LIMIT_BYTES = 56 * 1024 * 1024
SUBLANES = 8
LANES = 128
DFT_INNER = 128
HALO = 16
FFN_CHUNK = 512
MASK_BIAS = -1e30


def _cparams(*sem):
    return pltpu.CompilerParams(dimension_semantics=sem, vmem_limit_bytes=VMEM_LIMIT_BYTES)


def _rms(x, g):
    ms = jnp.mean(x * x, axis=-1, keepdims=True)
    return x * lax.rsqrt(ms + NORM_EPS) * g


def _modulated(h, vec_ref):
    return _rms(h, vec_ref[3:4, :]) * (1.0 + vec_ref[1:2, :]) + vec_ref[0:1, :]


def _store_modulated(h_ref, vec_ref, u_ref):
    gain = vec_ref[3:4, :] * (1.0 + vec_ref[1:2, :])
    u_ref[...] = (_rms(h_ref[...], gain) + vec_ref[0:1, :]).astype(u_ref.dtype)


def _mods_body(cb_ref, w_ref, b_ref, o_ref, s_ref):
    k_dim, bn = w_ref.shape[1], w_ref.shape[2]
    nl = bn // LANES

    @pl.when((pl.program_id(0) == 0) & (pl.program_id(1) == 0))
    def _():
        cv = cb_ref[...]
        s_ref[...] = cv * jax.nn.sigmoid(cv)

    def step(kg, acc):
        rows = pl.ds(pl.multiple_of(kg * SUBLANES, SUBLANES), SUBLANES)
        s = [s_ref[r, rows, :] for r in range(2)]
        new = list(acc)
        for j in range(nl):
            wv = w_ref[0, rows, j * LANES:(j + 1) * LANES]
            for r in range(2):
                new[r * nl + j] = acc[r * nl + j] + wv * s[r]
        return tuple(new)

    init = tuple(jnp.zeros((SUBLANES, LANES), F32) for _ in range(2 * nl))
    acc = lax.fori_loop(0, k_dim // SUBLANES, step, init, unroll=4)
    for r in range(2):
        for j in range(nl):
            cols = slice(j * LANES, (j + 1) * LANES)
            o_ref[0, r:r + 1, cols] = jnp.sum(acc[r * nl + j], axis=0, keepdims=True) + b_ref[0, :, cols]


def _adaln_mods(c, c_ctx, w_ada, b_ada):
    depth, d, n = w_ada.shape
    bn = 1024
    cb = jnp.broadcast_to(jnp.stack([c[0], c_ctx])[:, :, None], (2, d, LANES))
    return pl.pallas_call(
        _mods_body,
        grid=(depth, n // bn),
        in_specs=[pl.BlockSpec((2, d, LANES), lambda i, j: (0, 0, 0)),
                  pl.BlockSpec((1, d, bn), lambda i, j: (i, 0, j)),
                  pl.BlockSpec((1, 1, bn), lambda i, j: (i, 0, j))],
        out_specs=pl.BlockSpec((1, 2, bn), lambda i, j: (i, 0, j)),
        out_shape=jax.ShapeDtypeStruct((depth, 2, n), F32),
        scratch_shapes=[pltpu.VMEM((2, d, LANES), F32)],
        compiler_params=_cparams("arbitrary", "arbitrary"),
        name="adaln_mods",
    )(cb, w_ada, b_ada.reshape(depth, 1, n))


def _ffn_body(h_ref, vec_ref, wgu_ref, wo_ref, *rest, res_w, cast_rows, cast_hold):
    if cast_rows:
        ci_ref, co_ref, o_ref, cib_ref, cob_ref, u_ref, acc_ref = rest
    else:
        o_ref, u_ref, acc_ref = rest
    j = pl.program_id(1)
    bf = wo_ref.shape[0]

    @pl.when((pl.program_id(0) == 0) & (j == 0))
    def _():
        acc_ref[...] = jnp.zeros_like(acc_ref)

    @pl.when(j == 0)
    def _():
        _store_modulated(h_ref, vec_ref, u_ref)

    u = u_ref[...]
    g = jnp.dot(u, wgu_ref[:, :bf], preferred_element_type=F32)
    p = jnp.dot(u, wgu_ref[:, bf:], preferred_element_type=F32)
    a = (g * jax.nn.sigmoid(g) * p).astype(BF16)
    acc_ref[...] += jnp.dot(a, wo_ref[...], preferred_element_type=F32)

    if cast_rows:
        part = (pl.program_id(0) * pl.num_programs(1) + j) % cast_hold
        for src, dst, rows in ((ci_ref, cib_ref, cast_rows[0]), (co_ref, cob_ref, cast_rows[1])):
            rs = pl.ds(pl.multiple_of(part * rows, rows), rows)
            dst[rs, :] = src[rs, :].astype(dst.dtype)

    @pl.when(j == pl.num_programs(1) - 1)
    def _():
        gate = (res_w * vec_ref[2:3, :]) * vec_ref[4:5, :]
        o_ref[...] = h_ref[...] + _rms(acc_ref[...], gate)
        acc_ref[...] = jnp.zeros_like(acc_ref)


def _interleave_gate_up(w_in):
    d, f2 = w_in.shape
    nf = f2 // (2 * FFN_CHUNK)
    return jnp.swapaxes(w_in.reshape(d, 2, nf, FFN_CHUNK), 1, 2).reshape(d, f2)


def _cast_plan(n_rows, n_cols, steps, holds, n_col_blocks=None):
    for hold in holds:
        if steps % hold:
            continue
        blocks = steps // hold
        for ncb in ([n_col_blocks] if n_col_blocks else range(1, blocks + 1)):
            if blocks % ncb or n_cols % ncb or n_rows % (blocks // ncb):
                continue
            br, bc = n_rows // (blocks // ncb), n_cols // ncb
            if br % (hold * HALO) == 0 and bc % LANES == 0:
                return hold, br, ncb
    raise ValueError("no tiling of the weight cast fits this grid")


def _ffn(h, vecs, w_in, w_out, res_w, cast_next=None):
    m, d = h.shape
    f = w_out.shape[0]
    bm = min(m, 512)
    bf = FFN_CHUNK
    nf = f // bf
    in_specs = [pl.BlockSpec((bm, d), lambda i, j: (i, 0)),
                pl.BlockSpec((SUBLANES, d), lambda i, j: (0, 0)),
                pl.BlockSpec((d, 2 * bf), lambda i, j: (0, j)),
                pl.BlockSpec((bf, d), lambda i, j: (j, 0))]
    out_specs = [pl.BlockSpec((bm, d), lambda i, j: (i, 0))]
    out_shape = [jax.ShapeDtypeStruct((m, d), F32)]
    args = [h, vecs, w_in, w_out]
    cast_rows, hold = (), 1
    if cast_next is not None:
        src_in, src_out, layer, slot = cast_next
        steps = (m // bm) * nf
        hold, br_in, _ = _cast_plan(src_in.shape[2], src_in.shape[3], steps, (4, 2, 1), n_col_blocks=2 * nf)
        _, br_out, ncb_out = _cast_plan(src_out.shape[2], src_out.shape[3], steps, (hold,))
        cast_rows = (br_in // hold, br_out // hold)

        def blk(i, j):
            return (i * nf + j) // hold

        def interleaved(q):
            return jnp.where(q < nf, 2 * q, 2 * (q - nf) + 1)

        in_specs += [pl.BlockSpec((None, None, br_in, bf),
                                  lambda i, j: (layer, slot, blk(i, j) // (2 * nf), blk(i, j) % (2 * nf))),
                     pl.BlockSpec((None, None, br_out, src_out.shape[3] // ncb_out),
                                  lambda i, j: (layer, slot, blk(i, j) // ncb_out, blk(i, j) % ncb_out))]
        out_specs += [pl.BlockSpec((br_in, bf),
                                   lambda i, j: (blk(i, j) // (2 * nf), interleaved(blk(i, j) % (2 * nf)))),
                      pl.BlockSpec((br_out, src_out.shape[3] // ncb_out),
                                   lambda i, j: (blk(i, j) // ncb_out, blk(i, j) % ncb_out))]
        out_shape += [jax.ShapeDtypeStruct(src_in.shape[2:], BF16), jax.ShapeDtypeStruct(src_out.shape[2:], BF16)]
        args += [src_in, src_out]
    outs = pl.pallas_call(
        functools.partial(_ffn_body, res_w=res_w, cast_rows=cast_rows, cast_hold=hold),
        grid=(m // bm, nf),
        in_specs=in_specs,
        out_specs=out_specs,
        out_shape=out_shape,
        scratch_shapes=[pltpu.VMEM((bm, d), BF16), pltpu.VMEM((bm, d), F32)],
        compiler_params=_cparams("arbitrary", "arbitrary"),
        name="ffn",
    )(*args)
    return outs if cast_next is not None else outs[0]


def _proj_in_body(h_ref, vec_ref, w_ref, b_ref, *rest, rope_blocks, q_blocks, q_scale):
    if rope_blocks:
        cos_ref, sa_ref, sb_ref, o_ref, u_ref = rest
    else:
        o_ref, u_ref = rest
    j = pl.program_id(1)

    @pl.when(j == 0)
    def _():
        _store_modulated(h_ref, vec_ref, u_ref)

    y = jnp.dot(u_ref[...], w_ref[...], preferred_element_type=F32) + b_ref[...]
    if not rope_blocks:
        o_ref[...] = y.astype(o_ref.dtype)
        return

    rot = jnp.where(j < q_blocks, q_scale, jnp.where(j < rope_blocks, 1.0, 0.0)).astype(F32)
    keep = jnp.where(j < rope_blocks, 0.0, 1.0).astype(F32)
    cos, sa, sb = cos_ref[...] * rot + keep, sa_ref[...] * rot, sb_ref[...] * rot
    for hd in range(y.shape[1] // HEAD_DIM):
        cols = slice(hd * HEAD_DIM, (hd + 1) * HEAD_DIM)
        xh = y[:, cols]
        r = (xh * cos + pltpu.roll(xh, HEAD_DIM - HEAD_DIM // 4, 1) * sa
             + pltpu.roll(xh, HEAD_DIM // 4, 1) * sb)
        o_ref[:, cols] = r.astype(o_ref.dtype)


def _proj_in(h, vecs, w, b, out_dtype, rope=None, q_blocks=0, q_scale=1.0):
    m, d = h.shape
    n = w.shape[1]
    bm = min(m, 1024)
    bn = 512
    in_specs = [pl.BlockSpec((bm, d), lambda i, j: (i, 0)),
                pl.BlockSpec((SUBLANES, d), lambda i, j: (0, 0)),
                pl.BlockSpec((d, bn), lambda i, j: (0, j)),
                pl.BlockSpec((1, bn), lambda i, j: (0, j))]
    args = [h, vecs, w, b.reshape(1, n)]
    rope_blocks = 0
    if rope is not None:
        rope_blocks = q_blocks + 1
        in_specs += [pl.BlockSpec((bm, HEAD_DIM), lambda i, j: (i, 0))] * 3
        args += list(rope)
    return pl.pallas_call(
        functools.partial(_proj_in_body, rope_blocks=rope_blocks, q_blocks=q_blocks, q_scale=q_scale),
        grid=(m // bm, n // bn),
        in_specs=in_specs,
        out_specs=pl.BlockSpec((bm, bn), lambda i, j: (i, j)),
        out_shape=jax.ShapeDtypeStruct((m, n), out_dtype),
        scratch_shapes=[pltpu.VMEM((bm, d), BF16)],
        compiler_params=_cparams("arbitrary", "arbitrary"),
        name="proj_in",
    )(*args)


def _proj_out_body(a_ref, w_ref, b_ref, h_ref, vec_ref, o_ref):
    y = jnp.dot(a_ref[...], w_ref[...], preferred_element_type=F32) + b_ref[...]
    o_ref[...] = h_ref[...] + vec_ref[2:3, :] * _rms(y, vec_ref[4:5, :])


def _proj_out(a, w, b, h, vecs):
    m, d = h.shape
    k = a.shape[1]
    bm = min(m, 256)
    return pl.pallas_call(
        _proj_out_body,
        grid=(m // bm,),
        in_specs=[pl.BlockSpec((bm, k), lambda i: (i, 0)),
                  pl.BlockSpec((k, d), lambda i: (0, 0)),
                  pl.BlockSpec((1, d), lambda i: (0, 0)),
                  pl.BlockSpec((bm, d), lambda i: (i, 0)),
                  pl.BlockSpec((SUBLANES, d), lambda i: (0, 0))],
        out_specs=pl.BlockSpec((bm, d), lambda i: (i, 0)),
        out_shape=jax.ShapeDtypeStruct((m, d), F32),
        compiler_params=_cparams("arbitrary"),
        name="proj_out",
    )(a, w, b.reshape(1, d), h, vecs)


def _hy_in_body(h_ref, hp_ref, hn_ref, vec_ref, w0_ref, w1_ref, w2_ref, b0_ref, b1_ref, b2_ref, wsc_ref, bsc_ref,
                x0_out, xin_out, u_ref, uh_ref):
    i = pl.program_id(0)
    last = pl.num_programs(0) - 1

    @pl.when(pl.program_id(1) == 0)
    def _():
        _store_modulated(h_ref, vec_ref, u_ref)
        uh_ref[0:SUBLANES, :] = hp_ref[...]
        uh_ref[SUBLANES:, :] = hn_ref[...]
        _store_modulated(uh_ref, vec_ref, uh_ref)

    def conv(w_ref, b_ref, part):
        z = jnp.dot(u_ref[...], w_ref[...], preferred_element_type=F32) + b_ref[...]
        zh = jnp.dot(uh_ref[...].astype(BF16), w_ref[...], preferred_element_type=F32) + b_ref[...]
        bm = z.shape[0]
        prev_row = jnp.where(i > 0, zh[SUBLANES - 1:SUBLANES, :], 0.0)
        next_row = jnp.where(i < last, zh[SUBLANES:SUBLANES + 1, :], 0.0)
        rid = lax.broadcasted_iota(jnp.int32, z.shape, 0)
        up = jnp.where(rid == 0, prev_row, pltpu.roll(z, 1, 0))
        dn = jnp.where(rid == bm - 1, next_row, pltpu.roll(z, bm - 1, 0))
        return (up * wsc_ref[0, part:part + 1, :] + z * wsc_ref[1, part:part + 1, :]
                + dn * wsc_ref[2, part:part + 1, :] + bsc_ref[part:part + 1, :])

    x0_out[...] = conv(w0_ref, b0_ref, 0).astype(x0_out.dtype)
    xin_out[...] = (conv(w2_ref, b2_ref, 2) * conv(w1_ref, b1_ref, 1)).astype(xin_out.dtype)


def _hy_in_gate(h, vecs, w_in, b_in, w_sc, b_sc):
    l, d = h.shape
    bm = min(l, 1024)
    bc = min(d, 512)
    nc = d // bc
    rb = bm // SUBLANES
    nrow8 = l // SUBLANES
    wspecs = [pl.BlockSpec((d, bc), functools.partial(lambda i, j, p: (0, p * nc + j), p=part)) for part in range(3)]
    bspecs = [pl.BlockSpec((1, bc), functools.partial(lambda i, j, p: (0, p * nc + j), p=part)) for part in range(3)]
    out_spec = pl.BlockSpec((bm, bc), lambda i, j: (i, j))
    return pl.pallas_call(
        _hy_in_body,
        grid=(l // bm, nc),
        in_specs=[pl.BlockSpec((bm, d), lambda i, j: (i, 0)),
                  pl.BlockSpec((SUBLANES, d), lambda i, j: (jnp.maximum(i * rb - 1, 0), 0)),
                  pl.BlockSpec((SUBLANES, d), lambda i, j: (jnp.minimum((i + 1) * rb, nrow8 - 1), 0)),
                  pl.BlockSpec((SUBLANES, d), lambda i, j: (0, 0))] + wspecs + bspecs
                 + [pl.BlockSpec((3, 3, bc), lambda i, j: (0, 0, j)),
                    pl.BlockSpec((3, bc), lambda i, j: (0, j))],
        out_specs=[out_spec, out_spec],
        out_shape=[jax.ShapeDtypeStruct((l, d), BF16)] * 2,
        scratch_shapes=[pltpu.VMEM((bm, d), BF16), pltpu.VMEM((2 * SUBLANES, d), F32)],
        compiler_params=_cparams("arbitrary", "arbitrary"),
        name="hyena_in_gate",
    )(h, h, h, vecs, w_in, w_in, w_in, *([b_in.reshape(1, 3 * d)] * 3), w_sc.reshape(3, 3, d), b_sc.reshape(3, d))


def _filter_mlp_body(z_ref, w1, b1, w2, b2, w3, b3, fr, o_ref):
    hp = lax.Precision.HIGHEST
    f = jnp.sin(fr[0:1, :] * (jnp.dot(z_ref[...], w1[...], precision=hp, preferred_element_type=F32) + b1[...]))
    f = jnp.sin(fr[1:2, :] * (jnp.dot(f, w2[...], precision=hp, preferred_element_type=F32) + b2[...]))
    o_ref[...] = jnp.sin(fr[2:3, :] * (jnp.dot(f, w3[...], precision=hp, preferred_element_type=F32) + b3[...]))


def _filter_table(l, w1, b1, w2, b2, w3, b3, freq):
    hid = w1.shape[1]
    t = jnp.linspace(0.0, 1.0, l, dtype=F32)[:, None]
    omega = 2.0 * math.pi * jnp.arange(l, dtype=F32)[:, None] / l
    bands = jnp.linspace(1e-4, FILTER_BANDS - 1, FILTER_BANDS, dtype=F32)[None, :]
    emb = jnp.concatenate([t, jnp.cos(bands * omega), -jnp.sin(bands * omega)], axis=-1)
    emb = jnp.pad(emb, ((0, 0), (0, LANES - emb.shape[1])))
    w1p = jnp.pad(w1, ((0, LANES - w1.shape[0]), (0, 0)))
    pack = LANES // hid
    rows = l // pack
    eye = jnp.eye(pack, dtype=F32)
    emb_p = jnp.concatenate([emb[p * rows:(p + 1) * rows] for p in range(pack)], axis=1)
    wide = pack * hid
    bm = min(rows, 512)

    def full(shape):
        return pl.BlockSpec(shape, lambda i: (0,) * len(shape))

    f_p = pl.pallas_call(
        _filter_mlp_body,
        grid=(rows // bm,),
        in_specs=[pl.BlockSpec((bm, pack * LANES), lambda i: (i, 0)),
                  full((pack * LANES, wide)), full((1, wide)), full((wide, wide)), full((1, wide)),
                  full((wide, wide)), full((1, wide)), full((3, wide))],
        out_specs=pl.BlockSpec((bm, wide), lambda i: (i, 0)),
        out_shape=jax.ShapeDtypeStruct((rows, wide), F32),
        compiler_params=_cparams("arbitrary"),
        name="hyena_filter_mlp",
    )(emb_p, jnp.kron(eye, w1p), jnp.tile(b1, pack).reshape(1, wide), jnp.kron(eye, w2),
      jnp.tile(b2, pack).reshape(1, wide), jnp.kron(eye, w3), jnp.tile(b3, pack).reshape(1, wide),
      jnp.tile(freq, (1, pack)))
    f = jnp.concatenate([f_p[:, p * hid:(p + 1) * hid] for p in range(pack)], axis=0)
    tab = jnp.concatenate([f, t, jnp.ones((l, 1), F32)], axis=-1)
    tab = jnp.pad(tab, ((0, 0), (0, LANES - tab.shape[1])))
    return jnp.concatenate([tab, jnp.zeros((1, LANES), F32), tab[1:][::-1]], axis=0)


def _decay_rates(d):
    return jnp.abs(jnp.linspace(math.log(DECAY_TARGET) / LONG_DECAY_PCT,
                                math.log(DECAY_TARGET) / SHORT_DECAY_PCT, d, dtype=F32))[None, :]


def _split_bf16(x):
    hi = x.astype(BF16)
    return hi, (x - hi.astype(F32)).astype(BF16)


def _stack_3pass(w):
    w_hi, w_lo = _split_bf16(w)
    return jnp.concatenate([w_hi, w_hi, w_lo], axis=0)


def _dot_3pass(f, w3):
    f_hi, f_lo = _split_bf16(f)
    return jnp.dot(jnp.concatenate([f_hi, f_lo, f_hi], axis=1), w3, preferred_element_type=F32)


def _filter_rows(ft, wa3, wb3, dl_ref):
    hid = wa3.shape[0] // 3
    half = ft.shape[0] // 2
    f, t, keep = ft[:, :hid], ft[:, hid:hid + 1], ft[:, hid + 1:hid + 2]
    y = jnp.concatenate([_dot_3pass(f[:half], wa3), _dot_3pass(f[half:], wb3)], axis=0)
    return y * (jnp.exp(-t * dl_ref[...]) * keep)


def _outer_stage(fa_ref, cols, o_ref):
    ys = [jnp.dot(fa_ref[...], xj, preferred_element_type=F32).astype(BF16) for xj in cols]
    o_ref[...] = pltpu.einshape("brd->rbd", jnp.stack(ys))


def _dft_a_body(x_ref, fa_ref, o_ref):
    xb = pltpu.einshape("abd->bad", x_ref[...])
    _outer_stage(fa_ref, [xb[j] for j in range(xb.shape[0])], o_ref)


def _dft_a_filter_body(ft_ref, wa_ref, wb_ref, dl_ref, fa_ref, o_ref):
    wa3, wb3 = _stack_3pass(wa_ref[...]), _stack_3pass(wb_ref[...])
    cols = [_filter_rows(ft_ref[j], wa3, wb3, dl_ref).astype(BF16) for j in range(ft_ref.shape[0])]
    _outer_stage(fa_ref, cols, o_ref)


def _dft_a(x3, fa, d_chunk):
    a, b, d = x3.shape
    rows = fa.shape[0]
    return pl.pallas_call(
        _dft_a_body,
        grid=(b // HALO, d // d_chunk),
        in_specs=[pl.BlockSpec((a, HALO, d_chunk), lambda i, j: (0, i, j)),
                  pl.BlockSpec((rows, a), lambda i, j: (0, 0))],
        out_specs=pl.BlockSpec((rows, HALO, d_chunk), lambda i, j: (0, i, j)),
        out_shape=jax.ShapeDtypeStruct((rows, b, d), BF16),
        compiler_params=_cparams("arbitrary", "arbitrary"),
        name="dft_outer",
    )(x3, fa)


def _dft_a_filter(ft3, w4, deltas, fa, d_chunk):
    b, n1, _ = ft3.shape
    hid, d2 = w4.shape
    d = d2 // 2
    rows = fa.shape[0]
    nd = d // d_chunk
    return pl.pallas_call(
        _dft_a_filter_body,
        grid=(b // HALO, nd),
        in_specs=[pl.BlockSpec((HALO, n1, LANES), lambda i, j: (i, 0, 0)),
                  pl.BlockSpec((hid, d_chunk), lambda i, j: (0, j)),
                  pl.BlockSpec((hid, d_chunk), lambda i, j: (0, j + nd)),
                  pl.BlockSpec((1, d_chunk), lambda i, j: (0, j)),
                  pl.BlockSpec((rows, n1), lambda i, j: (0, 0))],
        out_specs=pl.BlockSpec((rows, HALO, d_chunk), lambda i, j: (0, i, j)),
        out_shape=jax.ShapeDtypeStruct((rows, b, d), BF16),
        compiler_params=_cparams("arbitrary", "arbitrary"),
        name="dft_outer_filter",
    )(ft3, w4, w4, deltas, fa)


def _dft_inner_body(y_ref, yf_ref, gf_ref, gi_ref, o_ref):
    b, dc = y_ref.shape[2], y_ref.shape[3]
    for ci in range(y_ref.shape[1]):
        z = jnp.dot(gf_ref[ci], y_ref[:, ci].reshape(2 * b, dc), preferred_element_type=F32)
        hh = jnp.dot(gf_ref[ci], yf_ref[:, ci].reshape(2 * b, dc), preferred_element_type=F32)
        zr, zi, hr, hi = z[:b], z[b:], hh[:b], hh[b:]
        p = jnp.concatenate([zr * hr - zi * hi, zr * hi + zi * hr], axis=0).astype(BF16)
        cc = jnp.dot(gi_ref[ci], p, preferred_element_type=F32)
        o_ref[:, ci] = cc.reshape(2, b, dc).astype(o_ref.dtype)


def _dft_inner(y4, yf4, gf, gi):
    _, nc, b, d = y4.shape
    cblk = SUBLANES
    dc = min(d, 1024)
    yspec = pl.BlockSpec((2, cblk, b, dc), lambda c, j: (0, c, 0, j))
    gspec = pl.BlockSpec((cblk, 2 * b, 2 * b), lambda c, j: (c, 0, 0))
    return pl.pallas_call(
        _dft_inner_body,
        grid=(nc // cblk, d // dc),
        in_specs=[yspec, yspec, gspec, gspec],
        out_specs=yspec,
        out_shape=jax.ShapeDtypeStruct(y4.shape, BF16),
        compiler_params=_cparams("arbitrary", "arbitrary"),
        name="dft_inner",
    )(y4, yf4, gf, gi)


def _dft_c_body(c_ref, m_ref, xin_ref, x0_ref, skip_ref, o_ref):
    cb = pltpu.einshape("rbd->brd", c_ref[...])
    outs = [jnp.dot(m_ref[...], cb[j], preferred_element_type=F32).astype(BF16) for j in range(cb.shape[0])]
    conv = pltpu.einshape("bad->abd", jnp.stack(outs)).astype(F32)
    gated = x0_ref[...].astype(F32) * (conv + xin_ref[...].astype(F32) * skip_ref[...][None])
    o_ref[...] = gated.astype(o_ref.dtype)


def _dft_c(c3, mc, xin3, x03, skip, d_chunk):
    rows, b, d = c3.shape
    a = mc.shape[0]
    xspec = pl.BlockSpec((a, HALO, d_chunk), lambda i, j: (0, i, j))
    return pl.pallas_call(
        _dft_c_body,
        grid=(b // HALO, d // d_chunk),
        in_specs=[pl.BlockSpec((rows, HALO, d_chunk), lambda i, j: (0, i, j)),
                  pl.BlockSpec((a, rows), lambda i, j: (0, 0)),
                  xspec, xspec,
                  pl.BlockSpec((1, d_chunk), lambda i, j: (0, j))],
        out_specs=xspec,
        out_shape=jax.ShapeDtypeStruct((a, b, d), BF16),
        compiler_params=_cparams("arbitrary", "arbitrary"),
        name="dft_outer_inv",
    )(c3, mc, xin3, x03, skip)


def _cis(num, period):
    ang = (2.0 * math.pi / period) * num.astype(F32)
    return jnp.cos(ang), jnp.sin(ang)


def _long_conv_gated(xin, x0, ftab, w4, skip):
    l, d = xin.shape
    b = DFT_INNER
    a = l // b
    n1, n = 2 * a, 2 * l
    nc = n1 // 2 + SUBLANES
    dc = min(d, 512)
    ci = jnp.arange(nc, dtype=jnp.int32)
    ai = jnp.arange(n1, dtype=jnp.int32)
    live = (ci <= n1 // 2).astype(F32)[:, None]
    cr, sr = _cis((ci[:, None] * ai[None, :]) % n1, n1)
    cr, sr = cr * live, sr * live
    fa_full = jnp.concatenate([cr, -sr], axis=0).astype(BF16)
    fa = fa_full[:, :a]
    fold = jnp.where((ci == 0) | (ci == n1 // 2), 1.0, 2.0)[None, :] / n
    mc = jnp.concatenate([cr[:, :a].T * fold, -sr[:, :a].T * fold], axis=1).astype(BF16)
    ei = jnp.arange(b, dtype=jnp.int32)
    kk = ci[:, None, None] + n1 * ei[None, :, None]
    tr, ts = _cis((kk * ei[None, None, :]) % n, n)
    ti = -ts
    gf = jnp.concatenate([jnp.concatenate([tr, -ti], axis=2),
                          jnp.concatenate([ti, tr], axis=2)], axis=1).astype(BF16)
    trt, tit = jnp.swapaxes(tr, 1, 2), jnp.swapaxes(ti, 1, 2)
    gi = jnp.concatenate([jnp.concatenate([trt, tit], axis=2),
                          jnp.concatenate([-tit, trt], axis=2)], axis=1).astype(BF16)

    ft3 = jnp.swapaxes(ftab.reshape(n1, b, LANES), 0, 1)
    hf = _dft_a_filter(ft3, w4, _decay_rates(d), fa_full, dc)
    yx = _dft_a(xin.reshape(a, b, d), fa, dc)
    c4 = _dft_inner(yx.reshape(2, nc, b, d), hf.reshape(2, nc, b, d), gf, gi)
    y3 = _dft_c(c4.reshape(2 * nc, b, d), mc, xin.reshape(a, b, d), x0.reshape(a, b, d), skip.reshape(1, d), dc)
    return y3.reshape(l, d)


def _ctx_conv_body(x_ref, x0_ref, ft_ref, wa_ref, wb_ref, dl_ref, skip_ref, ff_ref, fh_ref, mi_ref, o_ref):
    x = x_ref[...].astype(F32)
    n = ft_ref.shape[0]
    filt = _filter_rows(ft_ref[...], _stack_3pass(wa_ref[...]), _stack_3pass(wb_ref[...]), dl_ref)
    xs = jnp.dot(ff_ref[...], x.astype(BF16), preferred_element_type=F32)
    hs = jnp.dot(fh_ref[...], filt.astype(BF16), preferred_element_type=F32)
    xr, xi, hr, hi = xs[:n], xs[n:], hs[:n], hs[n:]
    p = jnp.concatenate([xr * hr - xi * hi, xr * hi + xi * hr], axis=0).astype(BF16)
    conv = jnp.dot(mi_ref[...], p, preferred_element_type=F32)
    o_ref[...] = (x0_ref[...].astype(F32) * (conv + x * skip_ref[...])).astype(o_ref.dtype)


def _short_seq_conv_gated(xin, x0, ftab, w4, skip):
    l, d = xin.shape
    n = 2 * l
    hid = w4.shape[0]
    ni = jnp.arange(n, dtype=jnp.int32)
    cr, sr = _cis((ni[:, None] * ni[None, :]) % n, n)
    fh = jnp.concatenate([cr, -sr], axis=0).astype(BF16)
    ff = fh[:, :l]
    mi = (jnp.concatenate([cr[:l], -sr[:l]], axis=1) * (1.0 / n)).astype(BF16)
    dc = min(d, 512)
    nd = d // dc
    return pl.pallas_call(
        _ctx_conv_body,
        grid=(nd,),
        in_specs=[pl.BlockSpec((l, dc), lambda j: (0, j)),
                  pl.BlockSpec((l, dc), lambda j: (0, j)),
                  pl.BlockSpec((n, LANES), lambda j: (0, 0)),
                  pl.BlockSpec((hid, dc), lambda j: (0, j)),
                  pl.BlockSpec((hid, dc), lambda j: (0, j + nd)),
                  pl.BlockSpec((1, dc), lambda j: (0, j)),
                  pl.BlockSpec((1, dc), lambda j: (0, j)),
                  pl.BlockSpec((2 * n, l), lambda j: (0, 0)),
                  pl.BlockSpec((2 * n, n), lambda j: (0, 0)),
                  pl.BlockSpec((l, 2 * n), lambda j: (0, 0))],
        out_specs=pl.BlockSpec((l, dc), lambda j: (0, j)),
        out_shape=jax.ShapeDtypeStruct((l, d), BF16),
        compiler_params=_cparams("arbitrary"),
        name="ctx_conv",
    )(xin, x0, ftab, w4, w4, _decay_rates(d), skip.reshape(1, d), ff, fh, mi)


def _hyena_mixer(h, vecs, p, long_seq):
    w_in, b_in, w_sc, b_sc, f_w1, f_b1, f_w2, f_b2, f_w3, f_b3, f_w4, f_freq, skip, w_out, b_out = p
    l = h.shape[0]
    x0, xin = _hy_in_gate(h, vecs, w_in, b_in, w_sc, b_sc)
    ftab = _filter_table(l, f_w1, f_b1, f_w2, f_b2, f_w3, f_b3, f_freq)
    conv = _long_conv_gated if long_seq else _short_seq_conv_gated
    y = conv(xin, x0, ftab, f_w4, skip)
    return _proj_out(y, w_out, b_out, h, vecs)


def _attn_body(sink_ref, q_ref, kp_ref, kc_ref, kn_ref, vp_ref, vc_ref, vn_ref, kx_ref, vx_ref, bias_ref, o_ref,
               *, group):
    blk = q_ref.shape[0]
    bias = bias_ref[0]
    rid = lax.broadcasted_iota(jnp.int32, (group * blk, 1), 0)
    for kh in range(N_KV_HEADS):
        hs = slice(kh * HEAD_DIM, (kh + 1) * HEAD_DIM)
        heads = [kh * group + g for g in range(group)]
        q = jnp.concatenate([q_ref[:, hd * HEAD_DIM:(hd + 1) * HEAD_DIM] for hd in heads], axis=0)
        keys = jnp.concatenate([kp_ref[:, hs], kc_ref[:, hs], kn_ref[:, hs], kx_ref[:, hs]], axis=0)
        vals = jnp.concatenate([vp_ref[:, hs], vc_ref[:, hs], vn_ref[:, hs], vx_ref[:, hs]], axis=0)
        s = lax.dot_general(q, keys, (((1,), (1,)), ((), ())), preferred_element_type=F32) + bias
        sink = jnp.zeros((group * blk, 1), F32)
        for g, hd in enumerate(heads):
            sink = jnp.where((rid >= g * blk) & (rid < (g + 1) * blk), sink_ref[hd], sink)
        mx = jnp.maximum(jnp.max(s, axis=-1, keepdims=True), sink)
        pr = jnp.exp(s - mx)
        denom = jnp.sum(pr, axis=-1, keepdims=True) + jnp.exp(sink - mx)
        o = jnp.dot(pr.astype(BF16), vals, preferred_element_type=F32) / denom
        for g, hd in enumerate(heads):
            o_ref[:, hd * HEAD_DIM:(hd + 1) * HEAD_DIM] = o[g * blk:(g + 1) * blk].astype(o_ref.dtype)


def _attention(qkv, kvc, sink, d):
    l = qkv.shape[0]
    c = kvc.shape[0]
    blk = ATTN_BLOCK
    nb = l // blk
    group = d // HEAD_DIM // N_KV_HEADS
    kvw = N_KV_HEADS * HEAD_DIM
    kcol = d // kvw
    qi = jnp.arange(group * blk, dtype=jnp.int32)[:, None] % blk
    si = jnp.arange(3 * blk + c, dtype=jnp.int32)[None, :]
    kb, ki = si // blk, si % blk
    prev_ok, next_ok = (kb == 0) & (ki >= qi), (kb == 2) & (ki <= qi)
    always = (kb == 1) | (kb >= 3)
    variants = [always | next_ok, always | prev_ok | next_ok, always | prev_ok]
    bias = jnp.stack([jnp.where(v, 0.0, MASK_BIAS).astype(F32) for v in variants])

    def kv_spec(col, shift):
        return pl.BlockSpec((blk, kvw), lambda n: (jnp.clip(n + shift, 0, nb - 1), col))

    return pl.pallas_call(
        functools.partial(_attn_body, group=group),
        grid=(nb,),
        in_specs=[pl.BlockSpec(memory_space=pltpu.SMEM),
                  pl.BlockSpec((blk, d), lambda n: (n, 0)),
                  kv_spec(kcol, -1), kv_spec(kcol, 0), kv_spec(kcol, 1),
                  kv_spec(kcol + 1, -1), kv_spec(kcol + 1, 0), kv_spec(kcol + 1, 1),
                  pl.BlockSpec((c, kvw), lambda n: (0, 0)),
                  pl.BlockSpec((c, kvw), lambda n: (0, 1)),
                  pl.BlockSpec((1, group * blk, 3 * blk + c),
                               lambda n: (jnp.where(n == 0, 0, jnp.where(n == nb - 1, 2, 1)), 0, 0))],
        out_specs=pl.BlockSpec((blk, d), lambda n: (n, 0)),
        out_shape=jax.ShapeDtypeStruct((l, d), BF16),
        compiler_params=_cparams("arbitrary"),
        name="window_attn",
    )(sink, qkv, qkv, qkv, qkv, qkv, qkv, qkv, kvc, kvc, bias)


def _rope_tables(l):
    rows = l // GRID_W
    pos_row = jnp.broadcast_to(jnp.arange(rows)[:, None], (rows, GRID_W)).reshape(-1).astype(F32)
    pos_col = jnp.broadcast_to(jnp.arange(GRID_W)[None, :], (rows, GRID_W)).reshape(-1).astype(F32)
    pairs = HEAD_DIM // 4
    inv_freq = ROPE_THETA ** (-jnp.arange(pairs, dtype=F32) / pairs)
    ang_row = pos_row[:, None] * inv_freq[None, :]
    ang_col = pos_col[:, None] * inv_freq[None, :]
    zeros = jnp.zeros_like(ang_row)
    cos = jnp.concatenate([jnp.cos(ang_row)] * 2 + [jnp.cos(ang_col)] * 2, axis=-1)
    sin_a = jnp.concatenate([-jnp.sin(ang_row), zeros, -jnp.sin(ang_col), zeros], axis=-1)
    sin_b = jnp.concatenate([zeros, jnp.sin(ang_row), zeros, jnp.sin(ang_col)], axis=-1)
    return cos, sin_a, sin_b


def _pool_body(hc_ref, hp_ref, hn_ref, vec_ref, w_ref, b_ref, sc_ref, o_ref, y_ref, *, seq_len):
    i = pl.program_id(0)
    last = pl.num_programs(0) - 1
    bm, d = hc_ref.shape
    gw = d // len(POOL_SIZES)
    h = hc_ref[...]
    u = _modulated(h, vec_ref)
    up = jnp.where(i > 0, _modulated(hp_ref[...], vec_ref), 0.0)
    un = jnp.where(i < last, _modulated(hn_ref[...], vec_ref), 0.0)
    ext_rows = bm + 2 * SUBLANES
    t = i * bm + lax.broadcasted_iota(jnp.int32, (bm, 1), 0)
    for g, size in enumerate(POOL_SIZES):
        cols = slice(g * gw, (g + 1) * gw)
        ext = jnp.concatenate([up[:, cols], u[:, cols], un[:, cols]], axis=0)
        acc, span = ext, 1
        while span < size:
            acc = acc + pltpu.roll(acc, ext_rows - span, 0)
            span *= 2
        start = SUBLANES - size // 2
        win = pltpu.roll(acc, ext_rows - start, 0)[:bm] if start else acc[:bm]
        lo = jnp.clip(t - size // 2, 0, seq_len)
        hi = jnp.clip(t - size // 2 + size, 0, seq_len)
        part = win / (hi - lo).astype(F32) - u[:, cols]
        yg = jnp.dot(part.astype(BF16), w_ref[g], preferred_element_type=F32)
        y_ref[:, cols] = (yg + b_ref[:, cols]) * sc_ref[:, cols]
    o_ref[...] = h + vec_ref[2:3, :] * _rms(y_ref[...], vec_ref[4:5, :])


def _pool_mixer(h, vecs, w, b, scale):
    l, d = h.shape
    bm = min(l, 256)
    rb = bm // SUBLANES
    nrow8 = l // SUBLANES
    ng, gw = w.shape[0], w.shape[1]
    return pl.pallas_call(
        functools.partial(_pool_body, seq_len=l),
        grid=(l // bm,),
        in_specs=[pl.BlockSpec((bm, d), lambda i: (i, 0)),
                  pl.BlockSpec((SUBLANES, d), lambda i: (jnp.maximum(i * rb - 1, 0), 0)),
                  pl.BlockSpec((SUBLANES, d), lambda i: (jnp.minimum((i + 1) * rb, nrow8 - 1), 0)),
                  pl.BlockSpec((SUBLANES, d), lambda i: (0, 0)),
                  pl.BlockSpec((ng, gw, gw), lambda i: (0, 0, 0)),
                  pl.BlockSpec((1, d), lambda i: (0, 0)),
                  pl.BlockSpec((1, d), lambda i: (0, 0))],
        out_specs=pl.BlockSpec((bm, d), lambda i: (i, 0)),
        out_shape=jax.ShapeDtypeStruct((l, d), F32),
        scratch_shapes=[pltpu.VMEM((bm, d), F32)],
        compiler_params=_cparams("arbitrary"),
        name="pool_mixer",
    )(h, h, h, vecs, w.astype(BF16), b.reshape(1, d), scale.reshape(1, d))


def _sub_vecs(mod, k, g_pre, g_post):
    rows = [mod[3 * k], mod[3 * k + 1], mod[3 * k + 2], g_pre, g_post]
    return jnp.stack(rows + [jnp.zeros_like(g_pre)] * (SUBLANES - len(rows)))


def kernel(x, c, ctx, c_ctx, w_ada, b_ada, norm_pre, norm_post, w_ffn_in, w_ffn_out, hy_w_in, hy_b_in, hy_w_sc, hy_b_sc, hy_f_w1, hy_f_b1, hy_f_w2, hy_f_b2, hy_f_w3, hy_f_b3, hy_f_w4, hy_f_freq, hy_skip, hy_w_out, hy_b_out, at_w_qkv, at_b_qkv, at_sink, at_w_o, at_b_o, pl_w, pl_b, pl_scale):
    bsz, l, d = x.shape
    assert bsz == 1, "kernel handles a single batch element"
    assert d % 1024 == 0 and l % 1024 == 0 and ctx.shape[1] % 256 == 0, "block sizes assume these multiples"
    assert w_ffn_out.shape[2] % 512 == 0, "FFN width must be a multiple of the 512-column weight blocks"
    depth = w_ada.shape[0]
    n_mixers = 3
    attn_layers = [i for i in range(depth) if i % n_mixers == 1]
    last_ctx_layer = attn_layers[-1] if attn_layers else -1

    mods = _adaln_mods(c, c_ctx, w_ada, b_ada).reshape(depth, 2, N_MOD, d)
    rope = _rope_tables(l)
    w_next = [_interleave_gate_up(w_ffn_in[0, 0].astype(BF16)), w_ffn_out[0, 0].astype(BF16)]
    h, hc = x[0], ctx[0]
    for i in range(depth):
        kind, j = i % n_mixers, i // n_mixers
        ctx_live = i <= last_ctx_layer
        ctx_out = i < last_ctx_layer
        vec = [_sub_vecs(mods[i, 0], k, norm_pre[i, k], norm_post[i, k]) for k in range(3)]
        vec_c = [_sub_vecs(mods[i, 1], k, norm_pre[i, k], norm_post[i, k]) for k in range(3)]

        w_now = w_next
        h, *w_next = _ffn(h, vec[0], *w_now, FFN_RES, cast_next=(w_ffn_in, w_ffn_out, i, 1))
        if ctx_live:
            hc = _ffn(hc, vec_c[0], *w_now, FFN_RES)

        if kind == 0:
            hp = (hy_w_in[j].astype(BF16), hy_b_in[j], hy_w_sc[j], hy_b_sc[j], hy_f_w1[j], hy_f_b1[j],
                  hy_f_w2[j], hy_f_b2[j], hy_f_w3[j], hy_f_b3[j], hy_f_w4[j], hy_f_freq[j], hy_skip[j],
                  hy_w_out[j].astype(BF16), hy_b_out[j])
            h = _hyena_mixer(h, vec[1], hp, long_seq=True)
            if ctx_out:
                hc = _hyena_mixer(hc, vec_c[1], hp, long_seq=False)
        elif kind == 1:
            w_qkv = at_w_qkv[j].astype(BF16)
            qkv = _proj_in(h, vec[1], w_qkv, at_b_qkv[j], BF16, rope=rope,
                           q_blocks=d // 512, q_scale=HEAD_DIM ** -0.5)
            kvc = _proj_in(hc, vec_c[1], w_qkv[:, d:], at_b_qkv[j][d:], BF16)
            o = _attention(qkv, kvc, at_sink[j], d)
            h = _proj_out(o, at_w_o[j].astype(BF16), at_b_o[j], h, vec[1])
            assert not ctx_out, "context-query attention path is not needed for this depth"
        else:
            h = _pool_mixer(h, vec[1], pl_w[j], pl_b[j], pl_scale[j])
            assert not ctx_out, "context pooling path is not needed for this depth"

        w_now = w_next
        if i + 1 < depth:
            h, *w_next = _ffn(h, vec[2], *w_now, FFN_RES, cast_next=(w_ffn_in, w_ffn_out, i + 1, 0))
        else:
            h = _ffn(h, vec[2], *w_now, FFN_RES)
        if ctx_out:
            hc = _ffn(hc, vec_c[2], *w_now, FFN_RES)
    return h[None]
```

```python
import functools
import math

import jax
import jax.numpy as jnp
from jax import lax
from jax.experimental import pallas as pl
from jax.experimental.pallas import tpu as pltpu

F32 = jnp.float32
BF16 = jnp.bfloat16

GRID_W = 64
N_MOD = 9
NORM_EPS = 1e-6
FFN_RES = 0.5
FILTER_BANDS = 16
FILTER_EMB = 1 + 2 * FILTER_BANDS
DECAY_TARGET = 1e-2
SHORT_DECAY_PCT = 0.3
LONG_DECAY_PCT = 1.5
HEAD_DIM = 128
N_KV_HEADS = 4
WINDOW = 128
ATTN_BLOCK = 128
ROPE_THETA = 10000.0
POOL_SIZES = (2, 4, 8, 16)

VMEM_LIMIT_BYTES = 56 * 1024 * 1024
FFN_VMEM_LIMIT_BYTES = 60 * 1024 * 1024
SUBLANES = 8
LANES = 128
DFT_INNER = 128
HALO = 16
FFN_CHUNK = 512
MASK_BIAS = -1e30


def _cparams(*sem):
    return pltpu.CompilerParams(dimension_semantics=sem, vmem_limit_bytes=VMEM_LIMIT_BYTES)


def _rms(x, g):
    ms = jnp.mean(x * x, axis=-1, keepdims=True)
    return x * lax.rsqrt(ms + NORM_EPS) * g


def _modulated(h, vec_ref):
    return _rms(h, vec_ref[3:4, :]) * (1.0 + vec_ref[1:2, :]) + vec_ref[0:1, :]


def _store_modulated(h_ref, vec_ref, u_ref):
    gain = vec_ref[3:4, :] * (1.0 + vec_ref[1:2, :])
    u_ref[...] = (_rms(h_ref[...], gain) + vec_ref[0:1, :]).astype(u_ref.dtype)


def _mods_body(cb_ref, w_ref, b_ref, o_ref, s_ref):
    k_dim, bn = w_ref.shape[1], w_ref.shape[2]
    nl = bn // LANES

    @pl.when((pl.program_id(0) == 0) & (pl.program_id(1) == 0))
    def _():
        cv = cb_ref[...]
        s_ref[...] = cv * jax.nn.sigmoid(cv)

    def step(kg, acc):
        rows = pl.ds(pl.multiple_of(kg * SUBLANES, SUBLANES), SUBLANES)
        s = [s_ref[r, rows, :] for r in range(2)]
        new = list(acc)
        for j in range(nl):
            wv = w_ref[0, rows, j * LANES:(j + 1) * LANES]
            for r in range(2):
                new[r * nl + j] = acc[r * nl + j] + wv * s[r]
        return tuple(new)

    init = tuple(jnp.zeros((SUBLANES, LANES), F32) for _ in range(2 * nl))
    acc = lax.fori_loop(0, k_dim // SUBLANES, step, init, unroll=4)
    for r in range(2):
        for j in range(nl):
            cols = slice(j * LANES, (j + 1) * LANES)
            o_ref[0, r:r + 1, cols] = jnp.sum(acc[r * nl + j], axis=0, keepdims=True) + b_ref[0, :, cols]


def _adaln_mods(c, c_ctx, w_ada, b_ada):
    depth, d, n = w_ada.shape
    bn = 1024
    cb = jnp.broadcast_to(jnp.stack([c[0], c_ctx])[:, :, None], (2, d, LANES))
    return pl.pallas_call(
        _mods_body,
        grid=(depth, n // bn),
        in_specs=[pl.BlockSpec((2, d, LANES), lambda i, j: (0, 0, 0)),
                  pl.BlockSpec((1, d, bn), lambda i, j: (i, 0, j)),
                  pl.BlockSpec((1, 1, bn), lambda i, j: (i, 0, j))],
        out_specs=pl.BlockSpec((1, 2, bn), lambda i, j: (i, 0, j)),
        out_shape=jax.ShapeDtypeStruct((depth, 2, n), F32),
        scratch_shapes=[pltpu.VMEM((2, d, LANES), F32)],
        compiler_params=_cparams("arbitrary", "arbitrary"),
        name="adaln_mods",
    )(cb, w_ada, b_ada.reshape(depth, 1, n))


def _ffn_body(h_ref, vec_ref, w_in_hbm, w_out_hbm, *rest, res_w, cast):
    if cast:
        ci_ref, co_ref, o_ref, cib_ref, cob_ref, u_ref, acc_ref, wg_buf, wu_buf, wo_buf, sem = rest
    else:
        o_ref, u_ref, acc_ref, wg_buf, wu_buf, wo_buf, sem = rest
    i = pl.program_id(0)
    nb = pl.num_programs(0)
    bf = FFN_CHUNK
    nf = w_out_hbm.shape[0] // bf

    def chunk_copies(chunk, slot):
        gate_cols = pl.ds(pl.multiple_of(chunk * bf, bf), bf)
        up_cols = pl.ds(pl.multiple_of((chunk + nf) * bf, bf), bf)
        return (pltpu.make_async_copy(w_in_hbm.at[:, gate_cols], wg_buf.at[slot], sem.at[0, slot]),
                pltpu.make_async_copy(w_in_hbm.at[:, up_cols], wu_buf.at[slot], sem.at[1, slot]),
                pltpu.make_async_copy(w_out_hbm.at[gate_cols, :], wo_buf.at[slot], sem.at[2, slot]))

    @pl.when(i == 0)
    def _():
        for cp in chunk_copies(0, 0):
            cp.start()

    _store_modulated(h_ref, vec_ref, u_ref)
    acc_ref[...] = jnp.zeros_like(acc_ref)
    if cast:
        cib_ref[...] = ci_ref[...].astype(cib_ref.dtype)
        cob_ref[...] = co_ref[...].astype(cob_ref.dtype)

    def chunk_step(c, carry):
        slot = lax.rem(i * nf + c, 2)
        for cp in chunk_copies(c, slot):
            cp.wait()
        for cp in chunk_copies(lax.rem(c + 1, nf), 1 - slot):
            cp.start()
        u = u_ref[...]
        g = jnp.dot(u, wg_buf[slot], preferred_element_type=F32)
        p = jnp.dot(u, wu_buf[slot], preferred_element_type=F32)
        a = (g * jax.nn.sigmoid(g) * p).astype(BF16)
        acc_ref[...] += jnp.dot(a, wo_buf[slot], preferred_element_type=F32)
        return carry

    lax.fori_loop(0, nf, chunk_step, 0)
    gate = (res_w * vec_ref[2:3, :]) * vec_ref[4:5, :]
    o_ref[...] = h_ref[...] + _rms(acc_ref[...], gate)

    @pl.when(i == nb - 1)
    def _():
        for cp in chunk_copies(0, lax.rem(nb * nf, 2)):
            cp.wait()


def _ffn(h, vecs, w_in, w_out, res_w, cast_next=None):
    m, d = h.shape
    f = w_out.shape[0]
    bm = min(m, 512)
    bf = FFN_CHUNK
    nb = m // bm
    in_specs = [pl.BlockSpec((bm, d), lambda i: (i, 0)),
                pl.BlockSpec((SUBLANES, d), lambda i: (0, 0)),
                pl.BlockSpec(memory_space=pl.ANY),
                pl.BlockSpec(memory_space=pl.ANY)]
    out_specs = [pl.BlockSpec((bm, d), lambda i: (i, 0))]
    out_shape = [jax.ShapeDtypeStruct((m, d), F32)]
    args = [h, vecs, w_in, w_out]
    if cast_next is not None:
        *srcs, layer, slot = cast_next
        for w in srcs:
            rows, cols = w.shape[2] // nb, w.shape[3]
            assert rows * nb == w.shape[2] and rows % HALO == 0, "weight rows must split evenly over the row blocks"
            in_specs.append(pl.BlockSpec((None, None, rows, cols), lambda i: (layer, slot, i, 0)))
            out_specs.append(pl.BlockSpec((rows, cols), lambda i: (i, 0)))
            out_shape.append(jax.ShapeDtypeStruct(w.shape[2:], BF16))
            args.append(w)
    outs = pl.pallas_call(
        functools.partial(_ffn_body, res_w=res_w, cast=cast_next is not None),
        grid=(nb,),
        in_specs=in_specs,
        out_specs=out_specs,
        out_shape=out_shape,
        scratch_shapes=[pltpu.VMEM((bm, d), BF16), pltpu.VMEM((bm, d), F32),
                        pltpu.VMEM((2, d, bf), BF16), pltpu.VMEM((2, d, bf), BF16), pltpu.VMEM((2, bf, d), BF16),
                        pltpu.SemaphoreType.DMA((3, 2))],
        compiler_params=pltpu.CompilerParams(dimension_semantics=("arbitrary",), vmem_limit_bytes=FFN_VMEM_LIMIT_BYTES),
        name="ffn",
    )(*args)
    return outs if cast_next is not None else outs[0]


def _proj_in_body(h_ref, vec_ref, w_ref, b_ref, *rest, rope_blocks, q_blocks, q_scale):
    if rope_blocks:
        cos_ref, sa_ref, sb_ref, o_ref, u_ref = rest
    else:
        o_ref, u_ref = rest
    j = pl.program_id(1)

    @pl.when(j == 0)
    def _():
        _store_modulated(h_ref, vec_ref, u_ref)

    y = jnp.dot(u_ref[...], w_ref[...], preferred_element_type=F32) + b_ref[...]
    if not rope_blocks:
        o_ref[...] = y.astype(o_ref.dtype)
        return

    rot = jnp.where(j < q_blocks, q_scale, jnp.where(j < rope_blocks, 1.0, 0.0)).astype(F32)
    keep = jnp.where(j < rope_blocks, 0.0, 1.0).astype(F32)
    cos, sa, sb = cos_ref[...] * rot + keep, sa_ref[...] * rot, sb_ref[...] * rot
    for hd in range(y.shape[1] // HEAD_DIM):
        cols = slice(hd * HEAD_DIM, (hd + 1) * HEAD_DIM)
        xh = y[:, cols]
        r = (xh * cos + pltpu.roll(xh, HEAD_DIM - HEAD_DIM // 4, 1) * sa
             + pltpu.roll(xh, HEAD_DIM // 4, 1) * sb)
        o_ref[:, cols] = r.astype(o_ref.dtype)


def _proj_in(h, vecs, w, b, out_dtype, rope=None, q_blocks=0, q_scale=1.0):
    m, d = h.shape
    n = w.shape[1]
    bm = min(m, 1024)
    bn = 512
    in_specs = [pl.BlockSpec((bm, d), lambda i, j: (i, 0)),
                pl.BlockSpec((SUBLANES, d), lambda i, j: (0, 0)),
                pl.BlockSpec((d, bn), lambda i, j: (0, j)),
                pl.BlockSpec((1, bn), lambda i, j: (0, j))]
    args = [h, vecs, w, b.reshape(1, n)]
    rope_blocks = 0
    if rope is not None:
        rope_blocks = q_blocks + 1
        in_specs += [pl.BlockSpec((bm, HEAD_DIM), lambda i, j: (i, 0))] * 3
        args += list(rope)
    return pl.pallas_call(
        functools.partial(_proj_in_body, rope_blocks=rope_blocks, q_blocks=q_blocks, q_scale=q_scale),
        grid=(m // bm, n // bn),
        in_specs=in_specs,
        out_specs=pl.BlockSpec((bm, bn), lambda i, j: (i, j)),
        out_shape=jax.ShapeDtypeStruct((m, n), out_dtype),
        scratch_shapes=[pltpu.VMEM((bm, d), BF16)],
        compiler_params=_cparams("arbitrary", "arbitrary"),
        name="proj_in",
    )(*args)


def _proj_out_body(a_ref, w_ref, b_ref, h_ref, vec_ref, o_ref):
    y = jnp.dot(a_ref[...], w_ref[...], preferred_element_type=F32) + b_ref[...]
    o_ref[...] = h_ref[...] + vec_ref[2:3, :] * _rms(y, vec_ref[4:5, :])


def _proj_out(a, w, b, h, vecs):
    m, d = h.shape
    k = a.shape[1]
    bm = min(m, 256)
    return pl.pallas_call(
        _proj_out_body,
        grid=(m // bm,),
        in_specs=[pl.BlockSpec((bm, k), lambda i: (i, 0)),
                  pl.BlockSpec((k, d), lambda i: (0, 0)),
                  pl.BlockSpec((1, d), lambda i: (0, 0)),
                  pl.BlockSpec((bm, d), lambda i: (i, 0)),
                  pl.BlockSpec((SUBLANES, d), lambda i: (0, 0))],
        out_specs=pl.BlockSpec((bm, d), lambda i: (i, 0)),
        out_shape=jax.ShapeDtypeStruct((m, d), F32),
        compiler_params=_cparams("arbitrary"),
        name="proj_out",
    )(a, w, b.reshape(1, d), h, vecs)


def _hy_in_body(h_ref, hp_ref, hn_ref, vec_ref, w0_ref, w1_ref, w2_ref, b0_ref, b1_ref, b2_ref, wsc_ref, bsc_ref,
                x0_out, xin_out, u_ref, uh_ref):
    i = pl.program_id(0)
    last = pl.num_programs(0) - 1

    @pl.when(pl.program_id(1) == 0)
    def _():
        _store_modulated(h_ref, vec_ref, u_ref)
        uh_ref[0:SUBLANES, :] = hp_ref[...]
        uh_ref[SUBLANES:, :] = hn_ref[...]
        _store_modulated(uh_ref, vec_ref, uh_ref)

    def conv(w_ref, b_ref, part):
        z = jnp.dot(u_ref[...], w_ref[...], preferred_element_type=F32) + b_ref[...]
        zh = jnp.dot(uh_ref[...].astype(BF16), w_ref[...], preferred_element_type=F32) + b_ref[...]
        bm = z.shape[0]
        prev_row = jnp.where(i > 0, zh[SUBLANES - 1:SUBLANES, :], 0.0)
        next_row = jnp.where(i < last, zh[SUBLANES:SUBLANES + 1, :], 0.0)
        rid = lax.broadcasted_iota(jnp.int32, z.shape, 0)
        up = jnp.where(rid == 0, prev_row, pltpu.roll(z, 1, 0))
        dn = jnp.where(rid == bm - 1, next_row, pltpu.roll(z, bm - 1, 0))
        return (up * wsc_ref[0, part:part + 1, :] + z * wsc_ref[1, part:part + 1, :]
                + dn * wsc_ref[2, part:part + 1, :] + bsc_ref[part:part + 1, :])

    x0_out[...] = conv(w0_ref, b0_ref, 0).astype(x0_out.dtype)
    xin_out[...] = (conv(w2_ref, b2_ref, 2) * conv(w1_ref, b1_ref, 1)).astype(xin_out.dtype)


def _hy_in_gate(h, vecs, w_in, b_in, w_sc, b_sc):
    l, d = h.shape
    bm = min(l, 1024)
    bc = min(d, 512)
    nc = d // bc
    rb = bm // SUBLANES
    nrow8 = l // SUBLANES
    wspecs = [pl.BlockSpec((d, bc), functools.partial(lambda i, j, p: (0, p * nc + j), p=part)) for part in range(3)]
    bspecs = [pl.BlockSpec((1, bc), functools.partial(lambda i, j, p: (0, p * nc + j), p=part)) for part in range(3)]
    out_spec = pl.BlockSpec((bm, bc), lambda i, j: (i, j))
    return pl.pallas_call(
        _hy_in_body,
        grid=(l // bm, nc),
        in_specs=[pl.BlockSpec((bm, d), lambda i, j: (i, 0)),
                  pl.BlockSpec((SUBLANES, d), lambda i, j: (jnp.maximum(i * rb - 1, 0), 0)),
                  pl.BlockSpec((SUBLANES, d), lambda i, j: (jnp.minimum((i + 1) * rb, nrow8 - 1), 0)),
                  pl.BlockSpec((SUBLANES, d), lambda i, j: (0, 0))] + wspecs + bspecs
                 + [pl.BlockSpec((3, 3, bc), lambda i, j: (0, 0, j)),
                    pl.BlockSpec((3, bc), lambda i, j: (0, j))],
        out_specs=[out_spec, out_spec],
        out_shape=[jax.ShapeDtypeStruct((l, d), BF16)] * 2,
        scratch_shapes=[pltpu.VMEM((bm, d), BF16), pltpu.VMEM((2 * SUBLANES, d), F32)],
        compiler_params=_cparams("arbitrary", "arbitrary"),
        name="hyena_in_gate",
    )(h, h, h, vecs, w_in, w_in, w_in, *([b_in.reshape(1, 3 * d)] * 3), w_sc.reshape(3, 3, d), b_sc.reshape(3, d))


def _filter_mlp_body(z_ref, w1, b1, w2, b2, w3, b3, fr, o_ref):
    hp = lax.Precision.HIGHEST
    f = jnp.sin(fr[0:1, :] * (jnp.dot(z_ref[...], w1[...], precision=hp, preferred_element_type=F32) + b1[...]))
    f = jnp.sin(fr[1:2, :] * (jnp.dot(f, w2[...], precision=hp, preferred_element_type=F32) + b2[...]))
    o_ref[...] = jnp.sin(fr[2:3, :] * (jnp.dot(f, w3[...], precision=hp, preferred_element_type=F32) + b3[...]))


def _filter_table(l, w1, b1, w2, b2, w3, b3, freq):
    hid = w1.shape[1]
    t = jnp.linspace(0.0, 1.0, l, dtype=F32)[:, None]
    omega = 2.0 * math.pi * jnp.arange(l, dtype=F32)[:, None] / l
    bands = jnp.linspace(1e-4, FILTER_BANDS - 1, FILTER_BANDS, dtype=F32)[None, :]
    emb = jnp.concatenate([t, jnp.cos(bands * omega), -jnp.sin(bands * omega)], axis=-1)
    emb = jnp.pad(emb, ((0, 0), (0, LANES - emb.shape[1])))
    w1p = jnp.pad(w1, ((0, LANES - w1.shape[0]), (0, 0)))
    pack = LANES // hid
    rows = l // pack
    eye = jnp.eye(pack, dtype=F32)
    emb_p = jnp.concatenate([emb[p * rows:(p + 1) * rows] for p in range(pack)], axis=1)
    wide = pack * hid
    bm = min(rows, 512)

    def full(shape):
        return pl.BlockSpec(shape, lambda i: (0,) * len(shape))

    f_p = pl.pallas_call(
        _filter_mlp_body,
        grid=(rows // bm,),
        in_specs=[pl.BlockSpec((bm, pack * LANES), lambda i: (i, 0)),
                  full((pack * LANES, wide)), full((1, wide)), full((wide, wide)), full((1, wide)),
                  full((wide, wide)), full((1, wide)), full((3, wide))],
        out_specs=pl.BlockSpec((bm, wide), lambda i: (i, 0)),
        out_shape=jax.ShapeDtypeStruct((rows, wide), F32),
        compiler_params=_cparams("arbitrary"),
        name="hyena_filter_mlp",
    )(emb_p, jnp.kron(eye, w1p), jnp.tile(b1, pack).reshape(1, wide), jnp.kron(eye, w2),
      jnp.tile(b2, pack).reshape(1, wide), jnp.kron(eye, w3), jnp.tile(b3, pack).reshape(1, wide),
      jnp.tile(freq, (1, pack)))
    f = jnp.concatenate([f_p[:, p * hid:(p + 1) * hid] for p in range(pack)], axis=0)
    tab = jnp.concatenate([f, t, jnp.ones((l, 1), F32)], axis=-1)
    tab = jnp.pad(tab, ((0, 0), (0, LANES - tab.shape[1])))
    return jnp.concatenate([tab, jnp.zeros((1, LANES), F32), tab[1:][::-1]], axis=0)


def _decay_rates(d):
    return jnp.abs(jnp.linspace(math.log(DECAY_TARGET) / LONG_DECAY_PCT,
                                math.log(DECAY_TARGET) / SHORT_DECAY_PCT, d, dtype=F32))[None, :]


def _split_bf16(x):
    hi = x.astype(BF16)
    return hi, (x - hi.astype(F32)).astype(BF16)


def _stack_3pass(w):
    w_hi, w_lo = _split_bf16(w)
    return jnp.concatenate([w_hi, w_hi, w_lo], axis=0)


def _dot_3pass(f, w3):
    f_hi, f_lo = _split_bf16(f)
    return jnp.dot(jnp.concatenate([f_hi, f_lo, f_hi], axis=1), w3, preferred_element_type=F32)


def _filter_rows(ft, wa3, wb3, dl_ref):
    hid = wa3.shape[0] // 3
    half = ft.shape[0] // 2
    f, t, keep = ft[:, :hid], ft[:, hid:hid + 1], ft[:, hid + 1:hid + 2]
    y = jnp.concatenate([_dot_3pass(f[:half], wa3), _dot_3pass(f[half:], wb3)], axis=0)
    return y * (jnp.exp(-t * dl_ref[...]) * keep)


def _outer_stage(fa_ref, cols, o_ref):
    ys = [jnp.dot(fa_ref[...], xj, preferred_element_type=F32).astype(BF16) for xj in cols]
    o_ref[...] = pltpu.einshape("brd->rbd", jnp.stack(ys))


def _dft_a_body(x_ref, fa_ref, o_ref):
    xb = pltpu.einshape("abd->bad", x_ref[...])
    _outer_stage(fa_ref, [xb[j] for j in range(xb.shape[0])], o_ref)


def _dft_a_filter_body(ft_ref, wa_ref, wb_ref, dl_ref, fa_ref, o_ref):
    wa3, wb3 = _stack_3pass(wa_ref[...]), _stack_3pass(wb_ref[...])
    cols = [_filter_rows(ft_ref[j], wa3, wb3, dl_ref).astype(BF16) for j in range(ft_ref.shape[0])]
    _outer_stage(fa_ref, cols, o_ref)


def _dft_a(x3, fa, d_chunk):
    a, b, d = x3.shape
    rows = fa.shape[0]
    return pl.pallas_call(
        _dft_a_body,
        grid=(b // HALO, d // d_chunk),
        in_specs=[pl.BlockSpec((a, HALO, d_chunk), lambda i, j: (0, i, j)),
                  pl.BlockSpec((rows, a), lambda i, j: (0, 0))],
        out_specs=pl.BlockSpec((rows, HALO, d_chunk), lambda i, j: (0, i, j)),
        out_shape=jax.ShapeDtypeStruct((rows, b, d), BF16),
        compiler_params=_cparams("arbitrary", "arbitrary"),
        name="dft_outer",
    )(x3, fa)


def _dft_a_filter(ft3, w4, deltas, fa, d_chunk):
    b, n1, _ = ft3.shape
    hid, d2 = w4.shape
    d = d2 // 2
    rows = fa.shape[0]
    nd = d // d_chunk
    return pl.pallas_call(
        _dft_a_filter_body,
        grid=(b // HALO, nd),
        in_specs=[pl.BlockSpec((HALO, n1, LANES), lambda i, j: (i, 0, 0)),
                  pl.BlockSpec((hid, d_chunk), lambda i, j: (0, j)),
                  pl.BlockSpec((hid, d_chunk), lambda i, j: (0, j + nd)),
                  pl.BlockSpec((1, d_chunk), lambda i, j: (0, j)),
                  pl.BlockSpec((rows, n1), lambda i, j: (0, 0))],
        out_specs=pl.BlockSpec((rows, HALO, d_chunk), lambda i, j: (0, i, j)),
        out_shape=jax.ShapeDtypeStruct((rows, b, d), BF16),
        compiler_params=_cparams("arbitrary", "arbitrary"),
        name="dft_outer_filter",
    )(ft3, w4, w4, deltas, fa)


def _dft_inner_body(y_ref, yf_ref, gf_ref, gi_ref, o_ref):
    b, dc = y_ref.shape[2], y_ref.shape[3]
    for ci in range(y_ref.shape[1]):
        z = jnp.dot(gf_ref[ci], y_ref[:, ci].reshape(2 * b, dc), preferred_element_type=F32)
        hh = jnp.dot(gf_ref[ci], yf_ref[:, ci].reshape(2 * b, dc), preferred_element_type=F32)
        zr, zi, hr, hi = z[:b], z[b:], hh[:b], hh[b:]
        p = jnp.concatenate([zr * hr - zi * hi, zr * hi + zi * hr], axis=0).astype(BF16)
        cc = jnp.dot(gi_ref[ci], p, preferred_element_type=F32)
        o_ref[:, ci] = cc.reshape(2, b, dc).astype(o_ref.dtype)


def _dft_inner(y4, yf4, gf, gi):
    _, nc, b, d = y4.shape
    cblk = SUBLANES
    dc = min(d, 1024)
    yspec = pl.BlockSpec((2, cblk, b, dc), lambda c, j: (0, c, 0, j))
    gspec = pl.BlockSpec((cblk, 2 * b, 2 * b), lambda c, j: (c, 0, 0))
    return pl.pallas_call(
        _dft_inner_body,
        grid=(nc // cblk, d // dc),
        in_specs=[yspec, yspec, gspec, gspec],
        out_specs=yspec,
        out_shape=jax.ShapeDtypeStruct(y4.shape, BF16),
        compiler_params=_cparams("arbitrary", "arbitrary"),
        name="dft_inner",
    )(y4, yf4, gf, gi)


def _dft_c_body(c_ref, m_ref, xin_ref, x0_ref, skip_ref, o_ref):
    cb = pltpu.einshape("rbd->brd", c_ref[...])
    outs = [jnp.dot(m_ref[...], cb[j], preferred_element_type=F32).astype(BF16) for j in range(cb.shape[0])]
    conv = pltpu.einshape("bad->abd", jnp.stack(outs)).astype(F32)
    gated = x0_ref[...].astype(F32) * (conv + xin_ref[...].astype(F32) * skip_ref[...][None])
    o_ref[...] = gated.astype(o_ref.dtype)


def _dft_c(c3, mc, xin3, x03, skip, d_chunk):
    rows, b, d = c3.shape
    a = mc.shape[0]
    xspec = pl.BlockSpec((a, HALO, d_chunk), lambda i, j: (0, i, j))
    return pl.pallas_call(
        _dft_c_body,
        grid=(b // HALO, d // d_chunk),
        in_specs=[pl.BlockSpec((rows, HALO, d_chunk), lambda i, j: (0, i, j)),
                  pl.BlockSpec((a, rows), lambda i, j: (0, 0)),
                  xspec, xspec,
                  pl.BlockSpec((1, d_chunk), lambda i, j: (0, j))],
        out_specs=xspec,
        out_shape=jax.ShapeDtypeStruct((a, b, d), BF16),
        compiler_params=_cparams("arbitrary", "arbitrary"),
        name="dft_outer_inv",
    )(c3, mc, xin3, x03, skip)


def _cis(num, period):
    ang = (2.0 * math.pi / period) * num.astype(F32)
    return jnp.cos(ang), jnp.sin(ang)


def _long_conv_gated(xin, x0, ftab, w4, skip):
    l, d = xin.shape
    b = DFT_INNER
    a = l // b
    n1, n = 2 * a, 2 * l
    nc = n1 // 2 + SUBLANES
    dc = min(d, 512)
    ci = jnp.arange(nc, dtype=jnp.int32)
    ai = jnp.arange(n1, dtype=jnp.int32)
    live = (ci <= n1 // 2).astype(F32)[:, None]
    cr, sr = _cis((ci[:, None] * ai[None, :]) % n1, n1)
    cr, sr = cr * live, sr * live
    fa_full = jnp.concatenate([cr, -sr], axis=0).astype(BF16)
    fa = fa_full[:, :a]
    fold = jnp.where((ci == 0) | (ci == n1 // 2), 1.0, 2.0)[None, :] / n
    mc = jnp.concatenate([cr[:, :a].T * fold, -sr[:, :a].T * fold], axis=1).astype(BF16)
    ei = jnp.arange(b, dtype=jnp.int32)
    kk = ci[:, None, None] + n1 * ei[None, :, None]
    tr, ts = _cis((kk * ei[None, None, :]) % n, n)
    ti = -ts
    gf = jnp.concatenate([jnp.concatenate([tr, -ti], axis=2),
                          jnp.concatenate([ti, tr], axis=2)], axis=1).astype(BF16)
    trt, tit = jnp.swapaxes(tr, 1, 2), jnp.swapaxes(ti, 1, 2)
    gi = jnp.concatenate([jnp.concatenate([trt, tit], axis=2),
                          jnp.concatenate([-tit, trt], axis=2)], axis=1).astype(BF16)

    ft3 = jnp.swapaxes(ftab.reshape(n1, b, LANES), 0, 1)
    hf = _dft_a_filter(ft3, w4, _decay_rates(d), fa_full, dc)
    yx = _dft_a(xin.reshape(a, b, d), fa, dc)
    c4 = _dft_inner(yx.reshape(2, nc, b, d), hf.reshape(2, nc, b, d), gf, gi)
    y3 = _dft_c(c4.reshape(2 * nc, b, d), mc, xin.reshape(a, b, d), x0.reshape(a, b, d), skip.reshape(1, d), dc)
    return y3.reshape(l, d)


def _ctx_conv_body(x_ref, x0_ref, ft_ref, wa_ref, wb_ref, dl_ref, skip_ref, ff_ref, fh_ref, mi_ref, o_ref):
    x = x_ref[...].astype(F32)
    n = ft_ref.shape[0]
    filt = _filter_rows(ft_ref[...], _stack_3pass(wa_ref[...]), _stack_3pass(wb_ref[...]), dl_ref)
    xs = jnp.dot(ff_ref[...], x.astype(BF16), preferred_element_type=F32)
    hs = jnp.dot(fh_ref[...], filt.astype(BF16), preferred_element_type=F32)
    xr, xi, hr, hi = xs[:n], xs[n:], hs[:n], hs[n:]
    p = jnp.concatenate([xr * hr - xi * hi, xr * hi + xi * hr], axis=0).astype(BF16)
    conv = jnp.dot(mi_ref[...], p, preferred_element_type=F32)
    o_ref[...] = (x0_ref[...].astype(F32) * (conv + x * skip_ref[...])).astype(o_ref.dtype)


def _short_seq_conv_gated(xin, x0, ftab, w4, skip):
    l, d = xin.shape
    n = 2 * l
    hid = w4.shape[0]
    ni = jnp.arange(n, dtype=jnp.int32)
    cr, sr = _cis((ni[:, None] * ni[None, :]) % n, n)
    fh = jnp.concatenate([cr, -sr], axis=0).astype(BF16)
    ff = fh[:, :l]
    mi = (jnp.concatenate([cr[:l], -sr[:l]], axis=1) * (1.0 / n)).astype(BF16)
    dc = min(d, 512)
    nd = d // dc
    return pl.pallas_call(
        _ctx_conv_body,
        grid=(nd,),
        in_specs=[pl.BlockSpec((l, dc), lambda j: (0, j)),
                  pl.BlockSpec((l, dc), lambda j: (0, j)),
                  pl.BlockSpec((n, LANES), lambda j: (0, 0)),
                  pl.BlockSpec((hid, dc), lambda j: (0, j)),
                  pl.BlockSpec((hid, dc), lambda j: (0, j + nd)),
                  pl.BlockSpec((1, dc), lambda j: (0, j)),
                  pl.BlockSpec((1, dc), lambda j: (0, j)),
                  pl.BlockSpec((2 * n, l), lambda j: (0, 0)),
                  pl.BlockSpec((2 * n, n), lambda j: (0, 0)),
                  pl.BlockSpec((l, 2 * n), lambda j: (0, 0))],
        out_specs=pl.BlockSpec((l, dc), lambda j: (0, j)),
        out_shape=jax.ShapeDtypeStruct((l, d), BF16),
        compiler_params=_cparams("arbitrary"),
        name="ctx_conv",
    )(xin, x0, ftab, w4, w4, _decay_rates(d), skip.reshape(1, d), ff, fh, mi)


def _hyena_mixer(h, vecs, p, long_seq):
    w_in, b_in, w_sc, b_sc, f_w1, f_b1, f_w2, f_b2, f_w3, f_b3, f_w4, f_freq, skip, w_out, b_out = p
    l = h.shape[0]
    x0, xin = _hy_in_gate(h, vecs, w_in, b_in, w_sc, b_sc)
    ftab = _filter_table(l, f_w1, f_b1, f_w2, f_b2, f_w3, f_b3, f_freq)
    conv = _long_conv_gated if long_seq else _short_seq_conv_gated
    y = conv(xin, x0, ftab, f_w4, skip)
    return _proj_out(y, w_out, b_out, h, vecs)


def _attn_body(sink_ref, q_ref, kp_ref, kc_ref, kn_ref, vp_ref, vc_ref, vn_ref, kx_ref, vx_ref, bias_ref, o_ref,
               *, group):
    blk = q_ref.shape[0]
    bias = bias_ref[0]
    rid = lax.broadcasted_iota(jnp.int32, (group * blk, 1), 0)
    for kh in range(N_KV_HEADS):
        hs = slice(kh * HEAD_DIM, (kh + 1) * HEAD_DIM)
        heads = [kh * group + g for g in range(group)]
        q = jnp.concatenate([q_ref[:, hd * HEAD_DIM:(hd + 1) * HEAD_DIM] for hd in heads], axis=0)
        keys = jnp.concatenate([kp_ref[:, hs], kc_ref[:, hs], kn_ref[:, hs], kx_ref[:, hs]], axis=0)
        vals = jnp.concatenate([vp_ref[:, hs], vc_ref[:, hs], vn_ref[:, hs], vx_ref[:, hs]], axis=0)
        s = lax.dot_general(q, keys, (((1,), (1,)), ((), ())), preferred_element_type=F32) + bias
        sink = jnp.zeros((group * blk, 1), F32)
        for g, hd in enumerate(heads):
            sink = jnp.where((rid >= g * blk) & (rid < (g + 1) * blk), sink_ref[hd], sink)
        mx = jnp.maximum(jnp.max(s, axis=-1, keepdims=True), sink)
        pr = jnp.exp(s - mx)
        denom = jnp.sum(pr, axis=-1, keepdims=True) + jnp.exp(sink - mx)
        o = jnp.dot(pr.astype(BF16), vals, preferred_element_type=F32) / denom
        for g, hd in enumerate(heads):
            o_ref[:, hd * HEAD_DIM:(hd + 1) * HEAD_DIM] = o[g * blk:(g + 1) * blk].astype(o_ref.dtype)


def _attention(qkv, kvc, sink, d):
    l = qkv.shape[0]
    c = kvc.shape[0]
    blk = ATTN_BLOCK
    nb = l // blk
    group = d // HEAD_DIM // N_KV_HEADS
    kvw = N_KV_HEADS * HEAD_DIM
    kcol = d // kvw
    qi = jnp.arange(group * blk, dtype=jnp.int32)[:, None] % blk
    si = jnp.arange(3 * blk + c, dtype=jnp.int32)[None, :]
    kb, ki = si // blk, si % blk
    prev_ok, next_ok = (kb == 0) & (ki >= qi), (kb == 2) & (ki <= qi)
    always = (kb == 1) | (kb >= 3)
    variants = [always | next_ok, always | prev_ok | next_ok, always | prev_ok]
    bias = jnp.stack([jnp.where(v, 0.0, MASK_BIAS).astype(F32) for v in variants])

    def kv_spec(col, shift):
        return pl.BlockSpec((blk, kvw), lambda n: (jnp.clip(n + shift, 0, nb - 1), col))

    return pl.pallas_call(
        functools.partial(_attn_body, group=group),
        grid=(nb,),
        in_specs=[pl.BlockSpec(memory_space=pltpu.SMEM),
                  pl.BlockSpec((blk, d), lambda n: (n, 0)),
                  kv_spec(kcol, -1), kv_spec(kcol, 0), kv_spec(kcol, 1),
                  kv_spec(kcol + 1, -1), kv_spec(kcol + 1, 0), kv_spec(kcol + 1, 1),
                  pl.BlockSpec((c, kvw), lambda n: (0, 0)),
                  pl.BlockSpec((c, kvw), lambda n: (0, 1)),
                  pl.BlockSpec((1, group * blk, 3 * blk + c),
                               lambda n: (jnp.where(n == 0, 0, jnp.where(n == nb - 1, 2, 1)), 0, 0))],
        out_specs=pl.BlockSpec((blk, d), lambda n: (n, 0)),
        out_shape=jax.ShapeDtypeStruct((l, d), BF16),
        compiler_params=_cparams("arbitrary"),
        name="window_attn",
    )(sink, qkv, qkv, qkv, qkv, qkv, qkv, qkv, kvc, kvc, bias)


def _rope_tables(l):
    rows = l // GRID_W
    pos_row = jnp.broadcast_to(jnp.arange(rows)[:, None], (rows, GRID_W)).reshape(-1).astype(F32)
    pos_col = jnp.broadcast_to(jnp.arange(GRID_W)[None, :], (rows, GRID_W)).reshape(-1).astype(F32)
    pairs = HEAD_DIM // 4
    inv_freq = ROPE_THETA ** (-jnp.arange(pairs, dtype=F32) / pairs)
    ang_row = pos_row[:, None] * inv_freq[None, :]
    ang_col = pos_col[:, None] * inv_freq[None, :]
    zeros = jnp.zeros_like(ang_row)
    cos = jnp.concatenate([jnp.cos(ang_row)] * 2 + [jnp.cos(ang_col)] * 2, axis=-1)
    sin_a = jnp.concatenate([-jnp.sin(ang_row), zeros, -jnp.sin(ang_col), zeros], axis=-1)
    sin_b = jnp.concatenate([zeros, jnp.sin(ang_row), zeros, jnp.sin(ang_col)], axis=-1)
    return cos, sin_a, sin_b


def _pool_body(hc_ref, hp_ref, hn_ref, vec_ref, w_ref, b_ref, sc_ref, o_ref, y_ref, *, seq_len):
    i = pl.program_id(0)
    last = pl.num_programs(0) - 1
    bm, d = hc_ref.shape
    gw = d // len(POOL_SIZES)
    h = hc_ref[...]
    u = _modulated(h, vec_ref)
    up = jnp.where(i > 0, _modulated(hp_ref[...], vec_ref), 0.0)
    un = jnp.where(i < last, _modulated(hn_ref[...], vec_ref), 0.0)
    ext_rows = bm + 2 * SUBLANES
    t = i * bm + lax.broadcasted_iota(jnp.int32, (bm, 1), 0)
    for g, size in enumerate(POOL_SIZES):
        cols = slice(g * gw, (g + 1) * gw)
        ext = jnp.concatenate([up[:, cols], u[:, cols], un[:, cols]], axis=0)
        acc, span = ext, 1
        while span < size:
            acc = acc + pltpu.roll(acc, ext_rows - span, 0)
            span *= 2
        start = SUBLANES - size // 2
        win = pltpu.roll(acc, ext_rows - start, 0)[:bm] if start else acc[:bm]
        lo = jnp.clip(t - size // 2, 0, seq_len)
        hi = jnp.clip(t - size // 2 + size, 0, seq_len)
        part = win / (hi - lo).astype(F32) - u[:, cols]
        yg = jnp.dot(part.astype(BF16), w_ref[g], preferred_element_type=F32)
        y_ref[:, cols] = (yg + b_ref[:, cols]) * sc_ref[:, cols]
    o_ref[...] = h + vec_ref[2:3, :] * _rms(y_ref[...], vec_ref[4:5, :])


def _pool_mixer(h, vecs, w, b, scale):
    l, d = h.shape
    bm = min(l, 256)
    rb = bm // SUBLANES
    nrow8 = l // SUBLANES
    ng, gw = w.shape[0], w.shape[1]
    return pl.pallas_call(
        functools.partial(_pool_body, seq_len=l),
        grid=(l // bm,),
        in_specs=[pl.BlockSpec((bm, d), lambda i: (i, 0)),
                  pl.BlockSpec((SUBLANES, d), lambda i: (jnp.maximum(i * rb - 1, 0), 0)),
                  pl.BlockSpec((SUBLANES, d), lambda i: (jnp.minimum((i + 1) * rb, nrow8 - 1), 0)),
                  pl.BlockSpec((SUBLANES, d), lambda i: (0, 0)),
                  pl.BlockSpec((ng, gw, gw), lambda i: (0, 0, 0)),
                  pl.BlockSpec((1, d), lambda i: (0, 0)),
                  pl.BlockSpec((1, d), lambda i: (0, 0))],
        out_specs=pl.BlockSpec((bm, d), lambda i: (i, 0)),
        out_shape=jax.ShapeDtypeStruct((l, d), F32),
        scratch_shapes=[pltpu.VMEM((bm, d), F32)],
        compiler_params=_cparams("arbitrary"),
        name="pool_mixer",
    )(h, h, h, vecs, w.astype(BF16), b.reshape(1, d), scale.reshape(1, d))


def _sub_vecs(mod, k, g_pre, g_post):
    rows = [mod[3 * k], mod[3 * k + 1], mod[3 * k + 2], g_pre, g_post]
    return jnp.stack(rows + [jnp.zeros_like(g_pre)] * (SUBLANES - len(rows)))


def kernel(x, c, ctx, c_ctx, w_ada, b_ada, norm_pre, norm_post, w_ffn_in, w_ffn_out, hy_w_in, hy_b_in, hy_w_sc, hy_b_sc, hy_f_w1, hy_f_b1, hy_f_w2, hy_f_b2, hy_f_w3, hy_f_b3, hy_f_w4, hy_f_freq, hy_skip, hy_w_out, hy_b_out, at_w_qkv, at_b_qkv, at_sink, at_w_o, at_b_o, pl_w, pl_b, pl_scale):
    bsz, l, d = x.shape
    assert bsz == 1, "kernel handles a single batch element"
    assert d % 1024 == 0 and l % 1024 == 0 and ctx.shape[1] % 256 == 0, "block sizes assume these multiples"
    assert w_ffn_out.shape[2] % 512 == 0, "FFN width must be a multiple of the 512-column weight blocks"
    depth = w_ada.shape[0]
    n_mixers = 3
    attn_layers = [i for i in range(depth) if i % n_mixers == 1]
    last_ctx_layer = attn_layers[-1] if attn_layers else -1

    mods = _adaln_mods(c, c_ctx, w_ada, b_ada).reshape(depth, 2, N_MOD, d)
    rope = _rope_tables(l)
    w_next = [w_ffn_in[0, 0].astype(BF16), w_ffn_out[0, 0].astype(BF16)]
    h, hc = x[0], ctx[0]
    for i in range(depth):
        kind, j = i % n_mixers, i // n_mixers
        ctx_live = i <= last_ctx_layer
        ctx_out = i < last_ctx_layer
        vec = [_sub_vecs(mods[i, 0], k, norm_pre[i, k], norm_post[i, k]) for k in range(3)]
        vec_c = [_sub_vecs(mods[i, 1], k, norm_pre[i, k], norm_post[i, k]) for k in range(3)]

        w_now = w_next
        h, *w_next = _ffn(h, vec[0], *w_now, FFN_RES, cast_next=(w_ffn_in, w_ffn_out, i, 1))
        if ctx_live:
            hc = _ffn(hc, vec_c[0], *w_now, FFN_RES)

        if kind == 0:
            hp = (hy_w_in[j].astype(BF16), hy_b_in[j], hy_w_sc[j], hy_b_sc[j], hy_f_w1[j], hy_f_b1[j],
                  hy_f_w2[j], hy_f_b2[j], hy_f_w3[j], hy_f_b3[j], hy_f_w4[j], hy_f_freq[j], hy_skip[j],
                  hy_w_out[j].astype(BF16), hy_b_out[j])
            h = _hyena_mixer(h, vec[1], hp, long_seq=True)
            if ctx_out:
                hc = _hyena_mixer(hc, vec_c[1], hp, long_seq=False)
        elif kind == 1:
            w_qkv = at_w_qkv[j].astype(BF16)
            qkv = _proj_in(h, vec[1], w_qkv, at_b_qkv[j], BF16, rope=rope,
                           q_blocks=d // 512, q_scale=HEAD_DIM ** -0.5)
            kvc = _proj_in(hc, vec_c[1], w_qkv[:, d:], at_b_qkv[j][d:], BF16)
            o = _attention(qkv, kvc, at_sink[j], d)
            h = _proj_out(o, at_w_o[j].astype(BF16), at_b_o[j], h, vec[1])
            assert not ctx_out, "context-query attention path is not needed for this depth"
        else:
            h = _pool_mixer(h, vec[1], pl_w[j], pl_b[j], pl_scale[j])
            assert not ctx_out, "context pooling path is not needed for this depth"

        w_now = w_next
        if i + 1 < depth:
            h, *w_next = _ffn(h, vec[2], *w_now, FFN_RES, cast_next=(w_ffn_in, w_ffn_out, i + 1, 0))
        else:
            h = _ffn(h, vec[2], *w_now, FFN_RES)
        if ctx_out:
            hc = _ffn(hc, vec_c[2], *w_now, FFN_RES)
    return h[None]
```

```python
import functools
import math

import jax
import jax.numpy as jnp
from jax import lax
from jax.experimental import pallas as pl
from jax.experimental.pallas import tpu as pltpu

F32 = jnp.float32
BF16 = jnp.bfloat16

GRID_W = 64
N_MOD = 9
NORM_EPS = 1e-6
FFN_RES = 0.5
FILTER_BANDS = 16
FILTER_EMB = 1 + 2 * FILTER_BANDS
DECAY_TARGET = 1e-2
SHORT_DECAY_PCT = 0.3
LONG_DECAY_PCT = 1.5
HEAD_DIM = 128
N_KV_HEADS = 4
WINDOW = 128
ATTN_BLOCK = 128
ROPE_THETA = 10000.0
POOL_SIZES = (2, 4, 8, 16)

VMEM_LIMIT_BYTES = 56 * 1024 * 1024
SUBLANES = 8
LANES = 128
DFT_INNER = 128
HALO = 16
FFN_CHUNK = 512
MASK_BIAS = -1e30
LOG2_E = math.log2(math.e)


def _cparams(*sem):
    return pltpu.CompilerParams(dimension_semantics=sem, vmem_limit_bytes=VMEM_LIMIT_BYTES)


def _rms(x, g):
    ms = jnp.mean(x * x, axis=-1, keepdims=True)
    return x * lax.rsqrt(ms + NORM_EPS) * g


def _modulated(h, vec_ref):
    return _rms(h, vec_ref[3:4, :]) * (1.0 + vec_ref[1:2, :]) + vec_ref[0:1, :]


def _store_modulated(h_ref, vec_ref, u_ref):
    gain = vec_ref[3:4, :] * (1.0 + vec_ref[1:2, :])
    u_ref[...] = (_rms(h_ref[...], gain) + vec_ref[0:1, :]).astype(u_ref.dtype)


def _mods_body(cb_ref, w_ref, b_ref, o_ref, s_ref):
    k_dim, bn = w_ref.shape[1], w_ref.shape[2]
    nl = bn // LANES

    @pl.when((pl.program_id(0) == 0) & (pl.program_id(1) == 0))
    def _():
        cv = cb_ref[...]
        s_ref[...] = cv * jax.nn.sigmoid(cv)

    def step(kg, acc):
        rows = pl.ds(pl.multiple_of(kg * SUBLANES, SUBLANES), SUBLANES)
        s = [s_ref[r, rows, :] for r in range(2)]
        new = list(acc)
        for j in range(nl):
            wv = w_ref[0, rows, j * LANES:(j + 1) * LANES]
            for r in range(2):
                new[r * nl + j] = acc[r * nl + j] + wv * s[r]
        return tuple(new)

    init = tuple(jnp.zeros((SUBLANES, LANES), F32) for _ in range(2 * nl))
    acc = lax.fori_loop(0, k_dim // SUBLANES, step, init, unroll=4)
    for r in range(2):
        for j in range(nl):
            cols = slice(j * LANES, (j + 1) * LANES)
            o_ref[0, r:r + 1, cols] = jnp.sum(acc[r * nl + j], axis=0, keepdims=True) + b_ref[0, :, cols]


def _adaln_mods(c, c_ctx, w_ada, b_ada):
    depth, d, n = w_ada.shape
    bn = 1024
    cb = jnp.broadcast_to(jnp.stack([c[0], c_ctx])[:, :, None], (2, d, LANES))
    return pl.pallas_call(
        _mods_body,
        grid=(depth, n // bn),
        in_specs=[pl.BlockSpec((2, d, LANES), lambda i, j: (0, 0, 0)),
                  pl.BlockSpec((1, d, bn), lambda i, j: (i, 0, j)),
                  pl.BlockSpec((1, 1, bn), lambda i, j: (i, 0, j))],
        out_specs=pl.BlockSpec((1, 2, bn), lambda i, j: (i, 0, j)),
        out_shape=jax.ShapeDtypeStruct((depth, 2, n), F32),
        scratch_shapes=[pltpu.VMEM((2, d, LANES), F32)],
        compiler_params=_cparams("arbitrary", "arbitrary"),
        name="adaln_mods",
    )(cb, w_ada, b_ada.reshape(depth, 1, n))


def _ffn_body(h_ref, vec_ref, wg_ref, wu_ref, wo_ref, *rest, res_w, cast):
    if cast:
        ci_ref, co_ref, o_ref, cib_ref, cob_ref, u_ref, acc_ref = rest
    else:
        o_ref, u_ref, acc_ref = rest
    j = pl.program_id(1)

    @pl.when((pl.program_id(0) == 0) & (j == 0))
    def _():
        acc_ref[...] = jnp.zeros_like(acc_ref)

    @pl.when(j == 0)
    def _():
        _store_modulated(h_ref, vec_ref, u_ref)

    u = u_ref[...]
    g = jnp.dot(u, wg_ref[...], preferred_element_type=F32)
    p = jnp.dot(u, wu_ref[...], preferred_element_type=F32)
    a = (g * jax.nn.sigmoid(g) * p).astype(BF16)
    acc_ref[...] += jnp.dot(a, wo_ref[...], preferred_element_type=F32)

    if cast:
        cib_ref[...] = ci_ref[...].astype(cib_ref.dtype)
        cob_ref[...] = co_ref[...].astype(cob_ref.dtype)

    @pl.when(j == pl.num_programs(1) - 1)
    def _():
        gate = (res_w * vec_ref[2:3, :]) * vec_ref[4:5, :]
        o_ref[...] = h_ref[...] + _rms(acc_ref[...], gate)
        acc_ref[...] = jnp.zeros_like(acc_ref)


def _cast_plan(n_rows, n_cols, steps):
    for n_col_blocks in range(1, steps + 1):
        if steps % n_col_blocks or n_cols % n_col_blocks or n_rows % (steps // n_col_blocks):
            continue
        br, bc = n_rows // (steps // n_col_blocks), n_cols // n_col_blocks
        if br % HALO == 0 and bc % LANES == 0:
            return br, bc, n_col_blocks
    raise ValueError("no tiling of the weight cast fits this grid")


def _ffn(h, vecs, w_in, w_out, res_w, cast_next=None):
    m, d = h.shape
    f = w_out.shape[0]
    bm = min(m, 512)
    bf = FFN_CHUNK
    nf = f // bf
    in_specs = [pl.BlockSpec((bm, d), lambda i, j: (i, 0)),
                pl.BlockSpec((SUBLANES, d), lambda i, j: (0, 0)),
                pl.BlockSpec((d, bf), lambda i, j: (0, j)),
                pl.BlockSpec((d, bf), lambda i, j: (0, j + nf)),
                pl.BlockSpec((bf, d), lambda i, j: (j, 0))]
    out_specs = [pl.BlockSpec((bm, d), lambda i, j: (i, 0))]
    out_shape = [jax.ShapeDtypeStruct((m, d), F32)]
    args = [h, vecs, w_in, w_in, w_out]
    if cast_next is not None:
        *srcs, layer, slot = cast_next
        for w in srcs:
            br, bc, ncb = _cast_plan(w.shape[2], w.shape[3], (m // bm) * nf)
            in_specs.append(pl.BlockSpec(
                (None, None, br, bc),
                functools.partial(lambda i, j, n: (layer, slot, (i * nf + j) // n, (i * nf + j) % n), n=ncb)))
            out_specs.append(pl.BlockSpec(
                (br, bc), functools.partial(lambda i, j, n: ((i * nf + j) // n, (i * nf + j) % n), n=ncb)))
            out_shape.append(jax.ShapeDtypeStruct(w.shape[2:], BF16))
            args.append(w)
    outs = pl.pallas_call(
        functools.partial(_ffn_body, res_w=res_w, cast=cast_next is not None),
        grid=(m // bm, nf),
        in_specs=in_specs,
        out_specs=out_specs,
        out_shape=out_shape,
        scratch_shapes=[pltpu.VMEM((bm, d), BF16), pltpu.VMEM((bm, d), F32)],
        compiler_params=_cparams("arbitrary", "arbitrary"),
        name="ffn",
    )(*args)
    return outs if cast_next is not None else outs[0]


def _proj_in_body(h_ref, vec_ref, w_ref, b_ref, *rest, rope_blocks, q_blocks, q_scale):
    if rope_blocks:
        cos_ref, sa_ref, sb_ref, o_ref, u_ref = rest
    else:
        o_ref, u_ref = rest
    j = pl.program_id(1)

    @pl.when(j == 0)
    def _():
        _store_modulated(h_ref, vec_ref, u_ref)

    y = jnp.dot(u_ref[...], w_ref[...], preferred_element_type=F32) + b_ref[...]
    if not rope_blocks:
        o_ref[...] = y.astype(o_ref.dtype)
        return

    rot = jnp.where(j < q_blocks, q_scale, jnp.where(j < rope_blocks, 1.0, 0.0)).astype(F32)
    keep = jnp.where(j < rope_blocks, 0.0, 1.0).astype(F32)
    cos, sa, sb = cos_ref[...] * rot + keep, sa_ref[...] * rot, sb_ref[...] * rot
    for hd in range(y.shape[1] // HEAD_DIM):
        cols = slice(hd * HEAD_DIM, (hd + 1) * HEAD_DIM)
        xh = y[:, cols]
        r = (xh * cos + pltpu.roll(xh, HEAD_DIM - HEAD_DIM // 4, 1) * sa
             + pltpu.roll(xh, HEAD_DIM // 4, 1) * sb)
        o_ref[:, cols] = r.astype(o_ref.dtype)


def _proj_in(h, vecs, w, b, out_dtype, rope=None, q_blocks=0, q_scale=1.0):
    m, d = h.shape
    n = w.shape[1]
    bm = min(m, 1024)
    bn = 512
    in_specs = [pl.BlockSpec((bm, d), lambda i, j: (i, 0)),
                pl.BlockSpec((SUBLANES, d), lambda i, j: (0, 0)),
                pl.BlockSpec((d, bn), lambda i, j: (0, j)),
                pl.BlockSpec((1, bn), lambda i, j: (0, j))]
    args = [h, vecs, w, b.reshape(1, n)]
    rope_blocks = 0
    if rope is not None:
        rope_blocks = q_blocks + 1
        in_specs += [pl.BlockSpec((bm, HEAD_DIM), lambda i, j: (i, 0))] * 3
        args += list(rope)
    return pl.pallas_call(
        functools.partial(_proj_in_body, rope_blocks=rope_blocks, q_blocks=q_blocks, q_scale=q_scale),
        grid=(m // bm, n // bn),
        in_specs=in_specs,
        out_specs=pl.BlockSpec((bm, bn), lambda i, j: (i, j)),
        out_shape=jax.ShapeDtypeStruct((m, n), out_dtype),
        scratch_shapes=[pltpu.VMEM((bm, d), BF16)],
        compiler_params=_cparams("arbitrary", "arbitrary"),
        name="proj_in",
    )(*args)


def _proj_out_body(a_ref, w_ref, b_ref, h_ref, vec_ref, o_ref):
    y = jnp.dot(a_ref[...], w_ref[...], preferred_element_type=F32) + b_ref[...]
    o_ref[...] = h_ref[...] + vec_ref[2:3, :] * _rms(y, vec_ref[4:5, :])


def _proj_out(a, w, b, h, vecs):
    m, d = h.shape
    k = a.shape[1]
    bm = min(m, 256)
    return pl.pallas_call(
        _proj_out_body,
        grid=(m // bm,),
        in_specs=[pl.BlockSpec((bm, k), lambda i: (i, 0)),
                  pl.BlockSpec((k, d), lambda i: (0, 0)),
                  pl.BlockSpec((1, d), lambda i: (0, 0)),
                  pl.BlockSpec((bm, d), lambda i: (i, 0)),
                  pl.BlockSpec((SUBLANES, d), lambda i: (0, 0))],
        out_specs=pl.BlockSpec((bm, d), lambda i: (i, 0)),
        out_shape=jax.ShapeDtypeStruct((m, d), F32),
        compiler_params=_cparams("arbitrary"),
        name="proj_out",
    )(a, w, b.reshape(1, d), h, vecs)


def _hy_in_body(h_ref, hp_ref, hn_ref, vec_ref, w0_ref, w1_ref, w2_ref, b0_ref, b1_ref, b2_ref, wsc_ref, bsc_ref,
                x0_out, xin_out, u_ref, uh_ref):
    i = pl.program_id(0)
    last = pl.num_programs(0) - 1

    @pl.when(pl.program_id(1) == 0)
    def _():
        _store_modulated(h_ref, vec_ref, u_ref)
        uh_ref[0:SUBLANES, :] = hp_ref[...]
        uh_ref[SUBLANES:, :] = hn_ref[...]
        _store_modulated(uh_ref, vec_ref, uh_ref)

    def conv(w_ref, b_ref, part):
        z = jnp.dot(u_ref[...], w_ref[...], preferred_element_type=F32) + b_ref[...]
        zh = jnp.dot(uh_ref[...].astype(BF16), w_ref[...], preferred_element_type=F32) + b_ref[...]
        bm = z.shape[0]
        prev_row = jnp.where(i > 0, zh[SUBLANES - 1:SUBLANES, :], 0.0)
        next_row = jnp.where(i < last, zh[SUBLANES:SUBLANES + 1, :], 0.0)
        rid = lax.broadcasted_iota(jnp.int32, z.shape, 0)
        up = jnp.where(rid == 0, prev_row, pltpu.roll(z, 1, 0))
        dn = jnp.where(rid == bm - 1, next_row, pltpu.roll(z, bm - 1, 0))
        return (up * wsc_ref[0, part:part + 1, :] + z * wsc_ref[1, part:part + 1, :]
                + dn * wsc_ref[2, part:part + 1, :] + bsc_ref[part:part + 1, :])

    x0_out[...] = conv(w0_ref, b0_ref, 0).astype(x0_out.dtype)
    xin_out[...] = (conv(w2_ref, b2_ref, 2) * conv(w1_ref, b1_ref, 1)).astype(xin_out.dtype)


def _hy_in_gate(h, vecs, w_in, b_in, w_sc, b_sc):
    l, d = h.shape
    bm = min(l, 1024)
    bc = min(d, 512)
    nc = d // bc
    rb = bm // SUBLANES
    nrow8 = l // SUBLANES
    wspecs = [pl.BlockSpec((d, bc), functools.partial(lambda i, j, p: (0, p * nc + j), p=part)) for part in range(3)]
    bspecs = [pl.BlockSpec((1, bc), functools.partial(lambda i, j, p: (0, p * nc + j), p=part)) for part in range(3)]
    out_spec = pl.BlockSpec((bm, bc), lambda i, j: (i, j))
    return pl.pallas_call(
        _hy_in_body,
        grid=(l // bm, nc),
        in_specs=[pl.BlockSpec((bm, d), lambda i, j: (i, 0)),
                  pl.BlockSpec((SUBLANES, d), lambda i, j: (jnp.maximum(i * rb - 1, 0), 0)),
                  pl.BlockSpec((SUBLANES, d), lambda i, j: (jnp.minimum((i + 1) * rb, nrow8 - 1), 0)),
                  pl.BlockSpec((SUBLANES, d), lambda i, j: (0, 0))] + wspecs + bspecs
                 + [pl.BlockSpec((3, 3, bc), lambda i, j: (0, 0, j)),
                    pl.BlockSpec((3, bc), lambda i, j: (0, j))],
        out_specs=[out_spec, out_spec],
        out_shape=[jax.ShapeDtypeStruct((l, d), BF16)] * 2,
        scratch_shapes=[pltpu.VMEM((bm, d), BF16), pltpu.VMEM((2 * SUBLANES, d), F32)],
        compiler_params=_cparams("arbitrary", "arbitrary"),
        name="hyena_in_gate",
    )(h, h, h, vecs, w_in, w_in, w_in, *([b_in.reshape(1, 3 * d)] * 3), w_sc.reshape(3, 3, d), b_sc.reshape(3, d))


def _filter_mlp_body(z_ref, w1, b1, w2, b2, w3, b3, fr, o_ref):
    hp = lax.Precision.HIGHEST
    f = jnp.sin(fr[0:1, :] * (jnp.dot(z_ref[...], w1[...], precision=hp, preferred_element_type=F32) + b1[...]))
    f = jnp.sin(fr[1:2, :] * (jnp.dot(f, w2[...], precision=hp, preferred_element_type=F32) + b2[...]))
    o_ref[...] = jnp.sin(fr[2:3, :] * (jnp.dot(f, w3[...], precision=hp, preferred_element_type=F32) + b3[...]))


def _filter_table(l, w1, b1, w2, b2, w3, b3, freq):
    hid = w1.shape[1]
    t = jnp.linspace(0.0, 1.0, l, dtype=F32)[:, None]
    omega = 2.0 * math.pi * jnp.arange(l, dtype=F32)[:, None] / l
    bands = jnp.linspace(1e-4, FILTER_BANDS - 1, FILTER_BANDS, dtype=F32)[None, :]
    emb = jnp.concatenate([t, jnp.cos(bands * omega), -jnp.sin(bands * omega)], axis=-1)
    emb = jnp.pad(emb, ((0, 0), (0, LANES - emb.shape[1])))
    w1p = jnp.pad(w1, ((0, LANES - w1.shape[0]), (0, 0)))
    pack = LANES // hid
    rows = l // pack
    eye = jnp.eye(pack, dtype=F32)
    emb_p = jnp.concatenate([emb[p * rows:(p + 1) * rows] for p in range(pack)], axis=1)
    wide = pack * hid
    bm = min(rows, 512)

    def full(shape):
        return pl.BlockSpec(shape, lambda i: (0,) * len(shape))

    f_p = pl.pallas_call(
        _filter_mlp_body,
        grid=(rows // bm,),
        in_specs=[pl.BlockSpec((bm, pack * LANES), lambda i: (i, 0)),
                  full((pack * LANES, wide)), full((1, wide)), full((wide, wide)), full((1, wide)),
                  full((wide, wide)), full((1, wide)), full((3, wide))],
        out_specs=pl.BlockSpec((bm, wide), lambda i: (i, 0)),
        out_shape=jax.ShapeDtypeStruct((rows, wide), F32),
        compiler_params=_cparams("arbitrary"),
        name="hyena_filter_mlp",
    )(emb_p, jnp.kron(eye, w1p), jnp.tile(b1, pack).reshape(1, wide), jnp.kron(eye, w2),
      jnp.tile(b2, pack).reshape(1, wide), jnp.kron(eye, w3), jnp.tile(b3, pack).reshape(1, wide),
      jnp.tile(freq, (1, pack)))
    f = jnp.concatenate([f_p[:, p * hid:(p + 1) * hid] for p in range(pack)], axis=0)
    tab = jnp.concatenate([f, t, jnp.ones((l, 1), F32)], axis=-1)
    tab = jnp.pad(tab, ((0, 0), (0, LANES - tab.shape[1])))
    return jnp.concatenate([tab, jnp.zeros((1, LANES), F32), tab[1:][::-1]], axis=0)


def _decay_rates(d):
    return jnp.abs(jnp.linspace(math.log(DECAY_TARGET) / LONG_DECAY_PCT,
                                math.log(DECAY_TARGET) / SHORT_DECAY_PCT, d, dtype=F32))[None, :]


def _split_bf16(x):
    hi = x.astype(BF16)
    return hi, (x - hi.astype(F32)).astype(BF16)


def _stack_3pass(w):
    w_hi, w_lo = _split_bf16(w)
    return jnp.concatenate([w_hi, w_hi, w_lo], axis=0)


def _dot_3pass(f, w3):
    f_hi, f_lo = _split_bf16(f)
    return jnp.dot(jnp.concatenate([f_hi, f_lo, f_hi], axis=1), w3, preferred_element_type=F32)


def _filter_rows(ft, wa3, wb3, dl_ref):
    hid = wa3.shape[0] // 3
    half = ft.shape[0] // 2
    f, t, keep = ft[:, :hid], ft[:, hid:hid + 1], ft[:, hid + 1:hid + 2]
    y = jnp.concatenate([_dot_3pass(f[:half], wa3), _dot_3pass(f[half:], wb3)], axis=0)
    return y * (jnp.exp(-t * dl_ref[...]) * keep)


def _outer_stage(fa_ref, cols, o_ref):
    ys = [jnp.dot(fa_ref[...], xj, preferred_element_type=F32).astype(BF16) for xj in cols]
    o_ref[...] = pltpu.einshape("brd->rbd", jnp.stack(ys))


def _dft_a_body(x_ref, fa_ref, o_ref):
    xb = pltpu.einshape("abd->bad", x_ref[...])
    _outer_stage(fa_ref, [xb[j] for j in range(xb.shape[0])], o_ref)


def _dft_a_filter_body(ft_ref, wa_ref, wb_ref, dl_ref, fa_ref, o_ref):
    wa3, wb3 = _stack_3pass(wa_ref[...]), _stack_3pass(wb_ref[...])
    cols = [_filter_rows(ft_ref[j], wa3, wb3, dl_ref).astype(BF16) for j in range(ft_ref.shape[0])]
    _outer_stage(fa_ref, cols, o_ref)


def _dft_a(x3, fa, d_chunk):
    a, b, d = x3.shape
    rows = fa.shape[0]
    return pl.pallas_call(
        _dft_a_body,
        grid=(b // HALO, d // d_chunk),
        in_specs=[pl.BlockSpec((a, HALO, d_chunk), lambda i, j: (0, i, j)),
                  pl.BlockSpec((rows, a), lambda i, j: (0, 0))],
        out_specs=pl.BlockSpec((rows, HALO, d_chunk), lambda i, j: (0, i, j)),
        out_shape=jax.ShapeDtypeStruct((rows, b, d), BF16),
        compiler_params=_cparams("arbitrary", "arbitrary"),
        name="dft_outer",
    )(x3, fa)


def _dft_a_filter(ft3, w4, deltas, fa, d_chunk):
    b, n1, _ = ft3.shape
    hid, d2 = w4.shape
    d = d2 // 2
    rows = fa.shape[0]
    nd = d // d_chunk
    return pl.pallas_call(
        _dft_a_filter_body,
        grid=(b // HALO, nd),
        in_specs=[pl.BlockSpec((HALO, n1, LANES), lambda i, j: (i, 0, 0)),
                  pl.BlockSpec((hid, d_chunk), lambda i, j: (0, j)),
                  pl.BlockSpec((hid, d_chunk), lambda i, j: (0, j + nd)),
                  pl.BlockSpec((1, d_chunk), lambda i, j: (0, j)),
                  pl.BlockSpec((rows, n1), lambda i, j: (0, 0))],
        out_specs=pl.BlockSpec((rows, HALO, d_chunk), lambda i, j: (0, i, j)),
        out_shape=jax.ShapeDtypeStruct((rows, b, d), BF16),
        compiler_params=_cparams("arbitrary", "arbitrary"),
        name="dft_outer_filter",
    )(ft3, w4, w4, deltas, fa)


def _dft_inner_body(y_ref, yf_ref, gf_ref, gi_ref, o_ref):
    b, dc = y_ref.shape[2], y_ref.shape[3]
    for ci in range(y_ref.shape[1]):
        z = jnp.dot(gf_ref[ci], y_ref[:, ci].reshape(2 * b, dc), preferred_element_type=F32)
        hh = jnp.dot(gf_ref[ci], yf_ref[:, ci].reshape(2 * b, dc), preferred_element_type=F32)
        zr, zi, hr, hi = z[:b], z[b:], hh[:b], hh[b:]
        p = jnp.concatenate([zr * hr - zi * hi, zr * hi + zi * hr], axis=0).astype(BF16)
        cc = jnp.dot(gi_ref[ci], p, preferred_element_type=F32)
        o_ref[:, ci] = cc.reshape(2, b, dc).astype(o_ref.dtype)


def _dft_inner(y4, yf4, gf, gi):
    _, nc, b, d = y4.shape
    cblk = SUBLANES
    dc = min(d, 1024)
    yspec = pl.BlockSpec((2, cblk, b, dc), lambda c, j: (0, c, 0, j))
    gspec = pl.BlockSpec((cblk, 2 * b, 2 * b), lambda c, j: (c, 0, 0))
    return pl.pallas_call(
        _dft_inner_body,
        grid=(nc // cblk, d // dc),
        in_specs=[yspec, yspec, gspec, gspec],
        out_specs=yspec,
        out_shape=jax.ShapeDtypeStruct(y4.shape, BF16),
        compiler_params=_cparams("arbitrary", "arbitrary"),
        name="dft_inner",
    )(y4, yf4, gf, gi)


def _dft_c_body(c_ref, m_ref, xin_ref, x0_ref, skip_ref, o_ref):
    cb = pltpu.einshape("rbd->brd", c_ref[...])
    outs = [jnp.dot(m_ref[...], cb[j], preferred_element_type=F32).astype(BF16) for j in range(cb.shape[0])]
    conv = pltpu.einshape("bad->abd", jnp.stack(outs)).astype(F32)
    gated = x0_ref[...].astype(F32) * (conv + xin_ref[...].astype(F32) * skip_ref[...][None])
    o_ref[...] = gated.astype(o_ref.dtype)


def _dft_c(c3, mc, xin3, x03, skip, d_chunk):
    rows, b, d = c3.shape
    a = mc.shape[0]
    xspec = pl.BlockSpec((a, HALO, d_chunk), lambda i, j: (0, i, j))
    return pl.pallas_call(
        _dft_c_body,
        grid=(b // HALO, d // d_chunk),
        in_specs=[pl.BlockSpec((rows, HALO, d_chunk), lambda i, j: (0, i, j)),
                  pl.BlockSpec((a, rows), lambda i, j: (0, 0)),
                  xspec, xspec,
                  pl.BlockSpec((1, d_chunk), lambda i, j: (0, j))],
        out_specs=xspec,
        out_shape=jax.ShapeDtypeStruct((a, b, d), BF16),
        compiler_params=_cparams("arbitrary", "arbitrary"),
        name="dft_outer_inv",
    )(c3, mc, xin3, x03, skip)


def _cis(num, period):
    ang = (2.0 * math.pi / period) * num.astype(F32)
    return jnp.cos(ang), jnp.sin(ang)


def _long_conv_gated(xin, x0, ftab, w4, skip):
    l, d = xin.shape
    b = DFT_INNER
    a = l // b
    n1, n = 2 * a, 2 * l
    nc = n1 // 2 + SUBLANES
    dc = min(d, 512)
    ci = jnp.arange(nc, dtype=jnp.int32)
    ai = jnp.arange(n1, dtype=jnp.int32)
    live = (ci <= n1 // 2).astype(F32)[:, None]
    cr, sr = _cis((ci[:, None] * ai[None, :]) % n1, n1)
    cr, sr = cr * live, sr * live
    fa_full = jnp.concatenate([cr, -sr], axis=0).astype(BF16)
    fa = fa_full[:, :a]
    fold = jnp.where((ci == 0) | (ci == n1 // 2), 1.0, 2.0)[None, :] / n
    mc = jnp.concatenate([cr[:, :a].T * fold, -sr[:, :a].T * fold], axis=1).astype(BF16)
    ei = jnp.arange(b, dtype=jnp.int32)
    kk = ci[:, None, None] + n1 * ei[None, :, None]
    tr, ts = _cis((kk * ei[None, None, :]) % n, n)
    ti = -ts
    gf = jnp.concatenate([jnp.concatenate([tr, -ti], axis=2),
                          jnp.concatenate([ti, tr], axis=2)], axis=1).astype(BF16)
    trt, tit = jnp.swapaxes(tr, 1, 2), jnp.swapaxes(ti, 1, 2)
    gi = jnp.concatenate([jnp.concatenate([trt, tit], axis=2),
                          jnp.concatenate([-tit, trt], axis=2)], axis=1).astype(BF16)

    ft3 = jnp.swapaxes(ftab.reshape(n1, b, LANES), 0, 1)
    hf = _dft_a_filter(ft3, w4, _decay_rates(d), fa_full, dc)
    yx = _dft_a(xin.reshape(a, b, d), fa, dc)
    c4 = _dft_inner(yx.reshape(2, nc, b, d), hf.reshape(2, nc, b, d), gf, gi)
    y3 = _dft_c(c4.reshape(2 * nc, b, d), mc, xin.reshape(a, b, d), x0.reshape(a, b, d), skip.reshape(1, d), dc)
    return y3.reshape(l, d)


def _ctx_conv_body(x_ref, x0_ref, ft_ref, wa_ref, wb_ref, dl_ref, skip_ref, ff_ref, fh_ref, mi_ref, o_ref):
    x = x_ref[...].astype(F32)
    n = ft_ref.shape[0]
    filt = _filter_rows(ft_ref[...], _stack_3pass(wa_ref[...]), _stack_3pass(wb_ref[...]), dl_ref)
    xs = jnp.dot(ff_ref[...], x.astype(BF16), preferred_element_type=F32)
    hs = jnp.dot(fh_ref[...], filt.astype(BF16), preferred_element_type=F32)
    xr, xi, hr, hi = xs[:n], xs[n:], hs[:n], hs[n:]
    p = jnp.concatenate([xr * hr - xi * hi, xr * hi + xi * hr], axis=0).astype(BF16)
    conv = jnp.dot(mi_ref[...], p, preferred_element_type=F32)
    o_ref[...] = (x0_ref[...].astype(F32) * (conv + x * skip_ref[...])).astype(o_ref.dtype)


def _short_seq_conv_gated(xin, x0, ftab, w4, skip):
    l, d = xin.shape
    n = 2 * l
    hid = w4.shape[0]
    ni = jnp.arange(n, dtype=jnp.int32)
    cr, sr = _cis((ni[:, None] * ni[None, :]) % n, n)
    fh = jnp.concatenate([cr, -sr], axis=0).astype(BF16)
    ff = fh[:, :l]
    mi = (jnp.concatenate([cr[:l], -sr[:l]], axis=1) * (1.0 / n)).astype(BF16)
    dc = min(d, 512)
    nd = d // dc
    return pl.pallas_call(
        _ctx_conv_body,
        grid=(nd,),
        in_specs=[pl.BlockSpec((l, dc), lambda j: (0, j)),
                  pl.BlockSpec((l, dc), lambda j: (0, j)),
                  pl.BlockSpec((n, LANES), lambda j: (0, 0)),
                  pl.BlockSpec((hid, dc), lambda j: (0, j)),
                  pl.BlockSpec((hid, dc), lambda j: (0, j + nd)),
                  pl.BlockSpec((1, dc), lambda j: (0, j)),
                  pl.BlockSpec((1, dc), lambda j: (0, j)),
                  pl.BlockSpec((2 * n, l), lambda j: (0, 0)),
                  pl.BlockSpec((2 * n, n), lambda j: (0, 0)),
                  pl.BlockSpec((l, 2 * n), lambda j: (0, 0))],
        out_specs=pl.BlockSpec((l, dc), lambda j: (0, j)),
        out_shape=jax.ShapeDtypeStruct((l, d), BF16),
        compiler_params=_cparams("arbitrary"),
        name="ctx_conv",
    )(xin, x0, ftab, w4, w4, _decay_rates(d), skip.reshape(1, d), ff, fh, mi)


def _hyena_mixer(h, vecs, p, long_seq):
    w_in, b_in, w_sc, b_sc, f_w1, f_b1, f_w2, f_b2, f_w3, f_b3, f_w4, f_freq, skip, w_out, b_out = p
    l = h.shape[0]
    x0, xin = _hy_in_gate(h, vecs, w_in, b_in, w_sc, b_sc)
    ftab = _filter_table(l, f_w1, f_b1, f_w2, f_b2, f_w3, f_b3, f_freq)
    conv = _long_conv_gated if long_seq else _short_seq_conv_gated
    y = conv(xin, x0, ftab, f_w4, skip)
    return _proj_out(y, w_out, b_out, h, vecs)


def _attn_body(sink_ref, q_ref, kp_ref, kc_ref, kn_ref, vp_ref, vc_ref, vn_ref, kx_ref, vx_ref, bias_ref, o_ref,
               *, group):
    blk = q_ref.shape[0]
    rid = lax.broadcasted_iota(jnp.int32, (group * blk, 1), 0)
    for kh in range(N_KV_HEADS):
        hs = slice(kh * HEAD_DIM, (kh + 1) * HEAD_DIM)
        heads = [kh * group + g for g in range(group)]
        q = jnp.concatenate([q_ref[:, hd * HEAD_DIM:(hd + 1) * HEAD_DIM] for hd in heads], axis=0)
        keys = jnp.concatenate([kp_ref[:, hs], kc_ref[:, hs], kn_ref[:, hs], kx_ref[:, hs]], axis=0)
        vals = jnp.concatenate([vp_ref[:, hs], vc_ref[:, hs], vn_ref[:, hs], vx_ref[:, hs]], axis=0)
        s = lax.dot_general(q, keys, (((1,), (1,)), ((), ())), preferred_element_type=F32)
        pieces = [s[:, :blk] + bias_ref[0, :, :blk], s[:, blk:2 * blk], s[:, 2 * blk:3 * blk] + bias_ref[0, :, blk:]]
        pieces += [s[:, c0:c0 + blk] for c0 in range(3 * blk, s.shape[1], blk)]
        sink = jnp.zeros((group * blk, 1), F32)
        for g, hd in enumerate(heads):
            sink = jnp.where((rid >= g * blk) & (rid < (g + 1) * blk), sink_ref[hd] * LOG2_E, sink)
        top = functools.reduce(jnp.maximum, pieces)
        mx = jnp.maximum(jnp.max(top, axis=-1, keepdims=True), sink)
        probs = [jnp.exp2(pc - mx) for pc in pieces]
        denom = jnp.sum(functools.reduce(jnp.add, probs), axis=-1, keepdims=True) + jnp.exp2(sink - mx)
        pr = jnp.concatenate([pp.astype(BF16) for pp in probs], axis=1)
        o = jnp.dot(pr, vals, preferred_element_type=F32) / denom
        for g, hd in enumerate(heads):
            o_ref[:, hd * HEAD_DIM:(hd + 1) * HEAD_DIM] = o[g * blk:(g + 1) * blk].astype(o_ref.dtype)


def _attention(qkv, kvc, sink, d):
    l = qkv.shape[0]
    c = kvc.shape[0]
    blk = ATTN_BLOCK
    nb = l // blk
    group = d // HEAD_DIM // N_KV_HEADS
    kvw = N_KV_HEADS * HEAD_DIM
    kcol = d // kvw
    qi = jnp.arange(group * blk, dtype=jnp.int32)[:, None] % blk
    ki = jnp.arange(blk, dtype=jnp.int32)[None, :]
    prev_ok, next_ok, never = ki >= qi, ki <= qi, jnp.zeros((group * blk, blk), bool)
    variants = [(never, next_ok), (prev_ok, next_ok), (prev_ok, never)]
    bias = jnp.stack([jnp.where(jnp.concatenate(v, axis=1), 0.0, MASK_BIAS).astype(F32) for v in variants])

    def kv_spec(col, shift):
        return pl.BlockSpec((blk, kvw), lambda n: (jnp.clip(n + shift, 0, nb - 1), col))

    return pl.pallas_call(
        functools.partial(_attn_body, group=group),
        grid=(nb,),
        in_specs=[pl.BlockSpec(memory_space=pltpu.SMEM),
                  pl.BlockSpec((blk, d), lambda n: (n, 0)),
                  kv_spec(kcol, -1), kv_spec(kcol, 0), kv_spec(kcol, 1),
                  kv_spec(kcol + 1, -1), kv_spec(kcol + 1, 0), kv_spec(kcol + 1, 1),
                  pl.BlockSpec((c, kvw), lambda n: (0, 0)),
                  pl.BlockSpec((c, kvw), lambda n: (0, 1)),
                  pl.BlockSpec((1, group * blk, 2 * blk),
                               lambda n: (jnp.where(n == 0, 0, jnp.where(n == nb - 1, 2, 1)), 0, 0))],
        out_specs=pl.BlockSpec((blk, d), lambda n: (n, 0)),
        out_shape=jax.ShapeDtypeStruct((l, d), BF16),
        compiler_params=_cparams("arbitrary"),
        name="window_attn",
    )(sink, qkv, qkv, qkv, qkv, qkv, qkv, qkv, kvc, kvc, bias)


def _rope_tables(l):
    rows = l // GRID_W
    pos_row = jnp.broadcast_to(jnp.arange(rows)[:, None], (rows, GRID_W)).reshape(-1).astype(F32)
    pos_col = jnp.broadcast_to(jnp.arange(GRID_W)[None, :], (rows, GRID_W)).reshape(-1).astype(F32)
    pairs = HEAD_DIM // 4
    inv_freq = ROPE_THETA ** (-jnp.arange(pairs, dtype=F32) / pairs)
    ang_row = pos_row[:, None] * inv_freq[None, :]
    ang_col = pos_col[:, None] * inv_freq[None, :]
    zeros = jnp.zeros_like(ang_row)
    cos = jnp.concatenate([jnp.cos(ang_row)] * 2 + [jnp.cos(ang_col)] * 2, axis=-1)
    sin_a = jnp.concatenate([-jnp.sin(ang_row), zeros, -jnp.sin(ang_col), zeros], axis=-1)
    sin_b = jnp.concatenate([zeros, jnp.sin(ang_row), zeros, jnp.sin(ang_col)], axis=-1)
    return cos, sin_a, sin_b


def _pool_body(hc_ref, hp_ref, hn_ref, vec_ref, w_ref, b_ref, sc_ref, o_ref, y_ref, *, seq_len):
    i = pl.program_id(0)
    last = pl.num_programs(0) - 1
    bm, d = hc_ref.shape
    gw = d // len(POOL_SIZES)
    h = hc_ref[...]
    u = _modulated(h, vec_ref)
    up = jnp.where(i > 0, _modulated(hp_ref[...], vec_ref), 0.0)
    un = jnp.where(i < last, _modulated(hn_ref[...], vec_ref), 0.0)
    ext_rows = bm + 2 * SUBLANES
    t = i * bm + lax.broadcasted_iota(jnp.int32, (bm, 1), 0)
    for g, size in enumerate(POOL_SIZES):
        cols = slice(g * gw, (g + 1) * gw)
        ext = jnp.concatenate([up[:, cols], u[:, cols], un[:, cols]], axis=0)
        acc, span = ext, 1
        while span < size:
            acc = acc + pltpu.roll(acc, ext_rows - span, 0)
            span *= 2
        start = SUBLANES - size // 2
        win = pltpu.roll(acc, ext_rows - start, 0)[:bm] if start else acc[:bm]
        lo = jnp.clip(t - size // 2, 0, seq_len)
        hi = jnp.clip(t - size // 2 + size, 0, seq_len)
        part = win / (hi - lo).astype(F32) - u[:, cols]
        yg = jnp.dot(part.astype(BF16), w_ref[g], preferred_element_type=F32)
        y_ref[:, cols] = (yg + b_ref[:, cols]) * sc_ref[:, cols]
    o_ref[...] = h + vec_ref[2:3, :] * _rms(y_ref[...], vec_ref[4:5, :])


def _pool_mixer(h, vecs, w, b, scale):
    l, d = h.shape
    bm = min(l, 256)
    rb = bm // SUBLANES
    nrow8 = l // SUBLANES
    ng, gw = w.shape[0], w.shape[1]
    return pl.pallas_call(
        functools.partial(_pool_body, seq_len=l),
        grid=(l // bm,),
        in_specs=[pl.BlockSpec((bm, d), lambda i: (i, 0)),
                  pl.BlockSpec((SUBLANES, d), lambda i: (jnp.maximum(i * rb - 1, 0), 0)),
                  pl.BlockSpec((SUBLANES, d), lambda i: (jnp.minimum((i + 1) * rb, nrow8 - 1), 0)),
                  pl.BlockSpec((SUBLANES, d), lambda i: (0, 0)),
                  pl.BlockSpec((ng, gw, gw), lambda i: (0, 0, 0)),
                  pl.BlockSpec((1, d), lambda i: (0, 0)),
                  pl.BlockSpec((1, d), lambda i: (0, 0))],
        out_specs=pl.BlockSpec((bm, d), lambda i: (i, 0)),
        out_shape=jax.ShapeDtypeStruct((l, d), F32),
        scratch_shapes=[pltpu.VMEM((bm, d), F32)],
        compiler_params=_cparams("arbitrary"),
        name="pool_mixer",
    )(h, h, h, vecs, w.astype(BF16), b.reshape(1, d), scale.reshape(1, d))


def _sub_vecs(mod, k, g_pre, g_post):
    rows = [mod[3 * k], mod[3 * k + 1], mod[3 * k + 2], g_pre, g_post]
    return jnp.stack(rows + [jnp.zeros_like(g_pre)] * (SUBLANES - len(rows)))


def kernel(x, c, ctx, c_ctx, w_ada, b_ada, norm_pre, norm_post, w_ffn_in, w_ffn_out, hy_w_in, hy_b_in, hy_w_sc, hy_b_sc, hy_f_w1, hy_f_b1, hy_f_w2, hy_f_b2, hy_f_w3, hy_f_b3, hy_f_w4, hy_f_freq, hy_skip, hy_w_out, hy_b_out, at_w_qkv, at_b_qkv, at_sink, at_w_o, at_b_o, pl_w, pl_b, pl_scale):
    bsz, l, d = x.shape
    assert bsz == 1, "kernel handles a single batch element"
    assert d % 1024 == 0 and l % 1024 == 0 and ctx.shape[1] % 256 == 0, "block sizes assume these multiples"
    assert w_ffn_out.shape[2] % 512 == 0, "FFN width must be a multiple of the 512-column weight blocks"
    depth = w_ada.shape[0]
    n_mixers = 3
    attn_layers = [i for i in range(depth) if i % n_mixers == 1]
    last_ctx_layer = attn_layers[-1] if attn_layers else -1

    mods = _adaln_mods(c, c_ctx, w_ada, b_ada).reshape(depth, 2, N_MOD, d)
    rope = _rope_tables(l)
    w_next = [w_ffn_in[0, 0].astype(BF16), w_ffn_out[0, 0].astype(BF16)]
    h, hc = x[0], ctx[0]
    for i in range(depth):
        kind, j = i % n_mixers, i // n_mixers
        ctx_live = i <= last_ctx_layer
        ctx_out = i < last_ctx_layer
        vec = [_sub_vecs(mods[i, 0], k, norm_pre[i, k], norm_post[i, k]) for k in range(3)]
        vec_c = [_sub_vecs(mods[i, 1], k, norm_pre[i, k], norm_post[i, k]) for k in range(3)]

        w_now = w_next
        h, *w_next = _ffn(h, vec[0], *w_now, FFN_RES, cast_next=(w_ffn_in, w_ffn_out, i, 1))
        if ctx_live:
            hc = _ffn(hc, vec_c[0], *w_now, FFN_RES)

        if kind == 0:
            hp = (hy_w_in[j].astype(BF16), hy_b_in[j], hy_w_sc[j], hy_b_sc[j], hy_f_w1[j], hy_f_b1[j],
                  hy_f_w2[j], hy_f_b2[j], hy_f_w3[j], hy_f_b3[j], hy_f_w4[j], hy_f_freq[j], hy_skip[j],
                  hy_w_out[j].astype(BF16), hy_b_out[j])
            h = _hyena_mixer(h, vec[1], hp, long_seq=True)
            if ctx_out:
                hc = _hyena_mixer(hc, vec_c[1], hp, long_seq=False)
        elif kind == 1:
            w_qkv = at_w_qkv[j].astype(BF16)
            qkv = _proj_in(h, vec[1], w_qkv, at_b_qkv[j], BF16, rope=rope,
                           q_blocks=d // 512, q_scale=LOG2_E * HEAD_DIM ** -0.5)
            kvc = _proj_in(hc, vec_c[1], w_qkv[:, d:], at_b_qkv[j][d:], BF16)
            o = _attention(qkv, kvc, at_sink[j], d)
            h = _proj_out(o, at_w_o[j].astype(BF16), at_b_o[j], h, vec[1])
            assert not ctx_out, "context-query attention path is not needed for this depth"
        else:
            h = _pool_mixer(h, vec[1], pl_w[j], pl_b[j], pl_scale[j])
            assert not ctx_out, "context pooling path is not needed for this depth"

        w_now = w_next
        if i + 1 < depth:
            h, *w_next = _ffn(h, vec[2], *w_now, FFN_RES, cast_next=(w_ffn_in, w_ffn_out, i + 1, 0))
        else:
            h = _ffn(h, vec[2], *w_now, FFN_RES)
        if ctx_out:
            hc = _ffn(hc, vec_c[2], *w_now, FFN_RES)
    return h[None]
```

```python
import functools
import math

import jax
import jax.numpy as jnp
from jax import lax
from jax.experimental import pallas as pl
from jax.experimental.pallas import tpu as pltpu

F32 = jnp.float32
BF16 = jnp.bfloat16

GRID_W = 64
N_MOD = 9
NORM_EPS = 1e-6
FFN_RES = 0.5
FILTER_BANDS = 16
FILTER_EMB = 1 + 2 * FILTER_BANDS
DECAY_TARGET = 1e-2
SHORT_DECAY_PCT = 0.3
LONG_DECAY_PCT = 1.5
HEAD_DIM = 128
N_KV_HEADS = 4
WINDOW = 128
ATTN_BLOCK = 128
ROPE_THETA = 10000.0
POOL_SIZES = (2, 4, 8, 16)

VMEM_LIMIT_BYTES = 56 * 1024 * 1024
SUBLANES = 8
LANES = 128
DFT_INNER = 128
HALO = 16
FFN_CHUNK = 512
MASK_BIAS = -1e30
LOG2_E = math.log2(math.e)


def _cparams(*sem):
    return pltpu.CompilerParams(dimension_semantics=sem, vmem_limit_bytes=VMEM_LIMIT_BYTES)


def _rms(x, g):
    ms = jnp.mean(x * x, axis=-1, keepdims=True)
    return x * lax.rsqrt(ms + NORM_EPS) * g


def _modulated(h, vec_ref):
    return _rms(h, vec_ref[3:4, :]) * (1.0 + vec_ref[1:2, :]) + vec_ref[0:1, :]


def _store_modulated(h_ref, vec_ref, u_ref):
    gain = vec_ref[3:4, :] * (1.0 + vec_ref[1:2, :])
    u_ref[...] = (_rms(h_ref[...], gain) + vec_ref[0:1, :]).astype(u_ref.dtype)


def _mods_body(cb_ref, w_ref, b_ref, o_ref, s_ref):
    k_dim, bn = w_ref.shape[1], w_ref.shape[2]
    nl = bn // LANES

    @pl.when((pl.program_id(0) == 0) & (pl.program_id(1) == 0))
    def _():
        cv = cb_ref[...]
        s_ref[...] = cv * jax.nn.sigmoid(cv)

    def step(kg, acc):
        rows = pl.ds(pl.multiple_of(kg * SUBLANES, SUBLANES), SUBLANES)
        s = [s_ref[r, rows, :] for r in range(2)]
        new = list(acc)
        for j in range(nl):
            wv = w_ref[0, rows, j * LANES:(j + 1) * LANES]
            for r in range(2):
                new[r * nl + j] = acc[r * nl + j] + wv * s[r]
        return tuple(new)

    init = tuple(jnp.zeros((SUBLANES, LANES), F32) for _ in range(2 * nl))
    acc = lax.fori_loop(0, k_dim // SUBLANES, step, init, unroll=4)
    for r in range(2):
        for j in range(nl):
            cols = slice(j * LANES, (j + 1) * LANES)
            o_ref[0, r:r + 1, cols] = jnp.sum(acc[r * nl + j], axis=0, keepdims=True) + b_ref[0, :, cols]


def _adaln_mods(c, c_ctx, w_ada, b_ada):
    depth, d, n = w_ada.shape
    bn = 1024
    cb = jnp.broadcast_to(jnp.stack([c[0], c_ctx])[:, :, None], (2, d, LANES))
    return pl.pallas_call(
        _mods_body,
        grid=(depth, n // bn),
        in_specs=[pl.BlockSpec((2, d, LANES), lambda i, j: (0, 0, 0)),
                  pl.BlockSpec((1, d, bn), lambda i, j: (i, 0, j)),
                  pl.BlockSpec((1, 1, bn), lambda i, j: (i, 0, j))],
        out_specs=pl.BlockSpec((1, 2, bn), lambda i, j: (i, 0, j)),
        out_shape=jax.ShapeDtypeStruct((depth, 2, n), F32),
        scratch_shapes=[pltpu.VMEM((2, d, LANES), F32)],
        compiler_params=_cparams("arbitrary", "arbitrary"),
        name="adaln_mods",
    )(cb, w_ada, b_ada.reshape(depth, 1, n))


def _ffn_body(h_ref, vec_ref, wg_ref, wu_ref, wo_ref, *rest, res_w, cast):
    if cast:
        ci_ref, co_ref, o_ref, cib_ref, cob_ref, u_ref, acc_ref = rest
    else:
        o_ref, u_ref, acc_ref = rest
    j = pl.program_id(1)

    @pl.when((pl.program_id(0) == 0) & (j == 0))
    def _():
        acc_ref[...] = jnp.zeros_like(acc_ref)

    @pl.when(j == 0)
    def _():
        _store_modulated(h_ref, vec_ref, u_ref)

    u = u_ref[...]
    g = jnp.dot(u, wg_ref[...], preferred_element_type=F32)
    p = jnp.dot(u, wu_ref[...], preferred_element_type=F32)
    a = (g * jax.nn.sigmoid(g) * p).astype(BF16)
    acc_ref[...] += jnp.dot(a, wo_ref[...], preferred_element_type=F32)

    if cast:
        cib_ref[...] = ci_ref[...].astype(cib_ref.dtype)
        cob_ref[...] = co_ref[...].astype(cob_ref.dtype)

    @pl.when(j == pl.num_programs(1) - 1)
    def _():
        gate = (res_w * vec_ref[2:3, :]) * vec_ref[4:5, :]
        o_ref[...] = h_ref[...] + _rms(acc_ref[...], gate)
        acc_ref[...] = jnp.zeros_like(acc_ref)


def _cast_plan(n_rows, n_cols, steps):
    for n_col_blocks in range(1, steps + 1):
        if steps % n_col_blocks or n_cols % n_col_blocks or n_rows % (steps // n_col_blocks):
            continue
        br, bc = n_rows // (steps // n_col_blocks), n_cols // n_col_blocks
        if br % HALO == 0 and bc % LANES == 0:
            return br, bc, n_col_blocks
    raise ValueError("no tiling of the weight cast fits this grid")


def _ffn(h, vecs, w_in, w_out, res_w, cast_next=None):
    m, d = h.shape
    f = w_out.shape[0]
    bm = min(m, 512)
    bf = FFN_CHUNK
    nf = f // bf
    in_specs = [pl.BlockSpec((bm, d), lambda i, j: (i, 0)),
                pl.BlockSpec((SUBLANES, d), lambda i, j: (0, 0)),
                pl.BlockSpec((d, bf), lambda i, j: (0, j)),
                pl.BlockSpec((d, bf), lambda i, j: (0, j + nf)),
                pl.BlockSpec((bf, d), lambda i, j: (j, 0))]
    out_specs = [pl.BlockSpec((bm, d), lambda i, j: (i, 0))]
    out_shape = [jax.ShapeDtypeStruct((m, d), F32)]
    args = [h, vecs, w_in, w_in, w_out]
    if cast_next is not None:
        *srcs, layer, slot = cast_next
        for w in srcs:
            br, bc, ncb = _cast_plan(w.shape[2], w.shape[3], (m // bm) * nf)
            if ncb == nf:
                where = lambda i, j: (i, j)
            elif ncb == 1:
                where = lambda i, j: (i * nf + j, 0)
            else:
                where = functools.partial(lambda i, j, n: ((i * nf + j) // n, (i * nf + j) % n), n=ncb)
            in_specs.append(pl.BlockSpec((None, None, br, bc),
                                         functools.partial(lambda i, j, f: (layer, slot) + f(i, j), f=where)))
            out_specs.append(pl.BlockSpec((br, bc), where))
            out_shape.append(jax.ShapeDtypeStruct(w.shape[2:], BF16))
            args.append(w)
    outs = pl.pallas_call(
        functools.partial(_ffn_body, res_w=res_w, cast=cast_next is not None),
        grid=(m // bm, nf),
        in_specs=in_specs,
        out_specs=out_specs,
        out_shape=out_shape,
        scratch_shapes=[pltpu.VMEM((bm, d), BF16), pltpu.VMEM((bm, d), F32)],
        compiler_params=_cparams("arbitrary", "arbitrary"),
        name="ffn",
    )(*args)
    return outs if cast_next is not None else outs[0]


def _proj_in_body(h_ref, vec_ref, w_ref, b_ref, *rest, rope_blocks, q_blocks, q_scale):
    if rope_blocks:
        cos_ref, sa_ref, sb_ref, o_ref, u_ref = rest
    else:
        o_ref, u_ref = rest
    j = pl.program_id(1)

    @pl.when(j == 0)
    def _():
        _store_modulated(h_ref, vec_ref, u_ref)

    y = jnp.dot(u_ref[...], w_ref[...], preferred_element_type=F32) + b_ref[...]
    if not rope_blocks:
        o_ref[...] = y.astype(o_ref.dtype)
        return

    rot = jnp.where(j < q_blocks, q_scale, jnp.where(j < rope_blocks, 1.0, 0.0)).astype(F32)
    keep = jnp.where(j < rope_blocks, 0.0, 1.0).astype(F32)
    cos, sa, sb = cos_ref[...] * rot + keep, sa_ref[...] * rot, sb_ref[...] * rot
    for hd in range(y.shape[1] // HEAD_DIM):
        cols = slice(hd * HEAD_DIM, (hd + 1) * HEAD_DIM)
        xh = y[:, cols]
        r = (xh * cos + pltpu.roll(xh, HEAD_DIM - HEAD_DIM // 4, 1) * sa
             + pltpu.roll(xh, HEAD_DIM // 4, 1) * sb)
        o_ref[:, cols] = r.astype(o_ref.dtype)


def _proj_in(h, vecs, w, b, out_dtype, rope=None, q_blocks=0, q_scale=1.0):
    m, d = h.shape
    n = w.shape[1]
    bm = min(m, 1024)
    bn = 512
    in_specs = [pl.BlockSpec((bm, d), lambda i, j: (i, 0)),
                pl.BlockSpec((SUBLANES, d), lambda i, j: (0, 0)),
                pl.BlockSpec((d, bn), lambda i, j: (0, j)),
                pl.BlockSpec((1, bn), lambda i, j: (0, j))]
    args = [h, vecs, w, b.reshape(1, n)]
    rope_blocks = 0
    if rope is not None:
        rope_blocks = q_blocks + 1
        in_specs += [pl.BlockSpec((bm, HEAD_DIM), lambda i, j: (i, 0))] * 3
        args += list(rope)
    return pl.pallas_call(
        functools.partial(_proj_in_body, rope_blocks=rope_blocks, q_blocks=q_blocks, q_scale=q_scale),
        grid=(m // bm, n // bn),
        in_specs=in_specs,
        out_specs=pl.BlockSpec((bm, bn), lambda i, j: (i, j)),
        out_shape=jax.ShapeDtypeStruct((m, n), out_dtype),
        scratch_shapes=[pltpu.VMEM((bm, d), BF16)],
        compiler_params=_cparams("arbitrary", "arbitrary"),
        name="proj_in",
    )(*args)


def _proj_out_body(a_ref, w_ref, b_ref, h_ref, vec_ref, o_ref):
    y = jnp.dot(a_ref[...], w_ref[...], preferred_element_type=F32) + b_ref[...]
    o_ref[...] = h_ref[...] + vec_ref[2:3, :] * _rms(y, vec_ref[4:5, :])


def _proj_out(a, w, b, h, vecs):
    m, d = h.shape
    k = a.shape[1]
    bm = min(m, 256)
    return pl.pallas_call(
        _proj_out_body,
        grid=(m // bm,),
        in_specs=[pl.BlockSpec((bm, k), lambda i: (i, 0)),
                  pl.BlockSpec((k, d), lambda i: (0, 0)),
                  pl.BlockSpec((1, d), lambda i: (0, 0)),
                  pl.BlockSpec((bm, d), lambda i: (i, 0)),
                  pl.BlockSpec((SUBLANES, d), lambda i: (0, 0))],
        out_specs=pl.BlockSpec((bm, d), lambda i: (i, 0)),
        out_shape=jax.ShapeDtypeStruct((m, d), F32),
        compiler_params=_cparams("arbitrary"),
        name="proj_out",
    )(a, w, b.reshape(1, d), h, vecs)


def _hy_in_body(h_ref, hp_ref, hn_ref, vec_ref, w0_ref, w1_ref, w2_ref, b0_ref, b1_ref, b2_ref, wsc_ref, bsc_ref,
                x0_out, xin_out, u_ref, uh_ref):
    i = pl.program_id(0)
    last = pl.num_programs(0) - 1

    @pl.when(pl.program_id(1) == 0)
    def _():
        _store_modulated(h_ref, vec_ref, u_ref)
        uh_ref[0:SUBLANES, :] = hp_ref[...]
        uh_ref[SUBLANES:, :] = hn_ref[...]
        _store_modulated(uh_ref, vec_ref, uh_ref)

    def conv(w_ref, b_ref, part):
        z = jnp.dot(u_ref[...], w_ref[...], preferred_element_type=F32) + b_ref[...]
        zh = jnp.dot(uh_ref[...].astype(BF16), w_ref[...], preferred_element_type=F32) + b_ref[...]
        bm = z.shape[0]
        prev_row = jnp.where(i > 0, zh[SUBLANES - 1:SUBLANES, :], 0.0)
        next_row = jnp.where(i < last, zh[SUBLANES:SUBLANES + 1, :], 0.0)
        rid = lax.broadcasted_iota(jnp.int32, z.shape, 0)
        up = jnp.where(rid == 0, prev_row, pltpu.roll(z, 1, 0))
        dn = jnp.where(rid == bm - 1, next_row, pltpu.roll(z, bm - 1, 0))
        return (up * wsc_ref[0, part:part + 1, :] + z * wsc_ref[1, part:part + 1, :]
                + dn * wsc_ref[2, part:part + 1, :] + bsc_ref[part:part + 1, :])

    x0_out[...] = conv(w0_ref, b0_ref, 0).astype(x0_out.dtype)
    xin_out[...] = (conv(w2_ref, b2_ref, 2) * conv(w1_ref, b1_ref, 1)).astype(xin_out.dtype)


def _hy_in_gate(h, vecs, w_in, b_in, w_sc, b_sc):
    l, d = h.shape
    bm = min(l, 1024)
    bc = min(d, 256)
    nc = d // bc
    rb = bm // SUBLANES
    nrow8 = l // SUBLANES
    wspecs = [pl.BlockSpec((d, bc), functools.partial(lambda i, j, p: (0, p * nc + j), p=part)) for part in range(3)]
    bspecs = [pl.BlockSpec((1, bc), functools.partial(lambda i, j, p: (0, p * nc + j), p=part)) for part in range(3)]
    out_spec = pl.BlockSpec((bm, bc), lambda i, j: (i, j))
    return pl.pallas_call(
        _hy_in_body,
        grid=(l // bm, nc),
        in_specs=[pl.BlockSpec((bm, d), lambda i, j: (i, 0)),
                  pl.BlockSpec((SUBLANES, d), lambda i, j: (jnp.maximum(i * rb - 1, 0), 0)),
                  pl.BlockSpec((SUBLANES, d), lambda i, j: (jnp.minimum((i + 1) * rb, nrow8 - 1), 0)),
                  pl.BlockSpec((SUBLANES, d), lambda i, j: (0, 0))] + wspecs + bspecs
                 + [pl.BlockSpec((3, 3, bc), lambda i, j: (0, 0, j)),
                    pl.BlockSpec((3, bc), lambda i, j: (0, j))],
        out_specs=[out_spec, out_spec],
        out_shape=[jax.ShapeDtypeStruct((l, d), BF16)] * 2,
        scratch_shapes=[pltpu.VMEM((bm, d), BF16), pltpu.VMEM((2 * SUBLANES, d), F32)],
        compiler_params=_cparams("arbitrary", "arbitrary"),
        name="hyena_in_gate",
    )(h, h, h, vecs, w_in, w_in, w_in, *([b_in.reshape(1, 3 * d)] * 3), w_sc.reshape(3, 3, d), b_sc.reshape(3, d))


def _filter_mlp_body(z_ref, aux_ref, auxr_ref, flip_ref, w1, b1, w2, b2, w3, b3, fr, tab_ref, tabr_ref):
    hp = lax.Precision.HIGHEST
    hid = w1.shape[1] // 2
    f = jnp.sin(fr[0:1, :] * (jnp.dot(z_ref[...], w1[...], precision=hp, preferred_element_type=F32) + b1[...]))
    f = jnp.sin(fr[1:2, :] * (jnp.dot(f, w2[...], precision=hp, preferred_element_type=F32) + b2[...]))
    f = jnp.sin(fr[2:3, :] * (jnp.dot(f, w3[...], precision=hp, preferred_element_type=F32) + b3[...]))
    low = lax.broadcasted_iota(jnp.int32, f.shape, 1) < hid
    tab_ref[0] = jnp.where(low, f, aux_ref[0])
    tab_ref[1] = jnp.where(low, pltpu.roll(f, hid, 1), aux_ref[1])
    fr_rows = jnp.dot(flip_ref[...], f, precision=hp, preferred_element_type=F32)
    tabr_ref[0] = jnp.where(low, pltpu.roll(fr_rows, hid, 1), auxr_ref[0])
    tabr_ref[1] = jnp.where(low, fr_rows, auxr_ref[1])


def _filter_table(l, w1, b1, w2, b2, w3, b3, freq):
    hid = w1.shape[1]
    assert 2 * hid == LANES, "two filter-MLP evaluations share one 128-lane row"
    t = jnp.linspace(0.0, 1.0, l, dtype=F32)[:, None]
    omega = 2.0 * math.pi * jnp.arange(l, dtype=F32)[:, None] / l
    bands = jnp.linspace(1e-4, FILTER_BANDS - 1, FILTER_BANDS, dtype=F32)[None, :]
    emb = jnp.concatenate([t, jnp.cos(bands * omega), -jnp.sin(bands * omega)], axis=-1)
    emb = jnp.pad(emb, ((0, 0), (0, LANES - emb.shape[1])))
    w1p = jnp.pad(w1, ((0, LANES - w1.shape[0]), (0, 0)))
    rows = l // 2
    eye = jnp.eye(2, dtype=F32)
    emb_p = jnp.concatenate([emb[:rows], emb[rows:]], axis=1)

    def side_columns(tcol):
        cols = jnp.concatenate([jnp.zeros((l, hid), F32), tcol, jnp.ones((l, 1), F32)], axis=-1)
        return jnp.pad(cols, ((0, 0), (0, LANES - cols.shape[1]))).reshape(2, rows, LANES)

    bm = min(rows, 512)
    nb = rows // bm
    flip = jnp.eye(bm, dtype=F32)[::-1]

    def full(shape):
        return pl.BlockSpec(shape, lambda i: (0,) * len(shape))

    fwd_spec = pl.BlockSpec((2, bm, LANES), lambda i: (0, i, 0))
    rev_spec = pl.BlockSpec((2, bm, LANES), lambda i: (0, nb - 1 - i, 0))
    tab, tab_rev = pl.pallas_call(
        _filter_mlp_body,
        grid=(nb,),
        in_specs=[pl.BlockSpec((bm, 2 * LANES), lambda i: (i, 0)), fwd_spec, rev_spec, full((bm, bm)),
                  full((2 * LANES, LANES)), full((1, LANES)), full((LANES, LANES)), full((1, LANES)),
                  full((LANES, LANES)), full((1, LANES)), full((3, LANES))],
        out_specs=[fwd_spec, rev_spec],
        out_shape=[jax.ShapeDtypeStruct((2, rows, LANES), F32)] * 2,
        compiler_params=_cparams("arbitrary"),
        name="hyena_filter_mlp",
    )(emb_p, side_columns(t), side_columns(t[::-1]), flip,
      jnp.kron(eye, w1p), jnp.tile(b1, 2).reshape(1, LANES), jnp.kron(eye, w2),
      jnp.tile(b2, 2).reshape(1, LANES), jnp.kron(eye, w3), jnp.tile(b3, 2).reshape(1, LANES),
      jnp.tile(freq, (1, 2)))
    return jnp.concatenate([tab.reshape(l, LANES), jnp.zeros((1, LANES), F32), tab_rev.reshape(l, LANES)[:l - 1]],
                           axis=0)


def _decay_rates(d):
    return jnp.abs(jnp.linspace(math.log(DECAY_TARGET) / LONG_DECAY_PCT,
                                math.log(DECAY_TARGET) / SHORT_DECAY_PCT, d, dtype=F32))[None, :]


def _split_bf16(x):
    hi = x.astype(BF16)
    return hi, (x - hi.astype(F32)).astype(BF16)


def _stack_3pass(w):
    w_hi, w_lo = _split_bf16(w)
    return jnp.concatenate([w_hi, w_hi, w_lo], axis=0)


def _dot_3pass(f, w3):
    f_hi, f_lo = _split_bf16(f)
    return jnp.dot(jnp.concatenate([f_hi, f_lo, f_hi], axis=1), w3, preferred_element_type=F32)


def _filter_rows(ft, wa3, wb3, dl_ref):
    hid = wa3.shape[0] // 3
    half = ft.shape[0] // 2
    f, t, keep = ft[:, :hid], ft[:, hid:hid + 1], ft[:, hid + 1:hid + 2]
    y = jnp.concatenate([_dot_3pass(f[:half], wa3), _dot_3pass(f[half:], wb3)], axis=0)
    return y * (jnp.exp(-t * dl_ref[...]) * keep)


def _outer_stage(fa_ref, cols, o_ref):
    ys = [jnp.dot(fa_ref[...], xj, preferred_element_type=F32).astype(BF16) for xj in cols]
    o_ref[...] = pltpu.einshape("brd->rbd", jnp.stack(ys))


def _dft_a_body(x_ref, fa_ref, o_ref):
    xb = pltpu.einshape("abd->bad", x_ref[...])
    _outer_stage(fa_ref, [xb[j] for j in range(xb.shape[0])], o_ref)


def _dft_a_filter_body(ft_ref, wa_ref, wb_ref, dl_ref, fa_ref, o_ref):
    wa3, wb3 = _stack_3pass(wa_ref[...]), _stack_3pass(wb_ref[...])
    cols = [_filter_rows(ft_ref[j], wa3, wb3, dl_ref).astype(BF16) for j in range(ft_ref.shape[0])]
    _outer_stage(fa_ref, cols, o_ref)


def _dft_a(x3, fa, d_chunk):
    a, b, d = x3.shape
    rows = fa.shape[0]
    return pl.pallas_call(
        _dft_a_body,
        grid=(b // HALO, d // d_chunk),
        in_specs=[pl.BlockSpec((a, HALO, d_chunk), lambda i, j: (0, i, j)),
                  pl.BlockSpec((rows, a), lambda i, j: (0, 0))],
        out_specs=pl.BlockSpec((rows, HALO, d_chunk), lambda i, j: (0, i, j)),
        out_shape=jax.ShapeDtypeStruct((rows, b, d), BF16),
        compiler_params=_cparams("arbitrary", "arbitrary"),
        name="dft_outer",
    )(x3, fa)


def _dft_a_filter(ft3, w4, deltas, fa, d_chunk):
    b, n1, _ = ft3.shape
    hid, d2 = w4.shape
    d = d2 // 2
    rows = fa.shape[0]
    nd = d // d_chunk
    return pl.pallas_call(
        _dft_a_filter_body,
        grid=(b // HALO, nd),
        in_specs=[pl.BlockSpec((HALO, n1, LANES), lambda i, j: (i, 0, 0)),
                  pl.BlockSpec((hid, d_chunk), lambda i, j: (0, j)),
                  pl.BlockSpec((hid, d_chunk), lambda i, j: (0, j + nd)),
                  pl.BlockSpec((1, d_chunk), lambda i, j: (0, j)),
                  pl.BlockSpec((rows, n1), lambda i, j: (0, 0))],
        out_specs=pl.BlockSpec((rows, HALO, d_chunk), lambda i, j: (0, i, j)),
        out_shape=jax.ShapeDtypeStruct((rows, b, d), BF16),
        compiler_params=_cparams("arbitrary", "arbitrary"),
        name="dft_outer_filter",
    )(ft3, w4, w4, deltas, fa)


def _dft_inner_body(y_ref, yf_ref, gf_ref, o_ref):
    b, dc = y_ref.shape[2], y_ref.shape[3]
    for ci in range(y_ref.shape[1]):
        z = jnp.dot(gf_ref[ci], y_ref[:, ci].reshape(2 * b, dc), preferred_element_type=F32)
        hh = jnp.dot(gf_ref[ci], yf_ref[:, ci].reshape(2 * b, dc), preferred_element_type=F32)
        zr, zi, hr, hi = z[:b], z[b:], hh[:b], hh[b:]
        p = jnp.concatenate([zr * hr - zi * hi, zr * hi + zi * hr], axis=0).astype(BF16)
        cc = lax.dot_general(gf_ref[ci], p, (((0,), (0,)), ((), ())), preferred_element_type=F32)
        o_ref[:, ci] = cc.reshape(2, b, dc).astype(o_ref.dtype)


def _dft_inner(y4, yf4, gf):
    _, nc, b, d = y4.shape
    cblk = SUBLANES
    dc = min(d, 1024)
    yspec = pl.BlockSpec((2, cblk, b, dc), lambda c, j: (0, c, 0, j))
    return pl.pallas_call(
        _dft_inner_body,
        grid=(nc // cblk, d // dc),
        in_specs=[yspec, yspec, pl.BlockSpec((cblk, 2 * b, 2 * b), lambda c, j: (c, 0, 0))],
        out_specs=yspec,
        out_shape=jax.ShapeDtypeStruct(y4.shape, BF16),
        compiler_params=_cparams("arbitrary", "arbitrary"),
        name="dft_inner",
    )(y4, yf4, gf)


def _dft_c_body(c_ref, m_ref, xin_ref, x0_ref, skip_ref, o_ref):
    cb = pltpu.einshape("rbd->brd", c_ref[...])
    outs = [jnp.dot(m_ref[...], cb[j], preferred_element_type=F32).astype(BF16) for j in range(cb.shape[0])]
    conv = pltpu.einshape("bad->abd", jnp.stack(outs)).astype(F32)
    gated = x0_ref[...].astype(F32) * (conv + xin_ref[...].astype(F32) * skip_ref[...][None])
    o_ref[...] = gated.astype(o_ref.dtype)


def _dft_c(c3, mc, xin3, x03, skip, d_chunk):
    rows, b, d = c3.shape
    a = mc.shape[0]
    xspec = pl.BlockSpec((a, HALO, d_chunk), lambda i, j: (0, i, j))
    return pl.pallas_call(
        _dft_c_body,
        grid=(b // HALO, d // d_chunk),
        in_specs=[pl.BlockSpec((rows, HALO, d_chunk), lambda i, j: (0, i, j)),
                  pl.BlockSpec((a, rows), lambda i, j: (0, 0)),
                  xspec, xspec,
                  pl.BlockSpec((1, d_chunk), lambda i, j: (0, j))],
        out_specs=xspec,
        out_shape=jax.ShapeDtypeStruct((a, b, d), BF16),
        compiler_params=_cparams("arbitrary", "arbitrary"),
        name="dft_outer_inv",
    )(c3, mc, xin3, x03, skip)


def _cis(num, period):
    ang = (2.0 * math.pi / period) * num.astype(F32)
    return jnp.cos(ang), jnp.sin(ang)


def _long_conv_gated(xin, x0, ftab, w4, skip):
    l, d = xin.shape
    b = DFT_INNER
    a = l // b
    n1, n = 2 * a, 2 * l
    nc = n1 // 2 + SUBLANES
    dc = min(d, 512)
    ci = jnp.arange(nc, dtype=jnp.int32)
    ai = jnp.arange(n1, dtype=jnp.int32)
    live = (ci <= n1 // 2).astype(F32)[:, None]
    cr, sr = _cis((ci[:, None] * ai[None, :]) % n1, n1)
    cr, sr = cr * live, sr * live
    fa_full = jnp.concatenate([cr, -sr], axis=0).astype(BF16)
    fa = fa_full[:, :a]
    fold = jnp.where((ci == 0) | (ci == n1 // 2), 1.0, 2.0)[None, :] / n
    mc = jnp.concatenate([cr[:, :a].T * fold, -sr[:, :a].T * fold], axis=1).astype(BF16)
    ei = jnp.arange(b, dtype=jnp.int32)
    kk = ci[:, None, None] + n1 * ei[None, :, None]
    tr, ts = _cis((kk * ei[None, None, :]) % n, n)
    ti = -ts
    gf = jnp.concatenate([jnp.concatenate([tr, -ti], axis=2),
                          jnp.concatenate([ti, tr], axis=2)], axis=1).astype(BF16)

    ft3 = jnp.swapaxes(ftab.reshape(n1, b, LANES), 0, 1)
    hf = _dft_a_filter(ft3, w4, _decay_rates(d), fa_full, dc)
    yx = _dft_a(xin.reshape(a, b, d), fa, dc)
    c4 = _dft_inner(yx.reshape(2, nc, b, d), hf.reshape(2, nc, b, d), gf)
    y3 = _dft_c(c4.reshape(2 * nc, b, d), mc, xin.reshape(a, b, d), x0.reshape(a, b, d), skip.reshape(1, d), dc)
    return y3.reshape(l, d)


def _ctx_conv_body(x_ref, x0_ref, ft_ref, wa_ref, wb_ref, dl_ref, skip_ref, ff_ref, fh_ref, mi_ref, o_ref):
    x = x_ref[...].astype(F32)
    n = ft_ref.shape[0]
    filt = _filter_rows(ft_ref[...], _stack_3pass(wa_ref[...]), _stack_3pass(wb_ref[...]), dl_ref)
    xs = jnp.dot(ff_ref[...], x.astype(BF16), preferred_element_type=F32)
    hs = jnp.dot(fh_ref[...], filt.astype(BF16), preferred_element_type=F32)
    xr, xi, hr, hi = xs[:n], xs[n:], hs[:n], hs[n:]
    p = jnp.concatenate([xr * hr - xi * hi, xr * hi + xi * hr], axis=0).astype(BF16)
    conv = jnp.dot(mi_ref[...], p, preferred_element_type=F32)
    o_ref[...] = (x0_ref[...].astype(F32) * (conv + x * skip_ref[...])).astype(o_ref.dtype)


def _short_seq_conv_gated(xin, x0, ftab, w4, skip):
    l, d = xin.shape
    n = 2 * l
    hid = w4.shape[0]
    ni = jnp.arange(n, dtype=jnp.int32)
    cr, sr = _cis((ni[:, None] * ni[None, :]) % n, n)
    fh = jnp.concatenate([cr, -sr], axis=0).astype(BF16)
    ff = fh[:, :l]
    mi = (jnp.concatenate([cr[:l], -sr[:l]], axis=1) * (1.0 / n)).astype(BF16)
    dc = min(d, 512)
    nd = d // dc
    return pl.pallas_call(
        _ctx_conv_body,
        grid=(nd,),
        in_specs=[pl.BlockSpec((l, dc), lambda j: (0, j)),
                  pl.BlockSpec((l, dc), lambda j: (0, j)),
                  pl.BlockSpec((n, LANES), lambda j: (0, 0)),
                  pl.BlockSpec((hid, dc), lambda j: (0, j)),
                  pl.BlockSpec((hid, dc), lambda j: (0, j + nd)),
                  pl.BlockSpec((1, dc), lambda j: (0, j)),
                  pl.BlockSpec((1, dc), lambda j: (0, j)),
                  pl.BlockSpec((2 * n, l), lambda j: (0, 0)),
                  pl.BlockSpec((2 * n, n), lambda j: (0, 0)),
                  pl.BlockSpec((l, 2 * n), lambda j: (0, 0))],
        out_specs=pl.BlockSpec((l, dc), lambda j: (0, j)),
        out_shape=jax.ShapeDtypeStruct((l, d), BF16),
        compiler_params=_cparams("arbitrary"),
        name="ctx_conv",
    )(xin, x0, ftab, w4, w4, _decay_rates(d), skip.reshape(1, d), ff, fh, mi)


def _hyena_mixer(h, vecs, p, long_seq):
    w_in, b_in, w_sc, b_sc, f_w1, f_b1, f_w2, f_b2, f_w3, f_b3, f_w4, f_freq, skip, w_out, b_out = p
    l = h.shape[0]
    x0, xin = _hy_in_gate(h, vecs, w_in, b_in, w_sc, b_sc)
    ftab = _filter_table(l, f_w1, f_b1, f_w2, f_b2, f_w3, f_b3, f_freq)
    conv = _long_conv_gated if long_seq else _short_seq_conv_gated
    y = conv(xin, x0, ftab, f_w4, skip)
    return _proj_out(y, w_out, b_out, h, vecs)


def _attn_body(sink_ref, q_ref, kp_ref, kc_ref, kn_ref, vp_ref, vc_ref, vn_ref, kx_ref, vx_ref, bias_ref, o_ref,
               *, group):
    blk = q_ref.shape[0]
    rid = lax.broadcasted_iota(jnp.int32, (group * blk, 1), 0)
    for kh in range(N_KV_HEADS):
        hs = slice(kh * HEAD_DIM, (kh + 1) * HEAD_DIM)
        heads = [kh * group + g for g in range(group)]
        q = jnp.concatenate([q_ref[:, hd * HEAD_DIM:(hd + 1) * HEAD_DIM] for hd in heads], axis=0)
        keys = jnp.concatenate([kp_ref[:, hs], kc_ref[:, hs], kn_ref[:, hs], kx_ref[:, hs]], axis=0)
        vals = jnp.concatenate([vp_ref[:, hs], vc_ref[:, hs], vn_ref[:, hs], vx_ref[:, hs]], axis=0)
        s = lax.dot_general(q, keys, (((1,), (1,)), ((), ())), preferred_element_type=F32)
        pieces = [s[:, :blk] + bias_ref[0, :, :blk], s[:, blk:2 * blk], s[:, 2 * blk:3 * blk] + bias_ref[0, :, blk:]]
        pieces += [s[:, c0:c0 + blk] for c0 in range(3 * blk, s.shape[1], blk)]
        sink = jnp.zeros((group * blk, 1), F32)
        for g, hd in enumerate(heads):
            sink = jnp.where((rid >= g * blk) & (rid < (g + 1) * blk), sink_ref[hd] * LOG2_E, sink)
        top = functools.reduce(jnp.maximum, pieces)
        mx = jnp.maximum(jnp.max(top, axis=-1, keepdims=True), sink)
        probs = [jnp.exp2(pc - mx) for pc in pieces]
        denom = jnp.sum(functools.reduce(jnp.add, probs), axis=-1, keepdims=True) + jnp.exp2(sink - mx)
        pr = jnp.concatenate([pp.astype(BF16) for pp in probs], axis=1)
        o = jnp.dot(pr, vals, preferred_element_type=F32) / denom
        for g, hd in enumerate(heads):
            o_ref[:, hd * HEAD_DIM:(hd + 1) * HEAD_DIM] = o[g * blk:(g + 1) * blk].astype(o_ref.dtype)


def _attention(qkv, kvc, sink, d):
    l = qkv.shape[0]
    c = kvc.shape[0]
    blk = ATTN_BLOCK
    nb = l // blk
    group = d // HEAD_DIM // N_KV_HEADS
    kvw = N_KV_HEADS * HEAD_DIM
    kcol = d // kvw
    qi = jnp.arange(group * blk, dtype=jnp.int32)[:, None] % blk
    ki = jnp.arange(blk, dtype=jnp.int32)[None, :]
    prev_ok, next_ok, never = ki >= qi, ki <= qi, jnp.zeros((group * blk, blk), bool)
    variants = [(never, next_ok), (prev_ok, next_ok), (prev_ok, never)]
    bias = jnp.stack([jnp.where(jnp.concatenate(v, axis=1), 0.0, MASK_BIAS).astype(F32) for v in variants])

    def kv_spec(col, shift):
        return pl.BlockSpec((blk, kvw), lambda n: (jnp.clip(n + shift, 0, nb - 1), col))

    return pl.pallas_call(
        functools.partial(_attn_body, group=group),
        grid=(nb,),
        in_specs=[pl.BlockSpec(memory_space=pltpu.SMEM),
                  pl.BlockSpec((blk, d), lambda n: (n, 0)),
                  kv_spec(kcol, -1), kv_spec(kcol, 0), kv_spec(kcol, 1),
                  kv_spec(kcol + 1, -1), kv_spec(kcol + 1, 0), kv_spec(kcol + 1, 1),
                  pl.BlockSpec((c, kvw), lambda n: (0, 0)),
                  pl.BlockSpec((c, kvw), lambda n: (0, 1)),
                  pl.BlockSpec((1, group * blk, 2 * blk),
                               lambda n: (jnp.where(n == 0, 0, jnp.where(n == nb - 1, 2, 1)), 0, 0))],
        out_specs=pl.BlockSpec((blk, d), lambda n: (n, 0)),
        out_shape=jax.ShapeDtypeStruct((l, d), BF16),
        compiler_params=_cparams("arbitrary"),
        name="window_attn",
    )(sink, qkv, qkv, qkv, qkv, qkv, qkv, qkv, kvc, kvc, bias)


def _rope_tables(l):
    rows = l // GRID_W
    pos_row = jnp.broadcast_to(jnp.arange(rows)[:, None], (rows, GRID_W)).reshape(-1).astype(F32)
    pos_col = jnp.broadcast_to(jnp.arange(GRID_W)[None, :], (rows, GRID_W)).reshape(-1).astype(F32)
    pairs = HEAD_DIM // 4
    inv_freq = ROPE_THETA ** (-jnp.arange(pairs, dtype=F32) / pairs)
    ang_row = pos_row[:, None] * inv_freq[None, :]
    ang_col = pos_col[:, None] * inv_freq[None, :]
    zeros = jnp.zeros_like(ang_row)
    cos = jnp.concatenate([jnp.cos(ang_row)] * 2 + [jnp.cos(ang_col)] * 2, axis=-1)
    sin_a = jnp.concatenate([-jnp.sin(ang_row), zeros, -jnp.sin(ang_col), zeros], axis=-1)
    sin_b = jnp.concatenate([zeros, jnp.sin(ang_row), zeros, jnp.sin(ang_col)], axis=-1)
    return cos, sin_a, sin_b


def _pool_body(hc_ref, hp_ref, hn_ref, vec_ref, w_ref, b_ref, sc_ref, o_ref, y_ref, *, seq_len):
    i = pl.program_id(0)
    last = pl.num_programs(0) - 1
    bm, d = hc_ref.shape
    gw = d // len(POOL_SIZES)
    h = hc_ref[...]
    u = _modulated(h, vec_ref)
    up = jnp.where(i > 0, _modulated(hp_ref[...], vec_ref), 0.0)
    un = jnp.where(i < last, _modulated(hn_ref[...], vec_ref), 0.0)
    ext_rows = bm + 2 * SUBLANES
    t = i * bm + lax.broadcasted_iota(jnp.int32, (bm, 1), 0)
    for g, size in enumerate(POOL_SIZES):
        cols = slice(g * gw, (g + 1) * gw)
        ext = jnp.concatenate([up[:, cols], u[:, cols], un[:, cols]], axis=0)
        acc, span = ext, 1
        while span < size:
            acc = acc + pltpu.roll(acc, ext_rows - span, 0)
            span *= 2
        start = SUBLANES - size // 2
        win = pltpu.roll(acc, ext_rows - start, 0)[:bm] if start else acc[:bm]
        lo = jnp.clip(t - size // 2, 0, seq_len)
        hi = jnp.clip(t - size // 2 + size, 0, seq_len)
        part = win / (hi - lo).astype(F32) - u[:, cols]
        yg = jnp.dot(part.astype(BF16), w_ref[g], preferred_element_type=F32)
        y_ref[:, cols] = (yg + b_ref[:, cols]) * sc_ref[:, cols]
    o_ref[...] = h + vec_ref[2:3, :] * _rms(y_ref[...], vec_ref[4:5, :])


def _pool_mixer(h, vecs, w, b, scale):
    l, d = h.shape
    bm = min(l, 256)
    rb = bm // SUBLANES
    nrow8 = l // SUBLANES
    ng, gw = w.shape[0], w.shape[1]
    return pl.pallas_call(
        functools.partial(_pool_body, seq_len=l),
        grid=(l // bm,),
        in_specs=[pl.BlockSpec((bm, d), lambda i: (i, 0)),
                  pl.BlockSpec((SUBLANES, d), lambda i: (jnp.maximum(i * rb - 1, 0), 0)),
                  pl.BlockSpec((SUBLANES, d), lambda i: (jnp.minimum((i + 1) * rb, nrow8 - 1), 0)),
                  pl.BlockSpec((SUBLANES, d), lambda i: (0, 0)),
                  pl.BlockSpec((ng, gw, gw), lambda i: (0, 0, 0)),
                  pl.BlockSpec((1, d), lambda i: (0, 0)),
                  pl.BlockSpec((1, d), lambda i: (0, 0))],
        out_specs=pl.BlockSpec((bm, d), lambda i: (i, 0)),
        out_shape=jax.ShapeDtypeStruct((l, d), F32),
        scratch_shapes=[pltpu.VMEM((bm, d), F32)],
        compiler_params=_cparams("arbitrary"),
        name="pool_mixer",
    )(h, h, h, vecs, w.astype(BF16), b.reshape(1, d), scale.reshape(1, d))


def _sub_vecs(mod, k, g_pre, g_post):
    rows = [mod[3 * k], mod[3 * k + 1], mod[3 * k + 2], g_pre, g_post]
    return jnp.stack(rows + [jnp.zeros_like(g_pre)] * (SUBLANES - len(rows)))


def kernel(x, c, ctx, c_ctx, w_ada, b_ada, norm_pre, norm_post, w_ffn_in, w_ffn_out, hy_w_in, hy_b_in, hy_w_sc, hy_b_sc, hy_f_w1, hy_f_b1, hy_f_w2, hy_f_b2, hy_f_w3, hy_f_b3, hy_f_w4, hy_f_freq, hy_skip, hy_w_out, hy_b_out, at_w_qkv, at_b_qkv, at_sink, at_w_o, at_b_o, pl_w, pl_b, pl_scale):
    bsz, l, d = x.shape
    assert bsz == 1, "kernel handles a single batch element"
    assert d % 1024 == 0 and l % 1024 == 0 and ctx.shape[1] % 256 == 0, "block sizes assume these multiples"
    assert w_ffn_out.shape[2] % 512 == 0, "FFN width must be a multiple of the 512-column weight blocks"
    depth = w_ada.shape[0]
    n_mixers = 3
    attn_layers = [i for i in range(depth) if i % n_mixers == 1]
    last_ctx_layer = attn_layers[-1] if attn_layers else -1

    mods = _adaln_mods(c, c_ctx, w_ada, b_ada).reshape(depth, 2, N_MOD, d)
    rope = _rope_tables(l)
    w_next = [w_ffn_in[0, 0].astype(BF16), w_ffn_out[0, 0].astype(BF16)]
    h, hc = x[0], ctx[0]
    for i in range(depth):
        kind, j = i % n_mixers, i // n_mixers
        ctx_live = i <= last_ctx_layer
        ctx_out = i < last_ctx_layer
        vec = [_sub_vecs(mods[i, 0], k, norm_pre[i, k], norm_post[i, k]) for k in range(3)]
        vec_c = [_sub_vecs(mods[i, 1], k, norm_pre[i, k], norm_post[i, k]) for k in range(3)]

        w_now = w_next
        h, *w_next = _ffn(h, vec[0], *w_now, FFN_RES, cast_next=(w_ffn_in, w_ffn_out, i, 1))
        if ctx_live:
            hc = _ffn(hc, vec_c[0], *w_now, FFN_RES)

        if kind == 0:
            hp = (hy_w_in[j].astype(BF16), hy_b_in[j], hy_w_sc[j], hy_b_sc[j], hy_f_w1[j], hy_f_b1[j],
                  hy_f_w2[j], hy_f_b2[j], hy_f_w3[j], hy_f_b3[j], hy_f_w4[j], hy_f_freq[j], hy_skip[j],
                  hy_w_out[j].astype(BF16), hy_b_out[j])
            h = _hyena_mixer(h, vec[1], hp, long_seq=True)
            if ctx_out:
                hc = _hyena_mixer(hc, vec_c[1], hp, long_seq=False)
        elif kind == 1:
            w_qkv = at_w_qkv[j].astype(BF16)
            qkv = _proj_in(h, vec[1], w_qkv, at_b_qkv[j], BF16, rope=rope,
                           q_blocks=d // 512, q_scale=LOG2_E * HEAD_DIM ** -0.5)
            kvc = _proj_in(hc, vec_c[1], w_qkv[:, d:], at_b_qkv[j][d:], BF16)
            o = _attention(qkv, kvc, at_sink[j], d)
            h = _proj_out(o, at_w_o[j].astype(BF16), at_b_o[j], h, vec[1])
            assert not ctx_out, "context-query attention path is not needed for this depth"
        else:
            h = _pool_mixer(h, vec[1], pl_w[j], pl_b[j], pl_scale[j])
            assert not ctx_out, "context pooling path is not needed for this depth"

        w_now = w_next
        if i + 1 < depth:
            h, *w_next = _ffn(h, vec[2], *w_now, FFN_RES, cast_next=(w_ffn_in, w_ffn_out, i + 1, 0))
        else:
            h = _ffn(h, vec[2], *w_now, FFN_RES)
        if ctx_out:
            hc = _ffn(hc, vec_c[2], *w_now, FFN_RES)
    return h[None]
```

```python
import functools
import math

import jax
import jax.numpy as jnp
from jax import lax
from jax.experimental import pallas as pl
from jax.experimental.pallas import tpu as pltpu

F32 = jnp.float32
BF16 = jnp.bfloat16

GRID_W = 64
N_MOD = 9
NORM_EPS = 1e-6
FFN_RES = 0.5
FILTER_BANDS = 16
FILTER_EMB = 1 + 2 * FILTER_BANDS
DECAY_TARGET = 1e-2
SHORT_DECAY_PCT = 0.3
LONG_DECAY_PCT = 1.5
HEAD_DIM = 128
N_KV_HEADS = 4
WINDOW = 128
ATTN_BLOCK = 128
ROPE_THETA = 10000.0
POOL_SIZES = (2, 4, 8, 16)

VMEM_LIMIT_BYTES = 56 * 1024 * 1024
SUBLANES = 8
LANES = 128
DFT_INNER = 128
HALO = 16
FFN_CHUNK = 512
MASK_BIAS = -1e30
LOG2_E = math.log2(math.e)


def _cparams(*sem):
    return pltpu.CompilerParams(dimension_semantics=sem, vmem_limit_bytes=VMEM_LIMIT_BYTES)


def _rms(x, g):
    ms = jnp.mean(x * x, axis=-1, keepdims=True)
    return x * lax.rsqrt(ms + NORM_EPS) * g


def _modulated(h, vec_ref):
    return _rms(h, vec_ref[3:4, :]) * (1.0 + vec_ref[1:2, :]) + vec_ref[0:1, :]


def _store_modulated(h_ref, vec_ref, u_ref):
    gain = vec_ref[3:4, :] * (1.0 + vec_ref[1:2, :])
    u_ref[...] = (_rms(h_ref[...], gain) + vec_ref[0:1, :]).astype(u_ref.dtype)


def _mods_body(cb_ref, w_ref, b_ref, o_ref, s_ref):
    k_dim, bn = w_ref.shape[1], w_ref.shape[2]
    nl = bn // LANES

    @pl.when((pl.program_id(0) == 0) & (pl.program_id(1) == 0))
    def _():
        cv = cb_ref[...]
        s_ref[...] = cv * jax.nn.sigmoid(cv)

    def step(kg, acc):
        rows = pl.ds(pl.multiple_of(kg * SUBLANES, SUBLANES), SUBLANES)
        s = [s_ref[r, rows, :] for r in range(2)]
        new = list(acc)
        for j in range(nl):
            wv = w_ref[0, rows, j * LANES:(j + 1) * LANES]
            for r in range(2):
                new[r * nl + j] = acc[r * nl + j] + wv * s[r]
        return tuple(new)

    init = tuple(jnp.zeros((SUBLANES, LANES), F32) for _ in range(2 * nl))
    acc = lax.fori_loop(0, k_dim // SUBLANES, step, init, unroll=4)
    for r in range(2):
        for j in range(nl):
            cols = slice(j * LANES, (j + 1) * LANES)
            o_ref[0, r:r + 1, cols] = jnp.sum(acc[r * nl + j], axis=0, keepdims=True) + b_ref[0, :, cols]


def _adaln_mods(c, c_ctx, w_ada, b_ada):
    depth, d, n = w_ada.shape
    bn = 1024
    cb = jnp.broadcast_to(jnp.stack([c[0], c_ctx])[:, :, None], (2, d, LANES))
    return pl.pallas_call(
        _mods_body,
        grid=(depth, n // bn),
        in_specs=[pl.BlockSpec((2, d, LANES), lambda i, j: (0, 0, 0)),
                  pl.BlockSpec((1, d, bn), lambda i, j: (i, 0, j)),
                  pl.BlockSpec((1, 1, bn), lambda i, j: (i, 0, j))],
        out_specs=pl.BlockSpec((1, 2, bn), lambda i, j: (i, 0, j)),
        out_shape=jax.ShapeDtypeStruct((depth, 2, n), F32),
        scratch_shapes=[pltpu.VMEM((2, d, LANES), F32)],
        compiler_params=_cparams("arbitrary", "arbitrary"),
        name="adaln_mods",
    )(cb, w_ada, b_ada.reshape(depth, 1, n))


def _ffn_body(h_ref, vec_ref, wg_ref, wu_ref, wo_ref, *rest, res_w, cast):
    if cast:
        ci_ref, co_ref, o_ref, cib_ref, cob_ref, u_ref, acc_ref = rest
    else:
        o_ref, u_ref, acc_ref = rest
    j = pl.program_id(1)

    @pl.when((pl.program_id(0) == 0) & (j == 0))
    def _():
        acc_ref[...] = jnp.zeros_like(acc_ref)

    @pl.when(j == 0)
    def _():
        _store_modulated(h_ref, vec_ref, u_ref)

    u = u_ref[...]
    g = jnp.dot(u, wg_ref[...], preferred_element_type=F32)
    p = jnp.dot(u, wu_ref[...], preferred_element_type=F32)
    a = (g * jax.nn.sigmoid(g) * p).astype(BF16)
    acc_ref[...] += jnp.dot(a, wo_ref[...], preferred_element_type=F32)

    if cast:
        cib_ref[...] = ci_ref[...].astype(cib_ref.dtype)
        cob_ref[...] = co_ref[...].astype(cob_ref.dtype)

    @pl.when(j == pl.num_programs(1) - 1)
    def _():
        gate = (res_w * vec_ref[2:3, :]) * vec_ref[4:5, :]
        o_ref[...] = h_ref[...] + _rms(acc_ref[...], gate)
        acc_ref[...] = jnp.zeros_like(acc_ref)


def _cast_plan(n_rows, n_cols, steps):
    for n_col_blocks in range(1, steps + 1):
        if steps % n_col_blocks or n_cols % n_col_blocks or n_rows % (steps // n_col_blocks):
            continue
        br, bc = n_rows // (steps // n_col_blocks), n_cols // n_col_blocks
        if br % HALO == 0 and bc % LANES == 0:
            return br, bc, n_col_blocks
    raise ValueError("no tiling of the weight cast fits this grid")


def _ffn(h, vecs, w_in, w_out, res_w, cast_next=None):
    m, d = h.shape
    f = w_out.shape[0]
    bm = min(m, 512)
    bf = FFN_CHUNK
    nf = f // bf
    in_specs = [pl.BlockSpec((bm, d), lambda i, j: (i, 0)),
                pl.BlockSpec((SUBLANES, d), lambda i, j: (0, 0)),
                pl.BlockSpec((d, bf), lambda i, j: (0, j)),
                pl.BlockSpec((d, bf), lambda i, j: (0, j + nf)),
                pl.BlockSpec((bf, d), lambda i, j: (j, 0))]
    out_specs = [pl.BlockSpec((bm, d), lambda i, j: (i, 0))]
    out_shape = [jax.ShapeDtypeStruct((m, d), F32)]
    args = [h, vecs, w_in, w_in, w_out]
    if cast_next is not None:
        *srcs, layer, slot = cast_next
        for w in srcs:
            br, bc, ncb = _cast_plan(w.shape[2], w.shape[3], (m // bm) * nf)
            if ncb == nf:
                where = lambda i, j: (i, j)
            elif ncb == 1:
                where = lambda i, j: (i * nf + j, 0)
            else:
                where = functools.partial(lambda i, j, n: ((i * nf + j) // n, (i * nf + j) % n), n=ncb)
            in_specs.append(pl.BlockSpec((None, None, br, bc),
                                         functools.partial(lambda i, j, f: (layer, slot) + f(i, j), f=where)))
            out_specs.append(pl.BlockSpec((br, bc), where))
            out_shape.append(jax.ShapeDtypeStruct(w.shape[2:], BF16))
            args.append(w)
    outs = pl.pallas_call(
        functools.partial(_ffn_body, res_w=res_w, cast=cast_next is not None),
        grid=(m // bm, nf),
        in_specs=in_specs,
        out_specs=out_specs,
        out_shape=out_shape,
        scratch_shapes=[pltpu.VMEM((bm, d), BF16), pltpu.VMEM((bm, d), F32)],
        compiler_params=_cparams("arbitrary", "arbitrary"),
        name="ffn",
    )(*args)
    return outs if cast_next is not None else outs[0]


def _proj_in_body(h_ref, vec_ref, w_ref, b_ref, *rest, rope_blocks, q_blocks, q_scale):
    if rope_blocks:
        cos_ref, sa_ref, sb_ref, o_ref, u_ref = rest
    else:
        o_ref, u_ref = rest
    j = pl.program_id(1)

    @pl.when(j == 0)
    def _():
        _store_modulated(h_ref, vec_ref, u_ref)

    y = jnp.dot(u_ref[...], w_ref[...], preferred_element_type=F32) + b_ref[...]
    if not rope_blocks:
        o_ref[...] = y.astype(o_ref.dtype)
        return

    rot = jnp.where(j < q_blocks, q_scale, jnp.where(j < rope_blocks, 1.0, 0.0)).astype(F32)
    keep = jnp.where(j < rope_blocks, 0.0, 1.0).astype(F32)
    cos, sa, sb = cos_ref[...] * rot + keep, sa_ref[...] * rot, sb_ref[...] * rot
    for hd in range(y.shape[1] // HEAD_DIM):
        cols = slice(hd * HEAD_DIM, (hd + 1) * HEAD_DIM)
        xh = y[:, cols]
        r = (xh * cos + pltpu.roll(xh, HEAD_DIM - HEAD_DIM // 4, 1) * sa
             + pltpu.roll(xh, HEAD_DIM // 4, 1) * sb)
        o_ref[:, cols] = r.astype(o_ref.dtype)


def _proj_in(h, vecs, w, b, out_dtype, rope=None, q_blocks=0, q_scale=1.0):
    m, d = h.shape
    n = w.shape[1]
    bm = min(m, 1024)
    bn = 512
    in_specs = [pl.BlockSpec((bm, d), lambda i, j: (i, 0)),
                pl.BlockSpec((SUBLANES, d), lambda i, j: (0, 0)),
                pl.BlockSpec((d, bn), lambda i, j: (0, j)),
                pl.BlockSpec((1, bn), lambda i, j: (0, j))]
    args = [h, vecs, w, b.reshape(1, n)]
    rope_blocks = 0
    if rope is not None:
        rope_blocks = q_blocks + 1
        in_specs += [pl.BlockSpec((bm, HEAD_DIM), lambda i, j: (i, 0))] * 3
        args += list(rope)
    return pl.pallas_call(
        functools.partial(_proj_in_body, rope_blocks=rope_blocks, q_blocks=q_blocks, q_scale=q_scale),
        grid=(m // bm, n // bn),
        in_specs=in_specs,
        out_specs=pl.BlockSpec((bm, bn), lambda i, j: (i, j)),
        out_shape=jax.ShapeDtypeStruct((m, n), out_dtype),
        scratch_shapes=[pltpu.VMEM((bm, d), BF16)],
        compiler_params=_cparams("arbitrary", "arbitrary"),
        name="proj_in",
    )(*args)


def _proj_out_body(a_ref, w_ref, b_ref, h_ref, vec_ref, o_ref):
    half = a_ref.shape[0] // 2
    gate = vec_ref[2:3, :] * vec_ref[4:5, :]
    halves = [slice(0, half), slice(half, 2 * half)]
    ys = [jnp.dot(a_ref[rows, :], w_ref[...], preferred_element_type=F32) + b_ref[...] for rows in halves]
    for rows, y in zip(halves, ys):
        o_ref[rows, :] = h_ref[rows, :] + _rms(y, gate)


def _proj_out(a, w, b, h, vecs):
    m, d = h.shape
    k = a.shape[1]
    bm = min(m, 512)
    return pl.pallas_call(
        _proj_out_body,
        grid=(m // bm,),
        in_specs=[pl.BlockSpec((bm, k), lambda i: (i, 0)),
                  pl.BlockSpec((k, d), lambda i: (0, 0)),
                  pl.BlockSpec((1, d), lambda i: (0, 0)),
                  pl.BlockSpec((bm, d), lambda i: (i, 0)),
                  pl.BlockSpec((SUBLANES, d), lambda i: (0, 0))],
        out_specs=pl.BlockSpec((bm, d), lambda i: (i, 0)),
        out_shape=jax.ShapeDtypeStruct((m, d), F32),
        compiler_params=_cparams("arbitrary"),
        name="proj_out",
    )(a, w, b.reshape(1, d), h, vecs)


def _hy_in_body(h_ref, hp_ref, hn_ref, vec_ref, w0_ref, w1_ref, w2_ref, b0_ref, b1_ref, b2_ref, wsc_ref, bsc_ref,
                x0_out, xin_out, u_ref, uh_ref):
    i = pl.program_id(0)
    last = pl.num_programs(0) - 1

    @pl.when(pl.program_id(1) == 0)
    def _():
        _store_modulated(h_ref, vec_ref, u_ref)
        uh_ref[0:SUBLANES, :] = hp_ref[...]
        uh_ref[SUBLANES:, :] = hn_ref[...]
        _store_modulated(uh_ref, vec_ref, uh_ref)

    u, uh = u_ref[...], uh_ref[...].astype(BF16)
    thirds = ((w0_ref, b0_ref), (w1_ref, b1_ref), (w2_ref, b2_ref))
    zs = [jnp.dot(u, w[...], preferred_element_type=F32) + b[...] for w, b in thirds]
    zhs = [jnp.dot(uh, w[...], preferred_element_type=F32) + b[...] for w, b in thirds]

    def conv(part):
        z, zh = zs[part], zhs[part]
        bm = z.shape[0]
        prev_row = jnp.where(i > 0, zh[SUBLANES - 1:SUBLANES, :], 0.0)
        next_row = jnp.where(i < last, zh[SUBLANES:SUBLANES + 1, :], 0.0)
        rid = lax.broadcasted_iota(jnp.int32, z.shape, 0)
        up = jnp.where(rid == 0, prev_row, pltpu.roll(z, 1, 0))
        dn = jnp.where(rid == bm - 1, next_row, pltpu.roll(z, bm - 1, 0))
        return (up * wsc_ref[0, part:part + 1, :] + z * wsc_ref[1, part:part + 1, :]
                + dn * wsc_ref[2, part:part + 1, :] + bsc_ref[part:part + 1, :])

    x0_out[...] = conv(0).astype(x0_out.dtype)
    xin_out[...] = (conv(2) * conv(1)).astype(xin_out.dtype)


def _hy_in_gate(h, vecs, w_in, b_in, w_sc, b_sc):
    l, d = h.shape
    bm = min(l, 1024)
    bc = min(d, 256)
    nc = d // bc
    rb = bm // SUBLANES
    nrow8 = l // SUBLANES
    wspecs = [pl.BlockSpec((d, bc), functools.partial(lambda i, j, p: (0, p * nc + j), p=part)) for part in range(3)]
    bspecs = [pl.BlockSpec((1, bc), functools.partial(lambda i, j, p: (0, p * nc + j), p=part)) for part in range(3)]
    out_spec = pl.BlockSpec((bm, bc), lambda i, j: (i, j))
    return pl.pallas_call(
        _hy_in_body,
        grid=(l // bm, nc),
        in_specs=[pl.BlockSpec((bm, d), lambda i, j: (i, 0)),
                  pl.BlockSpec((SUBLANES, d), lambda i, j: (jnp.maximum(i * rb - 1, 0), 0)),
                  pl.BlockSpec((SUBLANES, d), lambda i, j: (jnp.minimum((i + 1) * rb, nrow8 - 1), 0)),
                  pl.BlockSpec((SUBLANES, d), lambda i, j: (0, 0))] + wspecs + bspecs
                 + [pl.BlockSpec((3, 3, bc), lambda i, j: (0, 0, j)),
                    pl.BlockSpec((3, bc), lambda i, j: (0, j))],
        out_specs=[out_spec, out_spec],
        out_shape=[jax.ShapeDtypeStruct((l, d), BF16)] * 2,
        scratch_shapes=[pltpu.VMEM((bm, d), BF16), pltpu.VMEM((2 * SUBLANES, d), F32)],
        compiler_params=_cparams("arbitrary", "arbitrary"),
        name="hyena_in_gate",
    )(h, h, h, vecs, w_in, w_in, w_in, *([b_in.reshape(1, 3 * d)] * 3), w_sc.reshape(3, 3, d), b_sc.reshape(3, d))


def _filter_mlp_body(z_ref, aux_ref, auxr_ref, flip_ref, w1, b1, w2, b2, w3, b3, fr, tab_ref, tabr_ref):
    hp = lax.Precision.HIGHEST
    hid = w1.shape[1] // 2
    f = jnp.sin(fr[0:1, :] * (jnp.dot(z_ref[...], w1[...], precision=hp, preferred_element_type=F32) + b1[...]))
    f = jnp.sin(fr[1:2, :] * (jnp.dot(f, w2[...], precision=hp, preferred_element_type=F32) + b2[...]))
    f = jnp.sin(fr[2:3, :] * (jnp.dot(f, w3[...], precision=hp, preferred_element_type=F32) + b3[...]))
    low = lax.broadcasted_iota(jnp.int32, f.shape, 1) < hid
    tab_ref[0] = jnp.where(low, f, aux_ref[0])
    tab_ref[1] = jnp.where(low, pltpu.roll(f, hid, 1), aux_ref[1])
    fr_rows = jnp.dot(flip_ref[...], f, precision=hp, preferred_element_type=F32)
    tabr_ref[0] = jnp.where(low, pltpu.roll(fr_rows, hid, 1), auxr_ref[0])
    tabr_ref[1] = jnp.where(low, fr_rows, auxr_ref[1])


def _filter_table(l, w1, b1, w2, b2, w3, b3, freq):
    hid = w1.shape[1]
    assert 2 * hid == LANES, "two filter-MLP evaluations share one 128-lane row"
    t = jnp.linspace(0.0, 1.0, l, dtype=F32)[:, None]
    omega = 2.0 * math.pi * jnp.arange(l, dtype=F32)[:, None] / l
    bands = jnp.linspace(1e-4, FILTER_BANDS - 1, FILTER_BANDS, dtype=F32)[None, :]
    emb = jnp.concatenate([t, jnp.cos(bands * omega), -jnp.sin(bands * omega)], axis=-1)
    emb = jnp.pad(emb, ((0, 0), (0, LANES - emb.shape[1])))
    w1p = jnp.pad(w1, ((0, LANES - w1.shape[0]), (0, 0)))
    rows = l // 2
    eye = jnp.eye(2, dtype=F32)
    emb_p = jnp.concatenate([emb[:rows], emb[rows:]], axis=1)

    def side_columns(tcol):
        cols = jnp.concatenate([jnp.zeros((l, hid), F32), tcol, jnp.ones((l, 1), F32)], axis=-1)
        return jnp.pad(cols, ((0, 0), (0, LANES - cols.shape[1]))).reshape(2, rows, LANES)

    bm = min(rows, 512)
    nb = rows // bm
    flip = jnp.eye(bm, dtype=F32)[::-1]

    def full(shape):
        return pl.BlockSpec(shape, lambda i: (0,) * len(shape))

    fwd_spec = pl.BlockSpec((2, bm, LANES), lambda i: (0, i, 0))
    rev_spec = pl.BlockSpec((2, bm, LANES), lambda i: (0, nb - 1 - i, 0))
    tab, tab_rev = pl.pallas_call(
        _filter_mlp_body,
        grid=(nb,),
        in_specs=[pl.BlockSpec((bm, 2 * LANES), lambda i: (i, 0)), fwd_spec, rev_spec, full((bm, bm)),
                  full((2 * LANES, LANES)), full((1, LANES)), full((LANES, LANES)), full((1, LANES)),
                  full((LANES, LANES)), full((1, LANES)), full((3, LANES))],
        out_specs=[fwd_spec, rev_spec],
        out_shape=[jax.ShapeDtypeStruct((2, rows, LANES), F32)] * 2,
        compiler_params=_cparams("arbitrary"),
        name="hyena_filter_mlp",
    )(emb_p, side_columns(t), side_columns(t[::-1]), flip,
      jnp.kron(eye, w1p), jnp.tile(b1, 2).reshape(1, LANES), jnp.kron(eye, w2),
      jnp.tile(b2, 2).reshape(1, LANES), jnp.kron(eye, w3), jnp.tile(b3, 2).reshape(1, LANES),
      jnp.tile(freq, (1, 2)))
    return jnp.concatenate([tab.reshape(l, LANES), jnp.zeros((1, LANES), F32), tab_rev.reshape(l, LANES)[:l - 1]],
                           axis=0)


def _decay_rates(d):
    return jnp.abs(jnp.linspace(math.log(DECAY_TARGET) / LONG_DECAY_PCT,
                                math.log(DECAY_TARGET) / SHORT_DECAY_PCT, d, dtype=F32))[None, :]


def _split_bf16(x):
    hi = x.astype(BF16)
    return hi, (x - hi.astype(F32)).astype(BF16)


def _stack_3pass(w):
    w_hi, w_lo = _split_bf16(w)
    return jnp.concatenate([w_hi, w_hi, w_lo], axis=0)


def _dot_3pass(f, w3):
    f_hi, f_lo = _split_bf16(f)
    return jnp.dot(jnp.concatenate([f_hi, f_lo, f_hi], axis=1), w3, preferred_element_type=F32)


def _filter_rows(ft, wa3, wb3, dl_ref):
    hid = wa3.shape[0] // 3
    half = ft.shape[0] // 2
    f, t, keep = ft[:, :hid], ft[:, hid:hid + 1], ft[:, hid + 1:hid + 2]
    y = jnp.concatenate([_dot_3pass(f[:half], wa3), _dot_3pass(f[half:], wb3)], axis=0)
    return y * (jnp.exp(-t * dl_ref[...]) * keep)


def _outer_stage(fa_ref, cols, o_ref):
    ys = [jnp.dot(fa_ref[...], xj, preferred_element_type=F32).astype(BF16) for xj in cols]
    o_ref[...] = pltpu.einshape("brd->rbd", jnp.stack(ys))


def _dft_a_body(x_ref, fa_ref, o_ref):
    xb = pltpu.einshape("abd->bad", x_ref[...])
    _outer_stage(fa_ref, [xb[j] for j in range(xb.shape[0])], o_ref)


def _dft_a_filter_body(ft_ref, wa_ref, wb_ref, dl_ref, fa_ref, o_ref):
    wa3, wb3 = _stack_3pass(wa_ref[...]), _stack_3pass(wb_ref[...])
    cols = [_filter_rows(ft_ref[j], wa3, wb3, dl_ref).astype(BF16) for j in range(ft_ref.shape[0])]
    _outer_stage(fa_ref, cols, o_ref)


def _dft_a(x3, fa, d_chunk):
    a, b, d = x3.shape
    rows = fa.shape[0]
    return pl.pallas_call(
        _dft_a_body,
        grid=(b // HALO, d // d_chunk),
        in_specs=[pl.BlockSpec((a, HALO, d_chunk), lambda i, j: (0, i, j)),
                  pl.BlockSpec((rows, a), lambda i, j: (0, 0))],
        out_specs=pl.BlockSpec((rows, HALO, d_chunk), lambda i, j: (0, i, j)),
        out_shape=jax.ShapeDtypeStruct((rows, b, d), BF16),
        compiler_params=_cparams("arbitrary", "arbitrary"),
        name="dft_outer",
    )(x3, fa)


def _dft_a_filter(ft3, w4, deltas, fa, d_chunk):
    b, n1, _ = ft3.shape
    hid, d2 = w4.shape
    d = d2 // 2
    rows = fa.shape[0]
    nd = d // d_chunk
    return pl.pallas_call(
        _dft_a_filter_body,
        grid=(b // HALO, nd),
        in_specs=[pl.BlockSpec((HALO, n1, LANES), lambda i, j: (i, 0, 0)),
                  pl.BlockSpec((hid, d_chunk), lambda i, j: (0, j)),
                  pl.BlockSpec((hid, d_chunk), lambda i, j: (0, j + nd)),
                  pl.BlockSpec((1, d_chunk), lambda i, j: (0, j)),
                  pl.BlockSpec((rows, n1), lambda i, j: (0, 0))],
        out_specs=pl.BlockSpec((rows, HALO, d_chunk), lambda i, j: (0, i, j)),
        out_shape=jax.ShapeDtypeStruct((rows, b, d), BF16),
        compiler_params=_cparams("arbitrary", "arbitrary"),
        name="dft_outer_filter",
    )(ft3, w4, w4, deltas, fa)


def _dft_inner_body(y_ref, yf_ref, gf_ref, o_ref):
    b, dc = y_ref.shape[2], y_ref.shape[3]
    for ci in range(y_ref.shape[1]):
        z = jnp.dot(gf_ref[ci], y_ref[:, ci].reshape(2 * b, dc), preferred_element_type=F32)
        hh = jnp.dot(gf_ref[ci], yf_ref[:, ci].reshape(2 * b, dc), preferred_element_type=F32)
        zr, zi, hr, hi = z[:b], z[b:], hh[:b], hh[b:]
        p = jnp.concatenate([zr * hr - zi * hi, zr * hi + zi * hr], axis=0).astype(BF16)
        cc = lax.dot_general(gf_ref[ci], p, (((0,), (0,)), ((), ())), preferred_element_type=F32)
        o_ref[:, ci] = cc.reshape(2, b, dc).astype(o_ref.dtype)


def _dft_inner(y4, yf4, gf):
    _, nc, b, d = y4.shape
    cblk = SUBLANES
    dc = min(d, 1024)
    yspec = pl.BlockSpec((2, cblk, b, dc), lambda c, j: (0, c, 0, j))
    return pl.pallas_call(
        _dft_inner_body,
        grid=(nc // cblk, d // dc),
        in_specs=[yspec, yspec, pl.BlockSpec((cblk, 2 * b, 2 * b), lambda c, j: (c, 0, 0))],
        out_specs=yspec,
        out_shape=jax.ShapeDtypeStruct(y4.shape, BF16),
        compiler_params=_cparams("arbitrary", "arbitrary"),
        name="dft_inner",
    )(y4, yf4, gf)


def _dft_c_body(c_ref, m_ref, xin_ref, x0_ref, skip_ref, o_ref):
    cb = pltpu.einshape("rbd->brd", c_ref[...])
    outs = [jnp.dot(m_ref[...], cb[j], preferred_element_type=F32).astype(BF16) for j in range(cb.shape[0])]
    conv = pltpu.einshape("bad->abd", jnp.stack(outs)).astype(F32)
    gated = x0_ref[...].astype(F32) * (conv + xin_ref[...].astype(F32) * skip_ref[...][None])
    o_ref[...] = gated.astype(o_ref.dtype)


def _dft_c(c3, mc, xin3, x03, skip, d_chunk):
    rows, b, d = c3.shape
    a = mc.shape[0]
    xspec = pl.BlockSpec((a, HALO, d_chunk), lambda i, j: (0, i, j))
    return pl.pallas_call(
        _dft_c_body,
        grid=(b // HALO, d // d_chunk),
        in_specs=[pl.BlockSpec((rows, HALO, d_chunk), lambda i, j: (0, i, j)),
                  pl.BlockSpec((a, rows), lambda i, j: (0, 0)),
                  xspec, xspec,
                  pl.BlockSpec((1, d_chunk), lambda i, j: (0, j))],
        out_specs=xspec,
        out_shape=jax.ShapeDtypeStruct((a, b, d), BF16),
        compiler_params=_cparams("arbitrary", "arbitrary"),
        name="dft_outer_inv",
    )(c3, mc, xin3, x03, skip)


def _cis(num, period):
    ang = (2.0 * math.pi / period) * num.astype(F32)
    return jnp.cos(ang), jnp.sin(ang)


def _long_conv_gated(xin, x0, ftab, w4, skip):
    l, d = xin.shape
    b = DFT_INNER
    a = l // b
    n1, n = 2 * a, 2 * l
    nc = n1 // 2 + SUBLANES
    dc = min(d, 512)
    ci = jnp.arange(nc, dtype=jnp.int32)
    ai = jnp.arange(n1, dtype=jnp.int32)
    live = (ci <= n1 // 2).astype(F32)[:, None]
    cr, sr = _cis((ci[:, None] * ai[None, :]) % n1, n1)
    cr, sr = cr * live, sr * live
    fa_full = jnp.concatenate([cr, -sr], axis=0).astype(BF16)
    fa = fa_full[:, :a]
    fold = jnp.where((ci == 0) | (ci == n1 // 2), 1.0, 2.0)[None, :] / n
    mc = jnp.concatenate([cr[:, :a].T * fold, -sr[:, :a].T * fold], axis=1).astype(BF16)
    ei = jnp.arange(b, dtype=jnp.int32)
    pr, ps = _cis((ci[:, None] * ei[None, :]) % n, n)
    qr, qs = _cis((ei[:, None] * ei[None, :]) % b, b)
    tr = pr[:, None, :] * qr[None] - ps[:, None, :] * qs[None]
    ti = -(pr[:, None, :] * qs[None] + ps[:, None, :] * qr[None])
    gf = jnp.concatenate([jnp.concatenate([tr, -ti], axis=2),
                          jnp.concatenate([ti, tr], axis=2)], axis=1).astype(BF16)

    ft3 = jnp.swapaxes(ftab.reshape(n1, b, LANES), 0, 1)
    hf = _dft_a_filter(ft3, w4, _decay_rates(d), fa_full, dc)
    yx = _dft_a(xin.reshape(a, b, d), fa, dc)
    c4 = _dft_inner(yx.reshape(2, nc, b, d), hf.reshape(2, nc, b, d), gf)
    y3 = _dft_c(c4.reshape(2 * nc, b, d), mc, xin.reshape(a, b, d), x0.reshape(a, b, d), skip.reshape(1, d), dc)
    return y3.reshape(l, d)


def _ctx_conv_body(x_ref, x0_ref, ft_ref, wa_ref, wb_ref, dl_ref, skip_ref, ff_ref, fh_ref, mi_ref, o_ref):
    x = x_ref[...].astype(F32)
    n = ft_ref.shape[0]
    filt = _filter_rows(ft_ref[...], _stack_3pass(wa_ref[...]), _stack_3pass(wb_ref[...]), dl_ref)
    xs = jnp.dot(ff_ref[...], x.astype(BF16), preferred_element_type=F32)
    hs = jnp.dot(fh_ref[...], filt.astype(BF16), preferred_element_type=F32)
    xr, xi, hr, hi = xs[:n], xs[n:], hs[:n], hs[n:]
    p = jnp.concatenate([xr * hr - xi * hi, xr * hi + xi * hr], axis=0).astype(BF16)
    conv = jnp.dot(mi_ref[...], p, preferred_element_type=F32)
    o_ref[...] = (x0_ref[...].astype(F32) * (conv + x * skip_ref[...])).astype(o_ref.dtype)


def _short_seq_conv_gated(xin, x0, ftab, w4, skip):
    l, d = xin.shape
    n = 2 * l
    hid = w4.shape[0]
    ni = jnp.arange(n, dtype=jnp.int32)
    cr, sr = _cis((ni[:, None] * ni[None, :]) % n, n)
    fh = jnp.concatenate([cr, -sr], axis=0).astype(BF16)
    ff = fh[:, :l]
    mi = (jnp.concatenate([cr[:l], -sr[:l]], axis=1) * (1.0 / n)).astype(BF16)
    dc = min(d, 512)
    nd = d // dc
    return pl.pallas_call(
        _ctx_conv_body,
        grid=(nd,),
        in_specs=[pl.BlockSpec((l, dc), lambda j: (0, j)),
                  pl.BlockSpec((l, dc), lambda j: (0, j)),
                  pl.BlockSpec((n, LANES), lambda j: (0, 0)),
                  pl.BlockSpec((hid, dc), lambda j: (0, j)),
                  pl.BlockSpec((hid, dc), lambda j: (0, j + nd)),
                  pl.BlockSpec((1, dc), lambda j: (0, j)),
                  pl.BlockSpec((1, dc), lambda j: (0, j)),
                  pl.BlockSpec((2 * n, l), lambda j: (0, 0)),
                  pl.BlockSpec((2 * n, n), lambda j: (0, 0)),
                  pl.BlockSpec((l, 2 * n), lambda j: (0, 0))],
        out_specs=pl.BlockSpec((l, dc), lambda j: (0, j)),
        out_shape=jax.ShapeDtypeStruct((l, d), BF16),
        compiler_params=_cparams("arbitrary"),
        name="ctx_conv",
    )(xin, x0, ftab, w4, w4, _decay_rates(d), skip.reshape(1, d), ff, fh, mi)


def _hyena_mixer(h, vecs, p, long_seq):
    w_in, b_in, w_sc, b_sc, f_w1, f_b1, f_w2, f_b2, f_w3, f_b3, f_w4, f_freq, skip, w_out, b_out = p
    l = h.shape[0]
    x0, xin = _hy_in_gate(h, vecs, w_in, b_in, w_sc, b_sc)
    ftab = _filter_table(l, f_w1, f_b1, f_w2, f_b2, f_w3, f_b3, f_freq)
    conv = _long_conv_gated if long_seq else _short_seq_conv_gated
    y = conv(xin, x0, ftab, f_w4, skip)
    return _proj_out(y, w_out, b_out, h, vecs)


def _attn_body(sink_ref, q_ref, kp_ref, kc_ref, kn_ref, vp_ref, vc_ref, vn_ref, kx_ref, vx_ref, bias_ref, o_ref,
               *, group):
    blk = q_ref.shape[0]
    rid = lax.broadcasted_iota(jnp.int32, (group * blk, 1), 0)
    for kh in range(N_KV_HEADS):
        hs = slice(kh * HEAD_DIM, (kh + 1) * HEAD_DIM)
        heads = [kh * group + g for g in range(group)]
        q = jnp.concatenate([q_ref[:, hd * HEAD_DIM:(hd + 1) * HEAD_DIM] for hd in heads], axis=0)
        keys = jnp.concatenate([kp_ref[:, hs], kc_ref[:, hs], kn_ref[:, hs], kx_ref[:, hs]], axis=0)
        vals = jnp.concatenate([vp_ref[:, hs], vc_ref[:, hs], vn_ref[:, hs], vx_ref[:, hs]], axis=0)
        s = lax.dot_general(q, keys, (((1,), (1,)), ((), ())), preferred_element_type=F32)
        pieces = [s[:, :blk] + bias_ref[0, :, :blk], s[:, blk:2 * blk], s[:, 2 * blk:3 * blk] + bias_ref[0, :, blk:]]
        pieces += [s[:, c0:c0 + blk] for c0 in range(3 * blk, s.shape[1], blk)]
        sink = jnp.zeros((group * blk, 1), F32)
        for g, hd in enumerate(heads):
            sink = jnp.where((rid >= g * blk) & (rid < (g + 1) * blk), sink_ref[hd] * LOG2_E, sink)
        top = functools.reduce(jnp.maximum, pieces)
        mx = jnp.maximum(jnp.max(top, axis=-1, keepdims=True), sink)
        probs = [jnp.exp2(pc - mx) for pc in pieces]
        denom = jnp.sum(functools.reduce(jnp.add, probs), axis=-1, keepdims=True) + jnp.exp2(sink - mx)
        pr = jnp.concatenate([pp.astype(BF16) for pp in probs], axis=1)
        o = jnp.dot(pr, vals, preferred_element_type=F32) / denom
        for g, hd in enumerate(heads):
            o_ref[:, hd * HEAD_DIM:(hd + 1) * HEAD_DIM] = o[g * blk:(g + 1) * blk].astype(o_ref.dtype)


def _attention(qkv, kvc, sink, d):
    l = qkv.shape[0]
    c = kvc.shape[0]
    blk = ATTN_BLOCK
    nb = l // blk
    group = d // HEAD_DIM // N_KV_HEADS
    kvw = N_KV_HEADS * HEAD_DIM
    kcol = d // kvw
    qi = jnp.arange(group * blk, dtype=jnp.int32)[:, None] % blk
    ki = jnp.arange(blk, dtype=jnp.int32)[None, :]
    prev_ok, next_ok, never = ki >= qi, ki <= qi, jnp.zeros((group * blk, blk), bool)
    variants = [(never, next_ok), (prev_ok, next_ok), (prev_ok, never)]
    bias = jnp.stack([jnp.where(jnp.concatenate(v, axis=1), 0.0, MASK_BIAS).astype(F32) for v in variants])

    def kv_spec(col, shift):
        return pl.BlockSpec((blk, kvw), lambda n: (jnp.clip(n + shift, 0, nb - 1), col))

    return pl.pallas_call(
        functools.partial(_attn_body, group=group),
        grid=(nb,),
        in_specs=[pl.BlockSpec(memory_space=pltpu.SMEM),
                  pl.BlockSpec((blk, d), lambda n: (n, 0)),
                  kv_spec(kcol, -1), kv_spec(kcol, 0), kv_spec(kcol, 1),
                  kv_spec(kcol + 1, -1), kv_spec(kcol + 1, 0), kv_spec(kcol + 1, 1),
                  pl.BlockSpec((c, kvw), lambda n: (0, 0)),
                  pl.BlockSpec((c, kvw), lambda n: (0, 1)),
                  pl.BlockSpec((1, group * blk, 2 * blk),
                               lambda n: (jnp.where(n == 0, 0, jnp.where(n == nb - 1, 2, 1)), 0, 0))],
        out_specs=pl.BlockSpec((blk, d), lambda n: (n, 0)),
        out_shape=jax.ShapeDtypeStruct((l, d), BF16),
        compiler_params=_cparams("arbitrary"),
        name="window_attn",
    )(sink, qkv, qkv, qkv, qkv, qkv, qkv, qkv, kvc, kvc, bias)


def _rope_tables(l):
    rows = l // GRID_W
    pos_row = jnp.broadcast_to(jnp.arange(rows)[:, None], (rows, GRID_W)).reshape(-1).astype(F32)
    pos_col = jnp.broadcast_to(jnp.arange(GRID_W)[None, :], (rows, GRID_W)).reshape(-1).astype(F32)
    pairs = HEAD_DIM // 4
    inv_freq = ROPE_THETA ** (-jnp.arange(pairs, dtype=F32) / pairs)
    ang_row = pos_row[:, None] * inv_freq[None, :]
    ang_col = pos_col[:, None] * inv_freq[None, :]
    zeros = jnp.zeros_like(ang_row)
    cos = jnp.concatenate([jnp.cos(ang_row)] * 2 + [jnp.cos(ang_col)] * 2, axis=-1)
    sin_a = jnp.concatenate([-jnp.sin(ang_row), zeros, -jnp.sin(ang_col), zeros], axis=-1)
    sin_b = jnp.concatenate([zeros, jnp.sin(ang_row), zeros, jnp.sin(ang_col)], axis=-1)
    return cos, sin_a, sin_b


def _pool_body(hc_ref, hp_ref, hn_ref, vec_ref, w_ref, b_ref, sc_ref, o_ref, y_ref, *, seq_len):
    i = pl.program_id(0)
    last = pl.num_programs(0) - 1
    bm, d = hc_ref.shape
    gw = d // len(POOL_SIZES)
    h = hc_ref[...]
    u = _modulated(h, vec_ref)
    up = jnp.where(i > 0, _modulated(hp_ref[...], vec_ref), 0.0)
    un = jnp.where(i < last, _modulated(hn_ref[...], vec_ref), 0.0)
    ext_rows = bm + 2 * SUBLANES
    t = i * bm + lax.broadcasted_iota(jnp.int32, (bm, 1), 0)
    for g, size in enumerate(POOL_SIZES):
        cols = slice(g * gw, (g + 1) * gw)
        ext = jnp.concatenate([up[:, cols], u[:, cols], un[:, cols]], axis=0)
        acc, span = ext, 1
        while span < size:
            acc = acc + pltpu.roll(acc, ext_rows - span, 0)
            span *= 2
        start = SUBLANES - size // 2
        win = pltpu.roll(acc, ext_rows - start, 0)[:bm] if start else acc[:bm]
        lo = jnp.clip(t - size // 2, 0, seq_len)
        hi = jnp.clip(t - size // 2 + size, 0, seq_len)
        part = win / (hi - lo).astype(F32) - u[:, cols]
        yg = jnp.dot(part.astype(BF16), w_ref[g], preferred_element_type=F32)
        y_ref[:, cols] = (yg + b_ref[:, cols]) * sc_ref[:, cols]
    o_ref[...] = h + vec_ref[2:3, :] * _rms(y_ref[...], vec_ref[4:5, :])


def _pool_mixer(h, vecs, w, b, scale):
    l, d = h.shape
    bm = min(l, 256)
    rb = bm // SUBLANES
    nrow8 = l // SUBLANES
    ng, gw = w.shape[0], w.shape[1]
    return pl.pallas_call(
        functools.partial(_pool_body, seq_len=l),
        grid=(l // bm,),
        in_specs=[pl.BlockSpec((bm, d), lambda i: (i, 0)),
                  pl.BlockSpec((SUBLANES, d), lambda i: (jnp.maximum(i * rb - 1, 0), 0)),
                  pl.BlockSpec((SUBLANES, d), lambda i: (jnp.minimum((i + 1) * rb, nrow8 - 1), 0)),
                  pl.BlockSpec((SUBLANES, d), lambda i: (0, 0)),
                  pl.BlockSpec((ng, gw, gw), lambda i: (0, 0, 0)),
                  pl.BlockSpec((1, d), lambda i: (0, 0)),
                  pl.BlockSpec((1, d), lambda i: (0, 0))],
        out_specs=pl.BlockSpec((bm, d), lambda i: (i, 0)),
        out_shape=jax.ShapeDtypeStruct((l, d), F32),
        scratch_shapes=[pltpu.VMEM((bm, d), F32)],
        compiler_params=_cparams("arbitrary"),
        name="pool_mixer",
    )(h, h, h, vecs, w.astype(BF16), b.reshape(1, d), scale.reshape(1, d))


def _sub_vecs(mod, k, g_pre, g_post):
    rows = [mod[3 * k], mod[3 * k + 1], mod[3 * k + 2], g_pre, g_post]
    return jnp.stack(rows + [jnp.zeros_like(g_pre)] * (SUBLANES - len(rows)))


def kernel(x, c, ctx, c_ctx, w_ada, b_ada, norm_pre, norm_post, w_ffn_in, w_ffn_out, hy_w_in, hy_b_in, hy_w_sc, hy_b_sc, hy_f_w1, hy_f_b1, hy_f_w2, hy_f_b2, hy_f_w3, hy_f_b3, hy_f_w4, hy_f_freq, hy_skip, hy_w_out, hy_b_out, at_w_qkv, at_b_qkv, at_sink, at_w_o, at_b_o, pl_w, pl_b, pl_scale):
    bsz, l, d = x.shape
    assert bsz == 1, "kernel handles a single batch element"
    assert d % 1024 == 0 and l % 1024 == 0 and ctx.shape[1] % 256 == 0, "block sizes assume these multiples"
    assert w_ffn_out.shape[2] % 512 == 0, "FFN width must be a multiple of the 512-column weight blocks"
    depth = w_ada.shape[0]
    n_mixers = 3
    attn_layers = [i for i in range(depth) if i % n_mixers == 1]
    last_ctx_layer = attn_layers[-1] if attn_layers else -1

    mods = _adaln_mods(c, c_ctx, w_ada, b_ada).reshape(depth, 2, N_MOD, d)
    rope = _rope_tables(l)
    w_next = [w_ffn_in[0, 0].astype(BF16), w_ffn_out[0, 0].astype(BF16)]
    h, hc = x[0], ctx[0]
    for i in range(depth):
        kind, j = i % n_mixers, i // n_mixers
        ctx_live = i <= last_ctx_layer
        ctx_out = i < last_ctx_layer
        vec = [_sub_vecs(mods[i, 0], k, norm_pre[i, k], norm_post[i, k]) for k in range(3)]
        vec_c = [_sub_vecs(mods[i, 1], k, norm_pre[i, k], norm_post[i, k]) for k in range(3)]

        w_now = w_next
        h, *w_next = _ffn(h, vec[0], *w_now, FFN_RES, cast_next=(w_ffn_in, w_ffn_out, i, 1))
        if ctx_live:
            hc = _ffn(hc, vec_c[0], *w_now, FFN_RES)

        if kind == 0:
            hp = (hy_w_in[j].astype(BF16), hy_b_in[j], hy_w_sc[j], hy_b_sc[j], hy_f_w1[j], hy_f_b1[j],
                  hy_f_w2[j], hy_f_b2[j], hy_f_w3[j], hy_f_b3[j], hy_f_w4[j], hy_f_freq[j], hy_skip[j],
                  hy_w_out[j].astype(BF16), hy_b_out[j])
            h = _hyena_mixer(h, vec[1], hp, long_seq=True)
            if ctx_out:
                hc = _hyena_mixer(hc, vec_c[1], hp, long_seq=False)
        elif kind == 1:
            w_qkv = at_w_qkv[j].astype(BF16)
            qkv = _proj_in(h, vec[1], w_qkv, at_b_qkv[j], BF16, rope=rope,
                           q_blocks=d // 512, q_scale=LOG2_E * HEAD_DIM ** -0.5)
            kvc = _proj_in(hc, vec_c[1], w_qkv[:, d:], at_b_qkv[j][d:], BF16)
            o = _attention(qkv, kvc, at_sink[j], d)
            h = _proj_out(o, at_w_o[j].astype(BF16), at_b_o[j], h, vec[1])
            assert not ctx_out, "context-query attention path is not needed for this depth"
        else:
            h = _pool_mixer(h, vec[1], pl_w[j], pl_b[j], pl_scale[j])
            assert not ctx_out, "context pooling path is not needed for this depth"

        w_now = w_next
        if i + 1 < depth:
            h, *w_next = _ffn(h, vec[2], *w_now, FFN_RES, cast_next=(w_ffn_in, w_ffn_out, i + 1, 0))
        else:
            h = _ffn(h, vec[2], *w_now, FFN_RES)
        if ctx_out:
            hc = _ffn(hc, vec_c[2], *w_now, FFN_RES)
    return h[None]
```

```python
import functools
import math

import jax
import jax.numpy as jnp
from jax import lax
from jax.experimental import pallas as pl
from jax.experimental.pallas import tpu as pltpu

F32 = jnp.float32
BF16 = jnp.bfloat16

GRID_W = 64
N_MOD = 9
NORM_EPS = 1e-6
FFN_RES = 0.5
FILTER_BANDS = 16
FILTER_EMB = 1 + 2 * FILTER_BANDS
DECAY_TARGET = 1e-2
SHORT_DECAY_PCT = 0.3
LONG_DECAY_PCT = 1.5
HEAD_DIM = 128
N_KV_HEADS = 4
WINDOW = 128
ATTN_BLOCK = 128
ROPE_THETA = 10000.0
POOL_SIZES = (2, 4, 8, 16)

VMEM_LIMIT_BYTES = 56 * 1024 * 1024
SUBLANES = 8
LANES = 128
DFT_INNER = 128
HALO = 16
FFN_CHUNK = 512
FFN_ROWS = 512
PROJ_ROWS = 1024
PROJ_COLS = 512
HYENA_IN_COLS = 256
DFT_OUTER_COLS = 512
DFT_INNER_COLS = 1024
SMALL_ROWS = 256
MODS_COLS = 1024
FILTER_ROWS = 512
MASK_BIAS = -1e30
LOG2_E = math.log2(math.e)


def _cparams(*sem):
    return pltpu.CompilerParams(dimension_semantics=sem, vmem_limit_bytes=VMEM_LIMIT_BYTES)


def _rms(x, g):
    ms = jnp.mean(x * x, axis=-1, keepdims=True)
    return x * lax.rsqrt(ms + NORM_EPS) * g


def _modulated(h, vec_ref):
    return _rms(h, vec_ref[3:4, :]) * (1.0 + vec_ref[1:2, :]) + vec_ref[0:1, :]


def _store_modulated(h_ref, vec_ref, u_ref):
    gain = vec_ref[3:4, :] * (1.0 + vec_ref[1:2, :])
    u_ref[...] = (_rms(h_ref[...], gain) + vec_ref[0:1, :]).astype(u_ref.dtype)


def _mods_body(cb_ref, w_ref, b_ref, o_ref, s_ref):
    k_dim, bn = w_ref.shape[1], w_ref.shape[2]
    nl = bn // LANES

    @pl.when((pl.program_id(0) == 0) & (pl.program_id(1) == 0))
    def _():
        cv = cb_ref[...]
        s_ref[...] = cv * jax.nn.sigmoid(cv)

    def step(kg, acc):
        rows = pl.ds(pl.multiple_of(kg * SUBLANES, SUBLANES), SUBLANES)
        s = [s_ref[r, rows, :] for r in range(2)]
        new = list(acc)
        for j in range(nl):
            wv = w_ref[0, rows, j * LANES:(j + 1) * LANES]
            for r in range(2):
                new[r * nl + j] = acc[r * nl + j] + wv * s[r]
        return tuple(new)

    init = tuple(jnp.zeros((SUBLANES, LANES), F32) for _ in range(2 * nl))
    acc = lax.fori_loop(0, k_dim // SUBLANES, step, init, unroll=4)
    for r in range(2):
        for j in range(nl):
            cols = slice(j * LANES, (j + 1) * LANES)
            o_ref[0, r:r + 1, cols] = jnp.sum(acc[r * nl + j], axis=0, keepdims=True) + b_ref[0, :, cols]


def _adaln_mods(c, c_ctx, w_ada, b_ada):
    depth, d, n = w_ada.shape
    bn = MODS_COLS
    cb = jnp.broadcast_to(jnp.stack([c[0], c_ctx])[:, :, None], (2, d, LANES))
    return pl.pallas_call(
        _mods_body,
        grid=(depth, n // bn),
        in_specs=[pl.BlockSpec((2, d, LANES), lambda i, j: (0, 0, 0)),
                  pl.BlockSpec((1, d, bn), lambda i, j: (i, 0, j)),
                  pl.BlockSpec((1, 1, bn), lambda i, j: (i, 0, j))],
        out_specs=pl.BlockSpec((1, 2, bn), lambda i, j: (i, 0, j)),
        out_shape=jax.ShapeDtypeStruct((depth, 2, n), F32),
        scratch_shapes=[pltpu.VMEM((2, d, LANES), F32)],
        compiler_params=_cparams("arbitrary", "arbitrary"),
        name="adaln_mods",
    )(cb, w_ada, b_ada.reshape(depth, 1, n))


def _ffn_body(h_ref, vec_ref, wg_ref, wu_ref, wo_ref, *rest, res_w, cast):
    if cast:
        ci_ref, co_ref, o_ref, cib_ref, cob_ref, u_ref, acc_ref = rest
    else:
        o_ref, u_ref, acc_ref = rest
    j = pl.program_id(1)

    @pl.when((pl.program_id(0) == 0) & (j == 0))
    def _():
        acc_ref[...] = jnp.zeros_like(acc_ref)

    @pl.when(j == 0)
    def _():
        _store_modulated(h_ref, vec_ref, u_ref)

    u = u_ref[...]
    g = jnp.dot(u, wg_ref[...], preferred_element_type=F32)
    p = jnp.dot(u, wu_ref[...], preferred_element_type=F32)
    a = (g * jax.nn.sigmoid(g) * p).astype(BF16)
    acc_ref[...] += jnp.dot(a, wo_ref[...], preferred_element_type=F32)

    if cast:
        cib_ref[...] = ci_ref[...].astype(cib_ref.dtype)
        cob_ref[...] = co_ref[...].astype(cob_ref.dtype)

    @pl.when(j == pl.num_programs(1) - 1)
    def _():
        gate = (res_w * vec_ref[2:3, :]) * vec_ref[4:5, :]
        o_ref[...] = h_ref[...] + _rms(acc_ref[...], gate)
        acc_ref[...] = jnp.zeros_like(acc_ref)


def _cast_plan(n_rows, n_cols, steps):
    for n_col_blocks in range(1, steps + 1):
        if steps % n_col_blocks or n_cols % n_col_blocks or n_rows % (steps // n_col_blocks):
            continue
        br, bc = n_rows // (steps // n_col_blocks), n_cols // n_col_blocks
        if br % HALO == 0 and bc % LANES == 0:
            return br, bc, n_col_blocks
    raise ValueError("no tiling of the weight cast fits this grid")


def _ffn(h, vecs, w_in, w_out, res_w, cast_next=None):
    m, d = h.shape
    f = w_out.shape[0]
    bm = min(m, FFN_ROWS)
    bf = FFN_CHUNK
    nf = f // bf
    in_specs = [pl.BlockSpec((bm, d), lambda i, j: (i, 0)),
                pl.BlockSpec((SUBLANES, d), lambda i, j: (0, 0)),
                pl.BlockSpec((d, bf), lambda i, j: (0, j)),
                pl.BlockSpec((d, bf), lambda i, j: (0, j + nf)),
                pl.BlockSpec((bf, d), lambda i, j: (j, 0))]
    out_specs = [pl.BlockSpec((bm, d), lambda i, j: (i, 0))]
    out_shape = [jax.ShapeDtypeStruct((m, d), F32)]
    args = [h, vecs, w_in, w_in, w_out]
    if cast_next is not None:
        *srcs, layer, slot = cast_next
        for w in srcs:
            br, bc, ncb = _cast_plan(w.shape[2], w.shape[3], (m // bm) * nf)
            if ncb == nf:
                where = lambda i, j: (i, j)
            elif ncb == 1:
                where = lambda i, j: (i * nf + j, 0)
            else:
                where = functools.partial(lambda i, j, n: ((i * nf + j) // n, (i * nf + j) % n), n=ncb)
            in_specs.append(pl.BlockSpec((None, None, br, bc),
                                         functools.partial(lambda i, j, f: (layer, slot) + f(i, j), f=where)))
            out_specs.append(pl.BlockSpec((br, bc), where))
            out_shape.append(jax.ShapeDtypeStruct(w.shape[2:], BF16))
            args.append(w)
    outs = pl.pallas_call(
        functools.partial(_ffn_body, res_w=res_w, cast=cast_next is not None),
        grid=(m // bm, nf),
        in_specs=in_specs,
        out_specs=out_specs,
        out_shape=out_shape,
        scratch_shapes=[pltpu.VMEM((bm, d), BF16), pltpu.VMEM((bm, d), F32)],
        compiler_params=_cparams("arbitrary", "arbitrary"),
        name="ffn",
    )(*args)
    return outs if cast_next is not None else outs[0]


def _proj_in_body(h_ref, vec_ref, w_ref, b_ref, *rest, rope_blocks, q_blocks, q_scale):
    if rope_blocks:
        cos_ref, sa_ref, sb_ref, o_ref, u_ref = rest
    else:
        o_ref, u_ref = rest
    j = pl.program_id(1)

    @pl.when(j == 0)
    def _():
        _store_modulated(h_ref, vec_ref, u_ref)

    y = jnp.dot(u_ref[...], w_ref[...], preferred_element_type=F32) + b_ref[...]
    if not rope_blocks:
        o_ref[...] = y.astype(o_ref.dtype)
        return

    rot = jnp.where(j < q_blocks, q_scale, jnp.where(j < rope_blocks, 1.0, 0.0)).astype(F32)
    keep = jnp.where(j < rope_blocks, 0.0, 1.0).astype(F32)
    cos, sa, sb = cos_ref[...] * rot + keep, sa_ref[...] * rot, sb_ref[...] * rot
    for hd in range(y.shape[1] // HEAD_DIM):
        cols = slice(hd * HEAD_DIM, (hd + 1) * HEAD_DIM)
        xh = y[:, cols]
        r = (xh * cos + pltpu.roll(xh, HEAD_DIM - HEAD_DIM // 4, 1) * sa
             + pltpu.roll(xh, HEAD_DIM // 4, 1) * sb)
        o_ref[:, cols] = r.astype(o_ref.dtype)


def _proj_in(h, vecs, w, b, out_dtype, rope=None, q_blocks=0, q_scale=1.0):
    m, d = h.shape
    n = w.shape[1]
    bm = min(m, PROJ_ROWS)
    bn = PROJ_COLS
    in_specs = [pl.BlockSpec((bm, d), lambda i, j: (i, 0)),
                pl.BlockSpec((SUBLANES, d), lambda i, j: (0, 0)),
                pl.BlockSpec((d, bn), lambda i, j: (0, j)),
                pl.BlockSpec((1, bn), lambda i, j: (0, j))]
    args = [h, vecs, w, b.reshape(1, n)]
    rope_blocks = 0
    if rope is not None:
        rope_blocks = q_blocks + 1
        in_specs += [pl.BlockSpec((bm, HEAD_DIM), lambda i, j: (i, 0))] * 3
        args += list(rope)
    return pl.pallas_call(
        functools.partial(_proj_in_body, rope_blocks=rope_blocks, q_blocks=q_blocks, q_scale=q_scale),
        grid=(m // bm, n // bn),
        in_specs=in_specs,
        out_specs=pl.BlockSpec((bm, bn), lambda i, j: (i, j)),
        out_shape=jax.ShapeDtypeStruct((m, n), out_dtype),
        scratch_shapes=[pltpu.VMEM((bm, d), BF16)],
        compiler_params=_cparams("arbitrary", "arbitrary"),
        name="proj_in",
    )(*args)


def _proj_out_body(a_ref, w_ref, b_ref, h_ref, vec_ref, o_ref):
    half = a_ref.shape[0] // 2
    gate = vec_ref[2:3, :] * vec_ref[4:5, :]
    halves = [slice(0, half), slice(half, 2 * half)]
    ys = [jnp.dot(a_ref[rows, :], w_ref[...], preferred_element_type=F32) + b_ref[...] for rows in halves]
    for rows, y in zip(halves, ys):
        o_ref[rows, :] = h_ref[rows, :] + _rms(y, gate)


def _proj_out(a, w, b, h, vecs):
    m, d = h.shape
    k = a.shape[1]
    bm = min(m, FFN_ROWS)
    return pl.pallas_call(
        _proj_out_body,
        grid=(m // bm,),
        in_specs=[pl.BlockSpec((bm, k), lambda i: (i, 0)),
                  pl.BlockSpec((k, d), lambda i: (0, 0)),
                  pl.BlockSpec((1, d), lambda i: (0, 0)),
                  pl.BlockSpec((bm, d), lambda i: (i, 0)),
                  pl.BlockSpec((SUBLANES, d), lambda i: (0, 0))],
        out_specs=pl.BlockSpec((bm, d), lambda i: (i, 0)),
        out_shape=jax.ShapeDtypeStruct((m, d), F32),
        compiler_params=_cparams("arbitrary"),
        name="proj_out",
    )(a, w, b.reshape(1, d), h, vecs)


def _hy_in_body(h_ref, hp_ref, hn_ref, vec_ref, w0_ref, w1_ref, w2_ref, b0_ref, b1_ref, b2_ref, wsc_ref, bsc_ref,
                x0_out, xin_out, u_ref, uh_ref):
    i = pl.program_id(0)
    last = pl.num_programs(0) - 1

    @pl.when(pl.program_id(1) == 0)
    def _():
        _store_modulated(h_ref, vec_ref, u_ref)
        uh_ref[0:SUBLANES, :] = hp_ref[...]
        uh_ref[SUBLANES:, :] = hn_ref[...]
        _store_modulated(uh_ref, vec_ref, uh_ref)

    u, uh = u_ref[...], uh_ref[...].astype(BF16)
    thirds = ((w0_ref, b0_ref), (w1_ref, b1_ref), (w2_ref, b2_ref))
    zs = [jnp.dot(u, w[...], preferred_element_type=F32) + b[...] for w, b in thirds]
    zhs = [jnp.dot(uh, w[...], preferred_element_type=F32) + b[...] for w, b in thirds]

    def conv(part):
        z, zh = zs[part], zhs[part]
        bm = z.shape[0]
        prev_row = jnp.where(i > 0, zh[SUBLANES - 1:SUBLANES, :], 0.0)
        next_row = jnp.where(i < last, zh[SUBLANES:SUBLANES + 1, :], 0.0)
        rid = lax.broadcasted_iota(jnp.int32, z.shape, 0)
        up = jnp.where(rid == 0, prev_row, pltpu.roll(z, 1, 0))
        dn = jnp.where(rid == bm - 1, next_row, pltpu.roll(z, bm - 1, 0))
        return (up * wsc_ref[0, part:part + 1, :] + z * wsc_ref[1, part:part + 1, :]
                + dn * wsc_ref[2, part:part + 1, :] + bsc_ref[part:part + 1, :])

    x0_out[...] = conv(0).astype(x0_out.dtype)
    xin_out[...] = (conv(2) * conv(1)).astype(xin_out.dtype)


def _hy_in_gate(h, vecs, w_in, b_in, w_sc, b_sc):
    l, d = h.shape
    bm = min(l, PROJ_ROWS)
    bc = min(d, HYENA_IN_COLS)
    nc = d // bc
    rb = bm // SUBLANES
    nrow8 = l // SUBLANES
    wspecs = [pl.BlockSpec((d, bc), functools.partial(lambda i, j, p: (0, p * nc + j), p=part)) for part in range(3)]
    bspecs = [pl.BlockSpec((1, bc), functools.partial(lambda i, j, p: (0, p * nc + j), p=part)) for part in range(3)]
    out_spec = pl.BlockSpec((bm, bc), lambda i, j: (i, j))
    return pl.pallas_call(
        _hy_in_body,
        grid=(l // bm, nc),
        in_specs=[pl.BlockSpec((bm, d), lambda i, j: (i, 0)),
                  pl.BlockSpec((SUBLANES, d), lambda i, j: (jnp.maximum(i * rb - 1, 0), 0)),
                  pl.BlockSpec((SUBLANES, d), lambda i, j: (jnp.minimum((i + 1) * rb, nrow8 - 1), 0)),
                  pl.BlockSpec((SUBLANES, d), lambda i, j: (0, 0))] + wspecs + bspecs
                 + [pl.BlockSpec((3, 3, bc), lambda i, j: (0, 0, j)),
                    pl.BlockSpec((3, bc), lambda i, j: (0, j))],
        out_specs=[out_spec, out_spec],
        out_shape=[jax.ShapeDtypeStruct((l, d), BF16)] * 2,
        scratch_shapes=[pltpu.VMEM((bm, d), BF16), pltpu.VMEM((2 * SUBLANES, d), F32)],
        compiler_params=_cparams("arbitrary", "arbitrary"),
        name="hyena_in_gate",
    )(h, h, h, vecs, w_in, w_in, w_in, *([b_in.reshape(1, 3 * d)] * 3), w_sc.reshape(3, 3, d), b_sc.reshape(3, d))


def _filter_mlp_body(z_ref, aux_ref, auxr_ref, flip_ref, w1, b1, w2, b2, w3, b3, fr, tab_ref, tabr_ref):
    hp = lax.Precision.HIGHEST
    hid = w1.shape[1] // 2
    f = jnp.sin(fr[0:1, :] * (jnp.dot(z_ref[...], w1[...], precision=hp, preferred_element_type=F32) + b1[...]))
    f = jnp.sin(fr[1:2, :] * (jnp.dot(f, w2[...], precision=hp, preferred_element_type=F32) + b2[...]))
    f = jnp.sin(fr[2:3, :] * (jnp.dot(f, w3[...], precision=hp, preferred_element_type=F32) + b3[...]))
    low = lax.broadcasted_iota(jnp.int32, f.shape, 1) < hid
    tab_ref[0] = jnp.where(low, f, aux_ref[0])
    tab_ref[1] = jnp.where(low, pltpu.roll(f, hid, 1), aux_ref[1])
    fr_rows = jnp.dot(flip_ref[...], f, precision=hp, preferred_element_type=F32)
    tabr_ref[0] = jnp.where(low, pltpu.roll(fr_rows, hid, 1), auxr_ref[0])
    tabr_ref[1] = jnp.where(low, fr_rows, auxr_ref[1])


def _filter_table(l, w1, b1, w2, b2, w3, b3, freq):
    hid = w1.shape[1]
    assert 2 * hid == LANES, "two filter-MLP evaluations share one 128-lane row"
    t = jnp.linspace(0.0, 1.0, l, dtype=F32)[:, None]
    bands = jnp.linspace(1e-4, FILTER_BANDS - 1, FILTER_BANDS, dtype=F32)
    step = 2.0 * math.pi / l
    ang_hi = bands[None, None, :] * (step * LANES * jnp.arange(l // LANES, dtype=F32))[:, None, None]
    ang_lo = bands[None, None, :] * (step * jnp.arange(LANES, dtype=F32))[None, :, None]
    cos_e = (jnp.cos(ang_hi) * jnp.cos(ang_lo) - jnp.sin(ang_hi) * jnp.sin(ang_lo)).reshape(l, FILTER_BANDS)
    sin_e = (jnp.sin(ang_hi) * jnp.cos(ang_lo) + jnp.cos(ang_hi) * jnp.sin(ang_lo)).reshape(l, FILTER_BANDS)
    emb = jnp.concatenate([t, cos_e, -sin_e], axis=-1)
    emb = jnp.pad(emb, ((0, 0), (0, LANES - emb.shape[1])))
    w1p = jnp.pad(w1, ((0, LANES - w1.shape[0]), (0, 0)))
    rows = l // 2
    eye = jnp.eye(2, dtype=F32)
    emb_p = jnp.concatenate([emb[:rows], emb[rows:]], axis=1)

    def side_columns(tcol):
        return jnp.pad(tcol, ((0, 0), (hid, LANES - hid - 1))).reshape(2, rows, LANES)

    bm = min(rows, FILTER_ROWS)
    nb = rows // bm
    flip = jnp.eye(bm, dtype=F32)[::-1]

    def full(shape):
        return pl.BlockSpec(shape, lambda i: (0,) * len(shape))

    fwd_spec = pl.BlockSpec((2, bm, LANES), lambda i: (0, i, 0))
    rev_spec = pl.BlockSpec((2, bm, LANES), lambda i: (0, nb - 1 - i, 0))
    tab, tab_rev = pl.pallas_call(
        _filter_mlp_body,
        grid=(nb,),
        in_specs=[pl.BlockSpec((bm, 2 * LANES), lambda i: (i, 0)), fwd_spec, rev_spec, full((bm, bm)),
                  full((2 * LANES, LANES)), full((1, LANES)), full((LANES, LANES)), full((1, LANES)),
                  full((LANES, LANES)), full((1, LANES)), full((3, LANES))],
        out_specs=[fwd_spec, rev_spec],
        out_shape=[jax.ShapeDtypeStruct((2, rows, LANES), F32)] * 2,
        compiler_params=_cparams("arbitrary"),
        name="hyena_filter_mlp",
    )(emb_p, side_columns(t), side_columns(t[::-1]), flip,
      jnp.kron(eye, w1p), jnp.tile(b1, 2).reshape(1, LANES), jnp.kron(eye, w2),
      jnp.tile(b2, 2).reshape(1, LANES), jnp.kron(eye, w3), jnp.tile(b3, 2).reshape(1, LANES),
      jnp.tile(freq, (1, 2)))
    return jnp.concatenate([tab.reshape(l, LANES), jnp.zeros((1, LANES), F32), tab_rev.reshape(l, LANES)[:l - 1]],
                           axis=0)


def _decay_rates(d):
    return jnp.abs(jnp.linspace(math.log(DECAY_TARGET) / LONG_DECAY_PCT,
                                math.log(DECAY_TARGET) / SHORT_DECAY_PCT, d, dtype=F32))[None, :]


def _split_bf16(x):
    hi = x.astype(BF16)
    return hi, (x - hi.astype(F32)).astype(BF16)


def _stack_3pass(w):
    w_hi, w_lo = _split_bf16(w)
    return jnp.concatenate([w_hi, w_hi, w_lo], axis=0)


def _dot_3pass(f, w3):
    f_hi, f_lo = _split_bf16(f)
    return jnp.dot(jnp.concatenate([f_hi, f_lo, f_hi], axis=1), w3, preferred_element_type=F32)


def _filter_rows(ft, wa3, wb3, dl_ref):
    hid = wa3.shape[0] // 3
    half = ft.shape[0] // 2
    f, t = ft[:, :hid], ft[:, hid:hid + 1]
    y = jnp.concatenate([_dot_3pass(f[:half], wa3), _dot_3pass(f[half:], wb3)], axis=0)
    return y * jnp.exp(-t * dl_ref[...])


def _outer_stage(fa_ref, cols, o_ref):
    ys = [jnp.dot(fa_ref[...], xj, preferred_element_type=F32).astype(BF16) for xj in cols]
    o_ref[...] = pltpu.einshape("brd->rbd", jnp.stack(ys))


def _dft_a_body(x_ref, fa_ref, o_ref):
    xb = pltpu.einshape("abd->bad", x_ref[...])
    _outer_stage(fa_ref, [xb[j] for j in range(xb.shape[0])], o_ref)


def _dft_a_filter_body(ft_ref, wa_ref, wb_ref, dl_ref, fa_ref, o_ref):
    wa3, wb3 = _stack_3pass(wa_ref[...]), _stack_3pass(wb_ref[...])
    cols = [_filter_rows(ft_ref[j], wa3, wb3, dl_ref).astype(BF16) for j in range(ft_ref.shape[0])]
    _outer_stage(fa_ref, cols, o_ref)


def _dft_a(x3, fa, d_chunk):
    a, b, d = x3.shape
    rows = fa.shape[0]
    return pl.pallas_call(
        _dft_a_body,
        grid=(b // HALO, d // d_chunk),
        in_specs=[pl.BlockSpec((a, HALO, d_chunk), lambda i, j: (0, i, j)),
                  pl.BlockSpec((rows, a), lambda i, j: (0, 0))],
        out_specs=pl.BlockSpec((rows, HALO, d_chunk), lambda i, j: (0, i, j)),
        out_shape=jax.ShapeDtypeStruct((rows, b, d), BF16),
        compiler_params=_cparams("arbitrary", "arbitrary"),
        name="dft_outer",
    )(x3, fa)


def _dft_a_filter(ft3, w4, deltas, fa, d_chunk):
    b, n1, _ = ft3.shape
    hid, d2 = w4.shape
    d = d2 // 2
    rows = fa.shape[0]
    nd = d // d_chunk
    return pl.pallas_call(
        _dft_a_filter_body,
        grid=(b // HALO, nd),
        in_specs=[pl.BlockSpec((HALO, n1, LANES), lambda i, j: (i, 0, 0)),
                  pl.BlockSpec((hid, d_chunk), lambda i, j: (0, j)),
                  pl.BlockSpec((hid, d_chunk), lambda i, j: (0, j + nd)),
                  pl.BlockSpec((1, d_chunk), lambda i, j: (0, j)),
                  pl.BlockSpec((rows, n1), lambda i, j: (0, 0))],
        out_specs=pl.BlockSpec((rows, HALO, d_chunk), lambda i, j: (0, i, j)),
        out_shape=jax.ShapeDtypeStruct((rows, b, d), BF16),
        compiler_params=_cparams("arbitrary", "arbitrary"),
        name="dft_outer_filter",
    )(ft3, w4, w4, deltas, fa)


def _dft_inner_body(y_ref, yf_ref, gf_ref, o_ref):
    b, dc = y_ref.shape[2], y_ref.shape[3]
    for ci in range(y_ref.shape[1]):
        z = jnp.dot(gf_ref[ci], y_ref[:, ci].reshape(2 * b, dc), preferred_element_type=F32)
        hh = jnp.dot(gf_ref[ci], yf_ref[:, ci].reshape(2 * b, dc), preferred_element_type=F32)
        zr, zi, hr, hi = z[:b], z[b:], hh[:b], hh[b:]
        p = jnp.concatenate([zr * hr - zi * hi, zr * hi + zi * hr], axis=0).astype(BF16)
        cc = lax.dot_general(gf_ref[ci], p, (((0,), (0,)), ((), ())), preferred_element_type=F32)
        o_ref[:, ci] = cc.reshape(2, b, dc).astype(o_ref.dtype)


def _dft_inner(y4, yf4, gf):
    _, nc, b, d = y4.shape
    cblk = SUBLANES
    dc = min(d, DFT_INNER_COLS)
    yspec = pl.BlockSpec((2, cblk, b, dc), lambda c, j: (0, c, 0, j))
    return pl.pallas_call(
        _dft_inner_body,
        grid=(nc // cblk, d // dc),
        in_specs=[yspec, yspec, pl.BlockSpec((cblk, 2 * b, 2 * b), lambda c, j: (c, 0, 0))],
        out_specs=yspec,
        out_shape=jax.ShapeDtypeStruct(y4.shape, BF16),
        compiler_params=_cparams("arbitrary", "arbitrary"),
        name="dft_inner",
    )(y4, yf4, gf)


def _dft_c_body(c_ref, m_ref, xin_ref, x0_ref, skip_ref, o_ref):
    cb = pltpu.einshape("rbd->brd", c_ref[...])
    outs = [jnp.dot(m_ref[...], cb[j], preferred_element_type=F32).astype(BF16) for j in range(cb.shape[0])]
    conv = pltpu.einshape("bad->abd", jnp.stack(outs)).astype(F32)
    gated = x0_ref[...].astype(F32) * (conv + xin_ref[...].astype(F32) * skip_ref[...][None])
    o_ref[...] = gated.astype(o_ref.dtype)


def _dft_c(c3, mc, xin3, x03, skip, d_chunk):
    rows, b, d = c3.shape
    a = mc.shape[0]
    xspec = pl.BlockSpec((a, HALO, d_chunk), lambda i, j: (0, i, j))
    return pl.pallas_call(
        _dft_c_body,
        grid=(b // HALO, d // d_chunk),
        in_specs=[pl.BlockSpec((rows, HALO, d_chunk), lambda i, j: (0, i, j)),
                  pl.BlockSpec((a, rows), lambda i, j: (0, 0)),
                  xspec, xspec,
                  pl.BlockSpec((1, d_chunk), lambda i, j: (0, j))],
        out_specs=xspec,
        out_shape=jax.ShapeDtypeStruct((a, b, d), BF16),
        compiler_params=_cparams("arbitrary", "arbitrary"),
        name="dft_outer_inv",
    )(c3, mc, xin3, x03, skip)


def _cis(num, period):
    ang = (2.0 * math.pi / period) * num.astype(F32)
    return jnp.cos(ang), jnp.sin(ang)


def _long_conv_gated(xin, x0, ftab, w4, skip):
    l, d = xin.shape
    b = DFT_INNER
    a = l // b
    n1, n = 2 * a, 2 * l
    nc = n1 // 2 + SUBLANES
    dc = min(d, DFT_OUTER_COLS)
    ci = jnp.arange(nc, dtype=jnp.int32)
    ai = jnp.arange(n1, dtype=jnp.int32)
    live = (ci <= n1 // 2).astype(F32)[:, None]
    cr, sr = _cis((ci[:, None] * ai[None, :]) % n1, n1)
    cr, sr = cr * live, sr * live
    fa_full = jnp.concatenate([cr, -sr], axis=0).astype(BF16)
    fa = fa_full[:, :a]
    fold = jnp.where((ci == 0) | (ci == n1 // 2), 1.0, 2.0)[None, :] / n
    mc = jnp.concatenate([cr[:, :a].T * fold, -sr[:, :a].T * fold], axis=1).astype(BF16)
    ei = jnp.arange(b, dtype=jnp.int32)
    pr, ps = _cis((ci[:, None] * ei[None, :]) % n, n)
    qr, qs = _cis((ei[:, None] * ei[None, :]) % b, b)
    tr = pr[:, None, :] * qr[None] - ps[:, None, :] * qs[None]
    ti = -(pr[:, None, :] * qs[None] + ps[:, None, :] * qr[None])
    gf = jnp.concatenate([jnp.concatenate([tr, -ti], axis=2),
                          jnp.concatenate([ti, tr], axis=2)], axis=1).astype(BF16)

    ft3 = jnp.swapaxes(ftab.reshape(n1, b, LANES), 0, 1)
    hf = _dft_a_filter(ft3, w4, _decay_rates(d), fa_full, dc)
    yx = _dft_a(xin.reshape(a, b, d), fa, dc)
    c4 = _dft_inner(yx.reshape(2, nc, b, d), hf.reshape(2, nc, b, d), gf)
    y3 = _dft_c(c4.reshape(2 * nc, b, d), mc, xin.reshape(a, b, d), x0.reshape(a, b, d), skip.reshape(1, d), dc)
    return y3.reshape(l, d)


def _ctx_conv_body(x_ref, x0_ref, ft_ref, wa_ref, wb_ref, dl_ref, skip_ref, ff_ref, fh_ref, mi_ref, o_ref):
    x = x_ref[...].astype(F32)
    n = ft_ref.shape[0]
    filt = _filter_rows(ft_ref[...], _stack_3pass(wa_ref[...]), _stack_3pass(wb_ref[...]), dl_ref)
    xs = jnp.dot(ff_ref[...], x.astype(BF16), preferred_element_type=F32)
    hs = jnp.dot(fh_ref[...], filt.astype(BF16), preferred_element_type=F32)
    xr, xi, hr, hi = xs[:n], xs[n:], hs[:n], hs[n:]
    p = jnp.concatenate([xr * hr - xi * hi, xr * hi + xi * hr], axis=0).astype(BF16)
    conv = jnp.dot(mi_ref[...], p, preferred_element_type=F32)
    o_ref[...] = (x0_ref[...].astype(F32) * (conv + x * skip_ref[...])).astype(o_ref.dtype)


def _short_seq_conv_gated(xin, x0, ftab, w4, skip):
    l, d = xin.shape
    n = 2 * l
    hid = w4.shape[0]
    ni = jnp.arange(n, dtype=jnp.int32)
    cr, sr = _cis((ni[:, None] * ni[None, :]) % n, n)
    fh = jnp.concatenate([cr, -sr], axis=0).astype(BF16)
    ff = fh[:, :l]
    mi = (jnp.concatenate([cr[:l], -sr[:l]], axis=1) * (1.0 / n)).astype(BF16)
    dc = min(d, DFT_OUTER_COLS)
    nd = d // dc
    return pl.pallas_call(
        _ctx_conv_body,
        grid=(nd,),
        in_specs=[pl.BlockSpec((l, dc), lambda j: (0, j)),
                  pl.BlockSpec((l, dc), lambda j: (0, j)),
                  pl.BlockSpec((n, LANES), lambda j: (0, 0)),
                  pl.BlockSpec((hid, dc), lambda j: (0, j)),
                  pl.BlockSpec((hid, dc), lambda j: (0, j + nd)),
                  pl.BlockSpec((1, dc), lambda j: (0, j)),
                  pl.BlockSpec((1, dc), lambda j: (0, j)),
                  pl.BlockSpec((2 * n, l), lambda j: (0, 0)),
                  pl.BlockSpec((2 * n, n), lambda j: (0, 0)),
                  pl.BlockSpec((l, 2 * n), lambda j: (0, 0))],
        out_specs=pl.BlockSpec((l, dc), lambda j: (0, j)),
        out_shape=jax.ShapeDtypeStruct((l, d), BF16),
        compiler_params=_cparams("arbitrary"),
        name="ctx_conv",
    )(xin, x0, ftab, w4, w4, _decay_rates(d), skip.reshape(1, d), ff, fh, mi)


def _hyena_mixer(h, vecs, p, long_seq):
    w_in, b_in, w_sc, b_sc, f_w1, f_b1, f_w2, f_b2, f_w3, f_b3, f_w4, f_freq, skip, w_out, b_out = p
    l = h.shape[0]
    x0, xin = _hy_in_gate(h, vecs, w_in, b_in, w_sc, b_sc)
    ftab = _filter_table(l, f_w1, f_b1, f_w2, f_b2, f_w3, f_b3, f_freq)
    conv = _long_conv_gated if long_seq else _short_seq_conv_gated
    y = conv(xin, x0, ftab, f_w4, skip)
    return _proj_out(y, w_out, b_out, h, vecs)


def _attn_body(sink_ref, q_ref, kp_ref, kc_ref, kn_ref, vp_ref, vc_ref, vn_ref, kx_ref, vx_ref, bias_ref, o_ref,
               *, group):
    blk = q_ref.shape[0]
    rid = lax.broadcasted_iota(jnp.int32, (group * blk, 1), 0)
    for kh in range(N_KV_HEADS):
        hs = slice(kh * HEAD_DIM, (kh + 1) * HEAD_DIM)
        heads = [kh * group + g for g in range(group)]
        q = jnp.concatenate([q_ref[:, hd * HEAD_DIM:(hd + 1) * HEAD_DIM] for hd in heads], axis=0)
        keys = jnp.concatenate([kp_ref[:, hs], kc_ref[:, hs], kn_ref[:, hs], kx_ref[:, hs]], axis=0)
        vals = jnp.concatenate([vp_ref[:, hs], vc_ref[:, hs], vn_ref[:, hs], vx_ref[:, hs]], axis=0)
        s = lax.dot_general(q, keys, (((1,), (1,)), ((), ())), preferred_element_type=F32)
        pieces = [s[:, :blk] + bias_ref[0, :, :blk], s[:, blk:2 * blk], s[:, 2 * blk:3 * blk] + bias_ref[0, :, blk:]]
        pieces += [s[:, c0:c0 + blk] for c0 in range(3 * blk, s.shape[1], blk)]
        sink = jnp.zeros((group * blk, 1), F32)
        for g, hd in enumerate(heads):
            sink = jnp.where((rid >= g * blk) & (rid < (g + 1) * blk), sink_ref[hd] * LOG2_E, sink)
        top = functools.reduce(jnp.maximum, pieces)
        mx = jnp.maximum(jnp.max(top, axis=-1, keepdims=True), sink)
        probs = [jnp.exp2(pc - mx) for pc in pieces]
        denom = jnp.sum(functools.reduce(jnp.add, probs), axis=-1, keepdims=True) + jnp.exp2(sink - mx)
        pr = jnp.concatenate([pp.astype(BF16) for pp in probs], axis=1)
        o = jnp.dot(pr, vals, preferred_element_type=F32) / denom
        for g, hd in enumerate(heads):
            o_ref[:, hd * HEAD_DIM:(hd + 1) * HEAD_DIM] = o[g * blk:(g + 1) * blk].astype(o_ref.dtype)


def _attention(qkv, kvc, sink, d):
    l = qkv.shape[0]
    c = kvc.shape[0]
    blk = ATTN_BLOCK
    nb = l // blk
    group = d // HEAD_DIM // N_KV_HEADS
    kvw = N_KV_HEADS * HEAD_DIM
    kcol = d // kvw
    qi = jnp.arange(group * blk, dtype=jnp.int32)[:, None] % blk
    ki = jnp.arange(blk, dtype=jnp.int32)[None, :]
    prev_ok, next_ok, never = ki >= qi, ki <= qi, jnp.zeros((group * blk, blk), bool)
    variants = [(never, next_ok), (prev_ok, next_ok), (prev_ok, never)]
    bias = jnp.stack([jnp.where(jnp.concatenate(v, axis=1), 0.0, MASK_BIAS).astype(F32) for v in variants])

    def kv_spec(col, shift):
        return pl.BlockSpec((blk, kvw), lambda n: (jnp.clip(n + shift, 0, nb - 1), col))

    return pl.pallas_call(
        functools.partial(_attn_body, group=group),
        grid=(nb,),
        in_specs=[pl.BlockSpec(memory_space=pltpu.SMEM),
                  pl.BlockSpec((blk, d), lambda n: (n, 0)),
                  kv_spec(kcol, -1), kv_spec(kcol, 0), kv_spec(kcol, 1),
                  kv_spec(kcol + 1, -1), kv_spec(kcol + 1, 0), kv_spec(kcol + 1, 1),
                  pl.BlockSpec((c, kvw), lambda n: (0, 0)),
                  pl.BlockSpec((c, kvw), lambda n: (0, 1)),
                  pl.BlockSpec((1, group * blk, 2 * blk),
                               lambda n: (jnp.where(n == 0, 0, jnp.where(n == nb - 1, 2, 1)), 0, 0))],
        out_specs=pl.BlockSpec((blk, d), lambda n: (n, 0)),
        out_shape=jax.ShapeDtypeStruct((l, d), BF16),
        compiler_params=_cparams("arbitrary"),
        name="window_attn",
    )(sink, qkv, qkv, qkv, qkv, qkv, qkv, qkv, kvc, kvc, bias)


def _rope_tables(l):
    rows = l // GRID_W
    pos_row = jnp.broadcast_to(jnp.arange(rows)[:, None], (rows, GRID_W)).reshape(-1).astype(F32)
    pos_col = jnp.broadcast_to(jnp.arange(GRID_W)[None, :], (rows, GRID_W)).reshape(-1).astype(F32)
    pairs = HEAD_DIM // 4
    inv_freq = ROPE_THETA ** (-jnp.arange(pairs, dtype=F32) / pairs)
    ang_row = pos_row[:, None] * inv_freq[None, :]
    ang_col = pos_col[:, None] * inv_freq[None, :]
    zeros = jnp.zeros_like(ang_row)
    cos = jnp.concatenate([jnp.cos(ang_row)] * 2 + [jnp.cos(ang_col)] * 2, axis=-1)
    sin_a = jnp.concatenate([-jnp.sin(ang_row), zeros, -jnp.sin(ang_col), zeros], axis=-1)
    sin_b = jnp.concatenate([zeros, jnp.sin(ang_row), zeros, jnp.sin(ang_col)], axis=-1)
    return cos, sin_a, sin_b


def _pool_body(hc_ref, hp_ref, hn_ref, vec_ref, w_ref, b_ref, sc_ref, o_ref, y_ref, *, seq_len):
    i = pl.program_id(0)
    last = pl.num_programs(0) - 1
    bm, d = hc_ref.shape
    gw = d // len(POOL_SIZES)
    h = hc_ref[...]
    u = _modulated(h, vec_ref)
    up = jnp.where(i > 0, _modulated(hp_ref[...], vec_ref), 0.0)
    un = jnp.where(i < last, _modulated(hn_ref[...], vec_ref), 0.0)
    ext_rows = bm + 2 * SUBLANES
    t = i * bm + lax.broadcasted_iota(jnp.int32, (bm, 1), 0)
    for g, size in enumerate(POOL_SIZES):
        cols = slice(g * gw, (g + 1) * gw)
        ext = jnp.concatenate([up[:, cols], u[:, cols], un[:, cols]], axis=0)
        acc, span = ext, 1
        while span < size:
            acc = acc + pltpu.roll(acc, ext_rows - span, 0)
            span *= 2
        start = SUBLANES - size // 2
        win = pltpu.roll(acc, ext_rows - start, 0)[:bm] if start else acc[:bm]
        lo = jnp.clip(t - size // 2, 0, seq_len)
        hi = jnp.clip(t - size // 2 + size, 0, seq_len)
        part = win / (hi - lo).astype(F32) - u[:, cols]
        yg = jnp.dot(part.astype(BF16), w_ref[g], preferred_element_type=F32)
        y_ref[:, cols] = (yg + b_ref[:, cols]) * sc_ref[:, cols]
    o_ref[...] = h + vec_ref[2:3, :] * _rms(y_ref[...], vec_ref[4:5, :])


def _pool_mixer(h, vecs, w, b, scale):
    l, d = h.shape
    bm = min(l, SMALL_ROWS)
    rb = bm // SUBLANES
    nrow8 = l // SUBLANES
    ng, gw = w.shape[0], w.shape[1]
    return pl.pallas_call(
        functools.partial(_pool_body, seq_len=l),
        grid=(l // bm,),
        in_specs=[pl.BlockSpec((bm, d), lambda i: (i, 0)),
                  pl.BlockSpec((SUBLANES, d), lambda i: (jnp.maximum(i * rb - 1, 0), 0)),
                  pl.BlockSpec((SUBLANES, d), lambda i: (jnp.minimum((i + 1) * rb, nrow8 - 1), 0)),
                  pl.BlockSpec((SUBLANES, d), lambda i: (0, 0)),
                  pl.BlockSpec((ng, gw, gw), lambda i: (0, 0, 0)),
                  pl.BlockSpec((1, d), lambda i: (0, 0)),
                  pl.BlockSpec((1, d), lambda i: (0, 0))],
        out_specs=pl.BlockSpec((bm, d), lambda i: (i, 0)),
        out_shape=jax.ShapeDtypeStruct((l, d), F32),
        scratch_shapes=[pltpu.VMEM((bm, d), F32)],
        compiler_params=_cparams("arbitrary"),
        name="pool_mixer",
    )(h, h, h, vecs, w.astype(BF16), b.reshape(1, d), scale.reshape(1, d))


def _sub_vecs(mod, k, g_pre, g_post):
    rows = [mod[3 * k], mod[3 * k + 1], mod[3 * k + 2], g_pre, g_post]
    return jnp.stack(rows + [jnp.zeros_like(g_pre)] * (SUBLANES - len(rows)))


def kernel(x, c, ctx, c_ctx, w_ada, b_ada, norm_pre, norm_post, w_ffn_in, w_ffn_out, hy_w_in, hy_b_in, hy_w_sc, hy_b_sc, hy_f_w1, hy_f_b1, hy_f_w2, hy_f_b2, hy_f_w3, hy_f_b3, hy_f_w4, hy_f_freq, hy_skip, hy_w_out, hy_b_out, at_w_qkv, at_b_qkv, at_sink, at_w_o, at_b_o, pl_w, pl_b, pl_scale):
    bsz, l, d = x.shape
    assert bsz == 1, "kernel handles a single batch element"
    assert d % DFT_INNER_COLS == 0 and l % PROJ_ROWS == 0 and ctx.shape[1] % SMALL_ROWS == 0, \
        "channel and token counts must be multiples of the block sizes"
    assert w_ffn_out.shape[2] % FFN_CHUNK == 0, "FFN width must be a multiple of the hidden chunk"
    depth = w_ada.shape[0]
    n_mixers = 3
    attn_layers = [i for i in range(depth) if i % n_mixers == 1]
    last_ctx_layer = attn_layers[-1] if attn_layers else -1

    mods = _adaln_mods(c, c_ctx, w_ada, b_ada).reshape(depth, 2, N_MOD, d)
    rope = _rope_tables(l)
    w_next = [w_ffn_in[0, 0].astype(BF16), w_ffn_out[0, 0].astype(BF16)]
    h, hc = x[0], ctx[0]
    for i in range(depth):
        kind, j = i % n_mixers, i // n_mixers
        ctx_live = i <= last_ctx_layer
        ctx_out = i < last_ctx_layer
        vec = [_sub_vecs(mods[i, 0], k, norm_pre[i, k], norm_post[i, k]) for k in range(3)]
        vec_c = [_sub_vecs(mods[i, 1], k, norm_pre[i, k], norm_post[i, k]) for k in range(3)]

        w_now = w_next
        h, *w_next = _ffn(h, vec[0], *w_now, FFN_RES, cast_next=(w_ffn_in, w_ffn_out, i, 1))
        if ctx_live:
            hc = _ffn(hc, vec_c[0], *w_now, FFN_RES)

        if kind == 0:
            hp = (hy_w_in[j].astype(BF16), hy_b_in[j], hy_w_sc[j], hy_b_sc[j], hy_f_w1[j], hy_f_b1[j],
                  hy_f_w2[j], hy_f_b2[j], hy_f_w3[j], hy_f_b3[j], hy_f_w4[j], hy_f_freq[j], hy_skip[j],
                  hy_w_out[j].astype(BF16), hy_b_out[j])
            h = _hyena_mixer(h, vec[1], hp, long_seq=True)
            if ctx_out:
                hc = _hyena_mixer(hc, vec_c[1], hp, long_seq=False)
        elif kind == 1:
            w_qkv = at_w_qkv[j].astype(BF16)
            qkv = _proj_in(h, vec[1], w_qkv, at_b_qkv[j], BF16, rope=rope,
                           q_blocks=d // PROJ_COLS, q_scale=LOG2_E * HEAD_DIM ** -0.5)
            kvc = _proj_in(hc, vec_c[1], w_qkv[:, d:], at_b_qkv[j][d:], BF16)
            o = _attention(qkv, kvc, at_sink[j], d)
            h = _proj_out(o, at_w_o[j].astype(BF16), at_b_o[j], h, vec[1])
            assert not ctx_out, "context-query attention path is not needed for this depth"
        else:
            h = _pool_mixer(h, vec[1], pl_w[j], pl_b[j], pl_scale[j])
            assert not ctx_out, "context pooling path is not needed for this depth"

        w_now = w_next
        if i + 1 < depth:
            h, *w_next = _ffn(h, vec[2], *w_now, FFN_RES, cast_next=(w_ffn_in, w_ffn_out, i + 1, 0))
        else:
            h = _ffn(h, vec[2], *w_now, FFN_RES)
        if ctx_out:
            hc = _ffn(hc, vec_c[2], *w_now, FFN_RES)
    return h[None]
```

```python
import functools
import math

import jax
import jax.numpy as jnp
from jax import lax
from jax.experimental import pallas as pl
from jax.experimental.pallas import tpu as pltpu

F32 = jnp.float32
BF16 = jnp.bfloat16

GRID_W = 64
N_MOD = 9
NORM_EPS = 1e-6
FFN_RES = 0.5
FILTER_BANDS = 16
FILTER_EMB = 1 + 2 * FILTER_BANDS
DECAY_TARGET = 1e-2
SHORT_DECAY_PCT = 0.3
LONG_DECAY_PCT = 1.5
HEAD_DIM = 128
N_KV_HEADS = 4
WINDOW = 128
ATTN_BLOCK = 128
ROPE_THETA = 10000.0
POOL_SIZES = (2, 4, 8, 16)

VMEM_LIMIT_BYTES = 56 * 1024 * 1024
SUBLANES = 8
LANES = 128
DFT_INNER = 128
HALO = 16
FFN_CHUNK = 512
FFN_ROWS = 512
PROJ_ROWS = 1024
PROJ_COLS = 512
HYENA_IN_COLS = 256
DFT_OUTER_COLS = 512
DFT_INNER_COLS = 1024
SMALL_ROWS = 256
MODS_COLS = 1024
FILTER_ROWS = 512
MASK_BIAS = -1e30
LOG2_E = math.log2(math.e)


def _cparams(*sem):
    return pltpu.CompilerParams(dimension_semantics=sem, vmem_limit_bytes=VMEM_LIMIT_BYTES)


def _rms(x, g):
    ms = jnp.mean(x * x, axis=-1, keepdims=True)
    return x * lax.rsqrt(ms + NORM_EPS) * g


def _modulated(h, vec_ref):
    return _rms(h, vec_ref[3:4, :]) * (1.0 + vec_ref[1:2, :]) + vec_ref[0:1, :]


def _store_modulated(h_ref, vec_ref, u_ref):
    gain = vec_ref[3:4, :] * (1.0 + vec_ref[1:2, :])
    u_ref[...] = (_rms(h_ref[...], gain) + vec_ref[0:1, :]).astype(u_ref.dtype)


def _mods_body(cb_ref, w_ref, b_ref, o_ref, s_ref):
    k_dim, bn = w_ref.shape[1], w_ref.shape[2]
    nl = bn // LANES

    @pl.when((pl.program_id(0) == 0) & (pl.program_id(1) == 0))
    def _():
        cv = cb_ref[...]
        s_ref[...] = cv * jax.nn.sigmoid(cv)

    def step(kg, acc):
        rows = pl.ds(pl.multiple_of(kg * SUBLANES, SUBLANES), SUBLANES)
        s = [s_ref[r, rows, :] for r in range(2)]
        new = list(acc)
        for j in range(nl):
            wv = w_ref[0, rows, j * LANES:(j + 1) * LANES]
            for r in range(2):
                new[r * nl + j] = acc[r * nl + j] + wv * s[r]
        return tuple(new)

    init = tuple(jnp.zeros((SUBLANES, LANES), F32) for _ in range(2 * nl))
    acc = lax.fori_loop(0, k_dim // SUBLANES, step, init, unroll=4)
    for r in range(2):
        for j in range(nl):
            cols = slice(j * LANES, (j + 1) * LANES)
            o_ref[0, r:r + 1, cols] = jnp.sum(acc[r * nl + j], axis=0, keepdims=True) + b_ref[0, :, cols]


def _adaln_mods(c, c_ctx, w_ada, b_ada):
    depth, d, n = w_ada.shape
    bn = MODS_COLS
    cb = jnp.broadcast_to(jnp.stack([c[0], c_ctx])[:, :, None], (2, d, LANES))
    return pl.pallas_call(
        _mods_body,
        grid=(depth, n // bn),
        in_specs=[pl.BlockSpec((2, d, LANES), lambda i, j: (0, 0, 0)),
                  pl.BlockSpec((1, d, bn), lambda i, j: (i, 0, j)),
                  pl.BlockSpec((1, 1, bn), lambda i, j: (i, 0, j))],
        out_specs=pl.BlockSpec((1, 2, bn), lambda i, j: (i, 0, j)),
        out_shape=jax.ShapeDtypeStruct((depth, 2, n), F32),
        scratch_shapes=[pltpu.VMEM((2, d, LANES), F32)],
        compiler_params=_cparams("arbitrary", "arbitrary"),
        name="adaln_mods",
    )(cb, w_ada, b_ada.reshape(depth, 1, n))


def _ffn_body(h_ref, vec_ref, wg_ref, wu_ref, wo_ref, *rest, res_w, cast):
    if cast:
        ci_ref, co_ref, o_ref, cib_ref, cob_ref, u_ref, acc_ref = rest
    else:
        o_ref, u_ref, acc_ref = rest
    j = pl.program_id(1)

    @pl.when((pl.program_id(0) == 0) & (j == 0))
    def _():
        acc_ref[...] = jnp.zeros_like(acc_ref)

    @pl.when(j == 0)
    def _():
        _store_modulated(h_ref, vec_ref, u_ref)

    u = u_ref[...]
    g = jnp.dot(u, wg_ref[...], preferred_element_type=F32)
    p = jnp.dot(u, wu_ref[...], preferred_element_type=F32)
    a = (g * jax.nn.sigmoid(g) * p).astype(BF16)
    acc_ref[...] += jnp.dot(a, wo_ref[...], preferred_element_type=F32)

    if cast:
        cib_ref[...] = ci_ref[...].astype(cib_ref.dtype)
        cob_ref[...] = co_ref[...].astype(cob_ref.dtype)

    @pl.when(j == pl.num_programs(1) - 1)
    def _():
        gate = (res_w * vec_ref[2:3, :]) * vec_ref[4:5, :]
        o_ref[...] = h_ref[...] + _rms(acc_ref[...], gate)
        acc_ref[...] = jnp.zeros_like(acc_ref)


def _cast_plan(n_rows, n_cols, steps):
    for n_col_blocks in range(1, steps + 1):
        if steps % n_col_blocks or n_cols % n_col_blocks or n_rows % (steps // n_col_blocks):
            continue
        br, bc = n_rows // (steps // n_col_blocks), n_cols // n_col_blocks
        if br % HALO == 0 and bc % LANES == 0:
            return br, bc, n_col_blocks
    raise ValueError("no tiling of the weight cast fits this grid")


def _ffn(h, vecs, w_in, w_out, res_w, cast_next=None):
    m, d = h.shape
    f = w_out.shape[0]
    bm = min(m, FFN_ROWS)
    bf = FFN_CHUNK
    nf = f // bf
    in_specs = [pl.BlockSpec((bm, d), lambda i, j: (i, 0)),
                pl.BlockSpec((SUBLANES, d), lambda i, j: (0, 0)),
                pl.BlockSpec((d, bf), lambda i, j: (0, j)),
                pl.BlockSpec((d, bf), lambda i, j: (0, j + nf)),
                pl.BlockSpec((bf, d), lambda i, j: (j, 0))]
    out_specs = [pl.BlockSpec((bm, d), lambda i, j: (i, 0))]
    out_shape = [jax.ShapeDtypeStruct((m, d), F32)]
    args = [h, vecs, w_in, w_in, w_out]
    if cast_next is not None:
        *srcs, layer, slot = cast_next
        for w in srcs:
            br, bc, ncb = _cast_plan(w.shape[2], w.shape[3], (m // bm) * nf)
            if ncb == nf:
                where = lambda i, j: (i, j)
            elif ncb == 1:
                where = lambda i, j: (i * nf + j, 0)
            else:
                where = functools.partial(lambda i, j, n: ((i * nf + j) // n, (i * nf + j) % n), n=ncb)
            in_specs.append(pl.BlockSpec((None, None, br, bc),
                                         functools.partial(lambda i, j, f: (layer, slot) + f(i, j), f=where)))
            out_specs.append(pl.BlockSpec((br, bc), where))
            out_shape.append(jax.ShapeDtypeStruct(w.shape[2:], BF16))
            args.append(w)
    outs = pl.pallas_call(
        functools.partial(_ffn_body, res_w=res_w, cast=cast_next is not None),
        grid=(m // bm, nf),
        in_specs=in_specs,
        out_specs=out_specs,
        out_shape=out_shape,
        scratch_shapes=[pltpu.VMEM((bm, d), BF16), pltpu.VMEM((bm, d), F32)],
        compiler_params=_cparams("arbitrary", "arbitrary"),
        name="ffn",
    )(*args)
    return outs if cast_next is not None else outs[0]


def _proj_in_body(h_ref, vec_ref, w_ref, b_ref, *rest, rope_blocks, q_blocks, q_scale):
    if rope_blocks:
        cos_ref, sa_ref, sb_ref, o_ref, u_ref = rest
    else:
        o_ref, u_ref = rest
    j = pl.program_id(1)

    @pl.when(j == 0)
    def _():
        _store_modulated(h_ref, vec_ref, u_ref)

    y = jnp.dot(u_ref[...], w_ref[...], preferred_element_type=F32) + b_ref[...]
    if not rope_blocks:
        o_ref[...] = y.astype(o_ref.dtype)
        return

    rot = jnp.where(j < q_blocks, q_scale, jnp.where(j < rope_blocks, 1.0, 0.0)).astype(F32)
    keep = jnp.where(j < rope_blocks, 0.0, 1.0).astype(F32)
    cos, sa, sb = cos_ref[...] * rot + keep, sa_ref[...] * rot, sb_ref[...] * rot
    for hd in range(y.shape[1] // HEAD_DIM):
        cols = slice(hd * HEAD_DIM, (hd + 1) * HEAD_DIM)
        xh = y[:, cols]
        r = (xh * cos + pltpu.roll(xh, HEAD_DIM - HEAD_DIM // 4, 1) * sa
             + pltpu.roll(xh, HEAD_DIM // 4, 1) * sb)
        o_ref[:, cols] = r.astype(o_ref.dtype)


def _proj_in(h, vecs, w, b, out_dtype, rope=None, q_blocks=0, q_scale=1.0):
    m, d = h.shape
    n = w.shape[1]
    bm = min(m, PROJ_ROWS)
    bn = PROJ_COLS
    in_specs = [pl.BlockSpec((bm, d), lambda i, j: (i, 0)),
                pl.BlockSpec((SUBLANES, d), lambda i, j: (0, 0)),
                pl.BlockSpec((d, bn), lambda i, j: (0, j)),
                pl.BlockSpec((1, bn), lambda i, j: (0, j))]
    args = [h, vecs, w, b.reshape(1, n)]
    rope_blocks = 0
    if rope is not None:
        rope_blocks = q_blocks + 1
        in_specs += [pl.BlockSpec((bm, HEAD_DIM), lambda i, j: (i, 0))] * 3
        args += list(rope)
    return pl.pallas_call(
        functools.partial(_proj_in_body, rope_blocks=rope_blocks, q_blocks=q_blocks, q_scale=q_scale),
        grid=(m // bm, n // bn),
        in_specs=in_specs,
        out_specs=pl.BlockSpec((bm, bn), lambda i, j: (i, j)),
        out_shape=jax.ShapeDtypeStruct((m, n), out_dtype),
        scratch_shapes=[pltpu.VMEM((bm, d), BF16)],
        compiler_params=_cparams("arbitrary", "arbitrary"),
        name="proj_in",
    )(*args)


def _proj_out_body(a_ref, w_ref, b_ref, h_ref, vec_ref, o_ref):
    half = a_ref.shape[0] // 2
    gate = vec_ref[2:3, :] * vec_ref[4:5, :]
    halves = [slice(0, half), slice(half, 2 * half)]
    ys = [jnp.dot(a_ref[rows, :], w_ref[...], preferred_element_type=F32) + b_ref[...] for rows in halves]
    for rows, y in zip(halves, ys):
        o_ref[rows, :] = h_ref[rows, :] + _rms(y, gate)


def _proj_out(a, w, b, h, vecs):
    m, d = h.shape
    k = a.shape[1]
    bm = min(m, FFN_ROWS)
    return pl.pallas_call(
        _proj_out_body,
        grid=(m // bm,),
        in_specs=[pl.BlockSpec((bm, k), lambda i: (i, 0)),
                  pl.BlockSpec((k, d), lambda i: (0, 0)),
                  pl.BlockSpec((1, d), lambda i: (0, 0)),
                  pl.BlockSpec((bm, d), lambda i: (i, 0)),
                  pl.BlockSpec((SUBLANES, d), lambda i: (0, 0))],
        out_specs=pl.BlockSpec((bm, d), lambda i: (i, 0)),
        out_shape=jax.ShapeDtypeStruct((m, d), F32),
        compiler_params=_cparams("arbitrary"),
        name="proj_out",
    )(a, w, b.reshape(1, d), h, vecs)


def _hy_in_body(h_ref, hp_ref, hn_ref, vec_ref, w0_ref, w1_ref, w2_ref, b0_ref, b1_ref, b2_ref, wsc_ref, bsc_ref,
                x0_out, xin_out, u_ref, uh_ref):
    i = pl.program_id(0)
    last = pl.num_programs(0) - 1

    @pl.when(pl.program_id(1) == 0)
    def _():
        _store_modulated(h_ref, vec_ref, u_ref)
        uh_ref[0:SUBLANES, :] = hp_ref[...]
        uh_ref[SUBLANES:, :] = hn_ref[...]
        _store_modulated(uh_ref, vec_ref, uh_ref)

    u, uh = u_ref[...], uh_ref[...].astype(BF16)
    thirds = ((w0_ref, b0_ref), (w1_ref, b1_ref), (w2_ref, b2_ref))
    zs = [jnp.dot(u, w[...], preferred_element_type=F32) + b[...] for w, b in thirds]
    zhs = [jnp.dot(uh, w[...], preferred_element_type=F32) + b[...] for w, b in thirds]

    def conv(part):
        z, zh = zs[part], zhs[part]
        bm = z.shape[0]
        prev_row = jnp.where(i > 0, zh[SUBLANES - 1:SUBLANES, :], 0.0)
        next_row = jnp.where(i < last, zh[SUBLANES:SUBLANES + 1, :], 0.0)
        rid = lax.broadcasted_iota(jnp.int32, z.shape, 0)
        up = jnp.where(rid == 0, prev_row, pltpu.roll(z, 1, 0))
        dn = jnp.where(rid == bm - 1, next_row, pltpu.roll(z, bm - 1, 0))
        return (up * wsc_ref[0, part:part + 1, :] + z * wsc_ref[1, part:part + 1, :]
                + dn * wsc_ref[2, part:part + 1, :] + bsc_ref[part:part + 1, :])

    x0_out[...] = conv(0).astype(x0_out.dtype)
    xin_out[...] = (conv(2) * conv(1)).astype(xin_out.dtype)


def _hy_in_gate(h, vecs, w_in_all, layer, b_in, w_sc, b_sc):
    l, d = h.shape
    bm = min(l, PROJ_ROWS)
    bc = min(d, HYENA_IN_COLS)
    nc = d // bc
    rb = bm // SUBLANES
    nrow8 = l // SUBLANES
    wspecs = [pl.BlockSpec((None, d, bc), functools.partial(lambda i, j, p: (layer, 0, p * nc + j), p=part))
              for part in range(3)]
    bspecs = [pl.BlockSpec((1, bc), functools.partial(lambda i, j, p: (0, p * nc + j), p=part)) for part in range(3)]
    out_spec = pl.BlockSpec((bm, bc), lambda i, j: (i, j))
    return pl.pallas_call(
        _hy_in_body,
        grid=(l // bm, nc),
        in_specs=[pl.BlockSpec((bm, d), lambda i, j: (i, 0)),
                  pl.BlockSpec((SUBLANES, d), lambda i, j: (jnp.maximum(i * rb - 1, 0), 0)),
                  pl.BlockSpec((SUBLANES, d), lambda i, j: (jnp.minimum((i + 1) * rb, nrow8 - 1), 0)),
                  pl.BlockSpec((SUBLANES, d), lambda i, j: (0, 0))] + wspecs + bspecs
                 + [pl.BlockSpec((3, 3, bc), lambda i, j: (0, 0, j)),
                    pl.BlockSpec((3, bc), lambda i, j: (0, j))],
        out_specs=[out_spec, out_spec],
        out_shape=[jax.ShapeDtypeStruct((l, d), BF16)] * 2,
        scratch_shapes=[pltpu.VMEM((bm, d), BF16), pltpu.VMEM((2 * SUBLANES, d), F32)],
        compiler_params=_cparams("arbitrary", "arbitrary"),
        name="hyena_in_gate",
    )(h, h, h, vecs, w_in_all, w_in_all, w_in_all, *([b_in.reshape(1, 3 * d)] * 3), w_sc.reshape(3, 3, d),
      b_sc.reshape(3, d))


def _filter_mlp_body(z_ref, aux_ref, auxr_ref, flip_ref, w1, b1, w2, b2, w3, b3, fr, tab_ref, tabr_ref):
    hp = lax.Precision.HIGHEST
    hid = w1.shape[1] // 2
    f = jnp.sin(fr[0:1, :] * (jnp.dot(z_ref[...], w1[...], precision=hp, preferred_element_type=F32) + b1[...]))
    f = jnp.sin(fr[1:2, :] * (jnp.dot(f, w2[...], precision=hp, preferred_element_type=F32) + b2[...]))
    f = jnp.sin(fr[2:3, :] * (jnp.dot(f, w3[...], precision=hp, preferred_element_type=F32) + b3[...]))
    low = lax.broadcasted_iota(jnp.int32, f.shape, 1) < hid
    tab_ref[0] = jnp.where(low, f, aux_ref[0])
    tab_ref[1] = jnp.where(low, pltpu.roll(f, hid, 1), aux_ref[1])
    fr_rows = jnp.dot(flip_ref[...], f, precision=hp, preferred_element_type=F32)
    tabr_ref[0] = jnp.where(low, pltpu.roll(fr_rows, hid, 1), auxr_ref[0])
    tabr_ref[1] = jnp.where(low, fr_rows, auxr_ref[1])


def _filter_table(l, w1, b1, w2, b2, w3, b3, freq):
    hid = w1.shape[1]
    assert 2 * hid == LANES, "two filter-MLP evaluations share one 128-lane row"
    t = jnp.linspace(0.0, 1.0, l, dtype=F32)[:, None]
    bands = jnp.linspace(1e-4, FILTER_BANDS - 1, FILTER_BANDS, dtype=F32)
    step = 2.0 * math.pi / l
    ang_hi = bands[None, None, :] * (step * LANES * jnp.arange(l // LANES, dtype=F32))[:, None, None]
    ang_lo = bands[None, None, :] * (step * jnp.arange(LANES, dtype=F32))[None, :, None]
    cos_e = (jnp.cos(ang_hi) * jnp.cos(ang_lo) - jnp.sin(ang_hi) * jnp.sin(ang_lo)).reshape(l, FILTER_BANDS)
    sin_e = (jnp.sin(ang_hi) * jnp.cos(ang_lo) + jnp.cos(ang_hi) * jnp.sin(ang_lo)).reshape(l, FILTER_BANDS)
    emb = jnp.concatenate([t, cos_e, -sin_e], axis=-1)
    emb = jnp.pad(emb, ((0, 0), (0, LANES - emb.shape[1])))
    w1p = jnp.pad(w1, ((0, LANES - w1.shape[0]), (0, 0)))
    rows = l // 2
    eye = jnp.eye(2, dtype=F32)
    emb_p = jnp.concatenate([emb[:rows], emb[rows:]], axis=1)

    def side_columns(tcol):
        return jnp.pad(tcol, ((0, 0), (hid, LANES - hid - 1))).reshape(2, rows, LANES)

    bm = min(rows, FILTER_ROWS)
    nb = rows // bm
    flip = jnp.eye(bm, dtype=F32)[::-1]

    def full(shape):
        return pl.BlockSpec(shape, lambda i: (0,) * len(shape))

    fwd_spec = pl.BlockSpec((2, bm, LANES), lambda i: (0, i, 0))
    rev_spec = pl.BlockSpec((2, bm, LANES), lambda i: (0, nb - 1 - i, 0))
    tab, tab_rev = pl.pallas_call(
        _filter_mlp_body,
        grid=(nb,),
        in_specs=[pl.BlockSpec((bm, 2 * LANES), lambda i: (i, 0)), fwd_spec, rev_spec, full((bm, bm)),
                  full((2 * LANES, LANES)), full((1, LANES)), full((LANES, LANES)), full((1, LANES)),
                  full((LANES, LANES)), full((1, LANES)), full((3, LANES))],
        out_specs=[fwd_spec, rev_spec],
        out_shape=[jax.ShapeDtypeStruct((2, rows, LANES), F32)] * 2,
        compiler_params=_cparams("arbitrary"),
        name="hyena_filter_mlp",
    )(emb_p, side_columns(t), side_columns(t[::-1]), flip,
      jnp.kron(eye, w1p), jnp.tile(b1, 2).reshape(1, LANES), jnp.kron(eye, w2),
      jnp.tile(b2, 2).reshape(1, LANES), jnp.kron(eye, w3), jnp.tile(b3, 2).reshape(1, LANES),
      jnp.tile(freq, (1, 2)))
    return jnp.concatenate([tab.reshape(l, LANES), jnp.zeros((1, LANES), F32), tab_rev.reshape(l, LANES)[:l - 1]],
                           axis=0)


def _decay_rates(d):
    return jnp.abs(jnp.linspace(math.log(DECAY_TARGET) / LONG_DECAY_PCT,
                                math.log(DECAY_TARGET) / SHORT_DECAY_PCT, d, dtype=F32))[None, :]


def _split_bf16(x):
    hi = x.astype(BF16)
    return hi, (x - hi.astype(F32)).astype(BF16)


def _stack_3pass(w):
    w_hi, w_lo = _split_bf16(w)
    return jnp.concatenate([w_hi, w_hi, w_lo], axis=0)


def _dot_3pass(f, w3):
    f_hi, f_lo = _split_bf16(f)
    return jnp.dot(jnp.concatenate([f_hi, f_lo, f_hi], axis=1), w3, preferred_element_type=F32)


def _filter_rows(ft, wa3, wb3, dl_ref):
    hid = wa3.shape[0] // 3
    half = ft.shape[0] // 2
    f, t = ft[:, :hid], ft[:, hid:hid + 1]
    y = jnp.concatenate([_dot_3pass(f[:half], wa3), _dot_3pass(f[half:], wb3)], axis=0)
    return y * jnp.exp(-t * dl_ref[...])


def _outer_stage(fa_ref, cols, o_ref):
    ys = [jnp.dot(fa_ref[...], xj, preferred_element_type=F32).astype(BF16) for xj in cols]
    o_ref[...] = pltpu.einshape("brd->rbd", jnp.stack(ys))


def _dft_a_body(x_ref, fa_ref, o_ref):
    xb = pltpu.einshape("abd->bad", x_ref[...])
    _outer_stage(fa_ref, [xb[j] for j in range(xb.shape[0])], o_ref)


def _dft_a_filter_body(ft_ref, wa_ref, wb_ref, dl_ref, fa_ref, o_ref):
    wa3, wb3 = _stack_3pass(wa_ref[...]), _stack_3pass(wb_ref[...])
    cols = [_filter_rows(ft_ref[j], wa3, wb3, dl_ref).astype(BF16) for j in range(ft_ref.shape[0])]
    _outer_stage(fa_ref, cols, o_ref)


def _dft_a(x3, fa, d_chunk):
    a, b, d = x3.shape
    rows = fa.shape[0]
    return pl.pallas_call(
        _dft_a_body,
        grid=(b // HALO, d // d_chunk),
        in_specs=[pl.BlockSpec((a, HALO, d_chunk), lambda i, j: (0, i, j)),
                  pl.BlockSpec((rows, a), lambda i, j: (0, 0))],
        out_specs=pl.BlockSpec((rows, HALO, d_chunk), lambda i, j: (0, i, j)),
        out_shape=jax.ShapeDtypeStruct((rows, b, d), BF16),
        compiler_params=_cparams("arbitrary", "arbitrary"),
        name="dft_outer",
    )(x3, fa)


def _dft_a_filter(ft3, w4, deltas, fa, d_chunk):
    b, n1, _ = ft3.shape
    hid, d2 = w4.shape
    d = d2 // 2
    rows = fa.shape[0]
    nd = d // d_chunk
    return pl.pallas_call(
        _dft_a_filter_body,
        grid=(b // HALO, nd),
        in_specs=[pl.BlockSpec((HALO, n1, LANES), lambda i, j: (i, 0, 0)),
                  pl.BlockSpec((hid, d_chunk), lambda i, j: (0, j)),
                  pl.BlockSpec((hid, d_chunk), lambda i, j: (0, j + nd)),
                  pl.BlockSpec((1, d_chunk), lambda i, j: (0, j)),
                  pl.BlockSpec((rows, n1), lambda i, j: (0, 0))],
        out_specs=pl.BlockSpec((rows, HALO, d_chunk), lambda i, j: (0, i, j)),
        out_shape=jax.ShapeDtypeStruct((rows, b, d), BF16),
        compiler_params=_cparams("arbitrary", "arbitrary"),
        name="dft_outer_filter",
    )(ft3, w4, w4, deltas, fa)


def _dft_inner_body(y_ref, yf_ref, gf_ref, o_ref):
    b, dc = y_ref.shape[2], y_ref.shape[3]
    for ci in range(y_ref.shape[1]):
        z = jnp.dot(gf_ref[ci], y_ref[:, ci].reshape(2 * b, dc), preferred_element_type=F32)
        hh = jnp.dot(gf_ref[ci], yf_ref[:, ci].reshape(2 * b, dc), preferred_element_type=F32)
        zr, zi, hr, hi = z[:b], z[b:], hh[:b], hh[b:]
        p = jnp.concatenate([zr * hr - zi * hi, zr * hi + zi * hr], axis=0).astype(BF16)
        cc = lax.dot_general(gf_ref[ci], p, (((0,), (0,)), ((), ())), preferred_element_type=F32)
        o_ref[:, ci] = cc.reshape(2, b, dc).astype(o_ref.dtype)


def _dft_inner(y4, yf4, gf):
    _, nc, b, d = y4.shape
    cblk = SUBLANES
    dc = min(d, DFT_INNER_COLS)
    yspec = pl.BlockSpec((2, cblk, b, dc), lambda c, j: (0, c, 0, j))
    return pl.pallas_call(
        _dft_inner_body,
        grid=(nc // cblk, d // dc),
        in_specs=[yspec, yspec, pl.BlockSpec((cblk, 2 * b, 2 * b), lambda c, j: (c, 0, 0))],
        out_specs=yspec,
        out_shape=jax.ShapeDtypeStruct(y4.shape, BF16),
        compiler_params=_cparams("arbitrary", "arbitrary"),
        name="dft_inner",
    )(y4, yf4, gf)


def _dft_c_body(c_ref, m_ref, xin_ref, x0_ref, skip_ref, o_ref):
    cb = pltpu.einshape("rbd->brd", c_ref[...])
    outs = [jnp.dot(m_ref[...], cb[j], preferred_element_type=F32).astype(BF16) for j in range(cb.shape[0])]
    conv = pltpu.einshape("bad->abd", jnp.stack(outs)).astype(F32)
    gated = x0_ref[...].astype(F32) * (conv + xin_ref[...].astype(F32) * skip_ref[...][None])
    o_ref[...] = gated.astype(o_ref.dtype)


def _dft_c(c3, mc, xin3, x03, skip, d_chunk):
    rows, b, d = c3.shape
    a = mc.shape[0]
    xspec = pl.BlockSpec((a, HALO, d_chunk), lambda i, j: (0, i, j))
    return pl.pallas_call(
        _dft_c_body,
        grid=(b // HALO, d // d_chunk),
        in_specs=[pl.BlockSpec((rows, HALO, d_chunk), lambda i, j: (0, i, j)),
                  pl.BlockSpec((a, rows), lambda i, j: (0, 0)),
                  xspec, xspec,
                  pl.BlockSpec((1, d_chunk), lambda i, j: (0, j))],
        out_specs=xspec,
        out_shape=jax.ShapeDtypeStruct((a, b, d), BF16),
        compiler_params=_cparams("arbitrary", "arbitrary"),
        name="dft_outer_inv",
    )(c3, mc, xin3, x03, skip)


def _cis(num, period):
    ang = (2.0 * math.pi / period) * num.astype(F32)
    return jnp.cos(ang), jnp.sin(ang)


def _long_conv_gated(xin, x0, ftab, w4, skip):
    l, d = xin.shape
    b = DFT_INNER
    a = l // b
    n1, n = 2 * a, 2 * l
    nc = n1 // 2 + SUBLANES
    dc = min(d, DFT_OUTER_COLS)
    ci = jnp.arange(nc, dtype=jnp.int32)
    ai = jnp.arange(n1, dtype=jnp.int32)
    live = (ci <= n1 // 2).astype(F32)[:, None]
    cr, sr = _cis((ci[:, None] * ai[None, :]) % n1, n1)
    cr, sr = cr * live, sr * live
    fa_full = jnp.concatenate([cr, -sr], axis=0).astype(BF16)
    fa = fa_full[:, :a]
    fold = jnp.where((ci == 0) | (ci == n1 // 2), 1.0, 2.0)[None, :] / n
    mc = jnp.concatenate([cr[:, :a].T * fold, -sr[:, :a].T * fold], axis=1).astype(BF16)
    ei = jnp.arange(b, dtype=jnp.int32)
    pr, ps = _cis((ci[:, None] * ei[None, :]) % n, n)
    qr, qs = _cis((ei[:, None] * ei[None, :]) % b, b)
    tr = pr[:, None, :] * qr[None] - ps[:, None, :] * qs[None]
    ti = -(pr[:, None, :] * qs[None] + ps[:, None, :] * qr[None])
    gf = jnp.concatenate([jnp.concatenate([tr, -ti], axis=2),
                          jnp.concatenate([ti, tr], axis=2)], axis=1).astype(BF16)

    ft3 = jnp.swapaxes(ftab.reshape(n1, b, LANES), 0, 1)
    hf = _dft_a_filter(ft3, w4, _decay_rates(d), fa_full, dc)
    yx = _dft_a(xin.reshape(a, b, d), fa, dc)
    c4 = _dft_inner(yx.reshape(2, nc, b, d), hf.reshape(2, nc, b, d), gf)
    y3 = _dft_c(c4.reshape(2 * nc, b, d), mc, xin.reshape(a, b, d), x0.reshape(a, b, d), skip.reshape(1, d), dc)
    return y3.reshape(l, d)


def _ctx_conv_body(x_ref, x0_ref, ft_ref, wa_ref, wb_ref, dl_ref, skip_ref, ff_ref, fh_ref, mi_ref, o_ref):
    x = x_ref[...].astype(F32)
    n = ft_ref.shape[0]
    filt = _filter_rows(ft_ref[...], _stack_3pass(wa_ref[...]), _stack_3pass(wb_ref[...]), dl_ref)
    xs = jnp.dot(ff_ref[...], x.astype(BF16), preferred_element_type=F32)
    hs = jnp.dot(fh_ref[...], filt.astype(BF16), preferred_element_type=F32)
    xr, xi, hr, hi = xs[:n], xs[n:], hs[:n], hs[n:]
    p = jnp.concatenate([xr * hr - xi * hi, xr * hi + xi * hr], axis=0).astype(BF16)
    conv = jnp.dot(mi_ref[...], p, preferred_element_type=F32)
    o_ref[...] = (x0_ref[...].astype(F32) * (conv + x * skip_ref[...])).astype(o_ref.dtype)


def _short_seq_conv_gated(xin, x0, ftab, w4, skip):
    l, d = xin.shape
    n = 2 * l
    hid = w4.shape[0]
    ni = jnp.arange(n, dtype=jnp.int32)
    cr, sr = _cis((ni[:, None] * ni[None, :]) % n, n)
    fh = jnp.concatenate([cr, -sr], axis=0).astype(BF16)
    ff = fh[:, :l]
    mi = (jnp.concatenate([cr[:l], -sr[:l]], axis=1) * (1.0 / n)).astype(BF16)
    dc = min(d, DFT_OUTER_COLS)
    nd = d // dc
    return pl.pallas_call(
        _ctx_conv_body,
        grid=(nd,),
        in_specs=[pl.BlockSpec((l, dc), lambda j: (0, j)),
                  pl.BlockSpec((l, dc), lambda j: (0, j)),
                  pl.BlockSpec((n, LANES), lambda j: (0, 0)),
                  pl.BlockSpec((hid, dc), lambda j: (0, j)),
                  pl.BlockSpec((hid, dc), lambda j: (0, j + nd)),
                  pl.BlockSpec((1, dc), lambda j: (0, j)),
                  pl.BlockSpec((1, dc), lambda j: (0, j)),
                  pl.BlockSpec((2 * n, l), lambda j: (0, 0)),
                  pl.BlockSpec((2 * n, n), lambda j: (0, 0)),
                  pl.BlockSpec((l, 2 * n), lambda j: (0, 0))],
        out_specs=pl.BlockSpec((l, dc), lambda j: (0, j)),
        out_shape=jax.ShapeDtypeStruct((l, d), BF16),
        compiler_params=_cparams("arbitrary"),
        name="ctx_conv",
    )(xin, x0, ftab, w4, w4, _decay_rates(d), skip.reshape(1, d), ff, fh, mi)


def _hyena_mixer(h, vecs, p, long_seq):
    (w_in_all, layer), b_in, w_sc, b_sc, f_w1, f_b1, f_w2, f_b2, f_w3, f_b3, f_w4, f_freq, skip, w_out, b_out = p
    l = h.shape[0]
    x0, xin = _hy_in_gate(h, vecs, w_in_all, layer, b_in, w_sc, b_sc)
    ftab = _filter_table(l, f_w1, f_b1, f_w2, f_b2, f_w3, f_b3, f_freq)
    conv = _long_conv_gated if long_seq else _short_seq_conv_gated
    y = conv(xin, x0, ftab, f_w4, skip)
    return _proj_out(y, w_out, b_out, h, vecs)


def _attn_body(sink_ref, q_ref, kp_ref, kc_ref, kn_ref, vp_ref, vc_ref, vn_ref, kx_ref, vx_ref, bias_ref, o_ref,
               *, group):
    blk = q_ref.shape[0]
    rid = lax.broadcasted_iota(jnp.int32, (group * blk, 1), 0)
    for kh in range(N_KV_HEADS):
        hs = slice(kh * HEAD_DIM, (kh + 1) * HEAD_DIM)
        heads = [kh * group + g for g in range(group)]
        q = jnp.concatenate([q_ref[:, hd * HEAD_DIM:(hd + 1) * HEAD_DIM] for hd in heads], axis=0)
        keys = jnp.concatenate([kp_ref[:, hs], kc_ref[:, hs], kn_ref[:, hs], kx_ref[:, hs]], axis=0)
        vals = jnp.concatenate([vp_ref[:, hs], vc_ref[:, hs], vn_ref[:, hs], vx_ref[:, hs]], axis=0)
        s = lax.dot_general(q, keys, (((1,), (1,)), ((), ())), preferred_element_type=F32)
        pieces = [s[:, :blk] + bias_ref[0, :, :blk], s[:, blk:2 * blk], s[:, 2 * blk:3 * blk] + bias_ref[0, :, blk:]]
        pieces += [s[:, c0:c0 + blk] for c0 in range(3 * blk, s.shape[1], blk)]
        sink = jnp.zeros((group * blk, 1), F32)
        for g, hd in enumerate(heads):
            sink = jnp.where((rid >= g * blk) & (rid < (g + 1) * blk), sink_ref[hd] * LOG2_E, sink)
        top = functools.reduce(jnp.maximum, pieces)
        mx = jnp.maximum(jnp.max(top, axis=-1, keepdims=True), sink)
        probs = [jnp.exp2(pc - mx) for pc in pieces]
        denom = jnp.sum(functools.reduce(jnp.add, probs), axis=-1, keepdims=True) + jnp.exp2(sink - mx)
        pr = jnp.concatenate([pp.astype(BF16) for pp in probs], axis=1)
        o = jnp.dot(pr, vals, preferred_element_type=F32) / denom
        for g, hd in enumerate(heads):
            o_ref[:, hd * HEAD_DIM:(hd + 1) * HEAD_DIM] = o[g * blk:(g + 1) * blk].astype(o_ref.dtype)


def _attention(qkv, kvc, sink, d):
    l = qkv.shape[0]
    c = kvc.shape[0]
    blk = ATTN_BLOCK
    nb = l // blk
    group = d // HEAD_DIM // N_KV_HEADS
    kvw = N_KV_HEADS * HEAD_DIM
    kcol = d // kvw
    qi = jnp.arange(group * blk, dtype=jnp.int32)[:, None] % blk
    ki = jnp.arange(blk, dtype=jnp.int32)[None, :]
    prev_ok, next_ok, never = ki >= qi, ki <= qi, jnp.zeros((group * blk, blk), bool)
    variants = [(never, next_ok), (prev_ok, next_ok), (prev_ok, never)]
    bias = jnp.stack([jnp.where(jnp.concatenate(v, axis=1), 0.0, MASK_BIAS).astype(F32) for v in variants])

    def kv_spec(col, shift):
        return pl.BlockSpec((blk, kvw), lambda n: (jnp.clip(n + shift, 0, nb - 1), col))

    return pl.pallas_call(
        functools.partial(_attn_body, group=group),
        grid=(nb,),
        in_specs=[pl.BlockSpec(memory_space=pltpu.SMEM),
                  pl.BlockSpec((blk, d), lambda n: (n, 0)),
                  kv_spec(kcol, -1), kv_spec(kcol, 0), kv_spec(kcol, 1),
                  kv_spec(kcol + 1, -1), kv_spec(kcol + 1, 0), kv_spec(kcol + 1, 1),
                  pl.BlockSpec((c, kvw), lambda n: (0, 0)),
                  pl.BlockSpec((c, kvw), lambda n: (0, 1)),
                  pl.BlockSpec((1, group * blk, 2 * blk),
                               lambda n: (jnp.where(n == 0, 0, jnp.where(n == nb - 1, 2, 1)), 0, 0))],
        out_specs=pl.BlockSpec((blk, d), lambda n: (n, 0)),
        out_shape=jax.ShapeDtypeStruct((l, d), BF16),
        compiler_params=_cparams("arbitrary"),
        name="window_attn",
    )(sink, qkv, qkv, qkv, qkv, qkv, qkv, qkv, kvc, kvc, bias)


def _rope_tables(l):
    rows = l // GRID_W
    pairs = HEAD_DIM // 4
    inv_freq = ROPE_THETA ** (-jnp.arange(pairs, dtype=F32) / pairs)
    ang_row = jnp.arange(rows).astype(F32)[:, None] * inv_freq[None, :]
    ang_col = jnp.arange(GRID_W).astype(F32)[:, None] * inv_freq[None, :]

    def per_token(row_tab, col_tab):
        by_row = jnp.broadcast_to(row_tab[:, None, :], (rows, GRID_W, pairs)).reshape(l, pairs)
        by_col = jnp.broadcast_to(col_tab[None, :, :], (rows, GRID_W, pairs)).reshape(l, pairs)
        return by_row, by_col

    cos_r, cos_c = per_token(jnp.cos(ang_row), jnp.cos(ang_col))
    sin_r, sin_c = per_token(jnp.sin(ang_row), jnp.sin(ang_col))
    zeros = jnp.zeros_like(cos_r)
    cos = jnp.concatenate([cos_r, cos_r, cos_c, cos_c], axis=-1)
    sin_a = jnp.concatenate([-sin_r, zeros, -sin_c, zeros], axis=-1)
    sin_b = jnp.concatenate([zeros, sin_r, zeros, sin_c], axis=-1)
    return cos, sin_a, sin_b


def _pool_body(hc_ref, hp_ref, hn_ref, vec_ref, w_ref, b_ref, sc_ref, o_ref, y_ref, *, seq_len):
    i = pl.program_id(0)
    last = pl.num_programs(0) - 1
    bm, d = hc_ref.shape
    gw = d // len(POOL_SIZES)
    h = hc_ref[...]
    u = _modulated(h, vec_ref)
    up = jnp.where(i > 0, _modulated(hp_ref[...], vec_ref), 0.0)
    un = jnp.where(i < last, _modulated(hn_ref[...], vec_ref), 0.0)
    ext_rows = bm + 2 * SUBLANES
    t = i * bm + lax.broadcasted_iota(jnp.int32, (bm, 1), 0)
    for g, size in enumerate(POOL_SIZES):
        cols = slice(g * gw, (g + 1) * gw)
        ext = jnp.concatenate([up[:, cols], u[:, cols], un[:, cols]], axis=0)
        acc, span = ext, 1
        while span < size:
            acc = acc + pltpu.roll(acc, ext_rows - span, 0)
            span *= 2
        start = SUBLANES - size // 2
        win = pltpu.roll(acc, ext_rows - start, 0)[:bm] if start else acc[:bm]
        lo = jnp.clip(t - size // 2, 0, seq_len)
        hi = jnp.clip(t - size // 2 + size, 0, seq_len)
        part = win / (hi - lo).astype(F32) - u[:, cols]
        yg = jnp.dot(part.astype(BF16), w_ref[g], preferred_element_type=F32)
        y_ref[:, cols] = (yg + b_ref[:, cols]) * sc_ref[:, cols]
    o_ref[...] = h + vec_ref[2:3, :] * _rms(y_ref[...], vec_ref[4:5, :])


def _pool_mixer(h, vecs, w, b, scale):
    l, d = h.shape
    bm = min(l, SMALL_ROWS)
    rb = bm // SUBLANES
    nrow8 = l // SUBLANES
    ng, gw = w.shape[0], w.shape[1]
    return pl.pallas_call(
        functools.partial(_pool_body, seq_len=l),
        grid=(l // bm,),
        in_specs=[pl.BlockSpec((bm, d), lambda i: (i, 0)),
                  pl.BlockSpec((SUBLANES, d), lambda i: (jnp.maximum(i * rb - 1, 0), 0)),
                  pl.BlockSpec((SUBLANES, d), lambda i: (jnp.minimum((i + 1) * rb, nrow8 - 1), 0)),
                  pl.BlockSpec((SUBLANES, d), lambda i: (0, 0)),
                  pl.BlockSpec((ng, gw, gw), lambda i: (0, 0, 0)),
                  pl.BlockSpec((1, d), lambda i: (0, 0)),
                  pl.BlockSpec((1, d), lambda i: (0, 0))],
        out_specs=pl.BlockSpec((bm, d), lambda i: (i, 0)),
        out_shape=jax.ShapeDtypeStruct((l, d), F32),
        scratch_shapes=[pltpu.VMEM((bm, d), F32)],
        compiler_params=_cparams("arbitrary"),
        name="pool_mixer",
    )(h, h, h, vecs, w.astype(BF16), b.reshape(1, d), scale.reshape(1, d))


def _sub_vecs(mod, k, g_pre, g_post):
    rows = [mod[3 * k], mod[3 * k + 1], mod[3 * k + 2], g_pre, g_post]
    return jnp.stack(rows + [jnp.zeros_like(g_pre)] * (SUBLANES - len(rows)))


def kernel(x, c, ctx, c_ctx, w_ada, b_ada, norm_pre, norm_post, w_ffn_in, w_ffn_out, hy_w_in, hy_b_in, hy_w_sc, hy_b_sc, hy_f_w1, hy_f_b1, hy_f_w2, hy_f_b2, hy_f_w3, hy_f_b3, hy_f_w4, hy_f_freq, hy_skip, hy_w_out, hy_b_out, at_w_qkv, at_b_qkv, at_sink, at_w_o, at_b_o, pl_w, pl_b, pl_scale):
    bsz, l, d = x.shape
    assert bsz == 1, "kernel handles a single batch element"
    assert d % DFT_INNER_COLS == 0 and l % PROJ_ROWS == 0 and ctx.shape[1] % SMALL_ROWS == 0, \
        "channel and token counts must be multiples of the block sizes"
    assert w_ffn_out.shape[2] % FFN_CHUNK == 0, "FFN width must be a multiple of the hidden chunk"
    depth = w_ada.shape[0]
    n_mixers = 3
    attn_layers = [i for i in range(depth) if i % n_mixers == 1]
    last_ctx_layer = attn_layers[-1] if attn_layers else -1

    mods = _adaln_mods(c, c_ctx, w_ada, b_ada).reshape(depth, 2, N_MOD, d)
    rope = _rope_tables(l)
    w_next = [w_ffn_in[0, 0].astype(BF16), w_ffn_out[0, 0].astype(BF16)]
    hy_w_in_bf = hy_w_in.astype(BF16)
    h, hc = x[0], ctx[0]
    for i in range(depth):
        kind, j = i % n_mixers, i // n_mixers
        ctx_live = i <= last_ctx_layer
        ctx_out = i < last_ctx_layer
        vec = [_sub_vecs(mods[i, 0], k, norm_pre[i, k], norm_post[i, k]) for k in range(3)]
        vec_c = [_sub_vecs(mods[i, 1], k, norm_pre[i, k], norm_post[i, k]) for k in range(3)]

        w_now = w_next
        h, *w_next = _ffn(h, vec[0], *w_now, FFN_RES, cast_next=(w_ffn_in, w_ffn_out, i, 1))
        if ctx_live:
            hc = _ffn(hc, vec_c[0], *w_now, FFN_RES)

        if kind == 0:
            hp = ((hy_w_in_bf, j), hy_b_in[j], hy_w_sc[j], hy_b_sc[j], hy_f_w1[j], hy_f_b1[j],
                  hy_f_w2[j], hy_f_b2[j], hy_f_w3[j], hy_f_b3[j], hy_f_w4[j], hy_f_freq[j], hy_skip[j],
                  hy_w_out[j].astype(BF16), hy_b_out[j])
            h = _hyena_mixer(h, vec[1], hp, long_seq=True)
            if ctx_out:
                hc = _hyena_mixer(hc, vec_c[1], hp, long_seq=False)
        elif kind == 1:
            w_qkv = at_w_qkv[j].astype(BF16)
            qkv = _proj_in(h, vec[1], w_qkv, at_b_qkv[j], BF16, rope=rope,
                           q_blocks=d // PROJ_COLS, q_scale=LOG2_E * HEAD_DIM ** -0.5)
            kvc = _proj_in(hc, vec_c[1], w_qkv[:, d:], at_b_qkv[j][d:], BF16)
            o = _attention(qkv, kvc, at_sink[j], d)
            h = _proj_out(o, at_w_o[j].astype(BF16), at_b_o[j], h, vec[1])
            assert not ctx_out, "context-query attention path is not needed for this depth"
        else:
            h = _pool_mixer(h, vec[1], pl_w[j], pl_b[j], pl_scale[j])
            assert not ctx_out, "context pooling path is not needed for this depth"

        w_now = w_next
        if i + 1 < depth:
            h, *w_next = _ffn(h, vec[2], *w_now, FFN_RES, cast_next=(w_ffn_in, w_ffn_out, i + 1, 0))
        else:
            h = _ffn(h, vec[2], *w_now, FFN_RES)
        if ctx_out:
            hc = _ffn(hc, vec_c[2], *w_now, FFN_RES)
    return h[None]
```

```python
import functools
import math

import jax
import jax.numpy as jnp
from jax import lax
from jax.experimental import pallas as pl
from jax.experimental.pallas import tpu as pltpu

F32 = jnp.float32
BF16 = jnp.bfloat16

GRID_W = 64
N_MOD = 9
NORM_EPS = 1e-6
FFN_RES = 0.5
FILTER_BANDS = 16
FILTER_EMB = 1 + 2 * FILTER_BANDS
DECAY_TARGET = 1e-2
SHORT_DECAY_PCT = 0.3
LONG_DECAY_PCT = 1.5
HEAD_DIM = 128
N_KV_HEADS = 4
WINDOW = 128
ATTN_BLOCK = 128
ROPE_THETA = 10000.0
POOL_SIZES = (2, 4, 8, 16)

VMEM_LIMIT_BYTES = 56 * 1024 * 1024
SUBLANES = 8
LANES = 128
DFT_INNER = 128
HALO = 16
FFN_CHUNK = 512
FFN_ROWS = 512
PROJ_ROWS = 1024
PROJ_COLS = 512
HYENA_IN_COLS = 256
DFT_OUTER_COLS = 512
DFT_INNER_COLS = 1024
SMALL_ROWS = 256
MODS_COLS = 1024
FILTER_ROWS = 512
MASK_BIAS = -1e30
LOG2_E = math.log2(math.e)


def _cparams(*sem):
    return pltpu.CompilerParams(dimension_semantics=sem, vmem_limit_bytes=VMEM_LIMIT_BYTES)


def _rms(x, g):
    ms = jnp.mean(x * x, axis=-1, keepdims=True)
    return x * lax.rsqrt(ms + NORM_EPS) * g


def _modulated(h, vec_ref):
    return _rms(h, vec_ref[3:4, :]) * (1.0 + vec_ref[1:2, :]) + vec_ref[0:1, :]


def _store_modulated(h_ref, vec_ref, u_ref):
    gain = vec_ref[3:4, :] * (1.0 + vec_ref[1:2, :])
    u_ref[...] = (_rms(h_ref[...], gain) + vec_ref[0:1, :]).astype(u_ref.dtype)


def _mods_body(cb_ref, w_ref, b_ref, o_ref, s_ref):
    k_dim, bn = w_ref.shape[1], w_ref.shape[2]
    nl = bn // LANES

    @pl.when((pl.program_id(0) == 0) & (pl.program_id(1) == 0))
    def _():
        cv = cb_ref[...]
        s_ref[...] = cv * jax.nn.sigmoid(cv)

    def step(kg, acc):
        rows = pl.ds(pl.multiple_of(kg * SUBLANES, SUBLANES), SUBLANES)
        s = [s_ref[r, rows, :] for r in range(2)]
        new = list(acc)
        for j in range(nl):
            wv = w_ref[0, rows, j * LANES:(j + 1) * LANES]
            for r in range(2):
                new[r * nl + j] = acc[r * nl + j] + wv * s[r]
        return tuple(new)

    init = tuple(jnp.zeros((SUBLANES, LANES), F32) for _ in range(2 * nl))
    acc = lax.fori_loop(0, k_dim // SUBLANES, step, init, unroll=4)
    for r in range(2):
        for j in range(nl):
            cols = slice(j * LANES, (j + 1) * LANES)
            o_ref[0, r:r + 1, cols] = jnp.sum(acc[r * nl + j], axis=0, keepdims=True) + b_ref[0, :, cols]


def _adaln_mods(c, c_ctx, w_ada, b_ada):
    depth, d, n = w_ada.shape
    bn = MODS_COLS
    cb = jnp.broadcast_to(jnp.stack([c[0], c_ctx])[:, :, None], (2, d, LANES))
    return pl.pallas_call(
        _mods_body,
        grid=(depth, n // bn),
        in_specs=[pl.BlockSpec((2, d, LANES), lambda i, j: (0, 0, 0)),
                  pl.BlockSpec((1, d, bn), lambda i, j: (i, 0, j)),
                  pl.BlockSpec((1, 1, bn), lambda i, j: (i, 0, j))],
        out_specs=pl.BlockSpec((1, 2, bn), lambda i, j: (i, 0, j)),
        out_shape=jax.ShapeDtypeStruct((depth, 2, n), F32),
        scratch_shapes=[pltpu.VMEM((2, d, LANES), F32)],
        compiler_params=_cparams("arbitrary", "arbitrary"),
        name="adaln_mods",
    )(cb, w_ada, b_ada.reshape(depth, 1, n))


def _ffn_body(h_ref, vec_ref, wg_ref, wu_ref, wo_ref, *rest, res_w, cast):
    if cast:
        ci_ref, co_ref, o_ref, cib_ref, cob_ref, u_ref, acc_ref = rest
    else:
        o_ref, u_ref, acc_ref = rest
    j = pl.program_id(1)

    @pl.when((pl.program_id(0) == 0) & (j == 0))
    def _():
        acc_ref[...] = jnp.zeros_like(acc_ref)

    @pl.when(j == 0)
    def _():
        _store_modulated(h_ref, vec_ref, u_ref)

    u = u_ref[...]
    g = jnp.dot(u, wg_ref[...], preferred_element_type=F32)
    p = jnp.dot(u, wu_ref[...], preferred_element_type=F32)
    a = (g * jax.nn.sigmoid(g) * p).astype(BF16)
    carry = jnp.where(j > 0, 1.0, 0.0).astype(F32)
    acc_ref[...] = acc_ref[...] * carry + jnp.dot(a, wo_ref[...], preferred_element_type=F32)

    if cast:
        cib_ref[...] = ci_ref[...].astype(cib_ref.dtype)
        cob_ref[...] = co_ref[...].astype(cob_ref.dtype)

    @pl.when(j == pl.num_programs(1) - 1)
    def _():
        gate = (res_w * vec_ref[2:3, :]) * vec_ref[4:5, :]
        o_ref[...] = h_ref[...] + _rms(acc_ref[...], gate)


def _cast_plan(n_rows, n_cols, steps):
    for n_col_blocks in range(1, steps + 1):
        if steps % n_col_blocks or n_cols % n_col_blocks or n_rows % (steps // n_col_blocks):
            continue
        br, bc = n_rows // (steps // n_col_blocks), n_cols // n_col_blocks
        if br % HALO == 0 and bc % LANES == 0:
            return br, bc, n_col_blocks
    raise ValueError("no tiling of the weight cast fits this grid")


def _ffn(h, vecs, w_in, w_out, res_w, cast_next=None):
    m, d = h.shape
    f = w_out.shape[0]
    bm = min(m, FFN_ROWS)
    bf = FFN_CHUNK
    nf = f // bf
    in_specs = [pl.BlockSpec((bm, d), lambda i, j: (i, 0)),
                pl.BlockSpec((SUBLANES, d), lambda i, j: (0, 0)),
                pl.BlockSpec((d, bf), lambda i, j: (0, j)),
                pl.BlockSpec((d, bf), lambda i, j: (0, j + nf)),
                pl.BlockSpec((bf, d), lambda i, j: (j, 0))]
    out_specs = [pl.BlockSpec((bm, d), lambda i, j: (i, 0))]
    out_shape = [jax.ShapeDtypeStruct((m, d), F32)]
    args = [h, vecs, w_in, w_in, w_out]
    if cast_next is not None:
        *srcs, layer, slot = cast_next
        for w in srcs:
            br, bc, ncb = _cast_plan(w.shape[2], w.shape[3], (m // bm) * nf)
            if ncb == nf:
                where = lambda i, j: (i, j)
            elif ncb == 1:
                where = lambda i, j: (i * nf + j, 0)
            else:
                where = functools.partial(lambda i, j, n: ((i * nf + j) // n, (i * nf + j) % n), n=ncb)
            in_specs.append(pl.BlockSpec((None, None, br, bc),
                                         functools.partial(lambda i, j, f: (layer, slot) + f(i, j), f=where)))
            out_specs.append(pl.BlockSpec((br, bc), where))
            out_shape.append(jax.ShapeDtypeStruct(w.shape[2:], BF16))
            args.append(w)
    outs = pl.pallas_call(
        functools.partial(_ffn_body, res_w=res_w, cast=cast_next is not None),
        grid=(m // bm, nf),
        in_specs=in_specs,
        out_specs=out_specs,
        out_shape=out_shape,
        scratch_shapes=[pltpu.VMEM((bm, d), BF16), pltpu.VMEM((bm, d), F32)],
        compiler_params=_cparams("arbitrary", "arbitrary"),
        name="ffn",
    )(*args)
    return outs if cast_next is not None else outs[0]


def _proj_in_body(h_ref, vec_ref, w_ref, b_ref, *rest, rope_blocks, q_blocks, q_scale):
    if rope_blocks:
        cos_ref, sa_ref, sb_ref, o_ref, u_ref = rest
    else:
        o_ref, u_ref = rest
    j = pl.program_id(1)

    @pl.when(j == 0)
    def _():
        _store_modulated(h_ref, vec_ref, u_ref)

    y = jnp.dot(u_ref[...], w_ref[...], preferred_element_type=F32) + b_ref[...]
    if not rope_blocks:
        o_ref[...] = y.astype(o_ref.dtype)
        return

    rot = jnp.where(j < q_blocks, q_scale, jnp.where(j < rope_blocks, 1.0, 0.0)).astype(F32)
    keep = jnp.where(j < rope_blocks, 0.0, 1.0).astype(F32)
    cos, sa, sb = cos_ref[...] * rot + keep, sa_ref[...] * rot, sb_ref[...] * rot
    for hd in range(y.shape[1] // HEAD_DIM):
        cols = slice(hd * HEAD_DIM, (hd + 1) * HEAD_DIM)
        xh = y[:, cols]
        r = (xh * cos + pltpu.roll(xh, HEAD_DIM - HEAD_DIM // 4, 1) * sa
             + pltpu.roll(xh, HEAD_DIM // 4, 1) * sb)
        o_ref[:, cols] = r.astype(o_ref.dtype)


def _proj_in(h, vecs, w, b, out_dtype, rope=None, q_blocks=0, q_scale=1.0):
    m, d = h.shape
    n = w.shape[1]
    bm = min(m, PROJ_ROWS)
    bn = PROJ_COLS
    in_specs = [pl.BlockSpec((bm, d), lambda i, j: (i, 0)),
                pl.BlockSpec((SUBLANES, d), lambda i, j: (0, 0)),
                pl.BlockSpec((d, bn), lambda i, j: (0, j)),
                pl.BlockSpec((1, bn), lambda i, j: (0, j))]
    args = [h, vecs, w, b.reshape(1, n)]
    rope_blocks = 0
    if rope is not None:
        rope_blocks = q_blocks + 1
        in_specs += [pl.BlockSpec((bm, HEAD_DIM), lambda i, j: (i, 0))] * 3
        args += list(rope)
    return pl.pallas_call(
        functools.partial(_proj_in_body, rope_blocks=rope_blocks, q_blocks=q_blocks, q_scale=q_scale),
        grid=(m // bm, n // bn),
        in_specs=in_specs,
        out_specs=pl.BlockSpec((bm, bn), lambda i, j: (i, j)),
        out_shape=jax.ShapeDtypeStruct((m, n), out_dtype),
        scratch_shapes=[pltpu.VMEM((bm, d), BF16)],
        compiler_params=_cparams("arbitrary", "arbitrary"),
        name="proj_in",
    )(*args)


def _proj_out_body(a_ref, w_ref, b_ref, h_ref, vec_ref, o_ref):
    half = a_ref.shape[0] // 2
    gate = vec_ref[2:3, :] * vec_ref[4:5, :]
    halves = [slice(0, half), slice(half, 2 * half)]
    ys = [jnp.dot(a_ref[rows, :], w_ref[...], preferred_element_type=F32) + b_ref[...] for rows in halves]
    for rows, y in zip(halves, ys):
        o_ref[rows, :] = h_ref[rows, :] + _rms(y, gate)


def _proj_out(a, w, b, h, vecs):
    m, d = h.shape
    k = a.shape[1]
    bm = min(m, FFN_ROWS)
    return pl.pallas_call(
        _proj_out_body,
        grid=(m // bm,),
        in_specs=[pl.BlockSpec((bm, k), lambda i: (i, 0)),
                  pl.BlockSpec((k, d), lambda i: (0, 0)),
                  pl.BlockSpec((1, d), lambda i: (0, 0)),
                  pl.BlockSpec((bm, d), lambda i: (i, 0)),
                  pl.BlockSpec((SUBLANES, d), lambda i: (0, 0))],
        out_specs=pl.BlockSpec((bm, d), lambda i: (i, 0)),
        out_shape=jax.ShapeDtypeStruct((m, d), F32),
        compiler_params=_cparams("arbitrary"),
        name="proj_out",
    )(a, w, b.reshape(1, d), h, vecs)


def _hy_in_body(h_ref, hp_ref, hn_ref, vec_ref, w0_ref, w1_ref, w2_ref, b0_ref, b1_ref, b2_ref, wsc_ref, bsc_ref,
                x0_out, xin_out, u_ref, uh_ref):
    i = pl.program_id(0)
    last = pl.num_programs(0) - 1

    @pl.when(pl.program_id(1) == 0)
    def _():
        _store_modulated(h_ref, vec_ref, u_ref)
        uh_ref[0:SUBLANES, :] = hp_ref[...]
        uh_ref[SUBLANES:, :] = hn_ref[...]
        _store_modulated(uh_ref, vec_ref, uh_ref)

    u, uh = u_ref[...], uh_ref[...].astype(BF16)
    thirds = ((w0_ref, b0_ref), (w1_ref, b1_ref), (w2_ref, b2_ref))
    zs = [jnp.dot(u, w[...], preferred_element_type=F32) + b[...] for w, b in thirds]
    zhs = [jnp.dot(uh, w[...], preferred_element_type=F32) + b[...] for w, b in thirds]

    def conv(part):
        z, zh = zs[part], zhs[part]
        bm = z.shape[0]
        prev_row = jnp.where(i > 0, zh[SUBLANES - 1:SUBLANES, :], 0.0)
        next_row = jnp.where(i < last, zh[SUBLANES:SUBLANES + 1, :], 0.0)
        rid = lax.broadcasted_iota(jnp.int32, z.shape, 0)
        up = jnp.where(rid == 0, prev_row, pltpu.roll(z, 1, 0))
        dn = jnp.where(rid == bm - 1, next_row, pltpu.roll(z, bm - 1, 0))
        return (up * wsc_ref[0, part:part + 1, :] + z * wsc_ref[1, part:part + 1, :]
                + dn * wsc_ref[2, part:part + 1, :] + bsc_ref[part:part + 1, :])

    x0_out[...] = conv(0).astype(x0_out.dtype)
    xin_out[...] = (conv(2) * conv(1)).astype(xin_out.dtype)


def _hy_in_gate(h, vecs, w_in_all, layer, b_in, w_sc, b_sc):
    l, d = h.shape
    bm = min(l, PROJ_ROWS)
    bc = min(d, HYENA_IN_COLS)
    nc = d // bc
    rb = bm // SUBLANES
    nrow8 = l // SUBLANES
    wspecs = [pl.BlockSpec((None, d, bc), functools.partial(lambda i, j, p: (layer, 0, p * nc + j), p=part))
              for part in range(3)]
    bspecs = [pl.BlockSpec((1, bc), functools.partial(lambda i, j, p: (0, p * nc + j), p=part)) for part in range(3)]
    out_spec = pl.BlockSpec((bm, bc), lambda i, j: (i, j))
    return pl.pallas_call(
        _hy_in_body,
        grid=(l // bm, nc),
        in_specs=[pl.BlockSpec((bm, d), lambda i, j: (i, 0)),
                  pl.BlockSpec((SUBLANES, d), lambda i, j: (jnp.maximum(i * rb - 1, 0), 0)),
                  pl.BlockSpec((SUBLANES, d), lambda i, j: (jnp.minimum((i + 1) * rb, nrow8 - 1), 0)),
                  pl.BlockSpec((SUBLANES, d), lambda i, j: (0, 0))] + wspecs + bspecs
                 + [pl.BlockSpec((3, 3, bc), lambda i, j: (0, 0, j)),
                    pl.BlockSpec((3, bc), lambda i, j: (0, j))],
        out_specs=[out_spec, out_spec],
        out_shape=[jax.ShapeDtypeStruct((l, d), BF16)] * 2,
        scratch_shapes=[pltpu.VMEM((bm, d), BF16), pltpu.VMEM((2 * SUBLANES, d), F32)],
        compiler_params=_cparams("arbitrary", "arbitrary"),
        name="hyena_in_gate",
    )(h, h, h, vecs, w_in_all, w_in_all, w_in_all, *([b_in.reshape(1, 3 * d)] * 3), w_sc.reshape(3, 3, d),
      b_sc.reshape(3, d))


def _filter_mlp_body(z_ref, aux_ref, auxr_ref, flip_ref, w1, b1, w2, b2, w3, b3, fr, tab_ref, tabr_ref):
    hp = lax.Precision.HIGHEST
    hid = w1.shape[1] // 2
    f = jnp.sin(fr[0:1, :] * (jnp.dot(z_ref[...], w1[...], precision=hp, preferred_element_type=F32) + b1[...]))
    f = jnp.sin(fr[1:2, :] * (jnp.dot(f, w2[...], precision=hp, preferred_element_type=F32) + b2[...]))
    f = jnp.sin(fr[2:3, :] * (jnp.dot(f, w3[...], precision=hp, preferred_element_type=F32) + b3[...]))
    low = lax.broadcasted_iota(jnp.int32, f.shape, 1) < hid
    tab_ref[0] = jnp.where(low, f, aux_ref[0])
    tab_ref[1] = jnp.where(low, pltpu.roll(f, hid, 1), aux_ref[1])
    fr_rows = jnp.dot(flip_ref[...], f, precision=hp, preferred_element_type=F32)
    tabr_ref[0] = jnp.where(low, pltpu.roll(fr_rows, hid, 1), auxr_ref[0])
    tabr_ref[1] = jnp.where(low, fr_rows, auxr_ref[1])


def _filter_table(l, w1, b1, w2, b2, w3, b3, freq):
    hid = w1.shape[1]
    assert 2 * hid == LANES, "two filter-MLP evaluations share one 128-lane row"
    t = jnp.linspace(0.0, 1.0, l, dtype=F32)[:, None]
    bands = jnp.linspace(1e-4, FILTER_BANDS - 1, FILTER_BANDS, dtype=F32)
    step = 2.0 * math.pi / l
    ang_hi = bands[None, None, :] * (step * LANES * jnp.arange(l // LANES, dtype=F32))[:, None, None]
    ang_lo = bands[None, None, :] * (step * jnp.arange(LANES, dtype=F32))[None, :, None]
    cos_e = (jnp.cos(ang_hi) * jnp.cos(ang_lo) - jnp.sin(ang_hi) * jnp.sin(ang_lo)).reshape(l, FILTER_BANDS)
    sin_e = (jnp.sin(ang_hi) * jnp.cos(ang_lo) + jnp.cos(ang_hi) * jnp.sin(ang_lo)).reshape(l, FILTER_BANDS)
    emb = jnp.concatenate([t, cos_e, -sin_e], axis=-1)
    emb = jnp.pad(emb, ((0, 0), (0, LANES - emb.shape[1])))
    w1p = jnp.pad(w1, ((0, LANES - w1.shape[0]), (0, 0)))
    rows = l // 2
    eye = jnp.eye(2, dtype=F32)
    emb_p = jnp.concatenate([emb[:rows], emb[rows:]], axis=1)

    def side_columns(tcol):
        return jnp.pad(tcol, ((0, 0), (hid, LANES - hid - 1))).reshape(2, rows, LANES)

    bm = min(rows, FILTER_ROWS)
    nb = rows // bm
    flip = jnp.eye(bm, dtype=F32)[::-1]

    def full(shape):
        return pl.BlockSpec(shape, lambda i: (0,) * len(shape))

    fwd_spec = pl.BlockSpec((2, bm, LANES), lambda i: (0, i, 0))
    rev_spec = pl.BlockSpec((2, bm, LANES), lambda i: (0, nb - 1 - i, 0))
    tab, tab_rev = pl.pallas_call(
        _filter_mlp_body,
        grid=(nb,),
        in_specs=[pl.BlockSpec((bm, 2 * LANES), lambda i: (i, 0)), fwd_spec, rev_spec, full((bm, bm)),
                  full((2 * LANES, LANES)), full((1, LANES)), full((LANES, LANES)), full((1, LANES)),
                  full((LANES, LANES)), full((1, LANES)), full((3, LANES))],
        out_specs=[fwd_spec, rev_spec],
        out_shape=[jax.ShapeDtypeStruct((2, rows, LANES), F32)] * 2,
        compiler_params=_cparams("arbitrary"),
        name="hyena_filter_mlp",
    )(emb_p, side_columns(t), side_columns(t[::-1]), flip,
      jnp.kron(eye, w1p), jnp.tile(b1, 2).reshape(1, LANES), jnp.kron(eye, w2),
      jnp.tile(b2, 2).reshape(1, LANES), jnp.kron(eye, w3), jnp.tile(b3, 2).reshape(1, LANES),
      jnp.tile(freq, (1, 2)))
    return jnp.concatenate([tab.reshape(l, LANES), jnp.zeros((1, LANES), F32), tab_rev.reshape(l, LANES)[:l - 1]],
                           axis=0)


def _decay_rates(d):
    return jnp.abs(jnp.linspace(math.log(DECAY_TARGET) / LONG_DECAY_PCT,
                                math.log(DECAY_TARGET) / SHORT_DECAY_PCT, d, dtype=F32))[None, :]


def _split_bf16(x):
    hi = x.astype(BF16)
    return hi, (x - hi.astype(F32)).astype(BF16)


def _stack_3pass(w):
    w_hi, w_lo = _split_bf16(w)
    return jnp.concatenate([w_hi, w_hi, w_lo], axis=0)


def _dot_3pass(f, w3):
    f_hi, f_lo = _split_bf16(f)
    return jnp.dot(jnp.concatenate([f_hi, f_lo, f_hi], axis=1), w3, preferred_element_type=F32)


def _filter_rows(ft, wa3, wb3, dl_ref):
    hid = wa3.shape[0] // 3
    half = ft.shape[0] // 2
    f, t = ft[:, :hid], ft[:, hid:hid + 1]
    y = jnp.concatenate([_dot_3pass(f[:half], wa3), _dot_3pass(f[half:], wb3)], axis=0)
    return y * jnp.exp(-t * dl_ref[...])


def _outer_stage(fa_ref, cols, o_ref):
    ys = [jnp.dot(fa_ref[...], xj, preferred_element_type=F32).astype(BF16) for xj in cols]
    o_ref[...] = pltpu.einshape("brd->rbd", jnp.stack(ys))


def _dft_a_body(x_ref, fa_ref, o_ref):
    xb = pltpu.einshape("abd->bad", x_ref[...])
    _outer_stage(fa_ref, [xb[j] for j in range(xb.shape[0])], o_ref)


def _dft_a_filter_body(ft_ref, wa_ref, wb_ref, dl_ref, fa_ref, o_ref):
    wa3, wb3 = _stack_3pass(wa_ref[...]), _stack_3pass(wb_ref[...])
    cols = [_filter_rows(ft_ref[j], wa3, wb3, dl_ref).astype(BF16) for j in range(ft_ref.shape[0])]
    _outer_stage(fa_ref, cols, o_ref)


def _dft_a(x3, fa, d_chunk):
    a, b, d = x3.shape
    rows = fa.shape[0]
    return pl.pallas_call(
        _dft_a_body,
        grid=(b // HALO, d // d_chunk),
        in_specs=[pl.BlockSpec((a, HALO, d_chunk), lambda i, j: (0, i, j)),
                  pl.BlockSpec((rows, a), lambda i, j: (0, 0))],
        out_specs=pl.BlockSpec((rows, HALO, d_chunk), lambda i, j: (0, i, j)),
        out_shape=jax.ShapeDtypeStruct((rows, b, d), BF16),
        compiler_params=_cparams("arbitrary", "arbitrary"),
        name="dft_outer",
    )(x3, fa)


def _dft_a_filter(ft3, w4, deltas, fa, d_chunk):
    b, n1, _ = ft3.shape
    hid, d2 = w4.shape
    d = d2 // 2
    rows = fa.shape[0]
    nd = d // d_chunk
    return pl.pallas_call(
        _dft_a_filter_body,
        grid=(b // HALO, nd),
        in_specs=[pl.BlockSpec((HALO, n1, LANES), lambda i, j: (i, 0, 0)),
                  pl.BlockSpec((hid, d_chunk), lambda i, j: (0, j)),
                  pl.BlockSpec((hid, d_chunk), lambda i, j: (0, j + nd)),
                  pl.BlockSpec((1, d_chunk), lambda i, j: (0, j)),
                  pl.BlockSpec((rows, n1), lambda i, j: (0, 0))],
        out_specs=pl.BlockSpec((rows, HALO, d_chunk), lambda i, j: (0, i, j)),
        out_shape=jax.ShapeDtypeStruct((rows, b, d), BF16),
        compiler_params=_cparams("arbitrary", "arbitrary"),
        name="dft_outer_filter",
    )(ft3, w4, w4, deltas, fa)


def _dft_inner_body(y_ref, yf_ref, gf_ref, o_ref):
    b, dc = y_ref.shape[2], y_ref.shape[3]
    for ci in range(y_ref.shape[1]):
        z = jnp.dot(gf_ref[ci], y_ref[:, ci].reshape(2 * b, dc), preferred_element_type=F32)
        hh = jnp.dot(gf_ref[ci], yf_ref[:, ci].reshape(2 * b, dc), preferred_element_type=F32)
        zr, zi, hr, hi = z[:b], z[b:], hh[:b], hh[b:]
        p = jnp.concatenate([zr * hr - zi * hi, zr * hi + zi * hr], axis=0).astype(BF16)
        cc = lax.dot_general(gf_ref[ci], p, (((0,), (0,)), ((), ())), preferred_element_type=F32)
        o_ref[:, ci] = cc.reshape(2, b, dc).astype(o_ref.dtype)


def _dft_inner(y4, yf4, gf):
    _, nc, b, d = y4.shape
    cblk = SUBLANES
    dc = min(d, DFT_INNER_COLS)
    yspec = pl.BlockSpec((2, cblk, b, dc), lambda c, j: (0, c, 0, j))
    return pl.pallas_call(
        _dft_inner_body,
        grid=(nc // cblk, d // dc),
        in_specs=[yspec, yspec, pl.BlockSpec((cblk, 2 * b, 2 * b), lambda c, j: (c, 0, 0))],
        out_specs=yspec,
        out_shape=jax.ShapeDtypeStruct(y4.shape, BF16),
        compiler_params=_cparams("arbitrary", "arbitrary"),
        name="dft_inner",
    )(y4, yf4, gf)


def _dft_c_body(c_ref, m_ref, xin_ref, x0_ref, skip_ref, o_ref):
    cb = pltpu.einshape("rbd->brd", c_ref[...])
    outs = [jnp.dot(m_ref[...], cb[j], preferred_element_type=F32).astype(BF16) for j in range(cb.shape[0])]
    conv = pltpu.einshape("bad->abd", jnp.stack(outs)).astype(F32)
    gated = x0_ref[...].astype(F32) * (conv + xin_ref[...].astype(F32) * skip_ref[...][None])
    o_ref[...] = gated.astype(o_ref.dtype)


def _dft_c(c3, mc, xin3, x03, skip, d_chunk):
    rows, b, d = c3.shape
    a = mc.shape[0]
    xspec = pl.BlockSpec((a, HALO, d_chunk), lambda i, j: (0, i, j))
    return pl.pallas_call(
        _dft_c_body,
        grid=(b // HALO, d // d_chunk),
        in_specs=[pl.BlockSpec((rows, HALO, d_chunk), lambda i, j: (0, i, j)),
                  pl.BlockSpec((a, rows), lambda i, j: (0, 0)),
                  xspec, xspec,
                  pl.BlockSpec((1, d_chunk), lambda i, j: (0, j))],
        out_specs=xspec,
        out_shape=jax.ShapeDtypeStruct((a, b, d), BF16),
        compiler_params=_cparams("arbitrary", "arbitrary"),
        name="dft_outer_inv",
    )(c3, mc, xin3, x03, skip)


def _cis(num, period):
    ang = (2.0 * math.pi / period) * num.astype(F32)
    return jnp.cos(ang), jnp.sin(ang)


def _long_conv_gated(xin, x0, ftab, w4, skip):
    l, d = xin.shape
    b = DFT_INNER
    a = l // b
    n1, n = 2 * a, 2 * l
    nc = n1 // 2 + SUBLANES
    dc = min(d, DFT_OUTER_COLS)
    ci = jnp.arange(nc, dtype=jnp.int32)
    ai = jnp.arange(n1, dtype=jnp.int32)
    live = (ci <= n1 // 2).astype(F32)[:, None]
    cr, sr = _cis((ci[:, None] * ai[None, :]) % n1, n1)
    cr, sr = cr * live, sr * live
    fa_full = jnp.concatenate([cr, -sr], axis=0).astype(BF16)
    fa = fa_full[:, :a]
    fold = jnp.where((ci == 0) | (ci == n1 // 2), 1.0, 2.0)[None, :] / n
    mc = jnp.concatenate([cr[:, :a].T * fold, -sr[:, :a].T * fold], axis=1).astype(BF16)
    ei = jnp.arange(b, dtype=jnp.int32)
    pr, ps = _cis((ci[:, None] * ei[None, :]) % n, n)
    qr, qs = _cis((ei[:, None] * ei[None, :]) % b, b)
    tr = pr[:, None, :] * qr[None] - ps[:, None, :] * qs[None]
    ti = -(pr[:, None, :] * qs[None] + ps[:, None, :] * qr[None])
    gf = jnp.concatenate([jnp.concatenate([tr, -ti], axis=2),
                          jnp.concatenate([ti, tr], axis=2)], axis=1).astype(BF16)

    ft3 = jnp.swapaxes(ftab.reshape(n1, b, LANES), 0, 1)
    hf = _dft_a_filter(ft3, w4, _decay_rates(d), fa_full, dc)
    yx = _dft_a(xin.reshape(a, b, d), fa, dc)
    c4 = _dft_inner(yx.reshape(2, nc, b, d), hf.reshape(2, nc, b, d), gf)
    y3 = _dft_c(c4.reshape(2 * nc, b, d), mc, xin.reshape(a, b, d), x0.reshape(a, b, d), skip.reshape(1, d), dc)
    return y3.reshape(l, d)


def _ctx_conv_body(x_ref, x0_ref, ft_ref, wa_ref, wb_ref, dl_ref, skip_ref, ff_ref, fh_ref, mi_ref, o_ref):
    x = x_ref[...].astype(F32)
    n = ft_ref.shape[0]
    filt = _filter_rows(ft_ref[...], _stack_3pass(wa_ref[...]), _stack_3pass(wb_ref[...]), dl_ref)
    xs = jnp.dot(ff_ref[...], x.astype(BF16), preferred_element_type=F32)
    hs = jnp.dot(fh_ref[...], filt.astype(BF16), preferred_element_type=F32)
    xr, xi, hr, hi = xs[:n], xs[n:], hs[:n], hs[n:]
    p = jnp.concatenate([xr * hr - xi * hi, xr * hi + xi * hr], axis=0).astype(BF16)
    conv = jnp.dot(mi_ref[...], p, preferred_element_type=F32)
    o_ref[...] = (x0_ref[...].astype(F32) * (conv + x * skip_ref[...])).astype(o_ref.dtype)


def _short_seq_conv_gated(xin, x0, ftab, w4, skip):
    l, d = xin.shape
    n = 2 * l
    hid = w4.shape[0]
    ni = jnp.arange(n, dtype=jnp.int32)
    cr, sr = _cis((ni[:, None] * ni[None, :]) % n, n)
    fh = jnp.concatenate([cr, -sr], axis=0).astype(BF16)
    ff = fh[:, :l]
    mi = (jnp.concatenate([cr[:l], -sr[:l]], axis=1) * (1.0 / n)).astype(BF16)
    dc = min(d, DFT_OUTER_COLS)
    nd = d // dc
    return pl.pallas_call(
        _ctx_conv_body,
        grid=(nd,),
        in_specs=[pl.BlockSpec((l, dc), lambda j: (0, j)),
                  pl.BlockSpec((l, dc), lambda j: (0, j)),
                  pl.BlockSpec((n, LANES), lambda j: (0, 0)),
                  pl.BlockSpec((hid, dc), lambda j: (0, j)),
                  pl.BlockSpec((hid, dc), lambda j: (0, j + nd)),
                  pl.BlockSpec((1, dc), lambda j: (0, j)),
                  pl.BlockSpec((1, dc), lambda j: (0, j)),
                  pl.BlockSpec((2 * n, l), lambda j: (0, 0)),
                  pl.BlockSpec((2 * n, n), lambda j: (0, 0)),
                  pl.BlockSpec((l, 2 * n), lambda j: (0, 0))],
        out_specs=pl.BlockSpec((l, dc), lambda j: (0, j)),
        out_shape=jax.ShapeDtypeStruct((l, d), BF16),
        compiler_params=_cparams("arbitrary"),
        name="ctx_conv",
    )(xin, x0, ftab, w4, w4, _decay_rates(d), skip.reshape(1, d), ff, fh, mi)


def _hyena_mixer(h, vecs, p, long_seq):
    (w_in_all, layer), b_in, w_sc, b_sc, f_w1, f_b1, f_w2, f_b2, f_w3, f_b3, f_w4, f_freq, skip, w_out, b_out = p
    l = h.shape[0]
    x0, xin = _hy_in_gate(h, vecs, w_in_all, layer, b_in, w_sc, b_sc)
    ftab = _filter_table(l, f_w1, f_b1, f_w2, f_b2, f_w3, f_b3, f_freq)
    conv = _long_conv_gated if long_seq else _short_seq_conv_gated
    y = conv(xin, x0, ftab, f_w4, skip)
    return _proj_out(y, w_out, b_out, h, vecs)


def _attn_body(sink_ref, q_ref, kp_ref, kc_ref, kn_ref, vp_ref, vc_ref, vn_ref, kx_ref, vx_ref, bias_ref, o_ref,
               *, group):
    blk = q_ref.shape[0]
    rid = lax.broadcasted_iota(jnp.int32, (group * blk, 1), 0)
    for kh in range(N_KV_HEADS):
        hs = slice(kh * HEAD_DIM, (kh + 1) * HEAD_DIM)
        heads = [kh * group + g for g in range(group)]
        q = jnp.concatenate([q_ref[:, hd * HEAD_DIM:(hd + 1) * HEAD_DIM] for hd in heads], axis=0)
        keys = jnp.concatenate([kp_ref[:, hs], kc_ref[:, hs], kn_ref[:, hs], kx_ref[:, hs]], axis=0)
        vals = jnp.concatenate([vp_ref[:, hs], vc_ref[:, hs], vn_ref[:, hs], vx_ref[:, hs]], axis=0)
        s = lax.dot_general(q, keys, (((1,), (1,)), ((), ())), preferred_element_type=F32)
        pieces = [s[:, :blk] + bias_ref[0, :, :blk], s[:, blk:2 * blk], s[:, 2 * blk:3 * blk] + bias_ref[0, :, blk:]]
        pieces += [s[:, c0:c0 + blk] for c0 in range(3 * blk, s.shape[1], blk)]
        sink = jnp.zeros((group * blk, 1), F32)
        for g, hd in enumerate(heads):
            sink = jnp.where((rid >= g * blk) & (rid < (g + 1) * blk), sink_ref[hd] * LOG2_E, sink)
        top = functools.reduce(jnp.maximum, pieces)
        mx = jnp.maximum(jnp.max(top, axis=-1, keepdims=True), sink)
        probs = [jnp.exp2(pc - mx) for pc in pieces]
        denom = jnp.sum(functools.reduce(jnp.add, probs), axis=-1, keepdims=True) + jnp.exp2(sink - mx)
        pr = jnp.concatenate([pp.astype(BF16) for pp in probs], axis=1)
        o = jnp.dot(pr, vals, preferred_element_type=F32) / denom
        for g, hd in enumerate(heads):
            o_ref[:, hd * HEAD_DIM:(hd + 1) * HEAD_DIM] = o[g * blk:(g + 1) * blk].astype(o_ref.dtype)


def _attention(qkv, kvc, sink, d):
    l = qkv.shape[0]
    c = kvc.shape[0]
    blk = ATTN_BLOCK
    nb = l // blk
    group = d // HEAD_DIM // N_KV_HEADS
    kvw = N_KV_HEADS * HEAD_DIM
    kcol = d // kvw
    qi = jnp.arange(group * blk, dtype=jnp.int32)[:, None] % blk
    ki = jnp.arange(blk, dtype=jnp.int32)[None, :]
    prev_ok, next_ok, never = ki >= qi, ki <= qi, jnp.zeros((group * blk, blk), bool)
    variants = [(never, next_ok), (prev_ok, next_ok), (prev_ok, never)]
    bias = jnp.stack([jnp.where(jnp.concatenate(v, axis=1), 0.0, MASK_BIAS).astype(F32) for v in variants])

    def kv_spec(col, shift):
        return pl.BlockSpec((blk, kvw), lambda n: (jnp.clip(n + shift, 0, nb - 1), col))

    return pl.pallas_call(
        functools.partial(_attn_body, group=group),
        grid=(nb,),
        in_specs=[pl.BlockSpec(memory_space=pltpu.SMEM),
                  pl.BlockSpec((blk, d), lambda n: (n, 0)),
                  kv_spec(kcol, -1), kv_spec(kcol, 0), kv_spec(kcol, 1),
                  kv_spec(kcol + 1, -1), kv_spec(kcol + 1, 0), kv_spec(kcol + 1, 1),
                  pl.BlockSpec((c, kvw), lambda n: (0, 0)),
                  pl.BlockSpec((c, kvw), lambda n: (0, 1)),
                  pl.BlockSpec((1, group * blk, 2 * blk),
                               lambda n: (jnp.where(n == 0, 0, jnp.where(n == nb - 1, 2, 1)), 0, 0))],
        out_specs=pl.BlockSpec((blk, d), lambda n: (n, 0)),
        out_shape=jax.ShapeDtypeStruct((l, d), BF16),
        compiler_params=_cparams("arbitrary"),
        name="window_attn",
    )(sink, qkv, qkv, qkv, qkv, qkv, qkv, qkv, kvc, kvc, bias)


def _rope_tables(l):
    rows = l // GRID_W
    pairs = HEAD_DIM // 4
    inv_freq = ROPE_THETA ** (-jnp.arange(pairs, dtype=F32) / pairs)
    ang_row = jnp.arange(rows).astype(F32)[:, None] * inv_freq[None, :]
    ang_col = jnp.arange(GRID_W).astype(F32)[:, None] * inv_freq[None, :]

    def per_token(row_tab, col_tab):
        by_row = jnp.broadcast_to(row_tab[:, None, :], (rows, GRID_W, pairs)).reshape(l, pairs)
        by_col = jnp.broadcast_to(col_tab[None, :, :], (rows, GRID_W, pairs)).reshape(l, pairs)
        return by_row, by_col

    cos_r, cos_c = per_token(jnp.cos(ang_row), jnp.cos(ang_col))
    sin_r, sin_c = per_token(jnp.sin(ang_row), jnp.sin(ang_col))
    zeros = jnp.zeros_like(cos_r)
    cos = jnp.concatenate([cos_r, cos_r, cos_c, cos_c], axis=-1)
    sin_a = jnp.concatenate([-sin_r, zeros, -sin_c, zeros], axis=-1)
    sin_b = jnp.concatenate([zeros, sin_r, zeros, sin_c], axis=-1)
    return cos, sin_a, sin_b


def _pool_body(hc_ref, hp_ref, hn_ref, vec_ref, w_ref, b_ref, sc_ref, o_ref, y_ref, *, seq_len):
    i = pl.program_id(0)
    last = pl.num_programs(0) - 1
    bm, d = hc_ref.shape
    gw = d // len(POOL_SIZES)
    h = hc_ref[...]
    u = _modulated(h, vec_ref)
    up = jnp.where(i > 0, _modulated(hp_ref[...], vec_ref), 0.0)
    un = jnp.where(i < last, _modulated(hn_ref[...], vec_ref), 0.0)
    ext_rows = bm + 2 * SUBLANES
    t = i * bm + lax.broadcasted_iota(jnp.int32, (bm, 1), 0)
    for g, size in enumerate(POOL_SIZES):
        cols = slice(g * gw, (g + 1) * gw)
        ext = jnp.concatenate([up[:, cols], u[:, cols], un[:, cols]], axis=0)
        acc, span = ext, 1
        while span < size:
            acc = acc + pltpu.roll(acc, ext_rows - span, 0)
            span *= 2
        start = SUBLANES - size // 2
        win = pltpu.roll(acc, ext_rows - start, 0)[:bm] if start else acc[:bm]
        lo = jnp.clip(t - size // 2, 0, seq_len)
        hi = jnp.clip(t - size // 2 + size, 0, seq_len)
        part = win / (hi - lo).astype(F32) - u[:, cols]
        yg = jnp.dot(part.astype(BF16), w_ref[g], preferred_element_type=F32)
        y_ref[:, cols] = (yg + b_ref[:, cols]) * sc_ref[:, cols]
    o_ref[...] = h + vec_ref[2:3, :] * _rms(y_ref[...], vec_ref[4:5, :])


def _pool_mixer(h, vecs, w, b, scale):
    l, d = h.shape
    bm = min(l, SMALL_ROWS)
    rb = bm // SUBLANES
    nrow8 = l // SUBLANES
    ng, gw = w.shape[0], w.shape[1]
    return pl.pallas_call(
        functools.partial(_pool_body, seq_len=l),
        grid=(l // bm,),
        in_specs=[pl.BlockSpec((bm, d), lambda i: (i, 0)),
                  pl.BlockSpec((SUBLANES, d), lambda i: (jnp.maximum(i * rb - 1, 0), 0)),
                  pl.BlockSpec((SUBLANES, d), lambda i: (jnp.minimum((i + 1) * rb, nrow8 - 1), 0)),
                  pl.BlockSpec((SUBLANES, d), lambda i: (0, 0)),
                  pl.BlockSpec((ng, gw, gw), lambda i: (0, 0, 0)),
                  pl.BlockSpec((1, d), lambda i: (0, 0)),
                  pl.BlockSpec((1, d), lambda i: (0, 0))],
        out_specs=pl.BlockSpec((bm, d), lambda i: (i, 0)),
        out_shape=jax.ShapeDtypeStruct((l, d), F32),
        scratch_shapes=[pltpu.VMEM((bm, d), F32)],
        compiler_params=_cparams("arbitrary"),
        name="pool_mixer",
    )(h, h, h, vecs, w.astype(BF16), b.reshape(1, d), scale.reshape(1, d))


def _sub_vecs(mod, k, g_pre, g_post):
    rows = [mod[3 * k], mod[3 * k + 1], mod[3 * k + 2], g_pre, g_post]
    return jnp.stack(rows + [jnp.zeros_like(g_pre)] * (SUBLANES - len(rows)))


def kernel(x, c, ctx, c_ctx, w_ada, b_ada, norm_pre, norm_post, w_ffn_in, w_ffn_out, hy_w_in, hy_b_in, hy_w_sc, hy_b_sc, hy_f_w1, hy_f_b1, hy_f_w2, hy_f_b2, hy_f_w3, hy_f_b3, hy_f_w4, hy_f_freq, hy_skip, hy_w_out, hy_b_out, at_w_qkv, at_b_qkv, at_sink, at_w_o, at_b_o, pl_w, pl_b, pl_scale):
    bsz, l, d = x.shape
    assert bsz == 1, "kernel handles a single batch element"
    assert d % DFT_INNER_COLS == 0 and l % PROJ_ROWS == 0 and ctx.shape[1] % SMALL_ROWS == 0, \
        "channel and token counts must be multiples of the block sizes"
    assert w_ffn_out.shape[2] % FFN_CHUNK == 0, "FFN width must be a multiple of the hidden chunk"
    depth = w_ada.shape[0]
    n_mixers = 3
    attn_layers = [i for i in range(depth) if i % n_mixers == 1]
    last_ctx_layer = attn_layers[-1] if attn_layers else -1

    mods = _adaln_mods(c, c_ctx, w_ada, b_ada).reshape(depth, 2, N_MOD, d)
    rope = _rope_tables(l)
    w_next = [w_ffn_in[0, 0].astype(BF16), w_ffn_out[0, 0].astype(BF16)]
    hy_w_in_bf = hy_w_in.astype(BF16)
    h, hc = x[0], ctx[0]
    for i in range(depth):
        kind, j = i % n_mixers, i // n_mixers
        ctx_live = i <= last_ctx_layer
        ctx_out = i < last_ctx_layer
        vec = [_sub_vecs(mods[i, 0], k, norm_pre[i, k], norm_post[i, k]) for k in range(3)]
        vec_c = [_sub_vecs(mods[i, 1], k, norm_pre[i, k], norm_post[i, k]) for k in range(3)]

        w_now = w_next
        h, *w_next = _ffn(h, vec[0], *w_now, FFN_RES, cast_next=(w_ffn_in, w_ffn_out, i, 1))
        if ctx_live:
            hc = _ffn(hc, vec_c[0], *w_now, FFN_RES)

        if kind == 0:
            hp = ((hy_w_in_bf, j), hy_b_in[j], hy_w_sc[j], hy_b_sc[j], hy_f_w1[j], hy_f_b1[j],
                  hy_f_w2[j], hy_f_b2[j], hy_f_w3[j], hy_f_b3[j], hy_f_w4[j], hy_f_freq[j], hy_skip[j],
                  hy_w_out[j].astype(BF16), hy_b_out[j])
            h = _hyena_mixer(h, vec[1], hp, long_seq=True)
            if ctx_out:
                hc = _hyena_mixer(hc, vec_c[1], hp, long_seq=False)
        elif kind == 1:
            w_qkv = at_w_qkv[j].astype(BF16)
            qkv = _proj_in(h, vec[1], w_qkv, at_b_qkv[j], BF16, rope=rope,
                           q_blocks=d // PROJ_COLS, q_scale=LOG2_E * HEAD_DIM ** -0.5)
            kvc = _proj_in(hc, vec_c[1], w_qkv[:, d:], at_b_qkv[j][d:], BF16)
            o = _attention(qkv, kvc, at_sink[j], d)
            h = _proj_out(o, at_w_o[j].astype(BF16), at_b_o[j], h, vec[1])
            assert not ctx_out, "context-query attention path is not needed for this depth"
        else:
            h = _pool_mixer(h, vec[1], pl_w[j], pl_b[j], pl_scale[j])
            assert not ctx_out, "context pooling path is not needed for this depth"

        w_now = w_next
        if i + 1 < depth:
            h, *w_next = _ffn(h, vec[2], *w_now, FFN_RES, cast_next=(w_ffn_in, w_ffn_out, i + 1, 0))
        else:
            h = _ffn(h, vec[2], *w_now, FFN_RES)
        if ctx_out:
            hc = _ffn(hc, vec_c[2], *w_now, FFN_RES)
    return h[None]
```

```python
import functools
import math

import jax
import jax.numpy as jnp
from jax import lax
from jax.experimental import pallas as pl
from jax.experimental.pallas import tpu as pltpu

F32 = jnp.float32
BF16 = jnp.bfloat16

GRID_W = 64
N_MOD = 9
NORM_EPS = 1e-6
FFN_RES = 0.5
FILTER_BANDS = 16
FILTER_EMB = 1 + 2 * FILTER_BANDS
DECAY_TARGET = 1e-2
SHORT_DECAY_PCT = 0.3
LONG_DECAY_PCT = 1.5
HEAD_DIM = 128
N_KV_HEADS = 4
WINDOW = 128
ATTN_BLOCK = 128
ROPE_THETA = 10000.0
POOL_SIZES = (2, 4, 8, 16)

VMEM_LIMIT_BYTES = 56 * 1024 * 1024
SUBLANES = 8
LANES = 128
DFT_INNER = 128
HALO = 16
FFN_CHUNK = 512
FFN_ROWS = 512
PROJ_ROWS = 1024
PROJ_COLS = 512
HYENA_IN_COLS = 256
DFT_OUTER_COLS = 512
DFT_INNER_COLS = 1024
SMALL_ROWS = 256
MODS_COLS = 1024
FILTER_ROWS = 512
MASK_BIAS = -1e30
LOG2_E = math.log2(math.e)


def _cparams(*sem):
    return pltpu.CompilerParams(dimension_semantics=sem, vmem_limit_bytes=VMEM_LIMIT_BYTES)


def _rms(x, g):
    ms = jnp.mean(x * x, axis=-1, keepdims=True)
    return x * lax.rsqrt(ms + NORM_EPS) * g


def _modulated(h, vec_ref):
    return _rms(h, vec_ref[3:4, :]) * (1.0 + vec_ref[1:2, :]) + vec_ref[0:1, :]


def _store_modulated(h_ref, vec_ref, u_ref):
    gain = vec_ref[3:4, :] * (1.0 + vec_ref[1:2, :])
    u_ref[...] = (_rms(h_ref[...], gain) + vec_ref[0:1, :]).astype(u_ref.dtype)


def _mods_body(cb_ref, w_ref, b_ref, o_ref, s_ref):
    k_dim, bn = w_ref.shape[1], w_ref.shape[2]
    nl = bn // LANES

    @pl.when((pl.program_id(0) == 0) & (pl.program_id(1) == 0))
    def _():
        cv = cb_ref[...]
        s_ref[...] = cv * jax.nn.sigmoid(cv)

    def step(kg, acc):
        rows = pl.ds(pl.multiple_of(kg * SUBLANES, SUBLANES), SUBLANES)
        s = [s_ref[r, rows, :] for r in range(2)]
        new = list(acc)
        for j in range(nl):
            wv = w_ref[0, rows, j * LANES:(j + 1) * LANES]
            for r in range(2):
                new[r * nl + j] = acc[r * nl + j] + wv * s[r]
        return tuple(new)

    init = tuple(jnp.zeros((SUBLANES, LANES), F32) for _ in range(2 * nl))
    acc = lax.fori_loop(0, k_dim // SUBLANES, step, init, unroll=4)
    for r in range(2):
        for j in range(nl):
            cols = slice(j * LANES, (j + 1) * LANES)
            o_ref[0, r:r + 1, cols] = jnp.sum(acc[r * nl + j], axis=0, keepdims=True) + b_ref[0, :, cols]


def _adaln_mods(c, c_ctx, w_ada, b_ada):
    depth, d, n = w_ada.shape
    bn = MODS_COLS
    cb = jnp.broadcast_to(jnp.stack([c[0], c_ctx])[:, :, None], (2, d, LANES))
    return pl.pallas_call(
        _mods_body,
        grid=(depth, n // bn),
        in_specs=[pl.BlockSpec((2, d, LANES), lambda i, j: (0, 0, 0)),
                  pl.BlockSpec((1, d, bn), lambda i, j: (i, 0, j)),
                  pl.BlockSpec((1, 1, bn), lambda i, j: (i, 0, j))],
        out_specs=pl.BlockSpec((1, 2, bn), lambda i, j: (i, 0, j)),
        out_shape=jax.ShapeDtypeStruct((depth, 2, n), F32),
        scratch_shapes=[pltpu.VMEM((2, d, LANES), F32)],
        compiler_params=_cparams("arbitrary", "arbitrary"),
        name="adaln_mods",
    )(cb, w_ada, b_ada.reshape(depth, 1, n))


def _ffn_body(h_ref, vec_ref, wg_ref, wu_ref, wo_ref, *rest, res_w, cast):
    if cast:
        ci_ref, co_ref, o_ref, cib_ref, cob_ref, u_ref, acc_ref = rest
    else:
        o_ref, u_ref, acc_ref = rest
    j = pl.program_id(1)

    @pl.when((pl.program_id(0) == 0) & (j == 0))
    def _():
        acc_ref[...] = jnp.zeros_like(acc_ref)

    @pl.when(j == 0)
    def _():
        _store_modulated(h_ref, vec_ref, u_ref)

    u = u_ref[...]
    g = jnp.dot(u, wg_ref[...], preferred_element_type=F32)
    p = jnp.dot(u, wu_ref[...], preferred_element_type=F32)
    a = (g * jax.nn.sigmoid(g) * p).astype(BF16)
    carry = jnp.where(j > 0, 1.0, 0.0).astype(F32)
    acc_ref[...] = acc_ref[...] * carry + jnp.dot(a, wo_ref[...], preferred_element_type=F32)

    if cast:
        cib_ref[...] = ci_ref[...].astype(cib_ref.dtype)
        cob_ref[...] = co_ref[...].astype(cob_ref.dtype)

    @pl.when(j == pl.num_programs(1) - 1)
    def _():
        gate = (res_w * vec_ref[2:3, :]) * vec_ref[4:5, :]
        o_ref[...] = h_ref[...] + _rms(acc_ref[...], gate)


def _cast_plan(n_rows, n_cols, steps):
    for n_col_blocks in range(1, steps + 1):
        if steps % n_col_blocks or n_cols % n_col_blocks or n_rows % (steps // n_col_blocks):
            continue
        br, bc = n_rows // (steps // n_col_blocks), n_cols // n_col_blocks
        if br % HALO == 0 and bc % LANES == 0:
            return br, bc, n_col_blocks
    raise ValueError("no tiling of the weight cast fits this grid")


def _ffn(h, vecs, w_in, w_out, res_w, cast_next=None):
    m, d = h.shape
    f = w_out.shape[0]
    bm = min(m, FFN_ROWS)
    bf = FFN_CHUNK
    nf = f // bf
    in_specs = [pl.BlockSpec((bm, d), lambda i, j: (i, 0)),
                pl.BlockSpec((SUBLANES, d), lambda i, j: (0, 0)),
                pl.BlockSpec((d, bf), lambda i, j: (0, j)),
                pl.BlockSpec((d, bf), lambda i, j: (0, j + nf)),
                pl.BlockSpec((bf, d), lambda i, j: (j, 0))]
    out_specs = [pl.BlockSpec((bm, d), lambda i, j: (i, 0))]
    out_shape = [jax.ShapeDtypeStruct((m, d), F32)]
    args = [h, vecs, w_in, w_in, w_out]
    if cast_next is not None:
        *srcs, layer, slot = cast_next
        for w in srcs:
            br, bc, ncb = _cast_plan(w.shape[2], w.shape[3], (m // bm) * nf)
            if ncb == nf:
                where = lambda i, j: (i, j)
            elif ncb == 1:
                where = lambda i, j: (i * nf + j, 0)
            else:
                where = functools.partial(lambda i, j, n: ((i * nf + j) // n, (i * nf + j) % n), n=ncb)
            in_specs.append(pl.BlockSpec((None, None, br, bc),
                                         functools.partial(lambda i, j, f: (layer, slot) + f(i, j), f=where)))
            out_specs.append(pl.BlockSpec((br, bc), where))
            out_shape.append(jax.ShapeDtypeStruct(w.shape[2:], BF16))
            args.append(w)
    outs = pl.pallas_call(
        functools.partial(_ffn_body, res_w=res_w, cast=cast_next is not None),
        grid=(m // bm, nf),
        in_specs=in_specs,
        out_specs=out_specs,
        out_shape=out_shape,
        scratch_shapes=[pltpu.VMEM((bm, d), BF16), pltpu.VMEM((bm, d), F32)],
        compiler_params=_cparams("arbitrary", "arbitrary"),
        name="ffn",
    )(*args)
    return outs if cast_next is not None else outs[0]


def _proj_in_body(h_ref, vec_ref, w_ref, b_ref, *rest, rope_blocks, q_blocks, q_scale):
    if rope_blocks:
        cos_ref, sa_ref, sb_ref, o_ref, u_ref = rest
    else:
        o_ref, u_ref = rest
    j = pl.program_id(1)

    @pl.when(j == 0)
    def _():
        _store_modulated(h_ref, vec_ref, u_ref)

    y = jnp.dot(u_ref[...], w_ref[...], preferred_element_type=F32) + b_ref[...]
    if not rope_blocks:
        o_ref[...] = y.astype(o_ref.dtype)
        return

    rot = jnp.where(j < q_blocks, q_scale, jnp.where(j < rope_blocks, 1.0, 0.0)).astype(F32)
    keep = jnp.where(j < rope_blocks, 0.0, 1.0).astype(F32)
    cos, sa, sb = cos_ref[...] * rot + keep, sa_ref[...] * rot, sb_ref[...] * rot
    for hd in range(y.shape[1] // HEAD_DIM):
        cols = slice(hd * HEAD_DIM, (hd + 1) * HEAD_DIM)
        xh = y[:, cols]
        r = (xh * cos + pltpu.roll(xh, HEAD_DIM - HEAD_DIM // 4, 1) * sa
             + pltpu.roll(xh, HEAD_DIM // 4, 1) * sb)
        o_ref[:, cols] = r.astype(o_ref.dtype)


def _proj_in(h, vecs, w, b, out_dtype, rope=None, q_blocks=0, q_scale=1.0):
    m, d = h.shape
    n = w.shape[1]
    bm = min(m, PROJ_ROWS)
    bn = PROJ_COLS
    in_specs = [pl.BlockSpec((bm, d), lambda i, j: (i, 0)),
                pl.BlockSpec((SUBLANES, d), lambda i, j: (0, 0)),
                pl.BlockSpec((d, bn), lambda i, j: (0, j)),
                pl.BlockSpec((1, bn), lambda i, j: (0, j))]
    args = [h, vecs, w, b.reshape(1, n)]
    rope_blocks = 0
    if rope is not None:
        rope_blocks = q_blocks + 1
        in_specs += [pl.BlockSpec((bm, HEAD_DIM), lambda i, j: (i, 0))] * 3
        args += list(rope)
    return pl.pallas_call(
        functools.partial(_proj_in_body, rope_blocks=rope_blocks, q_blocks=q_blocks, q_scale=q_scale),
        grid=(m // bm, n // bn),
        in_specs=in_specs,
        out_specs=pl.BlockSpec((bm, bn), lambda i, j: (i, j)),
        out_shape=jax.ShapeDtypeStruct((m, n), out_dtype),
        scratch_shapes=[pltpu.VMEM((bm, d), BF16)],
        compiler_params=_cparams("arbitrary", "arbitrary"),
        name="proj_in",
    )(*args)


def _row_slab_cast(cast_next, nb):
    *srcs, layer, slot = cast_next
    in_specs, out_specs, out_shape = [], [], []
    for w in srcs:
        rows, cols = w.shape[2] // nb, w.shape[3]
        assert rows * nb == w.shape[2] and rows % HALO == 0, "weight rows must split evenly over the row blocks"
        in_specs.append(pl.BlockSpec((None, None, rows, cols), lambda i: (layer, slot, i, 0)))
        out_specs.append(pl.BlockSpec((rows, cols), lambda i: (i, 0)))
        out_shape.append(jax.ShapeDtypeStruct(w.shape[2:], BF16))
    return in_specs, out_specs, out_shape, srcs


def _proj_out_body(a_ref, w_ref, b_ref, h_ref, vec_ref, *rest, cast):
    o_ref = rest[2] if cast else rest[0]
    half = a_ref.shape[0] // 2
    gate = vec_ref[2:3, :] * vec_ref[4:5, :]
    halves = [slice(0, half), slice(half, 2 * half)]
    ys = [jnp.dot(a_ref[rows, :], w_ref[...], preferred_element_type=F32) + b_ref[...] for rows in halves]
    if cast:
        ci_ref, co_ref, _, cib_ref, cob_ref = rest
        cib_ref[...] = ci_ref[...].astype(cib_ref.dtype)
        cob_ref[...] = co_ref[...].astype(cob_ref.dtype)
    for rows, y in zip(halves, ys):
        o_ref[rows, :] = h_ref[rows, :] + _rms(y, gate)


def _proj_out(a, w, b, h, vecs, cast_next=None):
    m, d = h.shape
    k = a.shape[1]
    bm = min(m, FFN_ROWS)
    in_specs = [pl.BlockSpec((bm, k), lambda i: (i, 0)),
                pl.BlockSpec((k, d), lambda i: (0, 0)),
                pl.BlockSpec((1, d), lambda i: (0, 0)),
                pl.BlockSpec((bm, d), lambda i: (i, 0)),
                pl.BlockSpec((SUBLANES, d), lambda i: (0, 0))]
    out_specs = [pl.BlockSpec((bm, d), lambda i: (i, 0))]
    out_shape = [jax.ShapeDtypeStruct((m, d), F32)]
    args = [a, w, b.reshape(1, d), h, vecs]
    if cast_next is not None:
        c_in, c_out, c_shape, c_args = _row_slab_cast(cast_next, m // bm)
        in_specs, out_specs, out_shape, args = in_specs + c_in, out_specs + c_out, out_shape + c_shape, args + c_args
    outs = pl.pallas_call(
        functools.partial(_proj_out_body, cast=cast_next is not None),
        grid=(m // bm,),
        in_specs=in_specs,
        out_specs=out_specs,
        out_shape=out_shape,
        compiler_params=_cparams("arbitrary"),
        name="proj_out",
    )(*args)
    return outs if cast_next is not None else outs[0]


def _hy_in_body(h_ref, hp_ref, hn_ref, vec_ref, w0_ref, w1_ref, w2_ref, b0_ref, b1_ref, b2_ref, wsc_ref, bsc_ref,
                x0_out, xin_out, u_ref, uh_ref):
    i = pl.program_id(0)
    last = pl.num_programs(0) - 1

    @pl.when(pl.program_id(1) == 0)
    def _():
        _store_modulated(h_ref, vec_ref, u_ref)
        uh_ref[0:SUBLANES, :] = hp_ref[...]
        uh_ref[SUBLANES:, :] = hn_ref[...]
        _store_modulated(uh_ref, vec_ref, uh_ref)

    u, uh = u_ref[...], uh_ref[...].astype(BF16)
    thirds = ((w0_ref, b0_ref), (w1_ref, b1_ref), (w2_ref, b2_ref))
    zs = [jnp.dot(u, w[...], preferred_element_type=F32) + b[...] for w, b in thirds]
    zhs = [jnp.dot(uh, w[...], preferred_element_type=F32) + b[...] for w, b in thirds]

    def conv(part):
        z, zh = zs[part], zhs[part]
        bm = z.shape[0]
        prev_row = jnp.where(i > 0, zh[SUBLANES - 1:SUBLANES, :], 0.0)
        next_row = jnp.where(i < last, zh[SUBLANES:SUBLANES + 1, :], 0.0)
        rid = lax.broadcasted_iota(jnp.int32, z.shape, 0)
        up = jnp.where(rid == 0, prev_row, pltpu.roll(z, 1, 0))
        dn = jnp.where(rid == bm - 1, next_row, pltpu.roll(z, bm - 1, 0))
        return (up * wsc_ref[0, part:part + 1, :] + z * wsc_ref[1, part:part + 1, :]
                + dn * wsc_ref[2, part:part + 1, :] + bsc_ref[part:part + 1, :])

    x0_out[...] = conv(0).astype(x0_out.dtype)
    xin_out[...] = (conv(2) * conv(1)).astype(xin_out.dtype)


def _hy_in_gate(h, vecs, w_in_all, layer, b_in, w_sc, b_sc):
    l, d = h.shape
    bm = min(l, PROJ_ROWS)
    bc = min(d, HYENA_IN_COLS)
    nc = d // bc
    rb = bm // SUBLANES
    nrow8 = l // SUBLANES
    wspecs = [pl.BlockSpec((None, d, bc), functools.partial(lambda i, j, p: (layer, 0, p * nc + j), p=part))
              for part in range(3)]
    bspecs = [pl.BlockSpec((1, bc), functools.partial(lambda i, j, p: (0, p * nc + j), p=part)) for part in range(3)]
    out_spec = pl.BlockSpec((bm, bc), lambda i, j: (i, j))
    return pl.pallas_call(
        _hy_in_body,
        grid=(l // bm, nc),
        in_specs=[pl.BlockSpec((bm, d), lambda i, j: (i, 0)),
                  pl.BlockSpec((SUBLANES, d), lambda i, j: (jnp.maximum(i * rb - 1, 0), 0)),
                  pl.BlockSpec((SUBLANES, d), lambda i, j: (jnp.minimum((i + 1) * rb, nrow8 - 1), 0)),
                  pl.BlockSpec((SUBLANES, d), lambda i, j: (0, 0))] + wspecs + bspecs
                 + [pl.BlockSpec((3, 3, bc), lambda i, j: (0, 0, j)),
                    pl.BlockSpec((3, bc), lambda i, j: (0, j))],
        out_specs=[out_spec, out_spec],
        out_shape=[jax.ShapeDtypeStruct((l, d), BF16)] * 2,
        scratch_shapes=[pltpu.VMEM((bm, d), BF16), pltpu.VMEM((2 * SUBLANES, d), F32)],
        compiler_params=_cparams("arbitrary", "arbitrary"),
        name="hyena_in_gate",
    )(h, h, h, vecs, w_in_all, w_in_all, w_in_all, *([b_in.reshape(1, 3 * d)] * 3), w_sc.reshape(3, 3, d),
      b_sc.reshape(3, d))


def _filter_mlp_body(z_ref, aux_ref, auxr_ref, flip_ref, w1, b1, w2, b2, w3, b3, fr, tab_ref, tabr_ref):
    hp = lax.Precision.HIGHEST
    hid = w1.shape[1] // 2
    f = jnp.sin(fr[0:1, :] * (jnp.dot(z_ref[...], w1[...], precision=hp, preferred_element_type=F32) + b1[...]))
    f = jnp.sin(fr[1:2, :] * (jnp.dot(f, w2[...], precision=hp, preferred_element_type=F32) + b2[...]))
    f = jnp.sin(fr[2:3, :] * (jnp.dot(f, w3[...], precision=hp, preferred_element_type=F32) + b3[...]))
    low = lax.broadcasted_iota(jnp.int32, f.shape, 1) < hid
    tab_ref[0] = jnp.where(low, f, aux_ref[0])
    tab_ref[1] = jnp.where(low, pltpu.roll(f, hid, 1), aux_ref[1])
    fr_rows = jnp.dot(flip_ref[...], f, precision=hp, preferred_element_type=F32)
    tabr_ref[0] = jnp.where(low, pltpu.roll(fr_rows, hid, 1), auxr_ref[0])
    tabr_ref[1] = jnp.where(low, fr_rows, auxr_ref[1])


def _filter_table(l, w1, b1, w2, b2, w3, b3, freq):
    hid = w1.shape[1]
    assert 2 * hid == LANES, "two filter-MLP evaluations share one 128-lane row"
    t = jnp.linspace(0.0, 1.0, l, dtype=F32)[:, None]
    bands = jnp.linspace(1e-4, FILTER_BANDS - 1, FILTER_BANDS, dtype=F32)
    step = 2.0 * math.pi / l
    ang_hi = bands[None, None, :] * (step * LANES * jnp.arange(l // LANES, dtype=F32))[:, None, None]
    ang_lo = bands[None, None, :] * (step * jnp.arange(LANES, dtype=F32))[None, :, None]
    cos_e = (jnp.cos(ang_hi) * jnp.cos(ang_lo) - jnp.sin(ang_hi) * jnp.sin(ang_lo)).reshape(l, FILTER_BANDS)
    sin_e = (jnp.sin(ang_hi) * jnp.cos(ang_lo) + jnp.cos(ang_hi) * jnp.sin(ang_lo)).reshape(l, FILTER_BANDS)
    emb = jnp.concatenate([t, cos_e, -sin_e], axis=-1)
    emb = jnp.pad(emb, ((0, 0), (0, LANES - emb.shape[1])))
    w1p = jnp.pad(w1, ((0, LANES - w1.shape[0]), (0, 0)))
    rows = l // 2
    eye = jnp.eye(2, dtype=F32)
    emb_p = jnp.concatenate([emb[:rows], emb[rows:]], axis=1)

    def side_columns(tcol):
        return jnp.pad(tcol, ((0, 0), (hid, LANES - hid - 1))).reshape(2, rows, LANES)

    bm = min(rows, FILTER_ROWS)
    nb = rows // bm
    flip = jnp.eye(bm, dtype=F32)[::-1]

    def full(shape):
        return pl.BlockSpec(shape, lambda i: (0,) * len(shape))

    fwd_spec = pl.BlockSpec((2, bm, LANES), lambda i: (0, i, 0))
    rev_spec = pl.BlockSpec((2, bm, LANES), lambda i: (0, nb - 1 - i, 0))
    tab, tab_rev = pl.pallas_call(
        _filter_mlp_body,
        grid=(nb,),
        in_specs=[pl.BlockSpec((bm, 2 * LANES), lambda i: (i, 0)), fwd_spec, rev_spec, full((bm, bm)),
                  full((2 * LANES, LANES)), full((1, LANES)), full((LANES, LANES)), full((1, LANES)),
                  full((LANES, LANES)), full((1, LANES)), full((3, LANES))],
        out_specs=[fwd_spec, rev_spec],
        out_shape=[jax.ShapeDtypeStruct((2, rows, LANES), F32)] * 2,
        compiler_params=_cparams("arbitrary"),
        name="hyena_filter_mlp",
    )(emb_p, side_columns(t), side_columns(t[::-1]), flip,
      jnp.kron(eye, w1p), jnp.tile(b1, 2).reshape(1, LANES), jnp.kron(eye, w2),
      jnp.tile(b2, 2).reshape(1, LANES), jnp.kron(eye, w3), jnp.tile(b3, 2).reshape(1, LANES),
      jnp.tile(freq, (1, 2)))
    return jnp.concatenate([tab.reshape(l, LANES), jnp.zeros((1, LANES), F32), tab_rev.reshape(l, LANES)[:l - 1]],
                           axis=0)


def _decay_rates(d):
    return jnp.abs(jnp.linspace(math.log(DECAY_TARGET) / LONG_DECAY_PCT,
                                math.log(DECAY_TARGET) / SHORT_DECAY_PCT, d, dtype=F32))[None, :]


def _split_bf16(x):
    hi = x.astype(BF16)
    return hi, (x - hi.astype(F32)).astype(BF16)


def _stack_3pass(w):
    w_hi, w_lo = _split_bf16(w)
    return jnp.concatenate([w_hi, w_hi, w_lo], axis=0)


def _dot_3pass(f, w3):
    f_hi, f_lo = _split_bf16(f)
    return jnp.dot(jnp.concatenate([f_hi, f_lo, f_hi], axis=1), w3, preferred_element_type=F32)


def _filter_rows(ft, wa3, wb3, dl_ref):
    hid = wa3.shape[0] // 3
    half = ft.shape[0] // 2
    f, t = ft[:, :hid], ft[:, hid:hid + 1]
    y = jnp.concatenate([_dot_3pass(f[:half], wa3), _dot_3pass(f[half:], wb3)], axis=0)
    return y * jnp.exp(-t * dl_ref[...])


def _outer_stage(fa_ref, cols, o_ref):
    ys = [jnp.dot(fa_ref[...], xj, preferred_element_type=F32).astype(BF16) for xj in cols]
    o_ref[...] = pltpu.einshape("brd->rbd", jnp.stack(ys))


def _dft_a_body(x_ref, fa_ref, o_ref):
    xb = pltpu.einshape("abd->bad", x_ref[...])
    _outer_stage(fa_ref, [xb[j] for j in range(xb.shape[0])], o_ref)


def _dft_a_filter_body(ft_ref, wa_ref, wb_ref, dl_ref, fa_ref, o_ref):
    wa3, wb3 = _stack_3pass(wa_ref[...]), _stack_3pass(wb_ref[...])
    cols = [_filter_rows(ft_ref[j], wa3, wb3, dl_ref).astype(BF16) for j in range(ft_ref.shape[0])]
    _outer_stage(fa_ref, cols, o_ref)


def _dft_a(x3, fa, d_chunk):
    a, b, d = x3.shape
    rows = fa.shape[0]
    return pl.pallas_call(
        _dft_a_body,
        grid=(b // HALO, d // d_chunk),
        in_specs=[pl.BlockSpec((a, HALO, d_chunk), lambda i, j: (0, i, j)),
                  pl.BlockSpec((rows, a), lambda i, j: (0, 0))],
        out_specs=pl.BlockSpec((rows, HALO, d_chunk), lambda i, j: (0, i, j)),
        out_shape=jax.ShapeDtypeStruct((rows, b, d), BF16),
        compiler_params=_cparams("arbitrary", "arbitrary"),
        name="dft_outer",
    )(x3, fa)


def _dft_a_filter(ft3, w4, deltas, fa, d_chunk):
    b, n1, _ = ft3.shape
    hid, d2 = w4.shape
    d = d2 // 2
    rows = fa.shape[0]
    nd = d // d_chunk
    return pl.pallas_call(
        _dft_a_filter_body,
        grid=(b // HALO, nd),
        in_specs=[pl.BlockSpec((HALO, n1, LANES), lambda i, j: (i, 0, 0)),
                  pl.BlockSpec((hid, d_chunk), lambda i, j: (0, j)),
                  pl.BlockSpec((hid, d_chunk), lambda i, j: (0, j + nd)),
                  pl.BlockSpec((1, d_chunk), lambda i, j: (0, j)),
                  pl.BlockSpec((rows, n1), lambda i, j: (0, 0))],
        out_specs=pl.BlockSpec((rows, HALO, d_chunk), lambda i, j: (0, i, j)),
        out_shape=jax.ShapeDtypeStruct((rows, b, d), BF16),
        compiler_params=_cparams("arbitrary", "arbitrary"),
        name="dft_outer_filter",
    )(ft3, w4, w4, deltas, fa)


def _dft_inner_body(y_ref, yf_ref, gf_ref, o_ref):
    b, dc = y_ref.shape[2], y_ref.shape[3]
    for ci in range(y_ref.shape[1]):
        z = jnp.dot(gf_ref[ci], y_ref[:, ci].reshape(2 * b, dc), preferred_element_type=F32)
        hh = jnp.dot(gf_ref[ci], yf_ref[:, ci].reshape(2 * b, dc), preferred_element_type=F32)
        zr, zi, hr, hi = z[:b], z[b:], hh[:b], hh[b:]
        p = jnp.concatenate([zr * hr - zi * hi, zr * hi + zi * hr], axis=0).astype(BF16)
        cc = lax.dot_general(gf_ref[ci], p, (((0,), (0,)), ((), ())), preferred_element_type=F32)
        o_ref[:, ci] = cc.reshape(2, b, dc).astype(o_ref.dtype)


def _dft_inner(y4, yf4, gf):
    _, nc, b, d = y4.shape
    cblk = SUBLANES
    dc = min(d, DFT_INNER_COLS)
    yspec = pl.BlockSpec((2, cblk, b, dc), lambda c, j: (0, c, 0, j))
    return pl.pallas_call(
        _dft_inner_body,
        grid=(nc // cblk, d // dc),
        in_specs=[yspec, yspec, pl.BlockSpec((cblk, 2 * b, 2 * b), lambda c, j: (c, 0, 0))],
        out_specs=yspec,
        out_shape=jax.ShapeDtypeStruct(y4.shape, BF16),
        compiler_params=_cparams("arbitrary", "arbitrary"),
        name="dft_inner",
    )(y4, yf4, gf)


def _dft_c_body(c_ref, m_ref, xin_ref, x0_ref, skip_ref, o_ref):
    cb = pltpu.einshape("rbd->brd", c_ref[...])
    outs = [jnp.dot(m_ref[...], cb[j], preferred_element_type=F32).astype(BF16) for j in range(cb.shape[0])]
    conv = pltpu.einshape("bad->abd", jnp.stack(outs)).astype(F32)
    gated = x0_ref[...].astype(F32) * (conv + xin_ref[...].astype(F32) * skip_ref[...][None])
    o_ref[...] = gated.astype(o_ref.dtype)


def _dft_c(c3, mc, xin3, x03, skip, d_chunk):
    rows, b, d = c3.shape
    a = mc.shape[0]
    xspec = pl.BlockSpec((a, HALO, d_chunk), lambda i, j: (0, i, j))
    return pl.pallas_call(
        _dft_c_body,
        grid=(b // HALO, d // d_chunk),
        in_specs=[pl.BlockSpec((rows, HALO, d_chunk), lambda i, j: (0, i, j)),
                  pl.BlockSpec((a, rows), lambda i, j: (0, 0)),
                  xspec, xspec,
                  pl.BlockSpec((1, d_chunk), lambda i, j: (0, j))],
        out_specs=xspec,
        out_shape=jax.ShapeDtypeStruct((a, b, d), BF16),
        compiler_params=_cparams("arbitrary", "arbitrary"),
        name="dft_outer_inv",
    )(c3, mc, xin3, x03, skip)


def _cis(num, period):
    ang = (2.0 * math.pi / period) * num.astype(F32)
    return jnp.cos(ang), jnp.sin(ang)


def _long_conv_gated(xin, x0, ftab, w4, skip):
    l, d = xin.shape
    b = DFT_INNER
    a = l // b
    n1, n = 2 * a, 2 * l
    nc = n1 // 2 + SUBLANES
    dc = min(d, DFT_OUTER_COLS)
    ci = jnp.arange(nc, dtype=jnp.int32)
    ai = jnp.arange(n1, dtype=jnp.int32)
    live = (ci <= n1 // 2).astype(F32)[:, None]
    cr, sr = _cis((ci[:, None] * ai[None, :]) % n1, n1)
    cr, sr = cr * live, sr * live
    fa_full = jnp.concatenate([cr, -sr], axis=0).astype(BF16)
    fa = fa_full[:, :a]
    fold = jnp.where((ci == 0) | (ci == n1 // 2), 1.0, 2.0)[None, :] / n
    mc = jnp.concatenate([cr[:, :a].T * fold, -sr[:, :a].T * fold], axis=1).astype(BF16)
    ei = jnp.arange(b, dtype=jnp.int32)
    pr, ps = _cis((ci[:, None] * ei[None, :]) % n, n)
    qr, qs = _cis((ei[:, None] * ei[None, :]) % b, b)
    tr = pr[:, None, :] * qr[None] - ps[:, None, :] * qs[None]
    ti = -(pr[:, None, :] * qs[None] + ps[:, None, :] * qr[None])
    gf = jnp.concatenate([jnp.concatenate([tr, -ti], axis=2),
                          jnp.concatenate([ti, tr], axis=2)], axis=1).astype(BF16)

    ft3 = jnp.swapaxes(ftab.reshape(n1, b, LANES), 0, 1)
    hf = _dft_a_filter(ft3, w4, _decay_rates(d), fa_full, dc)
    yx = _dft_a(xin.reshape(a, b, d), fa, dc)
    c4 = _dft_inner(yx.reshape(2, nc, b, d), hf.reshape(2, nc, b, d), gf)
    y3 = _dft_c(c4.reshape(2 * nc, b, d), mc, xin.reshape(a, b, d), x0.reshape(a, b, d), skip.reshape(1, d), dc)
    return y3.reshape(l, d)


def _ctx_conv_body(x_ref, x0_ref, ft_ref, wa_ref, wb_ref, dl_ref, skip_ref, ff_ref, fh_ref, mi_ref, o_ref):
    x = x_ref[...].astype(F32)
    n = ft_ref.shape[0]
    filt = _filter_rows(ft_ref[...], _stack_3pass(wa_ref[...]), _stack_3pass(wb_ref[...]), dl_ref)
    xs = jnp.dot(ff_ref[...], x.astype(BF16), preferred_element_type=F32)
    hs = jnp.dot(fh_ref[...], filt.astype(BF16), preferred_element_type=F32)
    xr, xi, hr, hi = xs[:n], xs[n:], hs[:n], hs[n:]
    p = jnp.concatenate([xr * hr - xi * hi, xr * hi + xi * hr], axis=0).astype(BF16)
    conv = jnp.dot(mi_ref[...], p, preferred_element_type=F32)
    o_ref[...] = (x0_ref[...].astype(F32) * (conv + x * skip_ref[...])).astype(o_ref.dtype)


def _short_seq_conv_gated(xin, x0, ftab, w4, skip):
    l, d = xin.shape
    n = 2 * l
    hid = w4.shape[0]
    ni = jnp.arange(n, dtype=jnp.int32)
    cr, sr = _cis((ni[:, None] * ni[None, :]) % n, n)
    fh = jnp.concatenate([cr, -sr], axis=0).astype(BF16)
    ff = fh[:, :l]
    mi = (jnp.concatenate([cr[:l], -sr[:l]], axis=1) * (1.0 / n)).astype(BF16)
    dc = min(d, DFT_OUTER_COLS)
    nd = d // dc
    return pl.pallas_call(
        _ctx_conv_body,
        grid=(nd,),
        in_specs=[pl.BlockSpec((l, dc), lambda j: (0, j)),
                  pl.BlockSpec((l, dc), lambda j: (0, j)),
                  pl.BlockSpec((n, LANES), lambda j: (0, 0)),
                  pl.BlockSpec((hid, dc), lambda j: (0, j)),
                  pl.BlockSpec((hid, dc), lambda j: (0, j + nd)),
                  pl.BlockSpec((1, dc), lambda j: (0, j)),
                  pl.BlockSpec((1, dc), lambda j: (0, j)),
                  pl.BlockSpec((2 * n, l), lambda j: (0, 0)),
                  pl.BlockSpec((2 * n, n), lambda j: (0, 0)),
                  pl.BlockSpec((l, 2 * n), lambda j: (0, 0))],
        out_specs=pl.BlockSpec((l, dc), lambda j: (0, j)),
        out_shape=jax.ShapeDtypeStruct((l, d), BF16),
        compiler_params=_cparams("arbitrary"),
        name="ctx_conv",
    )(xin, x0, ftab, w4, w4, _decay_rates(d), skip.reshape(1, d), ff, fh, mi)


def _hyena_mixer(h, vecs, p, long_seq, cast_next=None):
    (w_in_all, layer), b_in, w_sc, b_sc, f_w1, f_b1, f_w2, f_b2, f_w3, f_b3, f_w4, f_freq, skip, w_out, b_out = p
    l = h.shape[0]
    x0, xin = _hy_in_gate(h, vecs, w_in_all, layer, b_in, w_sc, b_sc)
    ftab = _filter_table(l, f_w1, f_b1, f_w2, f_b2, f_w3, f_b3, f_freq)
    conv = _long_conv_gated if long_seq else _short_seq_conv_gated
    y = conv(xin, x0, ftab, f_w4, skip)
    return _proj_out(y, w_out, b_out, h, vecs, cast_next)


def _attn_body(sink_ref, q_ref, kp_ref, kc_ref, kn_ref, vp_ref, vc_ref, vn_ref, kx_ref, vx_ref, bias_ref, o_ref,
               *, group):
    blk = q_ref.shape[0]
    rid = lax.broadcasted_iota(jnp.int32, (group * blk, 1), 0)
    for kh in range(N_KV_HEADS):
        hs = slice(kh * HEAD_DIM, (kh + 1) * HEAD_DIM)
        heads = [kh * group + g for g in range(group)]
        q = jnp.concatenate([q_ref[:, hd * HEAD_DIM:(hd + 1) * HEAD_DIM] for hd in heads], axis=0)
        keys = jnp.concatenate([kp_ref[:, hs], kc_ref[:, hs], kn_ref[:, hs], kx_ref[:, hs]], axis=0)
        vals = jnp.concatenate([vp_ref[:, hs], vc_ref[:, hs], vn_ref[:, hs], vx_ref[:, hs]], axis=0)
        s = lax.dot_general(q, keys, (((1,), (1,)), ((), ())), preferred_element_type=F32)
        pieces = [s[:, :blk] + bias_ref[0, :, :blk], s[:, blk:2 * blk], s[:, 2 * blk:3 * blk] + bias_ref[0, :, blk:]]
        pieces += [s[:, c0:c0 + blk] for c0 in range(3 * blk, s.shape[1], blk)]
        sink = jnp.zeros((group * blk, 1), F32)
        for g, hd in enumerate(heads):
            sink = jnp.where((rid >= g * blk) & (rid < (g + 1) * blk), sink_ref[hd] * LOG2_E, sink)
        top = functools.reduce(jnp.maximum, pieces)
        mx = jnp.maximum(jnp.max(top, axis=-1, keepdims=True), sink)
        probs = [jnp.exp2(pc - mx) for pc in pieces]
        denom = jnp.sum(functools.reduce(jnp.add, probs), axis=-1, keepdims=True) + jnp.exp2(sink - mx)
        pr = jnp.concatenate([pp.astype(BF16) for pp in probs], axis=1)
        o = jnp.dot(pr, vals, preferred_element_type=F32) / denom
        for g, hd in enumerate(heads):
            o_ref[:, hd * HEAD_DIM:(hd + 1) * HEAD_DIM] = o[g * blk:(g + 1) * blk].astype(o_ref.dtype)


def _attention(qkv, kvc, sink, d):
    l = qkv.shape[0]
    c = kvc.shape[0]
    blk = ATTN_BLOCK
    nb = l // blk
    group = d // HEAD_DIM // N_KV_HEADS
    kvw = N_KV_HEADS * HEAD_DIM
    kcol = d // kvw
    qi = jnp.arange(group * blk, dtype=jnp.int32)[:, None] % blk
    ki = jnp.arange(blk, dtype=jnp.int32)[None, :]
    prev_ok, next_ok, never = ki >= qi, ki <= qi, jnp.zeros((group * blk, blk), bool)
    variants = [(never, next_ok), (prev_ok, next_ok), (prev_ok, never)]
    bias = jnp.stack([jnp.where(jnp.concatenate(v, axis=1), 0.0, MASK_BIAS).astype(F32) for v in variants])

    def kv_spec(col, shift):
        return pl.BlockSpec((blk, kvw), lambda n: (jnp.clip(n + shift, 0, nb - 1), col))

    return pl.pallas_call(
        functools.partial(_attn_body, group=group),
        grid=(nb,),
        in_specs=[pl.BlockSpec(memory_space=pltpu.SMEM),
                  pl.BlockSpec((blk, d), lambda n: (n, 0)),
                  kv_spec(kcol, -1), kv_spec(kcol, 0), kv_spec(kcol, 1),
                  kv_spec(kcol + 1, -1), kv_spec(kcol + 1, 0), kv_spec(kcol + 1, 1),
                  pl.BlockSpec((c, kvw), lambda n: (0, 0)),
                  pl.BlockSpec((c, kvw), lambda n: (0, 1)),
                  pl.BlockSpec((1, group * blk, 2 * blk),
                               lambda n: (jnp.where(n == 0, 0, jnp.where(n == nb - 1, 2, 1)), 0, 0))],
        out_specs=pl.BlockSpec((blk, d), lambda n: (n, 0)),
        out_shape=jax.ShapeDtypeStruct((l, d), BF16),
        compiler_params=_cparams("arbitrary"),
        name="window_attn",
    )(sink, qkv, qkv, qkv, qkv, qkv, qkv, qkv, kvc, kvc, bias)


def _rope_tables(l):
    rows = l // GRID_W
    pairs = HEAD_DIM // 4
    inv_freq = ROPE_THETA ** (-jnp.arange(pairs, dtype=F32) / pairs)
    ang_row = jnp.arange(rows).astype(F32)[:, None] * inv_freq[None, :]
    ang_col = jnp.arange(GRID_W).astype(F32)[:, None] * inv_freq[None, :]

    def per_token(row_tab, col_tab):
        by_row = jnp.broadcast_to(row_tab[:, None, :], (rows, GRID_W, pairs)).reshape(l, pairs)
        by_col = jnp.broadcast_to(col_tab[None, :, :], (rows, GRID_W, pairs)).reshape(l, pairs)
        return by_row, by_col

    cos_r, cos_c = per_token(jnp.cos(ang_row), jnp.cos(ang_col))
    sin_r, sin_c = per_token(jnp.sin(ang_row), jnp.sin(ang_col))
    zeros = jnp.zeros_like(cos_r)
    cos = jnp.concatenate([cos_r, cos_r, cos_c, cos_c], axis=-1)
    sin_a = jnp.concatenate([-sin_r, zeros, -sin_c, zeros], axis=-1)
    sin_b = jnp.concatenate([zeros, sin_r, zeros, sin_c], axis=-1)
    return cos, sin_a, sin_b


def _pool_body(hc_ref, hp_ref, hn_ref, vec_ref, w_ref, b_ref, sc_ref, *rest, seq_len, cast):
    if cast:
        ci_ref, co_ref, o_ref, cib_ref, cob_ref, y_ref = rest
        cib_ref[...] = ci_ref[...].astype(cib_ref.dtype)
        cob_ref[...] = co_ref[...].astype(cob_ref.dtype)
    else:
        o_ref, y_ref = rest
    i = pl.program_id(0)
    last = pl.num_programs(0) - 1
    bm, d = hc_ref.shape
    gw = d // len(POOL_SIZES)
    h = hc_ref[...]
    u = _modulated(h, vec_ref)
    up = jnp.where(i > 0, _modulated(hp_ref[...], vec_ref), 0.0)
    un = jnp.where(i < last, _modulated(hn_ref[...], vec_ref), 0.0)
    ext_rows = bm + 2 * SUBLANES
    t = i * bm + lax.broadcasted_iota(jnp.int32, (bm, 1), 0)
    for g, size in enumerate(POOL_SIZES):
        cols = slice(g * gw, (g + 1) * gw)
        ext = jnp.concatenate([up[:, cols], u[:, cols], un[:, cols]], axis=0)
        acc, span = ext, 1
        while span < size:
            acc = acc + pltpu.roll(acc, ext_rows - span, 0)
            span *= 2
        start = SUBLANES - size // 2
        win = pltpu.roll(acc, ext_rows - start, 0)[:bm] if start else acc[:bm]
        lo = jnp.clip(t - size // 2, 0, seq_len)
        hi = jnp.clip(t - size // 2 + size, 0, seq_len)
        part = win / (hi - lo).astype(F32) - u[:, cols]
        yg = jnp.dot(part.astype(BF16), w_ref[g], preferred_element_type=F32)
        y_ref[:, cols] = (yg + b_ref[:, cols]) * sc_ref[:, cols]
    o_ref[...] = h + vec_ref[2:3, :] * _rms(y_ref[...], vec_ref[4:5, :])


def _pool_mixer(h, vecs, w, b, scale, cast_next=None):
    l, d = h.shape
    bm = min(l, FFN_ROWS)
    rb = bm // SUBLANES
    nrow8 = l // SUBLANES
    ng, gw = w.shape[0], w.shape[1]
    in_specs = [pl.BlockSpec((bm, d), lambda i: (i, 0)),
                pl.BlockSpec((SUBLANES, d), lambda i: (jnp.maximum(i * rb - 1, 0), 0)),
                pl.BlockSpec((SUBLANES, d), lambda i: (jnp.minimum((i + 1) * rb, nrow8 - 1), 0)),
                pl.BlockSpec((SUBLANES, d), lambda i: (0, 0)),
                pl.BlockSpec((ng, gw, gw), lambda i: (0, 0, 0)),
                pl.BlockSpec((1, d), lambda i: (0, 0)),
                pl.BlockSpec((1, d), lambda i: (0, 0))]
    out_specs = [pl.BlockSpec((bm, d), lambda i: (i, 0))]
    out_shape = [jax.ShapeDtypeStruct((l, d), F32)]
    args = [h, h, h, vecs, w.astype(BF16), b.reshape(1, d), scale.reshape(1, d)]
    if cast_next is not None:
        c_in, c_out, c_shape, c_args = _row_slab_cast(cast_next, l // bm)
        in_specs, out_specs, out_shape, args = in_specs + c_in, out_specs + c_out, out_shape + c_shape, args + c_args
    outs = pl.pallas_call(
        functools.partial(_pool_body, seq_len=l, cast=cast_next is not None),
        grid=(l // bm,),
        in_specs=in_specs,
        out_specs=out_specs,
        out_shape=out_shape,
        scratch_shapes=[pltpu.VMEM((bm, d), F32)],
        compiler_params=_cparams("arbitrary"),
        name="pool_mixer",
    )(*args)
    return outs if cast_next is not None else outs[0]


def _sub_vecs(mod, k, g_pre, g_post):
    rows = [mod[3 * k], mod[3 * k + 1], mod[3 * k + 2], g_pre, g_post]
    return jnp.stack(rows + [jnp.zeros_like(g_pre)] * (SUBLANES - len(rows)))


def kernel(x, c, ctx, c_ctx, w_ada, b_ada, norm_pre, norm_post, w_ffn_in, w_ffn_out, hy_w_in, hy_b_in, hy_w_sc, hy_b_sc, hy_f_w1, hy_f_b1, hy_f_w2, hy_f_b2, hy_f_w3, hy_f_b3, hy_f_w4, hy_f_freq, hy_skip, hy_w_out, hy_b_out, at_w_qkv, at_b_qkv, at_sink, at_w_o, at_b_o, pl_w, pl_b, pl_scale):
    bsz, l, d = x.shape
    assert bsz == 1, "kernel handles a single batch element"
    assert d % DFT_INNER_COLS == 0 and l % PROJ_ROWS == 0 and ctx.shape[1] % SMALL_ROWS == 0, \
        "channel and token counts must be multiples of the block sizes"
    assert w_ffn_out.shape[2] % FFN_CHUNK == 0, "FFN width must be a multiple of the hidden chunk"
    depth = w_ada.shape[0]
    n_mixers = 3
    attn_layers = [i for i in range(depth) if i % n_mixers == 1]
    last_ctx_layer = attn_layers[-1] if attn_layers else -1

    mods = _adaln_mods(c, c_ctx, w_ada, b_ada).reshape(depth, 2, N_MOD, d)
    rope = _rope_tables(l)
    w_next = [w_ffn_in[0, 0].astype(BF16), w_ffn_out[0, 0].astype(BF16)]
    hy_w_in_bf = hy_w_in.astype(BF16)
    h, hc = x[0], ctx[0]
    for i in range(depth):
        kind, j = i % n_mixers, i // n_mixers
        ctx_live = i <= last_ctx_layer
        ctx_out = i < last_ctx_layer
        vec = [_sub_vecs(mods[i, 0], k, norm_pre[i, k], norm_post[i, k]) for k in range(3)]
        vec_c = [_sub_vecs(mods[i, 1], k, norm_pre[i, k], norm_post[i, k]) for k in range(3)]

        w_now = w_next
        h = _ffn(h, vec[0], *w_now, FFN_RES)
        if ctx_live:
            hc = _ffn(hc, vec_c[0], *w_now, FFN_RES)
        cast_mid = (w_ffn_in, w_ffn_out, i, 1)

        if kind == 0:
            hp = ((hy_w_in_bf, j), hy_b_in[j], hy_w_sc[j], hy_b_sc[j], hy_f_w1[j], hy_f_b1[j],
                  hy_f_w2[j], hy_f_b2[j], hy_f_w3[j], hy_f_b3[j], hy_f_w4[j], hy_f_freq[j], hy_skip[j],
                  hy_w_out[j].astype(BF16), hy_b_out[j])
            h, *w_next = _hyena_mixer(h, vec[1], hp, long_seq=True, cast_next=cast_mid)
            if ctx_out:
                hc = _hyena_mixer(hc, vec_c[1], hp, long_seq=False)
        elif kind == 1:
            w_qkv = at_w_qkv[j].astype(BF16)
            qkv = _proj_in(h, vec[1], w_qkv, at_b_qkv[j], BF16, rope=rope,
                           q_blocks=d // PROJ_COLS, q_scale=LOG2_E * HEAD_DIM ** -0.5)
            kvc = _proj_in(hc, vec_c[1], w_qkv[:, d:], at_b_qkv[j][d:], BF16)
            o = _attention(qkv, kvc, at_sink[j], d)
            h, *w_next = _proj_out(o, at_w_o[j].astype(BF16), at_b_o[j], h, vec[1], cast_mid)
            assert not ctx_out, "context-query attention path is not needed for this depth"
        else:
            h, *w_next = _pool_mixer(h, vec[1], pl_w[j], pl_b[j], pl_scale[j], cast_mid)
            assert not ctx_out, "context pooling path is not needed for this depth"

        w_now = w_next
        if i + 1 < depth:
            h, *w_next = _ffn(h, vec[2], *w_now, FFN_RES, cast_next=(w_ffn_in, w_ffn_out, i + 1, 0))
        else:
            h = _ffn(h, vec[2], *w_now, FFN_RES)
        if ctx_out:
            hc = _ffn(hc, vec_c[2], *w_now, FFN_RES)
    return h[None]
```

```python
import functools
import math

import jax
import jax.numpy as jnp
from jax import lax
from jax.experimental import pallas as pl
from jax.experimental.pallas import tpu as pltpu

F32 = jnp.float32
BF16 = jnp.bfloat16

GRID_W = 64
N_MOD = 9
NORM_EPS = 1e-6
FFN_RES = 0.5
FILTER_BANDS = 16
FILTER_EMB = 1 + 2 * FILTER_BANDS
DECAY_TARGET = 1e-2
SHORT_DECAY_PCT = 0.3
LONG_DECAY_PCT = 1.5
HEAD_DIM = 128
N_KV_HEADS = 4
WINDOW = 128
ATTN_BLOCK = 128
ROPE_THETA = 10000.0
POOL_SIZES = (2, 4, 8, 16)

VMEM_LIMIT_BYTES = 56 * 1024 * 1024
SUBLANES = 8
LANES = 128
DFT_INNER = 128
HALO = 16
FFN_CHUNK = 512
FFN_ROWS = 512
PROJ_ROWS = 1024
PROJ_COLS = 512
HYENA_IN_COLS = 256
DFT_OUTER_COLS = 512
DFT_INNER_COLS = 1024
SMALL_ROWS = 256
MODS_COLS = 1024
FILTER_ROWS = 512
MASK_BIAS = -1e30
LOG2_E = math.log2(math.e)


def _cparams(*sem):
    return pltpu.CompilerParams(dimension_semantics=sem, vmem_limit_bytes=VMEM_LIMIT_BYTES)


def _rms(x, g):
    ms = jnp.mean(x * x, axis=-1, keepdims=True)
    return x * lax.rsqrt(ms + NORM_EPS) * g


def _modulated(h, vec_ref):
    return _rms(h, vec_ref[3:4, :]) * (1.0 + vec_ref[1:2, :]) + vec_ref[0:1, :]


def _store_modulated(h_ref, vec_ref, u_ref):
    gain = vec_ref[3:4, :] * (1.0 + vec_ref[1:2, :])
    u_ref[...] = (_rms(h_ref[...], gain) + vec_ref[0:1, :]).astype(u_ref.dtype)


def _mods_body(cb_ref, w_ref, b_ref, o_ref, s_ref):
    k_dim, bn = w_ref.shape[1], w_ref.shape[2]
    nl = bn // LANES

    @pl.when((pl.program_id(0) == 0) & (pl.program_id(1) == 0))
    def _():
        cv = cb_ref[...]
        s_ref[...] = cv * jax.nn.sigmoid(cv)

    def step(kg, acc):
        rows = pl.ds(pl.multiple_of(kg * SUBLANES, SUBLANES), SUBLANES)
        s = [s_ref[r, rows, :] for r in range(2)]
        new = list(acc)
        for j in range(nl):
            wv = w_ref[0, rows, j * LANES:(j + 1) * LANES]
            for r in range(2):
                new[r * nl + j] = acc[r * nl + j] + wv * s[r]
        return tuple(new)

    init = tuple(jnp.zeros((SUBLANES, LANES), F32) for _ in range(2 * nl))
    acc = lax.fori_loop(0, k_dim // SUBLANES, step, init, unroll=4)
    for r in range(2):
        for j in range(nl):
            cols = slice(j * LANES, (j + 1) * LANES)
            o_ref[0, r:r + 1, cols] = jnp.sum(acc[r * nl + j], axis=0, keepdims=True) + b_ref[0, :, cols]


def _adaln_mods(c, c_ctx, w_ada, b_ada):
    depth, d, n = w_ada.shape
    bn = MODS_COLS
    cb = jnp.broadcast_to(jnp.stack([c[0], c_ctx])[:, :, None], (2, d, LANES))
    return pl.pallas_call(
        _mods_body,
        grid=(depth, n // bn),
        in_specs=[pl.BlockSpec((2, d, LANES), lambda i, j: (0, 0, 0)),
                  pl.BlockSpec((1, d, bn), lambda i, j: (i, 0, j)),
                  pl.BlockSpec((1, 1, bn), lambda i, j: (i, 0, j))],
        out_specs=pl.BlockSpec((1, 2, bn), lambda i, j: (i, 0, j)),
        out_shape=jax.ShapeDtypeStruct((depth, 2, n), F32),
        scratch_shapes=[pltpu.VMEM((2, d, LANES), F32)],
        compiler_params=_cparams("arbitrary", "arbitrary"),
        name="adaln_mods",
    )(cb, w_ada, b_ada.reshape(depth, 1, n))


def _ffn_body(h_ref, vec_ref, wg_ref, wu_ref, wo_ref, *rest, res_w, cast):
    if cast:
        ci_ref, co_ref, o_ref, cib_ref, cob_ref, u_ref, acc_ref = rest
    else:
        o_ref, u_ref, acc_ref = rest
    j = pl.program_id(1)

    @pl.when((pl.program_id(0) == 0) & (j == 0))
    def _():
        acc_ref[...] = jnp.zeros_like(acc_ref)

    @pl.when(j == 0)
    def _():
        _store_modulated(h_ref, vec_ref, u_ref)

    u = u_ref[...]
    half = wg_ref.shape[1] // 2
    down = None
    for cols in (slice(0, half), slice(half, 2 * half)):
        g = jnp.dot(u, wg_ref[:, cols], preferred_element_type=F32)
        p = jnp.dot(u, wu_ref[:, cols], preferred_element_type=F32)
        a = (g * jax.nn.sigmoid(g) * p).astype(BF16)
        t = jnp.dot(a, wo_ref[cols, :], preferred_element_type=F32)
        down = t if down is None else down + t
    carry = jnp.where(j > 0, 1.0, 0.0).astype(F32)
    acc_ref[...] = acc_ref[...] * carry + down

    if cast:
        cib_ref[...] = ci_ref[...].astype(cib_ref.dtype)
        cob_ref[...] = co_ref[...].astype(cob_ref.dtype)

    @pl.when(j == pl.num_programs(1) - 1)
    def _():
        gate = (res_w * vec_ref[2:3, :]) * vec_ref[4:5, :]
        o_ref[...] = h_ref[...] + _rms(acc_ref[...], gate)


def _cast_plan(n_rows, n_cols, steps):
    for n_col_blocks in range(1, steps + 1):
        if steps % n_col_blocks or n_cols % n_col_blocks or n_rows % (steps // n_col_blocks):
            continue
        br, bc = n_rows // (steps // n_col_blocks), n_cols // n_col_blocks
        if br % HALO == 0 and bc % LANES == 0:
            return br, bc, n_col_blocks
    raise ValueError("no tiling of the weight cast fits this grid")


def _ffn(h, vecs, w_in, w_out, res_w, cast_next=None):
    m, d = h.shape
    f = w_out.shape[0]
    bm = min(m, FFN_ROWS)
    bf = FFN_CHUNK
    nf = f // bf
    in_specs = [pl.BlockSpec((bm, d), lambda i, j: (i, 0)),
                pl.BlockSpec((SUBLANES, d), lambda i, j: (0, 0)),
                pl.BlockSpec((d, bf), lambda i, j: (0, j)),
                pl.BlockSpec((d, bf), lambda i, j: (0, j + nf)),
                pl.BlockSpec((bf, d), lambda i, j: (j, 0))]
    out_specs = [pl.BlockSpec((bm, d), lambda i, j: (i, 0))]
    out_shape = [jax.ShapeDtypeStruct((m, d), F32)]
    args = [h, vecs, w_in, w_in, w_out]
    if cast_next is not None:
        *srcs, layer, slot = cast_next
        for w in srcs:
            br, bc, ncb = _cast_plan(w.shape[2], w.shape[3], (m // bm) * nf)
            if ncb == nf:
                where = lambda i, j: (i, j)
            elif ncb == 1:
                where = lambda i, j: (i * nf + j, 0)
            else:
                where = functools.partial(lambda i, j, n: ((i * nf + j) // n, (i * nf + j) % n), n=ncb)
            in_specs.append(pl.BlockSpec((None, None, br, bc),
                                         functools.partial(lambda i, j, f: (layer, slot) + f(i, j), f=where)))
            out_specs.append(pl.BlockSpec((br, bc), where))
            out_shape.append(jax.ShapeDtypeStruct(w.shape[2:], BF16))
            args.append(w)
    outs = pl.pallas_call(
        functools.partial(_ffn_body, res_w=res_w, cast=cast_next is not None),
        grid=(m // bm, nf),
        in_specs=in_specs,
        out_specs=out_specs,
        out_shape=out_shape,
        scratch_shapes=[pltpu.VMEM((bm, d), BF16), pltpu.VMEM((bm, d), F32)],
        compiler_params=_cparams("arbitrary", "arbitrary"),
        name="ffn",
    )(*args)
    return outs if cast_next is not None else outs[0]


def _proj_in_body(h_ref, vec_ref, w_ref, b_ref, *rest, rope_blocks, q_blocks, q_scale):
    if rope_blocks:
        cos_ref, sa_ref, sb_ref, o_ref, u_ref = rest
    else:
        o_ref, u_ref = rest
    j = pl.program_id(1)

    @pl.when(j == 0)
    def _():
        _store_modulated(h_ref, vec_ref, u_ref)

    y = jnp.dot(u_ref[...], w_ref[...], preferred_element_type=F32) + b_ref[...]
    if not rope_blocks:
        o_ref[...] = y.astype(o_ref.dtype)
        return

    rot = jnp.where(j < q_blocks, q_scale, jnp.where(j < rope_blocks, 1.0, 0.0)).astype(F32)
    keep = jnp.where(j < rope_blocks, 0.0, 1.0).astype(F32)
    cos, sa, sb = cos_ref[...] * rot + keep, sa_ref[...] * rot, sb_ref[...] * rot
    for hd in range(y.shape[1] // HEAD_DIM):
        cols = slice(hd * HEAD_DIM, (hd + 1) * HEAD_DIM)
        xh = y[:, cols]
        r = (xh * cos + pltpu.roll(xh, HEAD_DIM - HEAD_DIM // 4, 1) * sa
             + pltpu.roll(xh, HEAD_DIM // 4, 1) * sb)
        o_ref[:, cols] = r.astype(o_ref.dtype)


def _proj_in(h, vecs, w, b, out_dtype, rope=None, q_blocks=0, q_scale=1.0):
    m, d = h.shape
    n = w.shape[1]
    bm = min(m, PROJ_ROWS)
    bn = PROJ_COLS
    in_specs = [pl.BlockSpec((bm, d), lambda i, j: (i, 0)),
                pl.BlockSpec((SUBLANES, d), lambda i, j: (0, 0)),
                pl.BlockSpec((d, bn), lambda i, j: (0, j)),
                pl.BlockSpec((1, bn), lambda i, j: (0, j))]
    args = [h, vecs, w, b.reshape(1, n)]
    rope_blocks = 0
    if rope is not None:
        rope_blocks = q_blocks + 1
        in_specs += [pl.BlockSpec((bm, HEAD_DIM), lambda i, j: (i, 0))] * 3
        args += list(rope)
    return pl.pallas_call(
        functools.partial(_proj_in_body, rope_blocks=rope_blocks, q_blocks=q_blocks, q_scale=q_scale),
        grid=(m // bm, n // bn),
        in_specs=in_specs,
        out_specs=pl.BlockSpec((bm, bn), lambda i, j: (i, j)),
        out_shape=jax.ShapeDtypeStruct((m, n), out_dtype),
        scratch_shapes=[pltpu.VMEM((bm, d), BF16)],
        compiler_params=_cparams("arbitrary", "arbitrary"),
        name="proj_in",
    )(*args)


def _row_slab_cast(cast_next, nb):
    *srcs, layer, slot = cast_next
    in_specs, out_specs, out_shape = [], [], []
    for w in srcs:
        rows, cols = w.shape[2] // nb, w.shape[3]
        assert rows * nb == w.shape[2] and rows % HALO == 0, "weight rows must split evenly over the row blocks"
        in_specs.append(pl.BlockSpec((None, None, rows, cols), lambda i: (layer, slot, i, 0)))
        out_specs.append(pl.BlockSpec((rows, cols), lambda i: (i, 0)))
        out_shape.append(jax.ShapeDtypeStruct(w.shape[2:], BF16))
    return in_specs, out_specs, out_shape, srcs


def _proj_out_body(a_ref, w_ref, b_ref, h_ref, vec_ref, *rest, cast):
    o_ref = rest[2] if cast else rest[0]
    half = a_ref.shape[0] // 2
    gate = vec_ref[2:3, :] * vec_ref[4:5, :]
    halves = [slice(0, half), slice(half, 2 * half)]
    ys = [jnp.dot(a_ref[rows, :], w_ref[...], preferred_element_type=F32) + b_ref[...] for rows in halves]
    if cast:
        ci_ref, co_ref, _, cib_ref, cob_ref = rest
        cib_ref[...] = ci_ref[...].astype(cib_ref.dtype)
        cob_ref[...] = co_ref[...].astype(cob_ref.dtype)
    for rows, y in zip(halves, ys):
        o_ref[rows, :] = h_ref[rows, :] + _rms(y, gate)


def _proj_out(a, w, b, h, vecs, cast_next=None):
    m, d = h.shape
    k = a.shape[1]
    bm = min(m, FFN_ROWS)
    in_specs = [pl.BlockSpec((bm, k), lambda i: (i, 0)),
                pl.BlockSpec((k, d), lambda i: (0, 0)),
                pl.BlockSpec((1, d), lambda i: (0, 0)),
                pl.BlockSpec((bm, d), lambda i: (i, 0)),
                pl.BlockSpec((SUBLANES, d), lambda i: (0, 0))]
    out_specs = [pl.BlockSpec((bm, d), lambda i: (i, 0))]
    out_shape = [jax.ShapeDtypeStruct((m, d), F32)]
    args = [a, w, b.reshape(1, d), h, vecs]
    if cast_next is not None:
        c_in, c_out, c_shape, c_args = _row_slab_cast(cast_next, m // bm)
        in_specs, out_specs, out_shape, args = in_specs + c_in, out_specs + c_out, out_shape + c_shape, args + c_args
    outs = pl.pallas_call(
        functools.partial(_proj_out_body, cast=cast_next is not None),
        grid=(m // bm,),
        in_specs=in_specs,
        out_specs=out_specs,
        out_shape=out_shape,
        compiler_params=_cparams("arbitrary"),
        name="proj_out",
    )(*args)
    return outs if cast_next is not None else outs[0]


def _hy_in_body(h_ref, hp_ref, hn_ref, vec_ref, w0_ref, w1_ref, w2_ref, b0_ref, b1_ref, b2_ref, wsc_ref, bsc_ref,
                x0_out, xin_out, u_ref, uh_ref):
    i = pl.program_id(0)
    last = pl.num_programs(0) - 1

    @pl.when(pl.program_id(1) == 0)
    def _():
        _store_modulated(h_ref, vec_ref, u_ref)
        uh_ref[0:SUBLANES, :] = hp_ref[...]
        uh_ref[SUBLANES:, :] = hn_ref[...]
        _store_modulated(uh_ref, vec_ref, uh_ref)

    u, uh = u_ref[...], uh_ref[...].astype(BF16)
    thirds = ((w0_ref, b0_ref), (w1_ref, b1_ref), (w2_ref, b2_ref))
    zs = [jnp.dot(u, w[...], preferred_element_type=F32) + b[...] for w, b in thirds]
    zhs = [jnp.dot(uh, w[...], preferred_element_type=F32) + b[...] for w, b in thirds]

    def conv(part):
        z, zh = zs[part], zhs[part]
        bm = z.shape[0]
        prev_row = jnp.where(i > 0, zh[SUBLANES - 1:SUBLANES, :], 0.0)
        next_row = jnp.where(i < last, zh[SUBLANES:SUBLANES + 1, :], 0.0)
        rid = lax.broadcasted_iota(jnp.int32, z.shape, 0)
        up = jnp.where(rid == 0, prev_row, pltpu.roll(z, 1, 0))
        dn = jnp.where(rid == bm - 1, next_row, pltpu.roll(z, bm - 1, 0))
        return (up * wsc_ref[0, part:part + 1, :] + z * wsc_ref[1, part:part + 1, :]
                + dn * wsc_ref[2, part:part + 1, :] + bsc_ref[part:part + 1, :])

    x0_out[...] = conv(0).astype(x0_out.dtype)
    xin_out[...] = (conv(2) * conv(1)).astype(xin_out.dtype)


def _hy_in_gate(h, vecs, w_in_all, layer, b_in, w_sc, b_sc):
    l, d = h.shape
    bm = min(l, PROJ_ROWS)
    bc = min(d, HYENA_IN_COLS)
    nc = d // bc
    rb = bm // SUBLANES
    nrow8 = l // SUBLANES
    wspecs = [pl.BlockSpec((None, d, bc), functools.partial(lambda i, j, p: (layer, 0, p * nc + j), p=part))
              for part in range(3)]
    bspecs = [pl.BlockSpec((1, bc), functools.partial(lambda i, j, p: (0, p * nc + j), p=part)) for part in range(3)]
    out_spec = pl.BlockSpec((bm, bc), lambda i, j: (i, j))
    return pl.pallas_call(
        _hy_in_body,
        grid=(l // bm, nc),
        in_specs=[pl.BlockSpec((bm, d), lambda i, j: (i, 0)),
                  pl.BlockSpec((SUBLANES, d), lambda i, j: (jnp.maximum(i * rb - 1, 0), 0)),
                  pl.BlockSpec((SUBLANES, d), lambda i, j: (jnp.minimum((i + 1) * rb, nrow8 - 1), 0)),
                  pl.BlockSpec((SUBLANES, d), lambda i, j: (0, 0))] + wspecs + bspecs
                 + [pl.BlockSpec((3, 3, bc), lambda i, j: (0, 0, j)),
                    pl.BlockSpec((3, bc), lambda i, j: (0, j))],
        out_specs=[out_spec, out_spec],
        out_shape=[jax.ShapeDtypeStruct((l, d), BF16)] * 2,
        scratch_shapes=[pltpu.VMEM((bm, d), BF16), pltpu.VMEM((2 * SUBLANES, d), F32)],
        compiler_params=_cparams("arbitrary", "arbitrary"),
        name="hyena_in_gate",
    )(h, h, h, vecs, w_in_all, w_in_all, w_in_all, *([b_in.reshape(1, 3 * d)] * 3), w_sc.reshape(3, 3, d),
      b_sc.reshape(3, d))


def _filter_mlp_body(z_ref, aux_ref, auxr_ref, flip_ref, w1, b1, w2, b2, w3, b3, fr, tab_ref, tabr_ref):
    hp = lax.Precision.HIGHEST
    hid = w1.shape[1] // 2
    f = jnp.sin(fr[0:1, :] * (jnp.dot(z_ref[...], w1[...], precision=hp, preferred_element_type=F32) + b1[...]))
    f = jnp.sin(fr[1:2, :] * (jnp.dot(f, w2[...], precision=hp, preferred_element_type=F32) + b2[...]))
    f = jnp.sin(fr[2:3, :] * (jnp.dot(f, w3[...], precision=hp, preferred_element_type=F32) + b3[...]))
    low = lax.broadcasted_iota(jnp.int32, f.shape, 1) < hid
    tab_ref[0] = jnp.where(low, f, aux_ref[0])
    tab_ref[1] = jnp.where(low, pltpu.roll(f, hid, 1), aux_ref[1])
    fr_rows = jnp.dot(flip_ref[...], f, precision=hp, preferred_element_type=F32)
    tabr_ref[0] = jnp.where(low, pltpu.roll(fr_rows, hid, 1), auxr_ref[0])
    tabr_ref[1] = jnp.where(low, fr_rows, auxr_ref[1])


def _filter_table(l, w1, b1, w2, b2, w3, b3, freq):
    hid = w1.shape[1]
    assert 2 * hid == LANES, "two filter-MLP evaluations share one 128-lane row"
    t = jnp.linspace(0.0, 1.0, l, dtype=F32)[:, None]
    bands = jnp.linspace(1e-4, FILTER_BANDS - 1, FILTER_BANDS, dtype=F32)
    step = 2.0 * math.pi / l
    ang_hi = bands[None, None, :] * (step * LANES * jnp.arange(l // LANES, dtype=F32))[:, None, None]
    ang_lo = bands[None, None, :] * (step * jnp.arange(LANES, dtype=F32))[None, :, None]
    cos_e = (jnp.cos(ang_hi) * jnp.cos(ang_lo) - jnp.sin(ang_hi) * jnp.sin(ang_lo)).reshape(l, FILTER_BANDS)
    sin_e = (jnp.sin(ang_hi) * jnp.cos(ang_lo) + jnp.cos(ang_hi) * jnp.sin(ang_lo)).reshape(l, FILTER_BANDS)
    emb = jnp.concatenate([t, cos_e, -sin_e], axis=-1)
    emb = jnp.pad(emb, ((0, 0), (0, LANES - emb.shape[1])))
    w1p = jnp.pad(w1, ((0, LANES - w1.shape[0]), (0, 0)))
    rows = l // 2
    eye = jnp.eye(2, dtype=F32)
    emb_p = jnp.concatenate([emb[:rows], emb[rows:]], axis=1)

    def side_columns(tcol):
        return jnp.pad(tcol, ((0, 0), (hid, LANES - hid - 1))).reshape(2, rows, LANES)

    bm = min(rows, FILTER_ROWS)
    nb = rows // bm
    flip = jnp.eye(bm, dtype=F32)[::-1]

    def full(shape):
        return pl.BlockSpec(shape, lambda i: (0,) * len(shape))

    fwd_spec = pl.BlockSpec((2, bm, LANES), lambda i: (0, i, 0))
    rev_spec = pl.BlockSpec((2, bm, LANES), lambda i: (0, nb - 1 - i, 0))
    tab, tab_rev = pl.pallas_call(
        _filter_mlp_body,
        grid=(nb,),
        in_specs=[pl.BlockSpec((bm, 2 * LANES), lambda i: (i, 0)), fwd_spec, rev_spec, full((bm, bm)),
                  full((2 * LANES, LANES)), full((1, LANES)), full((LANES, LANES)), full((1, LANES)),
                  full((LANES, LANES)), full((1, LANES)), full((3, LANES))],
        out_specs=[fwd_spec, rev_spec],
        out_shape=[jax.ShapeDtypeStruct((2, rows, LANES), F32)] * 2,
        compiler_params=_cparams("arbitrary"),
        name="hyena_filter_mlp",
    )(emb_p, side_columns(t), side_columns(t[::-1]), flip,
      jnp.kron(eye, w1p), jnp.tile(b1, 2).reshape(1, LANES), jnp.kron(eye, w2),
      jnp.tile(b2, 2).reshape(1, LANES), jnp.kron(eye, w3), jnp.tile(b3, 2).reshape(1, LANES),
      jnp.tile(freq, (1, 2)))
    return jnp.concatenate([tab.reshape(l, LANES), jnp.zeros((1, LANES), F32), tab_rev.reshape(l, LANES)[:l - 1]],
                           axis=0)


def _decay_rates(d):
    return jnp.abs(jnp.linspace(math.log(DECAY_TARGET) / LONG_DECAY_PCT,
                                math.log(DECAY_TARGET) / SHORT_DECAY_PCT, d, dtype=F32))[None, :]


def _split_bf16(x):
    hi = x.astype(BF16)
    return hi, (x - hi.astype(F32)).astype(BF16)


def _stack_3pass(w):
    w_hi, w_lo = _split_bf16(w)
    return jnp.concatenate([w_hi, w_hi, w_lo], axis=0)


def _dot_3pass(f, w3):
    f_hi, f_lo = _split_bf16(f)
    return jnp.dot(jnp.concatenate([f_hi, f_lo, f_hi], axis=1), w3, preferred_element_type=F32)


def _filter_rows(ft, wa3, wb3, dl_ref):
    hid = wa3.shape[0] // 3
    half = ft.shape[0] // 2
    f, t = ft[:, :hid], ft[:, hid:hid + 1]
    y = jnp.concatenate([_dot_3pass(f[:half], wa3), _dot_3pass(f[half:], wb3)], axis=0)
    return y * jnp.exp(-t * dl_ref[...])


def _outer_stage(fa_ref, cols, o_ref):
    ys = [jnp.dot(fa_ref[...], xj, preferred_element_type=F32).astype(BF16) for xj in cols]
    o_ref[...] = pltpu.einshape("brd->rbd", jnp.stack(ys))


def _dft_a_body(x_ref, fa_ref, o_ref):
    xb = pltpu.einshape("abd->bad", x_ref[...])
    _outer_stage(fa_ref, [xb[j] for j in range(xb.shape[0])], o_ref)


def _dft_a_filter_body(ft_ref, wa_ref, wb_ref, dl_ref, fa_ref, o_ref):
    wa3, wb3 = _stack_3pass(wa_ref[...]), _stack_3pass(wb_ref[...])
    cols = [_filter_rows(ft_ref[j], wa3, wb3, dl_ref).astype(BF16) for j in range(ft_ref.shape[0])]
    _outer_stage(fa_ref, cols, o_ref)


def _dft_a(x3, fa, d_chunk):
    a, b, d = x3.shape
    rows = fa.shape[0]
    return pl.pallas_call(
        _dft_a_body,
        grid=(b // HALO, d // d_chunk),
        in_specs=[pl.BlockSpec((a, HALO, d_chunk), lambda i, j: (0, i, j)),
                  pl.BlockSpec((rows, a), lambda i, j: (0, 0))],
        out_specs=pl.BlockSpec((rows, HALO, d_chunk), lambda i, j: (0, i, j)),
        out_shape=jax.ShapeDtypeStruct((rows, b, d), BF16),
        compiler_params=_cparams("arbitrary", "arbitrary"),
        name="dft_outer",
    )(x3, fa)


def _dft_a_filter(ft3, w4, deltas, fa, d_chunk):
    b, n1, _ = ft3.shape
    hid, d2 = w4.shape
    d = d2 // 2
    rows = fa.shape[0]
    nd = d // d_chunk
    return pl.pallas_call(
        _dft_a_filter_body,
        grid=(b // HALO, nd),
        in_specs=[pl.BlockSpec((HALO, n1, LANES), lambda i, j: (i, 0, 0)),
                  pl.BlockSpec((hid, d_chunk), lambda i, j: (0, j)),
                  pl.BlockSpec((hid, d_chunk), lambda i, j: (0, j + nd)),
                  pl.BlockSpec((1, d_chunk), lambda i, j: (0, j)),
                  pl.BlockSpec((rows, n1), lambda i, j: (0, 0))],
        out_specs=pl.BlockSpec((rows, HALO, d_chunk), lambda i, j: (0, i, j)),
        out_shape=jax.ShapeDtypeStruct((rows, b, d), BF16),
        compiler_params=_cparams("arbitrary", "arbitrary"),
        name="dft_outer_filter",
    )(ft3, w4, w4, deltas, fa)


def _dft_inner_body(y_ref, yf_ref, gf_ref, o_ref):
    b, dc = y_ref.shape[2], y_ref.shape[3]
    for ci in range(y_ref.shape[1]):
        z = jnp.dot(gf_ref[ci], y_ref[:, ci].reshape(2 * b, dc), preferred_element_type=F32)
        hh = jnp.dot(gf_ref[ci], yf_ref[:, ci].reshape(2 * b, dc), preferred_element_type=F32)
        zr, zi, hr, hi = z[:b], z[b:], hh[:b], hh[b:]
        p = jnp.concatenate([zr * hr - zi * hi, zr * hi + zi * hr], axis=0).astype(BF16)
        cc = lax.dot_general(gf_ref[ci], p, (((0,), (0,)), ((), ())), preferred_element_type=F32)
        o_ref[:, ci] = cc.reshape(2, b, dc).astype(o_ref.dtype)


def _dft_inner(y4, yf4, gf):
    _, nc, b, d = y4.shape
    cblk = SUBLANES
    dc = min(d, DFT_INNER_COLS)
    yspec = pl.BlockSpec((2, cblk, b, dc), lambda c, j: (0, c, 0, j))
    return pl.pallas_call(
        _dft_inner_body,
        grid=(nc // cblk, d // dc),
        in_specs=[yspec, yspec, pl.BlockSpec((cblk, 2 * b, 2 * b), lambda c, j: (c, 0, 0))],
        out_specs=yspec,
        out_shape=jax.ShapeDtypeStruct(y4.shape, BF16),
        compiler_params=_cparams("arbitrary", "arbitrary"),
        name="dft_inner",
    )(y4, yf4, gf)


def _dft_c_body(c_ref, m_ref, xin_ref, x0_ref, skip_ref, o_ref):
    cb = pltpu.einshape("rbd->brd", c_ref[...])
    outs = [jnp.dot(m_ref[...], cb[j], preferred_element_type=F32).astype(BF16) for j in range(cb.shape[0])]
    conv = pltpu.einshape("bad->abd", jnp.stack(outs)).astype(F32)
    gated = x0_ref[...].astype(F32) * (conv + xin_ref[...].astype(F32) * skip_ref[...][None])
    o_ref[...] = gated.astype(o_ref.dtype)


def _dft_c(c3, mc, xin3, x03, skip, d_chunk):
    rows, b, d = c3.shape
    a = mc.shape[0]
    xspec = pl.BlockSpec((a, HALO, d_chunk), lambda i, j: (0, i, j))
    return pl.pallas_call(
        _dft_c_body,
        grid=(b // HALO, d // d_chunk),
        in_specs=[pl.BlockSpec((rows, HALO, d_chunk), lambda i, j: (0, i, j)),
                  pl.BlockSpec((a, rows), lambda i, j: (0, 0)),
                  xspec, xspec,
                  pl.BlockSpec((1, d_chunk), lambda i, j: (0, j))],
        out_specs=xspec,
        out_shape=jax.ShapeDtypeStruct((a, b, d), BF16),
        compiler_params=_cparams("arbitrary", "arbitrary"),
        name="dft_outer_inv",
    )(c3, mc, xin3, x03, skip)


def _cis(num, period):
    ang = (2.0 * math.pi / period) * num.astype(F32)
    return jnp.cos(ang), jnp.sin(ang)


def _long_conv_gated(xin, x0, ftab, w4, skip):
    l, d = xin.shape
    b = DFT_INNER
    a = l // b
    n1, n = 2 * a, 2 * l
    nc = n1 // 2 + SUBLANES
    dc = min(d, DFT_OUTER_COLS)
    ci = jnp.arange(nc, dtype=jnp.int32)
    ai = jnp.arange(n1, dtype=jnp.int32)
    live = (ci <= n1 // 2).astype(F32)[:, None]
    cr, sr = _cis((ci[:, None] * ai[None, :]) % n1, n1)
    cr, sr = cr * live, sr * live
    fa_full = jnp.concatenate([cr, -sr], axis=0).astype(BF16)
    fa = fa_full[:, :a]
    fold = jnp.where((ci == 0) | (ci == n1 // 2), 1.0, 2.0)[None, :] / n
    mc = jnp.concatenate([cr[:, :a].T * fold, -sr[:, :a].T * fold], axis=1).astype(BF16)
    ei = jnp.arange(b, dtype=jnp.int32)
    pr, ps = _cis((ci[:, None] * ei[None, :]) % n, n)
    qr, qs = _cis((ei[:, None] * ei[None, :]) % b, b)
    tr = pr[:, None, :] * qr[None] - ps[:, None, :] * qs[None]
    ti = -(pr[:, None, :] * qs[None] + ps[:, None, :] * qr[None])
    gf = jnp.concatenate([jnp.concatenate([tr, -ti], axis=2),
                          jnp.concatenate([ti, tr], axis=2)], axis=1).astype(BF16)

    ft3 = jnp.swapaxes(ftab.reshape(n1, b, LANES), 0, 1)
    hf = _dft_a_filter(ft3, w4, _decay_rates(d), fa_full, dc)
    yx = _dft_a(xin.reshape(a, b, d), fa, dc)
    c4 = _dft_inner(yx.reshape(2, nc, b, d), hf.reshape(2, nc, b, d), gf)
    y3 = _dft_c(c4.reshape(2 * nc, b, d), mc, xin.reshape(a, b, d), x0.reshape(a, b, d), skip.reshape(1, d), dc)
    return y3.reshape(l, d)


def _ctx_conv_body(x_ref, x0_ref, ft_ref, wa_ref, wb_ref, dl_ref, skip_ref, ff_ref, fh_ref, mi_ref, o_ref):
    x = x_ref[...].astype(F32)
    n = ft_ref.shape[0]
    filt = _filter_rows(ft_ref[...], _stack_3pass(wa_ref[...]), _stack_3pass(wb_ref[...]), dl_ref)
    xs = jnp.dot(ff_ref[...], x.astype(BF16), preferred_element_type=F32)
    hs = jnp.dot(fh_ref[...], filt.astype(BF16), preferred_element_type=F32)
    xr, xi, hr, hi = xs[:n], xs[n:], hs[:n], hs[n:]
    p = jnp.concatenate([xr * hr - xi * hi, xr * hi + xi * hr], axis=0).astype(BF16)
    conv = jnp.dot(mi_ref[...], p, preferred_element_type=F32)
    o_ref[...] = (x0_ref[...].astype(F32) * (conv + x * skip_ref[...])).astype(o_ref.dtype)


def _short_seq_conv_gated(xin, x0, ftab, w4, skip):
    l, d = xin.shape
    n = 2 * l
    hid = w4.shape[0]
    ni = jnp.arange(n, dtype=jnp.int32)
    cr, sr = _cis((ni[:, None] * ni[None, :]) % n, n)
    fh = jnp.concatenate([cr, -sr], axis=0).astype(BF16)
    ff = fh[:, :l]
    mi = (jnp.concatenate([cr[:l], -sr[:l]], axis=1) * (1.0 / n)).astype(BF16)
    dc = min(d, DFT_OUTER_COLS)
    nd = d // dc
    return pl.pallas_call(
        _ctx_conv_body,
        grid=(nd,),
        in_specs=[pl.BlockSpec((l, dc), lambda j: (0, j)),
                  pl.BlockSpec((l, dc), lambda j: (0, j)),
                  pl.BlockSpec((n, LANES), lambda j: (0, 0)),
                  pl.BlockSpec((hid, dc), lambda j: (0, j)),
                  pl.BlockSpec((hid, dc), lambda j: (0, j + nd)),
                  pl.BlockSpec((1, dc), lambda j: (0, j)),
                  pl.BlockSpec((1, dc), lambda j: (0, j)),
                  pl.BlockSpec((2 * n, l), lambda j: (0, 0)),
                  pl.BlockSpec((2 * n, n), lambda j: (0, 0)),
                  pl.BlockSpec((l, 2 * n), lambda j: (0, 0))],
        out_specs=pl.BlockSpec((l, dc), lambda j: (0, j)),
        out_shape=jax.ShapeDtypeStruct((l, d), BF16),
        compiler_params=_cparams("arbitrary"),
        name="ctx_conv",
    )(xin, x0, ftab, w4, w4, _decay_rates(d), skip.reshape(1, d), ff, fh, mi)


def _hyena_mixer(h, vecs, p, long_seq, cast_next=None):
    (w_in_all, layer), b_in, w_sc, b_sc, f_w1, f_b1, f_w2, f_b2, f_w3, f_b3, f_w4, f_freq, skip, w_out, b_out = p
    l = h.shape[0]
    x0, xin = _hy_in_gate(h, vecs, w_in_all, layer, b_in, w_sc, b_sc)
    ftab = _filter_table(l, f_w1, f_b1, f_w2, f_b2, f_w3, f_b3, f_freq)
    conv = _long_conv_gated if long_seq else _short_seq_conv_gated
    y = conv(xin, x0, ftab, f_w4, skip)
    return _proj_out(y, w_out, b_out, h, vecs, cast_next)


def _attn_body(sink_ref, q_ref, kp_ref, kc_ref, kn_ref, vp_ref, vc_ref, vn_ref, kx_ref, vx_ref, bias_ref, o_ref,
               *, group):
    blk = q_ref.shape[0]
    rid = lax.broadcasted_iota(jnp.int32, (group * blk, 1), 0)
    for kh in range(N_KV_HEADS):
        hs = slice(kh * HEAD_DIM, (kh + 1) * HEAD_DIM)
        heads = [kh * group + g for g in range(group)]
        q = jnp.concatenate([q_ref[:, hd * HEAD_DIM:(hd + 1) * HEAD_DIM] for hd in heads], axis=0)
        keys = jnp.concatenate([kp_ref[:, hs], kc_ref[:, hs], kn_ref[:, hs], kx_ref[:, hs]], axis=0)
        vals = jnp.concatenate([vp_ref[:, hs], vc_ref[:, hs], vn_ref[:, hs], vx_ref[:, hs]], axis=0)
        s = lax.dot_general(q, keys, (((1,), (1,)), ((), ())), preferred_element_type=F32)
        pieces = [s[:, :blk] + bias_ref[0, :, :blk], s[:, blk:2 * blk], s[:, 2 * blk:3 * blk] + bias_ref[0, :, blk:]]
        pieces += [s[:, c0:c0 + blk] for c0 in range(3 * blk, s.shape[1], blk)]
        sink = jnp.zeros((group * blk, 1), F32)
        for g, hd in enumerate(heads):
            sink = jnp.where((rid >= g * blk) & (rid < (g + 1) * blk), sink_ref[hd] * LOG2_E, sink)
        top = functools.reduce(jnp.maximum, pieces)
        mx = jnp.maximum(jnp.max(top, axis=-1, keepdims=True), sink)
        probs = [jnp.exp2(pc - mx) for pc in pieces]
        denom = jnp.sum(functools.reduce(jnp.add, probs), axis=-1, keepdims=True) + jnp.exp2(sink - mx)
        pr = jnp.concatenate([pp.astype(BF16) for pp in probs], axis=1)
        o = jnp.dot(pr, vals, preferred_element_type=F32) / denom
        for g, hd in enumerate(heads):
            o_ref[:, hd * HEAD_DIM:(hd + 1) * HEAD_DIM] = o[g * blk:(g + 1) * blk].astype(o_ref.dtype)


def _attention(qkv, kvc, sink, d):
    l = qkv.shape[0]
    c = kvc.shape[0]
    blk = ATTN_BLOCK
    nb = l // blk
    group = d // HEAD_DIM // N_KV_HEADS
    kvw = N_KV_HEADS * HEAD_DIM
    kcol = d // kvw
    qi = jnp.arange(group * blk, dtype=jnp.int32)[:, None] % blk
    ki = jnp.arange(blk, dtype=jnp.int32)[None, :]
    prev_ok, next_ok, never = ki >= qi, ki <= qi, jnp.zeros((group * blk, blk), bool)
    variants = [(never, next_ok), (prev_ok, next_ok), (prev_ok, never)]
    bias = jnp.stack([jnp.where(jnp.concatenate(v, axis=1), 0.0, MASK_BIAS).astype(F32) for v in variants])

    def kv_spec(col, shift):
        return pl.BlockSpec((blk, kvw), lambda n: (jnp.clip(n + shift, 0, nb - 1), col))

    return pl.pallas_call(
        functools.partial(_attn_body, group=group),
        grid=(nb,),
        in_specs=[pl.BlockSpec(memory_space=pltpu.SMEM),
                  pl.BlockSpec((blk, d), lambda n: (n, 0)),
                  kv_spec(kcol, -1), kv_spec(kcol, 0), kv_spec(kcol, 1),
                  kv_spec(kcol + 1, -1), kv_spec(kcol + 1, 0), kv_spec(kcol + 1, 1),
                  pl.BlockSpec((c, kvw), lambda n: (0, 0)),
                  pl.BlockSpec((c, kvw), lambda n: (0, 1)),
                  pl.BlockSpec((1, group * blk, 2 * blk),
                               lambda n: (jnp.where(n == 0, 0, jnp.where(n == nb - 1, 2, 1)), 0, 0))],
        out_specs=pl.BlockSpec((blk, d), lambda n: (n, 0)),
        out_shape=jax.ShapeDtypeStruct((l, d), BF16),
        compiler_params=_cparams("arbitrary"),
        name="window_attn",
    )(sink, qkv, qkv, qkv, qkv, qkv, qkv, qkv, kvc, kvc, bias)


def _rope_tables(l):
    rows = l // GRID_W
    pairs = HEAD_DIM // 4
    inv_freq = ROPE_THETA ** (-jnp.arange(pairs, dtype=F32) / pairs)
    ang_row = jnp.arange(rows).astype(F32)[:, None] * inv_freq[None, :]
    ang_col = jnp.arange(GRID_W).astype(F32)[:, None] * inv_freq[None, :]

    def per_token(row_tab, col_tab):
        by_row = jnp.broadcast_to(row_tab[:, None, :], (rows, GRID_W, pairs)).reshape(l, pairs)
        by_col = jnp.broadcast_to(col_tab[None, :, :], (rows, GRID_W, pairs)).reshape(l, pairs)
        return by_row, by_col

    cos_r, cos_c = per_token(jnp.cos(ang_row), jnp.cos(ang_col))
    sin_r, sin_c = per_token(jnp.sin(ang_row), jnp.sin(ang_col))
    zeros = jnp.zeros_like(cos_r)
    cos = jnp.concatenate([cos_r, cos_r, cos_c, cos_c], axis=-1)
    sin_a = jnp.concatenate([-sin_r, zeros, -sin_c, zeros], axis=-1)
    sin_b = jnp.concatenate([zeros, sin_r, zeros, sin_c], axis=-1)
    return cos, sin_a, sin_b


def _pool_body(hc_ref, hp_ref, hn_ref, vec_ref, w_ref, b_ref, sc_ref, *rest, seq_len, cast):
    if cast:
        ci_ref, co_ref, o_ref, cib_ref, cob_ref, y_ref = rest
        cib_ref[...] = ci_ref[...].astype(cib_ref.dtype)
        cob_ref[...] = co_ref[...].astype(cob_ref.dtype)
    else:
        o_ref, y_ref = rest
    i = pl.program_id(0)
    last = pl.num_programs(0) - 1
    bm, d = hc_ref.shape
    gw = d // len(POOL_SIZES)
    h = hc_ref[...]
    u = _modulated(h, vec_ref)
    up = jnp.where(i > 0, _modulated(hp_ref[...], vec_ref), 0.0)
    un = jnp.where(i < last, _modulated(hn_ref[...], vec_ref), 0.0)
    ext_rows = bm + 2 * SUBLANES
    t = i * bm + lax.broadcasted_iota(jnp.int32, (bm, 1), 0)
    for g, size in enumerate(POOL_SIZES):
        cols = slice(g * gw, (g + 1) * gw)
        ext = jnp.concatenate([up[:, cols], u[:, cols], un[:, cols]], axis=0)
        acc, span = ext, 1
        while span < size:
            acc = acc + pltpu.roll(acc, ext_rows - span, 0)
            span *= 2
        start = SUBLANES - size // 2
        win = pltpu.roll(acc, ext_rows - start, 0)[:bm] if start else acc[:bm]
        lo = jnp.clip(t - size // 2, 0, seq_len)
        hi = jnp.clip(t - size // 2 + size, 0, seq_len)
        part = win / (hi - lo).astype(F32) - u[:, cols]
        yg = jnp.dot(part.astype(BF16), w_ref[g], preferred_element_type=F32)
        y_ref[:, cols] = (yg + b_ref[:, cols]) * sc_ref[:, cols]
    o_ref[...] = h + vec_ref[2:3, :] * _rms(y_ref[...], vec_ref[4:5, :])


def _pool_mixer(h, vecs, w, b, scale, cast_next=None):
    l, d = h.shape
    bm = min(l, FFN_ROWS)
    rb = bm // SUBLANES
    nrow8 = l // SUBLANES
    ng, gw = w.shape[0], w.shape[1]
    in_specs = [pl.BlockSpec((bm, d), lambda i: (i, 0)),
                pl.BlockSpec((SUBLANES, d), lambda i: (jnp.maximum(i * rb - 1, 0), 0)),
                pl.BlockSpec((SUBLANES, d), lambda i: (jnp.minimum((i + 1) * rb, nrow8 - 1), 0)),
                pl.BlockSpec((SUBLANES, d), lambda i: (0, 0)),
                pl.BlockSpec((ng, gw, gw), lambda i: (0, 0, 0)),
                pl.BlockSpec((1, d), lambda i: (0, 0)),
                pl.BlockSpec((1, d), lambda i: (0, 0))]
    out_specs = [pl.BlockSpec((bm, d), lambda i: (i, 0))]
    out_shape = [jax.ShapeDtypeStruct((l, d), F32)]
    args = [h, h, h, vecs, w.astype(BF16), b.reshape(1, d), scale.reshape(1, d)]
    if cast_next is not None:
        c_in, c_out, c_shape, c_args = _row_slab_cast(cast_next, l // bm)
        in_specs, out_specs, out_shape, args = in_specs + c_in, out_specs + c_out, out_shape + c_shape, args + c_args
    outs = pl.pallas_call(
        functools.partial(_pool_body, seq_len=l, cast=cast_next is not None),
        grid=(l // bm,),
        in_specs=in_specs,
        out_specs=out_specs,
        out_shape=out_shape,
        scratch_shapes=[pltpu.VMEM((bm, d), F32)],
        compiler_params=_cparams("arbitrary"),
        name="pool_mixer",
    )(*args)
    return outs if cast_next is not None else outs[0]


def _sub_vecs(mod, k, g_pre, g_post):
    rows = [mod[3 * k], mod[3 * k + 1], mod[3 * k + 2], g_pre, g_post]
    return jnp.stack(rows + [jnp.zeros_like(g_pre)] * (SUBLANES - len(rows)))


def kernel(x, c, ctx, c_ctx, w_ada, b_ada, norm_pre, norm_post, w_ffn_in, w_ffn_out, hy_w_in, hy_b_in, hy_w_sc, hy_b_sc, hy_f_w1, hy_f_b1, hy_f_w2, hy_f_b2, hy_f_w3, hy_f_b3, hy_f_w4, hy_f_freq, hy_skip, hy_w_out, hy_b_out, at_w_qkv, at_b_qkv, at_sink, at_w_o, at_b_o, pl_w, pl_b, pl_scale):
    bsz, l, d = x.shape
    assert bsz == 1, "kernel handles a single batch element"
    assert d % DFT_INNER_COLS == 0 and l % PROJ_ROWS == 0 and ctx.shape[1] % SMALL_ROWS == 0, \
        "channel and token counts must be multiples of the block sizes"
    assert w_ffn_out.shape[2] % FFN_CHUNK == 0, "FFN width must be a multiple of the hidden chunk"
    depth = w_ada.shape[0]
    n_mixers = 3
    attn_layers = [i for i in range(depth) if i % n_mixers == 1]
    last_ctx_layer = attn_layers[-1] if attn_layers else -1

    mods = _adaln_mods(c, c_ctx, w_ada, b_ada).reshape(depth, 2, N_MOD, d)
    rope = _rope_tables(l)
    w_next = [w_ffn_in[0, 0].astype(BF16), w_ffn_out[0, 0].astype(BF16)]
    hy_w_in_bf = hy_w_in.astype(BF16)
    h, hc = x[0], ctx[0]
    for i in range(depth):
        kind, j = i % n_mixers, i // n_mixers
        ctx_live = i <= last_ctx_layer
        ctx_out = i < last_ctx_layer
        vec = [_sub_vecs(mods[i, 0], k, norm_pre[i, k], norm_post[i, k]) for k in range(3)]
        vec_c = [_sub_vecs(mods[i, 1], k, norm_pre[i, k], norm_post[i, k]) for k in range(3)]

        w_now = w_next
        h = _ffn(h, vec[0], *w_now, FFN_RES)
        if ctx_live:
            hc = _ffn(hc, vec_c[0], *w_now, FFN_RES)
        cast_mid = (w_ffn_in, w_ffn_out, i, 1)

        if kind == 0:
            hp = ((hy_w_in_bf, j), hy_b_in[j], hy_w_sc[j], hy_b_sc[j], hy_f_w1[j], hy_f_b1[j],
                  hy_f_w2[j], hy_f_b2[j], hy_f_w3[j], hy_f_b3[j], hy_f_w4[j], hy_f_freq[j], hy_skip[j],
                  hy_w_out[j].astype(BF16), hy_b_out[j])
            h, *w_next = _hyena_mixer(h, vec[1], hp, long_seq=True, cast_next=cast_mid)
            if ctx_out:
                hc = _hyena_mixer(hc, vec_c[1], hp, long_seq=False)
        elif kind == 1:
            w_qkv = at_w_qkv[j].astype(BF16)
            qkv = _proj_in(h, vec[1], w_qkv, at_b_qkv[j], BF16, rope=rope,
                           q_blocks=d // PROJ_COLS, q_scale=LOG2_E * HEAD_DIM ** -0.5)
            kvc = _proj_in(hc, vec_c[1], w_qkv[:, d:], at_b_qkv[j][d:], BF16)
            o = _attention(qkv, kvc, at_sink[j], d)
            h, *w_next = _proj_out(o, at_w_o[j].astype(BF16), at_b_o[j], h, vec[1], cast_mid)
            assert not ctx_out, "context-query attention path is not needed for this depth"
        else:
            h, *w_next = _pool_mixer(h, vec[1], pl_w[j], pl_b[j], pl_scale[j], cast_mid)
            assert not ctx_out, "context pooling path is not needed for this depth"

        w_now = w_next
        if i + 1 < depth:
            h, *w_next = _ffn(h, vec[2], *w_now, FFN_RES, cast_next=(w_ffn_in, w_ffn_out, i + 1, 0))
        else:
            h = _ffn(h, vec[2], *w_now, FFN_RES)
        if ctx_out:
            hc = _ffn(hc, vec_c[2], *w_now, FFN_RES)
    return h[None]
```
